```python
import math
import jax, jax.numpy as jnp
from jax import lax
import numpy as np

D_MODEL = 1024
BATCH = 8
SEQ = 2048
DEPTH = 2

GRID_W = 64
CTX_LEN = 256
EPS = 1e-6
HA = 4
DKA = 128
DVA = 128
CONV_K = 3
CHUNK_A = 64
HB = 4
DKB = 64
DVB = 128
GATE_RANK = 16
GATE_NORM = 16.0
CHUNK_B = 16
HC = 4
DKC = 64
DVC = 128
Q_BLOCK = 128
ROPE_THETA = 10000.0
D_FF = 2816
N_EXPERTS = 8
TOP_K = 2
N_DENSE = (DEPTH + 1) // 2
N_MOE = DEPTH // 2
CONV_CH = 2 * HA * DKA + HA * DVA
IN_SIZES = (HA * DKA, HA * DKA, HA * DVA, HA * DVA, 2 * HA, 2 * HA,
            HB * DKB, HB * DKB, HB * DVB, HB * DVB, 2 * GATE_RANK,
            2 * HC * DKC, 2 * HC * DKC, HC * DVC,
            3 * D_MODEL)
IN_COLS = sum(IN_SIZES)

kernel_name = 'hybrid_deltanet_gla_diffattn_moe_dit'


def _rmsnorm(x, g):
    x32 = x.astype(jnp.float32)
    y = x32 * lax.rsqrt(jnp.mean(jnp.square(x32), axis=-1, keepdims=True) + EPS)
    return y * g.astype(jnp.float32)


def _modulate(x, g, shift, scale):
    return (_rmsnorm(x, g) * (1.0 + scale) + shift).astype(x.dtype)


def _l2norm(t):
    return t * lax.rsqrt(jnp.sum(jnp.square(t), axis=-1, keepdims=True) + EPS)


def _split(t, sizes):
    offsets, acc = [], 0
    for s in sizes[:-1]:
        acc += s
        offsets.append(acc)
    return jnp.split(t, offsets, axis=-1)


def _heads(t, n_heads):
    return t.reshape(t.shape[0], t.shape[1], n_heads, -1).transpose(0, 2, 1, 3)


def _short_conv(x, w):
    return lax.conv_general_dilated(x, w.astype(x.dtype)[:, None, :], window_strides=(1,),
                                    padding=[(CONV_K // 2, CONV_K // 2)],
                                    dimension_numbers=('NWC', 'WIO', 'NWC'),
                                    feature_group_count=x.shape[-1])


def _delta_chunks(q, k, v, g, beta, s0, with_out):
    bsz, nh, n, _ = q.shape
    nc = n // CHUNK_A
    q, k, v = (t.reshape(bsz, nh, nc, CHUNK_A, -1) for t in (q, k, v))
    gc = jnp.cumsum(g.reshape(bsz, nh, nc, CHUNK_A), axis=-1)
    beta = beta.reshape(bsz, nh, nc, CHUNK_A)[..., None]
    incl = jnp.tril(jnp.ones((CHUNK_A, CHUNK_A), bool))
    decay = jnp.exp(jnp.where(incl, gc[..., :, None] - gc[..., None, :], -jnp.inf))
    kb = k * beta
    eye = jnp.eye(CHUNK_A, dtype=jnp.float32)
    lmat = jnp.tril(jnp.einsum('bhnid,bhnjd->bhnij', kb, k) * decay, -1)
    tinv = lax.linalg.triangular_solve(eye + lmat, jnp.broadcast_to(eye, lmat.shape),
                                       left_side=True, lower=True, unit_diagonal=True)
    u = tinv @ (v * beta)
    w = tinv @ (kb * jnp.exp(gc)[..., None])
    kg = k * jnp.exp(gc[..., -1:] - gc)[..., None]
    glast = jnp.exp(gc[..., -1])[..., None, None]
    xs = (u, w, kg, glast)
    if with_out:
        qk = jnp.einsum('bhnid,bhnjd->bhnij', q, k) * decay
        xs = xs + (q * jnp.exp(gc)[..., None], qk)

    def step(s, xs_n):
        u_n, w_n, kg_n, gl_n, *rest = xs_n
        v_new = u_n - w_n @ s
        o = rest[0] @ s + rest[1] @ v_new if with_out else None
        return s * gl_n + jnp.swapaxes(kg_n, -1, -2) @ v_new, o

    s, o = lax.scan(step, s0, tuple(jnp.moveaxis(t, 2, 0) for t in xs))
    if with_out:
        o = jnp.moveaxis(o, 0, 2).reshape(bsz, nh, n, -1)
    return o, s


def _gla_chunks(q, k, v, glog, s0, with_out):
    bsz, nh, n, _ = q.shape
    nc = n // CHUNK_B
    q, k, v, glog = (t.reshape(bsz, nh, nc, CHUNK_B, -1) for t in (q, k, v, glog))
    gc = jnp.cumsum(glog, axis=3)
    kg = k * jnp.exp(gc[..., -1:, :] - gc)
    glast = jnp.exp(gc[..., -1, :])[..., None]
    xs = (kg, v, glast)
    if with_out:
        incl = jnp.tril(jnp.ones((CHUNK_B, CHUNK_B), bool))[..., None]
        rel = jnp.exp(jnp.where(incl, gc[..., :, None, :] - gc[..., None, :, :], -jnp.inf))
        o_intra = jnp.einsum('bhnid,bhnjd,bhnijd->bhnij', q, k, rel) @ v
        xs = xs + (q * jnp.exp(gc), o_intra)

    def step(s, xs_n):
        kg_n, v_n, gl_n, *rest = xs_n
        o = rest[0] @ s + rest[1] if with_out else None
        return s * gl_n + jnp.swapaxes(kg_n, -1, -2) @ v_n, o

    s, o = lax.scan(step, s0, tuple(jnp.moveaxis(t, 2, 0) for t in xs))
    if with_out:
        o = jnp.moveaxis(o, 0, 2).reshape(bsz, nh, n, -1)
    return o, s


def _bidir_scan(chunk_fn, ctx_dirs, lat_dirs, state_shape, ctx_out):
    o_ctx, o_lat = None, None
    for d in range(2):
        fl = (lambda t: jnp.flip(t, axis=2)) if d == 1 else (lambda t: t)
        oc, s_ctx = chunk_fn(*[fl(t) for t in ctx_dirs[d]], jnp.zeros(state_shape, jnp.float32), ctx_out)
        ol, _ = chunk_fn(*[fl(t) for t in lat_dirs[d]], s_ctx, True)
        o_lat = fl(ol) if o_lat is None else o_lat + fl(ol)
        if ctx_out:
            o_ctx = fl(oc) if o_ctx is None else o_ctx + fl(oc)
    return o_ctx, o_lat


def _deltanet_branch(pl, pc, conv_w, a_log, dt_bias, gn, ctx_out):
    def prep(p):
        q, k, v, z, b, a = p
        bsz, n = q.shape[:2]
        qkv = jax.nn.silu(_short_conv(jnp.concatenate([q, k, v], axis=-1), conv_w))
        q, k, v = jnp.split(qkv, [HA * DKA, 2 * HA * DKA], axis=-1)
        q = _l2norm(_heads(q, HA)) * DKA ** -0.5
        k = _l2norm(_heads(k, HA))
        v = _heads(v, HA)
        beta = jax.nn.sigmoid(b.reshape(bsz, n, 2, HA)).transpose(2, 0, 3, 1)
        g = (-jnp.exp(a_log) * jax.nn.softplus(a.reshape(bsz, n, 2, HA) + dt_bias)).transpose(2, 0, 3, 1)
        return [(q, k, v, g[d], beta[d]) for d in range(2)]

    o_c, o_l = _bidir_scan(_delta_chunks, prep(pc), prep(pl), (pl[0].shape[0], HA, DKA, DVA), ctx_out)

    def out(o, z):
        bsz, n = z.shape[:2]
        o = _rmsnorm(o.transpose(0, 2, 1, 3), gn) * jax.nn.silu(z.reshape(bsz, n, HA, DVA))
        return o.reshape(bsz, n, HA * DVA)

    return out(o_l, pl[3]), (out(o_c, pc[3]) if ctx_out else None)


def _gla_branch(pl, pc, w_gate2, b_gate, gn, ctx_out):
    def prep(p):
        q, k, v, r, glr = p
        bsz, n = q.shape[:2]
        q = _heads(q, HB) * DKB ** -0.5
        k = _heads(k, HB)
        v = _heads(v, HB)
        glr = glr.reshape(bsz, n, 2, GATE_RANK)
        return [(q, k, v, _heads(jax.nn.log_sigmoid(glr[:, :, d] @ w_gate2[d] + b_gate[d]) / GATE_NORM, HB))
                for d in range(2)]

    o_c, o_l = _bidir_scan(_gla_chunks, prep(pc), prep(pl), (pl[0].shape[0], HB, DKB, DVB), ctx_out)

    def out(o, r):
        bsz, n = r.shape[:2]
        o = _rmsnorm(o.transpose(0, 2, 1, 3), gn) * jax.nn.silu(r.reshape(bsz, n, HB, DVB))
        return o.reshape(bsz, n, HB * DVB)

    return out(o_l, pl[3]), (out(o_c, pc[3]) if ctx_out else None)


def _rope_tables(n_tokens):
    rows = n_tokens // GRID_W
    row = jnp.repeat(jnp.arange(rows, dtype=jnp.float32), GRID_W)
    col = jnp.tile(jnp.arange(GRID_W, dtype=jnp.float32), rows)
    n_freq = DKC // 4
    inv_freq = ROPE_THETA ** (-jnp.arange(n_freq, dtype=jnp.float32) / n_freq)
    ang_r = row[:, None] * inv_freq
    ang_c = col[:, None] * inv_freq
    return jnp.cos(ang_r), jnp.sin(ang_r), jnp.cos(ang_c), jnp.sin(ang_c)


def _rope_axis(x, cos, sin):
    x1, x2 = jnp.split(x, 2, axis=-1)
    return jnp.concatenate([x1 * cos - x2 * sin, x2 * cos + x1 * sin], axis=-1)


def _axial_rope(x, rope):
    cos_r, sin_r, cos_c, sin_c = rope
    xr, xc = jnp.split(x, 2, axis=-1)
    return jnp.concatenate([_rope_axis(xr, cos_r, sin_r), _rope_axis(xc, cos_c, sin_c)], axis=-1)


def _diff_branch(pl, pc, lam, gn, lam_init, ctx_out):
    def qk_heads(t):
        return t.reshape(t.shape[0], t.shape[1], HC, 2, DKC).transpose(0, 2, 3, 1, 4)

    ql, kl, vl = pl
    qc, kc, vc = pc
    bsz, n = ql.shape[:2]
    rope = _rope_tables(n)
    scale = DKC ** -0.5
    ql = _axial_rope(qk_heads(ql), rope) * scale
    kl = _axial_rope(qk_heads(kl), rope)
    qc = qk_heads(qc) * scale
    kc = qk_heads(kc)
    vl = _heads(vl, HC)
    vc = _heads(vc, HC)
    lam_f = (jnp.exp(jnp.sum(lam[0] * lam[1])) - jnp.exp(jnp.sum(lam[2] * lam[3])) + lam_init).astype(jnp.float32)

    def attend(qb, k, v):
        p = jax.nn.softmax(jnp.einsum('bhmqd,bhmkd->bhmqk', qb, k), axis=-1)
        return jnp.einsum('bhqk,bhkd->bhqd', p[:, :, 0] - lam_f * p[:, :, 1], v)

    k_all = jnp.concatenate([kc, kl], axis=3)
    v_all = jnp.concatenate([vc, vl], axis=2)
    n_blk = n // Q_BLOCK
    q_blocks = jnp.moveaxis(ql.reshape(bsz, HC, 2, n_blk, Q_BLOCK, DKC), 3, 0)
    o_l = lax.map(lambda qb: attend(qb, k_all, v_all), q_blocks)
    o_l = jnp.moveaxis(o_l, 0, 2).reshape(bsz, HC, n, DVC)

    def out(o):
        o = _rmsnorm(o, gn) * (1.0 - lam_init)
        return o.transpose(0, 2, 1, 3).reshape(o.shape[0], o.shape[2], HC * DVC)

    return out(o_l), (out(attend(qc, kc, vc)) if ctx_out else None)


def _mixer(hl, hc, w_in, conv_a, a_log, dt_bias, gn_a, w_gate2, b_gate, gn_b, lam_c, gn_c,
           w_o_a, w_o_b, w_o_c, w_out, lam_init, ctx_out):
    pl = _split((hl @ w_in).astype(jnp.float32), IN_SIZES)
    pc = _split((hc @ w_in).astype(jnp.float32), IN_SIZES)
    a_l, a_c = _deltanet_branch(pl[0:6], pc[0:6], conv_a, a_log, dt_bias, gn_a, ctx_out)
    b_l, b_c = _gla_branch(pl[6:11], pc[6:11], w_gate2, b_gate, gn_b, ctx_out)
    d_l, d_c = _diff_branch(pl[11:14], pc[11:14], lam_c, gn_c, lam_init, ctx_out)

    def merge(ya, yb, yd, gates):
        ga, gb, gd = jnp.split(jax.nn.sigmoid(gates), 3, axis=-1)
        return (ga * (ya @ w_o_a) + gb * (yb @ w_o_b) + gd * (yd @ w_o_c)) @ w_out

    return merge(a_l, b_l, d_l, pl[14]), (merge(a_c, b_c, d_c, pc[14]) if ctx_out else None)


def _swiglu(h, w1, w3, w2):
    return (jax.nn.silu(h @ w1) * (h @ w3)) @ w2


def _moe(h, router_w, w1, w3, w2):
    logits = (h @ router_w).astype(jnp.float32)
    top_val, top_idx = lax.top_k(logits, TOP_K)
    wts = jax.nn.softmax(top_val, axis=-1)
    gate = jnp.einsum('blk,blke->ble', wts, jax.nn.one_hot(top_idx, N_EXPERTS, dtype=jnp.float32))
    out = None
    for e in range(N_EXPERTS):
        contrib = gate[..., e:e + 1] * _swiglu(h, w1[e], w3[e], w2[e])
        out = contrib if out is None else out + contrib
    return out


def setup_inputs(seed: int = 0) -> dict:
    key = jax.random.key(seed)
    ks = iter(jax.random.split(key, 40))
    f32 = jnp.float32

    def nrm(shape, scale):
        return jax.random.normal(next(ks), shape, f32) * scale

    def gain(shape):
        return 1.0 + nrm(shape, 0.02)

    dt = jnp.exp(jax.random.uniform(next(ks), (DEPTH, 2, HA), f32, math.log(1e-3), math.log(1e-1)))
    return {
        'x': nrm((BATCH, SEQ, D_MODEL), 1.0),
        'c': nrm((BATCH, D_MODEL), 1.0),
        'ctx': nrm((BATCH, CTX_LEN, D_MODEL), 1.0),
        'c_ctx': nrm((D_MODEL,), 1.0),
        'w_mod': nrm((DEPTH, D_MODEL, 6 * D_MODEL), 0.5 * D_MODEL ** -0.5),
        'b_mod': nrm((DEPTH, 6 * D_MODEL), 0.01),
        'norm1_g': gain((DEPTH, D_MODEL)),
        'norm2_g': gain((DEPTH, D_MODEL)),
        'w_in': nrm((DEPTH, D_MODEL, IN_COLS), D_MODEL ** -0.5),
        'conv_a': nrm((DEPTH, CONV_K, CONV_CH), CONV_K ** -0.5),
        'a_log': jnp.log(jax.random.uniform(next(ks), (DEPTH, 2, HA), f32, 1.0, 16.0)),
        'dt_bias': dt + jnp.log(-jnp.expm1(-dt)),
        'gn_a': gain((DEPTH, DVA)),
        'w_gate2': nrm((DEPTH, 2, GATE_RANK, HB * DKB), GATE_RANK ** -0.5),
        'b_gate': nrm((DEPTH, 2, HB * DKB), 0.1),
        'gn_b': gain((DEPTH, DVB)),
        'lam_c': nrm((DEPTH, 4, DKC), 0.1),
        'gn_c': gain((DEPTH, DVC)),
        'w_o_a': nrm((DEPTH, HA * DVA, D_MODEL), (HA * DVA) ** -0.5),
        'w_o_b': nrm((DEPTH, HB * DVB, D_MODEL), (HB * DVB) ** -0.5),
        'w_o_c': nrm((DEPTH, HC * DVC, D_MODEL), (HC * DVC) ** -0.5),
        'w_out': nrm((DEPTH, D_MODEL, D_MODEL), D_MODEL ** -0.5),
        'ffn_w1': nrm((N_DENSE, D_MODEL, D_FF), D_MODEL ** -0.5),
        'ffn_w3': nrm((N_DENSE, D_MODEL, D_FF), D_MODEL ** -0.5),
        'ffn_w2': nrm((N_DENSE, D_FF, D_MODEL), D_FF ** -0.5),
        'router_w': nrm((N_MOE, D_MODEL, N_EXPERTS), D_MODEL ** -0.5),
        'moe_w1': nrm((N_MOE, N_EXPERTS, D_MODEL, D_FF), D_MODEL ** -0.5),
        'moe_w3': nrm((N_MOE, N_EXPERTS, D_MODEL, D_FF), D_MODEL ** -0.5),
        'moe_w2': nrm((N_MOE, N_EXPERTS, D_FF, D_MODEL), D_FF ** -0.5),
        'final_g': gain((D_MODEL,)),
    }


def reference(x, c, ctx, c_ctx, w_mod, b_mod, norm1_g, norm2_g, w_in, conv_a, a_log, dt_bias, gn_a,
              w_gate2, b_gate, gn_b, lam_c, gn_c, w_o_a, w_o_b, w_o_c, w_out, ffn_w1, ffn_w3, ffn_w2,
              router_w, moe_w1, moe_w3, moe_w2, final_g):
    xl, xc = x, ctx
    for l in range(DEPTH):
        ctx_out = l < DEPTH - 1
        lam_init = 0.8 - 0.6 * math.exp(-0.3 * l)
        mod_l = jnp.split((jax.nn.silu(c) @ w_mod[l] + b_mod[l])[:, None, :], 6, axis=-1)
        mod_c = jnp.split(jax.nn.silu(c_ctx) @ w_mod[l] + b_mod[l], 6, axis=-1)
        hl = _modulate(xl, norm1_g[l], mod_l[0], mod_l[1])
        hc = _modulate(xc, norm1_g[l], mod_c[0], mod_c[1])
        yl, yc = _mixer(hl, hc, w_in[l], conv_a[l], a_log[l], dt_bias[l], gn_a[l], w_gate2[l], b_gate[l],
                        gn_b[l], lam_c[l], gn_c[l], w_o_a[l], w_o_b[l], w_o_c[l], w_out[l], lam_init, ctx_out)
        xl = xl + (mod_l[2] * yl).astype(xl.dtype)
        if ctx_out:
            xc = xc + (mod_c[2] * yc).astype(xc.dtype)

        def ffn(h):
            i = l // 2
            if l % 2 == 0:
                return _swiglu(h, ffn_w1[i], ffn_w3[i], ffn_w2[i])
            return _moe(h, router_w[i], moe_w1[i], moe_w3[i], moe_w2[i])

        hl = _modulate(xl, norm2_g[l], mod_l[3], mod_l[4])
        xl = xl + (mod_l[5] * ffn(hl)).astype(xl.dtype)
        if ctx_out:
            hc = _modulate(xc, norm2_g[l], mod_c[3], mod_c[4])
            xc = xc + (mod_c[5] * ffn(hc)).astype(xc.dtype)
    return _rmsnorm(xl, final_g).astype(x.dtype)
```

```python
import functools
import math

import jax
import jax.numpy as jnp
from jax import lax
from jax.experimental import pallas as pl
from jax.experimental.pallas import tpu as pltpu

F32 = jnp.float32
BF16 = jnp.bfloat16
HIGHEST = lax.Precision.HIGHEST

EPS = 1e-6
HA, DKA, DVA = 4, 128, 128
HB, DKB, DVB = 4, 64, 128
GATE_RANK, GATE_NORM = 16, 16.0
HC, DKC, DVC = 4, 64, 128
GRID_W, ROPE_THETA = 64, 10000.0
N_EXPERTS, TOP_K = 8, 2
CHUNK = 64
LANES = 128

COL_A = 0
COL_B = 2048
COL_C = 3584
COL_G = 5120
COL_S = 8192
N_COLS = 8320

VMEM_LIMIT = 56 * 1024 * 1024


def _cparams(n_axes):
    return pltpu.CompilerParams(dimension_semantics=("arbitrary",) * n_axes,
                                vmem_limit_bytes=VMEM_LIMIT)


def _sigmoid(x):
    return 1.0 / (1.0 + jnp.exp(-x))


def _silu(x):
    return x * _sigmoid(x)


def _softplus(x):
    return jnp.maximum(x, 0.0) + jnp.log(1.0 + jnp.exp(-jnp.abs(x)))


def _dot(a, b):
    return jnp.dot(a, b, preferred_element_type=F32)


def _dot_nt(a, b):
    return lax.dot_general(a, b, (((1,), (1,)), ((), ())), preferred_element_type=F32)


def _dot_tn(a, b):
    return lax.dot_general(a, b, (((0,), (0,)), ((), ())), preferred_element_type=F32)


def _split(a):
    hi = a.astype(BF16)
    lo = (a - hi.astype(F32)).astype(BF16)
    return hi, lo


def _dot3(a, b):
    ah, al = _split(a)
    bh, bl = _split(b)
    return _dot(ah, bh) + (_dot(ah, bl) + _dot(al, bh))


def _rms_rows(x, gain):
    ms = jnp.mean(x * x, axis=-1, keepdims=True)
    return x * lax.rsqrt(ms + EPS) * gain


def _mod_kernel(c_ref, w_ref, b_ref, o_ref):
    s = _silu(c_ref[...])
    o_ref[0] = jnp.dot(s, w_ref[0], precision=HIGHEST, preferred_element_type=F32) + b_ref[0]


def _modulation(cvec, w_mod, b_mod):
    depth, d, d6 = w_mod.shape
    n = cvec.shape[0]
    return pl.pallas_call(
        _mod_kernel,
        grid=(depth, d6 // d),
        in_specs=[pl.BlockSpec((n, d), lambda l, j: (0, 0)),
                  pl.BlockSpec((1, d, d), lambda l, j: (l, 0, j)),
                  pl.BlockSpec((1, 1, d), lambda l, j: (l, 0, j))],
        out_specs=pl.BlockSpec((1, n, d), lambda l, j: (l, 0, j)),
        out_shape=jax.ShapeDtypeStruct((depth, n, d6), F32),
        compiler_params=_cparams(2),
        name="adaln_mod",
    )(cvec, w_mod, b_mod.reshape(depth, 1, d6))


def _inproj_kernel(x_ref, mod_ref, g_ref, w_ref, o_ref, h_scr):
    @pl.when(pl.program_id(1) == 0)
    def _():
        y = _rms_rows(x_ref[...], g_ref[...])
        h_scr[...] = (y * (1.0 + mod_ref[0, 1:2, :]) + mod_ref[0, 0:1, :]).astype(BF16)

    o_ref[...] = _dot(h_scr[...], w_ref[...])


def _in_projection(x, mod, g, w, rows_per_mod, tm, tn):
    t, d = x.shape
    nc = w.shape[1]
    per = rows_per_mod // tm
    return pl.pallas_call(
        _inproj_kernel,
        grid=(t // tm, nc // tn),
        in_specs=[pl.BlockSpec((tm, d), lambda i, j: (i, 0)),
                  pl.BlockSpec((1, 6, d), lambda i, j: (i // per, 0, 0)),
                  pl.BlockSpec((1, d), lambda i, j: (0, 0)),
                  pl.BlockSpec((d, tn), lambda i, j: (0, j))],
        out_specs=pl.BlockSpec((tm, tn), lambda i, j: (i, j)),
        out_shape=jax.ShapeDtypeStruct((t, nc), F32),
        scratch_shapes=[pltpu.VMEM((tm, d), BF16)],
        compiler_params=_cparams(2),
        name="norm_mod_inproj",
    )(x, mod, g.reshape(1, d), w)


def _conv_kernel(x_ref, w_ref, o_ref, *, lc, l):
    p = pl.program_id(0)
    j = pl.program_id(1)
    x = x_ref[...]
    n = x.shape[0]
    row = lax.broadcasted_iota(jnp.int32, x.shape, 0)
    seq = jnp.where(p == 0, lc, l)
    pos = row & (seq - 1)
    prev = jnp.where(pos == 0, 0.0, pltpu.roll(x, 1, 0))
    nxt = jnp.where(pos == seq - 1, 0.0, pltpu.roll(x, n - 1, 0))
    w = w_ref[...]
    y = _silu(prev * w[0:1] + x * w[1:2] + nxt * w[2:3])
    inv = lax.rsqrt(jnp.sum(y * y, axis=-1, keepdims=True) + EPS)
    fac = jnp.where(j < HA, inv * DKA ** -0.5, jnp.where(j < 2 * HA, inv, 1.0))
    o_ref[...] = y * fac


def _short_conv(proj, conv_w, lc, l):
    t = proj.shape[0]
    ncol = conv_w.shape[1]
    return pl.pallas_call(
        functools.partial(_conv_kernel, lc=lc, l=l),
        grid=(t // l, ncol // LANES),
        in_specs=[pl.BlockSpec((l, LANES), lambda p, j: (p, j)),
                  pl.BlockSpec((conv_w.shape[0], LANES), lambda p, j: (0, j))],
        out_specs=pl.BlockSpec((l, LANES), lambda p, j: (p, j)),
        out_shape=jax.ShapeDtypeStruct((t, ncol), F32),
        compiler_params=_cparams(2),
        name="conv_silu_l2",
    )(proj, conv_w)


def _row_block(b, g, n_b, n_lat):
    return jnp.where(g == 0, b, n_b + b * n_lat + g - 1)


def _rev_group(g, n_g):
    return jnp.where(g == 0, 0, n_g - g)


def _tri_inverse(lm, ii, jj, upper):
    c = lm.shape[0]
    eye = (ii == jj).astype(F32)
    t = None
    s, ls = 1, 0
    while s < c:
        same = (ii >> (ls + 1)) == (jj >> (ls + 1))
        io = (ii >> ls) & 1
        jo = (jj >> ls) & 1
        sel = (io == 0) & (jo == 1) if upper else (io == 1) & (jo == 0)
        cm = jnp.where(same & sel, lm, 0.0)
        if t is None:
            t = eye - cm
        else:
            t = t - _dot3(_dot3(t, cm), t)
        s, ls = s * 2, ls + 1
    return t


def _dn1_kernel(alog_ref, dtb_ref, q_ref, k_ref, v_ref, ba_ref, gt_ref,
                u_ref, w_ref, kg_ref, qg_ref, qk_ref, gl_ref, *, n_chunks):
    h = pl.program_id(2)
    c_ = CHUNK
    ii = lax.broadcasted_iota(jnp.int32, (c_, c_), 0)
    jj = lax.broadcasted_iota(jnp.int32, (c_, c_), 1)
    for c in range(n_chunks):
        r = slice(c * c_, (c + 1) * c_)
        q = q_ref[r, :]
        k = k_ref[r, :]
        v = v_ref[r, :]
        q16 = q.astype(BF16)
        k16 = k.astype(BF16)
        kk = _dot_nt(k16, k16)
        qk = _dot_nt(q16, k16)
        for d in range(2):
            incl = (jj <= ii) if d == 0 else (jj >= ii)
            strict = (jj < ii) if d == 0 else (jj > ii)
            incl_t = (ii <= jj) if d == 0 else (ii >= jj)
            a_scale = -jnp.exp(jnp.full((1, 1), alog_ref[d * HA + h], F32))
            dtb = dtb_ref[d * HA + h]
            beta = _sigmoid(ba_ref[0, r, d:d + 1])
            g_col = a_scale * _softplus(ba_ref[0, r, 2 + d:3 + d] + dtb)
            g_row = a_scale * _softplus(gt_ref[0, 2 + d:3 + d, r] + dtb)
            gc_col = jnp.sum(jnp.where(incl, g_row, 0.0), axis=1, keepdims=True)
            gc_row = jnp.sum(jnp.where(incl_t, g_col, 0.0), axis=0, keepdims=True)
            total = jnp.sum(g_col, axis=0, keepdims=True)
            gam = jnp.where(incl, jnp.exp(gc_col - gc_row), 0.0)
            lm = jnp.where(strict, beta * kk * gam, 0.0)
            tinv = _tri_inverse(lm, ii, jj, upper=(d == 1))
            eg = jnp.exp(gc_col)
            rhs = jnp.concatenate([v * beta, k * (beta * eg)], axis=1).astype(BF16)
            uw = _dot(tinv.astype(BF16), rhs)
            u_ref[d, r, :] = uw[:, :DVA]
            w_ref[d, r, :] = uw[:, DVA:].astype(BF16)
            kg_ref[d, r, :] = (k * jnp.exp(total - gc_col)).astype(BF16)
            qg_ref[d, r, :] = (q * eg).astype(BF16)
            qk_ref[d, 0, r, :] = (qk * gam).astype(BF16)
            gl_ref[0, 0, 0, c * 2 + d:c * 2 + d + 1, :] = jnp.broadcast_to(jnp.exp(total), (1, LANES))


def _dn_factors(qkvc, ba_h, gt_h, a_log, dt_bias, n_b, lc, l):
    t = qkvc.shape[0]
    n_lat = l // lc
    n_g = n_lat + 1
    n_chunks = lc // CHUNK
    rb = lambda b, g, h: _row_block(b, g, n_b, n_lat)
    tok = lambda dt: jax.ShapeDtypeStruct((2, t, HA * DVA), dt)
    smem = pl.BlockSpec(memory_space=pltpu.SMEM)
    return pl.pallas_call(
        functools.partial(_dn1_kernel, n_chunks=n_chunks),
        grid=(n_b, n_g, HA),
        in_specs=[smem, smem,
                  pl.BlockSpec((lc, LANES), lambda b, g, h: (rb(b, g, h), h)),
                  pl.BlockSpec((lc, LANES), lambda b, g, h: (rb(b, g, h), HA + h)),
                  pl.BlockSpec((lc, LANES), lambda b, g, h: (rb(b, g, h), 2 * HA + h)),
                  pl.BlockSpec((1, lc, 4), lambda b, g, h: (h, rb(b, g, h), 0)),
                  pl.BlockSpec((1, 4, lc), lambda b, g, h: (h, 0, rb(b, g, h)))],
        out_specs=[pl.BlockSpec((2, lc, LANES), lambda b, g, h: (0, rb(b, g, h), h))] * 4
        + [pl.BlockSpec((2, 1, lc, CHUNK), lambda b, g, h: (0, h, rb(b, g, h), 0)),
           pl.BlockSpec((1, 1, 1, 2 * n_chunks, LANES), lambda b, g, h: (b, g, h, 0, 0))],
        out_shape=[tok(F32), tok(BF16), tok(BF16), tok(BF16),
                   jax.ShapeDtypeStruct((2, HA, t, CHUNK), BF16),
                   jax.ShapeDtypeStruct((n_b, n_g, HA, 2 * n_chunks, LANES), F32)],
        compiler_params=_cparams(3),
        name="deltanet_chunk_factors",
    )(a_log.reshape(-1), dt_bias.reshape(-1), qkvc, qkvc, qkvc, ba_h, gt_h)


def _dn2_kernel(u0, w0, kg0, qg0, qk0, gl0, u1, w1, kg1, qg1, qk1, gl1, o0_ref, o1_ref, s_scr,
                *, n_chunks):
    @pl.when(pl.program_id(1) == 0)
    def _():
        s_scr[...] = jnp.zeros_like(s_scr)

    c_ = CHUNK
    for h in range(HA):
        hs = slice(h * DVA, (h + 1) * DVA)
        for d, (u, w, kg, qg, qk, gl, o_ref) in enumerate(
                ((u0, w0, kg0, qg0, qk0, gl0, o0_ref), (u1, w1, kg1, qg1, qk1, gl1, o1_ref))):
            s = s_scr[d * HA + h]
            order = range(n_chunks) if d == 0 else range(n_chunks - 1, -1, -1)
            for c in order:
                r = slice(c * c_, (c + 1) * c_)
                s16 = s.astype(BF16)
                vn = u[0, r, hs] - _dot(w[0, r, hs], s16)
                vn16 = vn.astype(BF16)
                o_ref[r, hs] = _dot(qg[0, r, hs], s16) + _dot(qk[0, h, r, :], vn16)
                s = gl[0, 0, h, c * 2 + d:c * 2 + d + 1, :] * s + _dot_tn(kg[0, r, hs], vn16)
            s_scr[d * HA + h] = s


def _dn_scan(factors, n_b, lc, l):
    u, w, kg, qg, qk, gl = factors
    t = u.shape[1]
    n_lat = l // lc
    n_g = n_lat + 1
    n_chunks = lc // CHUNK
    width = HA * DVA
    rb0 = lambda b, g: _row_block(b, g, n_b, n_lat)
    rb1 = lambda b, g: _row_block(b, _rev_group(g, n_g), n_b, n_lat)

    def specs(d, rb, gsel):
        tok = pl.BlockSpec((1, lc, width), lambda b, g: (d, rb(b, g), 0))
        return [tok, tok, tok, tok,
                pl.BlockSpec((1, HA, lc, CHUNK), lambda b, g: (d, 0, rb(b, g), 0)),
                pl.BlockSpec((1, 1, HA, 2 * n_chunks, LANES), lambda b, g: (b, gsel(g), 0, 0, 0))]

    return pl.pallas_call(
        functools.partial(_dn2_kernel, n_chunks=n_chunks),
        grid=(n_b, n_g),
        in_specs=specs(0, rb0, lambda g: g) + specs(1, rb1, lambda g: _rev_group(g, n_g)),
        out_specs=[pl.BlockSpec((lc, width), lambda b, g: (rb0(b, g), 0)),
                   pl.BlockSpec((lc, width), lambda b, g: (rb1(b, g), 0))],
        out_shape=[jax.ShapeDtypeStruct((t, width), F32)] * 2,
        scratch_shapes=[pltpu.VMEM((2 * HA, DKA, DVA), F32)],
        compiler_params=_cparams(2),
        name="deltanet_scan",
    )(u, w, kg, qg, qk, gl, u, w, kg, qg, qk, gl)


def _gla_kernel(q0, k0, v0, sm0, q1, k1, v1, sm1, wg_ref, bg_ref, o0_ref, o1_ref, s_scr, *, n_chunks):
    @pl.when(pl.program_id(1) == 0)
    def _():
        s_scr[...] = jnp.zeros_like(s_scr)

    c_ = CHUNK
    kw = HB * DKB
    vw = HB * DVB
    ii = lax.broadcasted_iota(jnp.int32, (c_, c_), 0)
    jj = lax.broadcasted_iota(jnp.int32, (c_, c_), 1)
    klane = lax.broadcasted_iota(jnp.int32, (1, kw), 1)
    srow = lax.broadcasted_iota(jnp.int32, (kw, vw), 0) >> (DKB.bit_length() - 1)
    scol = lax.broadcasted_iota(jnp.int32, (kw, vw), 1) >> (DVB.bit_length() - 1)
    same_head = srow == scol
    ones16 = jnp.ones((c_, LANES), BF16)
    for d, (q_ref, k_ref, v_ref, sm_ref, o_ref) in enumerate(
            ((q0, k0, v0, sm0, o0_ref), (q1, k1, v1, sm1, o1_ref))):
        incl = (jj <= ii) if d == 0 else (jj >= ii)
        tri16 = incl.astype(F32).astype(BF16)
        s = s_scr[d]
        order = range(n_chunks) if d == 0 else range(n_chunks - 1, -1, -1)
        for c in order:
            r = slice(c * c_, (c + 1) * c_)
            lo = 2 * HA + 2 * HA + d * GATE_RANK
            glr = sm_ref[r, lo:lo + GATE_RANK]
            pre = jnp.dot(glr, wg_ref[d], precision=HIGHEST, preferred_element_type=F32) + bg_ref[d]
            glog = (jnp.minimum(pre, 0.0) - jnp.log(1.0 + jnp.exp(-jnp.abs(pre)))) * (1.0 / GATE_NORM)
            ghi, glo = _split(glog)
            gc = _dot(tri16, ghi) + _dot(tri16, glo)
            last = c_ - 1 if d == 0 else 0
            total = gc[last:last + 1, :]
            mid = gc[c_ // 2 - 1:c_ // 2, :]
            tot_col = _dot_tn(ghi, ones16) + _dot_tn(glo, ones16)
            q = q_ref[r, :] * DKB ** -0.5
            k = k_ref[r, :]
            v16 = v_ref[r, :].astype(BF16)
            qt = q * jnp.exp(gc - mid)
            kt16 = (k * jnp.exp(mid - gc)).astype(BF16)
            qg16 = (q * jnp.exp(gc)).astype(BF16)
            kg16 = (k * jnp.exp(total - gc)).astype(BF16)
            s16 = s.astype(BF16)
            o_inter = _dot(qg16, s16)
            for h in range(HB):
                in_head = (klane >= h * DKB) & (klane < (h + 1) * DKB)
                a = _dot_nt(jnp.where(in_head, qt, 0.0).astype(BF16), kt16)
                a16 = jnp.where(incl, a, 0.0).astype(BF16)
                vs = slice(h * DVB, (h + 1) * DVB)
                o_ref[r, vs] = o_inter[:, vs] + _dot(a16, v16[:, vs])
            decay = jnp.exp(tot_col)
            s = jnp.concatenate([decay] * (vw // LANES), axis=1) * s + jnp.where(
                same_head, _dot_tn(kg16, v16), 0.0)
        s_scr[d] = s


def _gla(proj, w_gate2, b_gate, n_b, lc, l):
    t = proj.shape[0]
    n_lat = l // lc
    n_g = n_lat + 1
    n_chunks = lc // CHUNK
    kw, vw = HB * DKB, HB * DVB
    rb0 = lambda b, g: _row_block(b, g, n_b, n_lat)
    rb1 = lambda b, g: _row_block(b, _rev_group(g, n_g), n_b, n_lat)

    def specs(rb):
        return [pl.BlockSpec((lc, kw), lambda b, g: (rb(b, g), COL_B // kw)),
                pl.BlockSpec((lc, kw), lambda b, g: (rb(b, g), COL_B // kw + 1)),
                pl.BlockSpec((lc, vw), lambda b, g: (rb(b, g), (COL_B + 2 * kw) // vw)),
                pl.BlockSpec((lc, LANES), lambda b, g: (rb(b, g), COL_S // LANES))]

    return pl.pallas_call(
        functools.partial(_gla_kernel, n_chunks=n_chunks),
        grid=(n_b, n_g),
        in_specs=specs(rb0) + specs(rb1)
        + [pl.BlockSpec((2, GATE_RANK, kw), lambda b, g: (0, 0, 0)),
           pl.BlockSpec((2, 1, kw), lambda b, g: (0, 0, 0))],
        out_specs=[pl.BlockSpec((lc, vw), lambda b, g: (rb0(b, g), 0)),
                   pl.BlockSpec((lc, vw), lambda b, g: (rb1(b, g), 0))],
        out_shape=[jax.ShapeDtypeStruct((t, vw), F32)] * 2,
        scratch_shapes=[pltpu.VMEM((2, kw, vw), F32)],
        compiler_params=_cparams(2),
        name="gla_bidir",
    )(proj, proj, proj, proj, proj, proj, proj, proj, w_gate2, b_gate.reshape(2, 1, kw))


def _attn_kernel(lam_ref, gn_ref, q_ref, kc_ref, kl_ref, vc_ref, vl_ref, cq_ref, sq_ref, ck_ref, sk_ref,
                 o_ref, k_scr, v_scr, *, lc, lam_init, ctx_out):
    i = pl.program_id(2)
    lane = lax.broadcasted_iota(jnp.int32, (1, LANES), 1)
    first_half = (lane & 31) < 16

    def rope(x, cos, sin):
        swapped = jnp.where(first_half, pltpu.roll(x, LANES - 16, 1), pltpu.roll(x, 16, 1))
        return x * cos + swapped * sin

    @pl.when(i == 0)
    def _():
        k_scr[0:lc, :] = kc_ref[...].astype(BF16)
        k_scr[lc:, :] = rope(kl_ref[...], ck_ref[...], sk_ref[...]).astype(BF16)
        v_scr[0:lc, :] = vc_ref[...].astype(BF16)
        v_scr[lc:, :] = vl_ref[...].astype(BF16)

    lam = lam_ref[...]
    lam_f = (jnp.exp(jnp.sum(lam[0:1] * lam[1:2], axis=1, keepdims=True))
             - jnp.exp(jnp.sum(lam[2:3] * lam[3:4], axis=1, keepdims=True)) + lam_init)

    def attend(q, keys, vals):
        q = q * DKC ** -0.5
        probs = []
        for m in range(2):
            qm = jnp.where((lane >= m * DKC) & (lane < (m + 1) * DKC), q, 0.0).astype(BF16)
            s = _dot_nt(qm, keys)
            e = jnp.exp(s - jnp.max(s, axis=-1, keepdims=True))
            probs.append(e * (1.0 / jnp.sum(e, axis=-1, keepdims=True)))
        p = (probs[0] - lam_f * probs[1]).astype(BF16)
        o = _dot(p, vals)
        o_ref[...] = _rms_rows(o, gn_ref[...]) * (1.0 - lam_init)

    @pl.when(i == 0)
    def _():
        if ctx_out:
            attend(q_ref[...], k_scr[0:lc, :], v_scr[0:lc, :])
        else:
            o_ref[...] = jnp.zeros_like(o_ref)

    @pl.when(i > 0)
    def _():
        attend(rope(q_ref[...], cq_ref[...], sq_ref[...]), k_scr[...], v_scr[...])


def _attention(proj, lam, gn, rope_cos, rope_sin, n_b, lc, l, lam_init, ctx_out):
    t = proj.shape[0]
    n_lat = l // lc
    rb = lambda b, h, i: _row_block(b, i, n_b, n_lat)
    cq, ck, cv = COL_C // LANES, COL_C // LANES + HC, COL_C // LANES + 2 * HC
    tab_q = pl.BlockSpec((lc, LANES), lambda b, h, i: (jnp.maximum(i - 1, 0), 0))
    tab_k = pl.BlockSpec((l, LANES), lambda b, h, i: (0, 0))
    return pl.pallas_call(
        functools.partial(_attn_kernel, lc=lc, lam_init=lam_init, ctx_out=ctx_out),
        grid=(n_b, HC, n_lat + 1),
        in_specs=[pl.BlockSpec((4, DKC), lambda b, h, i: (0, 0)),
                  pl.BlockSpec((1, DVC), lambda b, h, i: (0, 0)),
                  pl.BlockSpec((lc, LANES), lambda b, h, i: (rb(b, h, i), cq + h)),
                  pl.BlockSpec((lc, LANES), lambda b, h, i: (b, ck + h)),
                  pl.BlockSpec((l, LANES), lambda b, h, i: (b + 1, ck + h)),
                  pl.BlockSpec((lc, LANES), lambda b, h, i: (b, cv + h)),
                  pl.BlockSpec((l, LANES), lambda b, h, i: (b + 1, cv + h)),
                  tab_q, tab_q, tab_k, tab_k],
        out_specs=pl.BlockSpec((lc, LANES), lambda b, h, i: (rb(b, h, i), h)),
        out_shape=jax.ShapeDtypeStruct((t, HC * DVC), F32),
        scratch_shapes=[pltpu.VMEM((lc + l, LANES), BF16), pltpu.VMEM((lc + l, LANES), BF16)],
        compiler_params=_cparams(3),
        name="diff_attention",
    )(lam, gn.reshape(1, DVC), proj, proj, proj, proj, proj, rope_cos, rope_sin, rope_cos, rope_sin)


def _rope_tables(n_tokens):
    rows = n_tokens // GRID_W
    row = jnp.repeat(jnp.arange(rows, dtype=F32), GRID_W)
    col = jnp.tile(jnp.arange(GRID_W, dtype=F32), rows)
    n_freq = DKC // 4
    inv_freq = ROPE_THETA ** (-jnp.arange(n_freq, dtype=F32) / n_freq)
    ang_r = row[:, None] * inv_freq
    ang_c = col[:, None] * inv_freq
    cos = jnp.concatenate([jnp.cos(ang_r)] * 2 + [jnp.cos(ang_c)] * 2, axis=1)
    sin = jnp.concatenate([-jnp.sin(ang_r), jnp.sin(ang_r), -jnp.sin(ang_c), jnp.sin(ang_c)], axis=1)
    return jnp.concatenate([cos, cos], axis=1), jnp.concatenate([sin, sin], axis=1)


def _head_rms(o, gain, n_heads, width):
    parts = []
    for h in range(n_heads):
        parts.append(_rms_rows(o[:, h * width:(h + 1) * width], gain))
    return jnp.concatenate(parts, axis=1)


def _merge_kernel(oa0, oa1, z_ref, ob0, ob1, r_ref, od_ref, ga_ref, gb_ref, gd_ref, x_ref, mod_ref,
                  gna_ref, gnb_ref, woa_ref, wob_ref, woc_ref, wout_ref, o_ref):
    ya = _head_rms(oa0[...] + oa1[...], gna_ref[...], HA, DVA) * _silu(z_ref[...])
    yb = _head_rms(ob0[...] + ob1[...], gnb_ref[...], HB, DVB) * _silu(r_ref[...])
    acc = _sigmoid(ga_ref[...]) * _dot(ya.astype(BF16), woa_ref[...])
    acc = acc + _sigmoid(gb_ref[...]) * _dot(yb.astype(BF16), wob_ref[...])
    acc = acc + _sigmoid(gd_ref[...]) * _dot(od_ref[...].astype(BF16), woc_ref[...])
    y = _dot(acc.astype(BF16), wout_ref[...])
    o_ref[...] = x_ref[...] + mod_ref[0, 2:3, :] * y


def _merge(oa0, oa1, ob0, ob1, od, proj, x, mod, gn_a, gn_b, woa, wob, woc, wout, rows_per_mod, row_off, tm):
    t, d = x.shape
    off = row_off // tm
    n_tiles = (t - row_off) // tm
    per = rows_per_mod // tm
    w5 = HA * DVA
    row = lambda i: (i + off, 0)
    col = lambda c: (lambda i: (i + off, c))
    return pl.pallas_call(
        _merge_kernel,
        grid=(n_tiles,),
        in_specs=[pl.BlockSpec((tm, w5), row),
                  pl.BlockSpec((tm, w5), row),
                  pl.BlockSpec((tm, w5), col((COL_A + 3 * w5) // w5)),
                  pl.BlockSpec((tm, w5), row),
                  pl.BlockSpec((tm, w5), row),
                  pl.BlockSpec((tm, w5), col((COL_B + 2 * HB * DKB + w5) // w5)),
                  pl.BlockSpec((tm, w5), row),
                  pl.BlockSpec((tm, d), col(COL_G // d)),
                  pl.BlockSpec((tm, d), col(COL_G // d + 1)),
                  pl.BlockSpec((tm, d), col(COL_G // d + 2)),
                  pl.BlockSpec((tm, d), row),
                  pl.BlockSpec((1, 6, d), lambda i: ((i + off) // per, 0, 0)),
                  pl.BlockSpec((1, DVA), lambda i: (0, 0)),
                  pl.BlockSpec((1, DVB), lambda i: (0, 0)),
                  pl.BlockSpec((w5, d), lambda i: (0, 0)),
                  pl.BlockSpec((w5, d), lambda i: (0, 0)),
                  pl.BlockSpec((w5, d), lambda i: (0, 0)),
                  pl.BlockSpec((d, d), lambda i: (0, 0))],
        out_specs=pl.BlockSpec((tm, d), lambda i: (i, 0)),
        out_shape=jax.ShapeDtypeStruct((t - row_off, d), F32),
        compiler_params=_cparams(1),
        name="merge_outproj",
    )(oa0, oa1, proj, ob0, ob1, proj, od, proj, proj, proj, x, mod,
      gn_a.reshape(1, DVA), gn_b.reshape(1, DVB), woa, wob, woc, wout)


def _ffn_kernel(x_ref, mod_ref, g_ref, rw_ref, fg_ref, w1_ref, w3_ref, w2_ref, o_ref, h_scr, gate_scr, acc_scr,
                *, routed, final):
    e = pl.program_id(1)
    f = pl.program_id(2)
    first = (e == 0) & (f == 0)
    last = (e == pl.num_programs(1) - 1) & (f == pl.num_programs(2) - 1)
    lane = lax.broadcasted_iota(jnp.int32, (1, LANES), 1)

    @pl.when(first)
    def _():
        y = _rms_rows(x_ref[...], g_ref[...])
        hmod = y * (1.0 + mod_ref[0, 4:5, :]) + mod_ref[0, 3:4, :]
        h_scr[...] = hmod.astype(BF16)
        acc_scr[...] = jnp.zeros_like(acc_scr)
        if routed:
            logits = jnp.dot(hmod, rw_ref[...], precision=HIGHEST, preferred_element_type=F32)
            logits = jnp.where(lane < N_EXPERTS, logits, -jnp.inf)
            lanef = lane.astype(F32)
            m1 = jnp.max(logits, axis=1, keepdims=True)
            i1 = jnp.min(jnp.where(logits == m1, lanef, float(LANES)), axis=1, keepdims=True)
            hit1 = lanef == i1
            rest = jnp.where(hit1, -jnp.inf, logits)
            m2 = jnp.max(rest, axis=1, keepdims=True)
            i2 = jnp.min(jnp.where(rest == m2, lanef, float(LANES)), axis=1, keepdims=True)
            hit2 = lanef == i2
            ex = jnp.exp(m2 - m1)
            inv = 1.0 / (1.0 + ex)
            gate_scr[...] = jnp.where(hit1, inv, 0.0) + jnp.where(hit2, ex * inv, 0.0)

    h = h_scr[...]
    t = _silu(_dot(h, w1_ref[0])) * _dot(h, w3_ref[0])
    if routed:
        t = t * jnp.sum(jnp.where(lane == e, gate_scr[...], 0.0), axis=1, keepdims=True)
    acc_scr[...] += _dot(t.astype(BF16), w2_ref[0])

    @pl.when(last)
    def _():
        out = x_ref[...] + mod_ref[0, 5:6, :] * acc_scr[...]
        if final:
            out = _rms_rows(out, fg_ref[...])
        o_ref[...] = out


def _ffn(x, mod, g, router_w, final_g, w1, w3, w2, rows_per_mod, mod_off, tm, tf, routed, final):
    t, d = x.shape
    n_e, _, dff = w1.shape
    per = rows_per_mod // tm
    return pl.pallas_call(
        functools.partial(_ffn_kernel, routed=routed, final=final),
        grid=(t // tm, n_e, dff // tf),
        in_specs=[pl.BlockSpec((tm, d), lambda i, e, f: (i, 0)),
                  pl.BlockSpec((1, 6, d), lambda i, e, f: (i // per + mod_off, 0, 0)),
                  pl.BlockSpec((1, d), lambda i, e, f: (0, 0)),
                  pl.BlockSpec((d, LANES), lambda i, e, f: (0, 0)),
                  pl.BlockSpec((1, d), lambda i, e, f: (0, 0)),
                  pl.BlockSpec((1, d, tf), lambda i, e, f: (e, 0, f)),
                  pl.BlockSpec((1, d, tf), lambda i, e, f: (e, 0, f)),
                  pl.BlockSpec((1, tf, d), lambda i, e, f: (e, f, 0))],
        out_specs=pl.BlockSpec((tm, d), lambda i, e, f: (i, 0)),
        out_shape=jax.ShapeDtypeStruct((t, d), F32),
        scratch_shapes=[pltpu.VMEM((tm, d), BF16), pltpu.VMEM((tm, LANES), F32), pltpu.VMEM((tm, d), F32)],
        compiler_params=_cparams(3),
        name="routed_ffn" if routed else "dense_ffn",
    )(x, mod, g.reshape(1, d), router_w, final_g.reshape(1, d), w1, w3, w2)


def _reorder_w_in(w):
    d = w.shape[0]
    a_end = 4 * HA * DKA
    b_start = a_end + 4 * HA
    b_end = b_start + 2 * HB * DKB + 2 * HB * DVB
    c_start = b_end + 2 * GATE_RANK
    pad = N_COLS - COL_S - 4 * HA - 2 * GATE_RANK
    return jnp.concatenate([w[:, :a_end], w[:, b_start:b_end], w[:, c_start:], w[:, a_end:b_start],
                            w[:, b_end:c_start], jnp.zeros((d, pad), w.dtype)], axis=1).astype(BF16)


def kernel(x, c, ctx, c_ctx, w_mod, b_mod, norm1_g, norm2_g, w_in, conv_a, a_log, dt_bias, gn_a, w_gate2, b_gate, gn_b, lam_c, gn_c, w_o_a, w_o_b, w_o_c, w_out, ffn_w1, ffn_w3, ffn_w2, router_w, moe_w1, moe_w3, moe_w2, final_g):
    n_b, l, d = x.shape
    lc = ctx.shape[1]
    depth = w_mod.shape[0]
    assert n_b * lc == l and lc % CHUNK == 0 and l & (l - 1) == 0 and lc & (lc - 1) == 0
    t_ctx = n_b * lc
    tm = min(1024, l)
    tm_small = min(512, l)

    n_mod = 16
    cvec = jnp.concatenate([c_ctx[None, :], c, jnp.zeros((n_mod - 1 - n_b, d), F32)], axis=0)
    mod_all = _modulation(cvec, w_mod, b_mod).reshape(depth, n_mod, 6, d)

    rope_cos, rope_sin = _rope_tables(l)
    xs = jnp.concatenate([ctx.reshape(t_ctx, d), x.reshape(n_b * l, d)], axis=0)

    for layer in range(depth):
        ctx_out = layer < depth - 1
        lam_init = 0.8 - 0.6 * math.exp(-0.3 * layer)
        mod = mod_all[layer]
        proj = _in_projection(xs, mod, norm1_g[layer], _reorder_w_in(w_in[layer]), l, tm, 1664)

        qkvc = _short_conv(proj, conv_a[layer], lc, l)
        small = proj[:, COL_S:COL_S + 4 * HA].reshape(-1, 4, HA)
        ba_h = small.transpose(2, 0, 1)
        gt_h = small.transpose(2, 1, 0)
        oa0, oa1 = _dn_scan(_dn_factors(qkvc, ba_h, gt_h, a_log[layer], dt_bias[layer], n_b, lc, l), n_b, lc, l)
        ob0, ob1 = _gla(proj, w_gate2[layer], b_gate[layer], n_b, lc, l)
        od = _attention(proj, lam_c[layer], gn_c[layer], rope_cos, rope_sin, n_b, lc, l, lam_init, ctx_out)

        row_off = 0 if ctx_out else t_ctx
        xs_new = _merge(oa0, oa1, ob0, ob1, od, proj, xs, mod, gn_a[layer], gn_b[layer],
                        w_o_a[layer].astype(BF16), w_o_b[layer].astype(BF16), w_o_c[layer].astype(BF16),
                        w_out[layer].astype(BF16), l, row_off, tm_small)

        i = layer // 2
        final = layer == depth - 1
        mod_off = 0 if ctx_out else 1
        if layer % 2 == 0:
            rw = jnp.zeros((d, LANES), F32)
            xs = _ffn(xs_new, mod, norm2_g[layer], rw, final_g, ffn_w1[i:i + 1].astype(BF16),
                      ffn_w3[i:i + 1].astype(BF16), ffn_w2[i:i + 1].astype(BF16), l, mod_off, tm_small,
                      ffn_w1.shape[2] // 2, False, final)
        else:
            rw = jnp.pad(router_w[i], ((0, 0), (0, LANES - N_EXPERTS)))
            xs = _ffn(xs_new, mod, norm2_g[layer], rw, final_g, moe_w1[i].astype(BF16),
                      moe_w3[i].astype(BF16), moe_w2[i].astype(BF16), l, mod_off, tm_small,
                      moe_w1.shape[3] // 2, True, final)

    return xs.reshape(n_b, l, d)
```

```python
import functools
import math

import jax
import jax.numpy as jnp
from jax import lax
from jax.experimental import pallas as pl
from jax.experimental.pallas import tpu as pltpu

F32 = jnp.float32
BF16 = jnp.bfloat16
HIGHEST = lax.Precision.HIGHEST

EPS = 1e-6
HA, DKA, DVA = 4, 128, 128
HB, DKB, DVB = 4, 64, 128
GATE_RANK, GATE_NORM = 16, 16.0
HC, DKC, DVC = 4, 64, 128
GRID_W, ROPE_THETA = 64, 10000.0
N_EXPERTS, TOP_K = 8, 2
CHUNK = 64
LANES = 128

COL_A = 0
COL_B = 2048
COL_C = 3584
COL_G = 5120
COL_S = 8192
N_COLS = 8320

VMEM_LIMIT = 56 * 1024 * 1024


def _cparams(n_axes):
    return pltpu.CompilerParams(dimension_semantics=("arbitrary",) * n_axes,
                                vmem_limit_bytes=VMEM_LIMIT)


def _sigmoid(x):
    return 1.0 / (1.0 + jnp.exp(-x))


def _silu(x):
    return x * _sigmoid(x)


def _softplus(x):
    return jnp.maximum(x, 0.0) + jnp.log(1.0 + jnp.exp(-jnp.abs(x)))


def _dot(a, b):
    return jnp.dot(a, b, preferred_element_type=F32)


def _dot_nt(a, b):
    return lax.dot_general(a, b, (((1,), (1,)), ((), ())), preferred_element_type=F32)


def _dot_tn(a, b):
    return lax.dot_general(a, b, (((0,), (0,)), ((), ())), preferred_element_type=F32)


def _split(a):
    hi = a.astype(BF16)
    lo = (a - hi.astype(F32)).astype(BF16)
    return hi, lo


def _dot3(a, b):
    ah, al = _split(a)
    bh, bl = _split(b)
    return _dot(ah, bh) + (_dot(ah, bl) + _dot(al, bh))


def _rms_rows(x, gain):
    ms = jnp.mean(x * x, axis=-1, keepdims=True)
    return x * lax.rsqrt(ms + EPS) * gain


def _mod_kernel(c_ref, w_ref, b_ref, o_ref):
    s = _silu(c_ref[...])
    o_ref[0] = jnp.dot(s, w_ref[0], precision=HIGHEST, preferred_element_type=F32) + b_ref[0]


def _modulation(cvec, w_mod, b_mod):
    depth, d, d6 = w_mod.shape
    n = cvec.shape[0]
    return pl.pallas_call(
        _mod_kernel,
        grid=(depth, d6 // d),
        in_specs=[pl.BlockSpec((n, d), lambda l, j: (0, 0)),
                  pl.BlockSpec((1, d, d), lambda l, j: (l, 0, j)),
                  pl.BlockSpec((1, 1, d), lambda l, j: (l, 0, j))],
        out_specs=pl.BlockSpec((1, n, d), lambda l, j: (l, 0, j)),
        out_shape=jax.ShapeDtypeStruct((depth, n, d6), F32),
        compiler_params=_cparams(2),
        name="adaln_mod",
    )(cvec, w_mod, b_mod.reshape(depth, 1, d6))


def _inproj_kernel(x_ref, mod_ref, g_ref, w_ref, o_ref, h_scr):
    @pl.when(pl.program_id(1) == 0)
    def _():
        y = _rms_rows(x_ref[...], g_ref[...])
        h_scr[...] = (y * (1.0 + mod_ref[0, 1:2, :]) + mod_ref[0, 0:1, :]).astype(BF16)

    o_ref[...] = _dot(h_scr[...], w_ref[...])


def _in_projection(x, mod, g, w, rows_per_mod, tm, tn):
    t, d = x.shape
    nc = w.shape[1]
    per = rows_per_mod // tm
    return pl.pallas_call(
        _inproj_kernel,
        grid=(t // tm, nc // tn),
        in_specs=[pl.BlockSpec((tm, d), lambda i, j: (i, 0)),
                  pl.BlockSpec((1, 6, d), lambda i, j: (i // per, 0, 0)),
                  pl.BlockSpec((1, d), lambda i, j: (0, 0)),
                  pl.BlockSpec((d, tn), lambda i, j: (0, j))],
        out_specs=pl.BlockSpec((tm, tn), lambda i, j: (i, j)),
        out_shape=jax.ShapeDtypeStruct((t, nc), F32),
        scratch_shapes=[pltpu.VMEM((tm, d), BF16)],
        compiler_params=_cparams(2),
        name="norm_mod_inproj",
    )(x, mod, g.reshape(1, d), w)


def _conv_kernel(x_ref, w_ref, o_ref, *, lc, l):
    p = pl.program_id(0)
    j = pl.program_id(1)
    x = x_ref[...]
    n = x.shape[0]
    row = lax.broadcasted_iota(jnp.int32, x.shape, 0)
    seq = jnp.where(p == 0, lc, l)
    pos = row & (seq - 1)
    prev = jnp.where(pos == 0, 0.0, pltpu.roll(x, 1, 0))
    nxt = jnp.where(pos == seq - 1, 0.0, pltpu.roll(x, n - 1, 0))
    w = w_ref[...]
    y = _silu(prev * w[0:1] + x * w[1:2] + nxt * w[2:3])
    inv = lax.rsqrt(jnp.sum(y * y, axis=-1, keepdims=True) + EPS)
    fac = jnp.where(j < HA, inv * DKA ** -0.5, jnp.where(j < 2 * HA, inv, 1.0))
    o_ref[...] = y * fac


def _short_conv(proj, conv_w, lc, l):
    t = proj.shape[0]
    ncol = conv_w.shape[1]
    return pl.pallas_call(
        functools.partial(_conv_kernel, lc=lc, l=l),
        grid=(t // l, ncol // LANES),
        in_specs=[pl.BlockSpec((l, LANES), lambda p, j: (p, j)),
                  pl.BlockSpec((conv_w.shape[0], LANES), lambda p, j: (0, j))],
        out_specs=pl.BlockSpec((l, LANES), lambda p, j: (p, j)),
        out_shape=jax.ShapeDtypeStruct((t, ncol), F32),
        compiler_params=_cparams(2),
        name="conv_silu_l2",
    )(proj, conv_w)


def _row_block(b, g, n_b, n_lat):
    return jnp.where(g == 0, b, n_b + b * n_lat + g - 1)


def _rev_group(g, n_g):
    return jnp.where(g == 0, 0, n_g - g)


def _tri_inverse_many(lms, ii, jj, uppers):
    c = lms[0].shape[0]
    eye = (ii == jj).astype(F32)
    n = len(lms)
    ts = None
    s, ls = 1, 0
    while s < c:
        same = (ii >> (ls + 1)) == (jj >> (ls + 1))
        io = (ii >> ls) & 1
        jo = (jj >> ls) & 1
        lower = same & (io == 1) & (jo == 0)
        upper = same & (io == 0) & (jo == 1)
        cms = [jnp.where(upper if uppers[i] else lower, lms[i], 0.0) for i in range(n)]
        if ts is None:
            ts = [eye - cm for cm in cms]
        else:
            t16 = [t.astype(BF16) for t in ts]
            ps = [_dot(t16[i], cms[i].astype(BF16)).astype(BF16) for i in range(n)]
            ts = [ts[i] - _dot(ps[i], t16[i]) for i in range(n)]
        s, ls = s * 2, ls + 1
    a_parts = [_split(eye + lm) for lm in lms]
    t_parts = [_split(t) for t in ts]
    res = [eye - (_dot(a_parts[i][0], t_parts[i][0])
                  + (_dot(a_parts[i][0], t_parts[i][1]) + _dot(a_parts[i][1], t_parts[i][0])))
           for i in range(n)]
    return [ts[i] + _dot(t_parts[i][0], res[i].astype(BF16)) for i in range(n)]


def _dn1_kernel(alog_ref, dtb_ref, q_ref, k_ref, v_ref, ba_ref, gt_ref,
                u_ref, w_ref, kg_ref, qg_ref, qk_ref, gl_ref, *, n_chunks):
    h = pl.program_id(2)
    c_ = CHUNK
    ii = lax.broadcasted_iota(jnp.int32, (c_, c_), 0)
    jj = lax.broadcasted_iota(jnp.int32, (c_, c_), 1)
    rows = [slice(c * c_, (c + 1) * c_) for c in range(n_chunks)]
    qs = [q_ref[r, :] for r in rows]
    ks = [k_ref[r, :] for r in rows]
    vs = [v_ref[r, :] for r in rows]
    k16 = [k.astype(BF16) for k in ks]
    kks = [_dot_nt(k16[c], k16[c]) for c in range(n_chunks)]
    qks = [_dot_nt(qs[c].astype(BF16), k16[c]) for c in range(n_chunks)]

    items = [(c, d) for c in range(n_chunks) for d in range(2)]
    betas, gcs, totals, gams, lms = [], [], [], [], []
    for c, d in items:
        r = rows[c]
        incl = (jj <= ii) if d == 0 else (jj >= ii)
        strict = (jj < ii) if d == 0 else (jj > ii)
        incl_t = (ii <= jj) if d == 0 else (ii >= jj)
        a_scale = -jnp.exp(jnp.full((1, 1), alog_ref[d * HA + h], F32))
        dtb = dtb_ref[d * HA + h]
        beta = _sigmoid(ba_ref[0, r, d:d + 1])
        g_col = a_scale * _softplus(ba_ref[0, r, 2 + d:3 + d] + dtb)
        g_row = a_scale * _softplus(gt_ref[0, 2 + d:3 + d, r] + dtb)
        gc_col = jnp.sum(jnp.where(incl, g_row, 0.0), axis=1, keepdims=True)
        gc_row = jnp.sum(jnp.where(incl_t, g_col, 0.0), axis=0, keepdims=True)
        gam = jnp.where(incl, jnp.exp(gc_col - gc_row), 0.0)
        betas.append(beta)
        gcs.append(gc_col)
        totals.append(jnp.sum(g_col, axis=0, keepdims=True))
        gams.append(gam)
        lms.append(jnp.where(strict, beta * kks[c] * gam, 0.0))

    tinvs = _tri_inverse_many(lms, ii, jj, [d == 1 for _, d in items])
    egs = [jnp.exp(gc) for gc in gcs]
    uws = []
    for i, (c, d) in enumerate(items):
        rhs = jnp.concatenate([vs[c] * betas[i], ks[c] * (betas[i] * egs[i])], axis=1).astype(BF16)
        uws.append(_dot(tinvs[i].astype(BF16), rhs))
    for i, (c, d) in enumerate(items):
        r = rows[c]
        u_ref[d, r, :] = uws[i][:, :DVA]
        w_ref[d, r, :] = uws[i][:, DVA:].astype(BF16)
        kg_ref[d, r, :] = (ks[c] * jnp.exp(totals[i] - gcs[i])).astype(BF16)
        qg_ref[d, r, :] = (qs[c] * egs[i]).astype(BF16)
        qk_ref[d, 0, r, :] = (qks[c] * gams[i]).astype(BF16)
        gl_ref[0, 0, 0, c * 2 + d:c * 2 + d + 1, :] = jnp.broadcast_to(jnp.exp(totals[i]), (1, LANES))


def _dn_factors(qkvc, ba_h, gt_h, a_log, dt_bias, n_b, lc, l):
    t = qkvc.shape[0]
    n_lat = l // lc
    n_g = n_lat + 1
    n_chunks = lc // CHUNK
    rb = lambda b, g, h: _row_block(b, g, n_b, n_lat)
    tok = lambda dt: jax.ShapeDtypeStruct((2, t, HA * DVA), dt)
    smem = pl.BlockSpec(memory_space=pltpu.SMEM)
    return pl.pallas_call(
        functools.partial(_dn1_kernel, n_chunks=n_chunks),
        grid=(n_b, n_g, HA),
        in_specs=[smem, smem,
                  pl.BlockSpec((lc, LANES), lambda b, g, h: (rb(b, g, h), h)),
                  pl.BlockSpec((lc, LANES), lambda b, g, h: (rb(b, g, h), HA + h)),
                  pl.BlockSpec((lc, LANES), lambda b, g, h: (rb(b, g, h), 2 * HA + h)),
                  pl.BlockSpec((1, lc, 4), lambda b, g, h: (h, rb(b, g, h), 0)),
                  pl.BlockSpec((1, 4, lc), lambda b, g, h: (h, 0, rb(b, g, h)))],
        out_specs=[pl.BlockSpec((2, lc, LANES), lambda b, g, h: (0, rb(b, g, h), h))] * 4
        + [pl.BlockSpec((2, 1, lc, CHUNK), lambda b, g, h: (0, h, rb(b, g, h), 0)),
           pl.BlockSpec((1, 1, 1, 2 * n_chunks, LANES), lambda b, g, h: (b, g, h, 0, 0))],
        out_shape=[tok(F32), tok(BF16), tok(BF16), tok(BF16),
                   jax.ShapeDtypeStruct((2, HA, t, CHUNK), BF16),
                   jax.ShapeDtypeStruct((n_b, n_g, HA, 2 * n_chunks, LANES), F32)],
        compiler_params=_cparams(3),
        name="deltanet_chunk_factors",
    )(a_log.reshape(-1), dt_bias.reshape(-1), qkvc, qkvc, qkvc, ba_h, gt_h)


def _dn2_kernel(u0, w0, kg0, qg0, qk0, gl0, u1, w1, kg1, qg1, qk1, gl1, o0_ref, o1_ref, s_scr,
                *, n_chunks):
    @pl.when(pl.program_id(1) == 0)
    def _():
        s_scr[...] = jnp.zeros_like(s_scr)

    c_ = CHUNK
    refs = ((u0, w0, kg0, qg0, qk0, gl0, o0_ref), (u1, w1, kg1, qg1, qk1, gl1, o1_ref))
    chains = [(h, d) for h in range(HA) for d in range(2)]
    states = [s_scr[d * HA + h] for h, d in chains]
    for step in range(n_chunks):
        pos = []
        for h, d in chains:
            c = step if d == 0 else n_chunks - 1 - step
            pos.append((slice(c * c_, (c + 1) * c_), slice(h * DVA, (h + 1) * DVA), c))
        s16 = [s.astype(BF16) for s in states]
        prods = []
        for i, (h, d) in enumerate(chains):
            r, hs, _ = pos[i]
            wq = jnp.concatenate([refs[d][1][0, r, hs], refs[d][3][0, r, hs]], axis=0)
            prods.append(_dot(wq, s16[i]))
        vn16 = [(refs[d][0][0, pos[i][0], pos[i][1]] - prods[i][:c_]).astype(BF16)
                for i, (h, d) in enumerate(chains)]
        for i, (h, d) in enumerate(chains):
            r, hs, c = pos[i]
            u, w, kg, qg, qk, gl, o_ref = refs[d]
            o_ref[r, hs] = prods[i][c_:] + _dot(qk[0, h, r, :], vn16[i])
            states[i] = (gl[0, 0, h, c * 2 + d:c * 2 + d + 1, :] * states[i]
                         + _dot_tn(kg[0, r, hs], vn16[i]))
    for i, (h, d) in enumerate(chains):
        s_scr[d * HA + h] = states[i]


def _dn_scan(factors, n_b, lc, l):
    u, w, kg, qg, qk, gl = factors
    t = u.shape[1]
    n_lat = l // lc
    n_g = n_lat + 1
    n_chunks = lc // CHUNK
    width = HA * DVA
    rb0 = lambda b, g: _row_block(b, g, n_b, n_lat)
    rb1 = lambda b, g: _row_block(b, _rev_group(g, n_g), n_b, n_lat)

    def specs(d, rb, gsel):
        tok = pl.BlockSpec((1, lc, width), lambda b, g: (d, rb(b, g), 0))
        return [tok, tok, tok, tok,
                pl.BlockSpec((1, HA, lc, CHUNK), lambda b, g: (d, 0, rb(b, g), 0)),
                pl.BlockSpec((1, 1, HA, 2 * n_chunks, LANES), lambda b, g: (b, gsel(g), 0, 0, 0))]

    return pl.pallas_call(
        functools.partial(_dn2_kernel, n_chunks=n_chunks),
        grid=(n_b, n_g),
        in_specs=specs(0, rb0, lambda g: g) + specs(1, rb1, lambda g: _rev_group(g, n_g)),
        out_specs=[pl.BlockSpec((lc, width), lambda b, g: (rb0(b, g), 0)),
                   pl.BlockSpec((lc, width), lambda b, g: (rb1(b, g), 0))],
        out_shape=[jax.ShapeDtypeStruct((t, width), F32)] * 2,
        scratch_shapes=[pltpu.VMEM((2 * HA, DKA, DVA), F32)],
        compiler_params=_cparams(2),
        name="deltanet_scan",
    )(u, w, kg, qg, qk, gl, u, w, kg, qg, qk, gl)


def _gla_kernel(q0, k0, v0, sm0, q1, k1, v1, sm1, wg_ref, bg_ref, o0_ref, o1_ref, s_scr, *, n_chunks):
    @pl.when(pl.program_id(1) == 0)
    def _():
        s_scr[...] = jnp.zeros_like(s_scr)

    c_ = CHUNK
    sh = c_.bit_length() - 1
    g_ = n_chunks * c_
    kw = HB * DKB
    vw = HB * DVB
    ii = lax.broadcasted_iota(jnp.int32, (g_, g_), 0)
    jj = lax.broadcasted_iota(jnp.int32, (g_, g_), 1)
    same_chunk = (ii >> sh) == (jj >> sh)
    klane = lax.broadcasted_iota(jnp.int32, (1, kw), 1)
    srow = lax.broadcasted_iota(jnp.int32, (kw, vw), 0) >> (DKB.bit_length() - 1)
    scol = lax.broadcasted_iota(jnp.int32, (kw, vw), 1) >> (DVB.bit_length() - 1)
    same_head = srow == scol
    ind16 = jnp.where((lax.broadcasted_iota(jnp.int32, (g_, n_chunks * LANES), 0) >> sh)
                      == (lax.broadcasted_iota(jnp.int32, (g_, n_chunks * LANES), 1) >> (LANES.bit_length() - 1)),
                      1.0, 0.0).astype(BF16)
    rows = [slice(c * c_, (c + 1) * c_) for c in range(n_chunks)]
    refs = ((q0, k0, v0, sm0, o0_ref), (q1, k1, v1, sm1, o1_ref))

    def spread(x, picks):
        return jnp.concatenate([jnp.broadcast_to(x[p:p + 1, :], (c_, kw)) for p in picks], axis=0)

    o_intra, qg16s, xs, decays = [], [], [], []
    for d, (q_ref, k_ref, v_ref, sm_ref, o_ref) in enumerate(refs):
        incl = same_chunk & ((jj <= ii) if d == 0 else (jj >= ii))
        tri16 = jnp.where(incl, 1.0, 0.0).astype(BF16)
        lo = 4 * HA + d * GATE_RANK
        pre = jnp.dot(sm_ref[:, lo:lo + GATE_RANK], wg_ref[d], precision=HIGHEST,
                      preferred_element_type=F32) + bg_ref[d]
        glog = (jnp.minimum(pre, 0.0) - jnp.log(1.0 + jnp.exp(-jnp.abs(pre)))) * (1.0 / GATE_NORM)
        ghi, glo = _split(glog)
        gc = _dot(tri16, ghi) + _dot(tri16, glo)
        tot_cols = _dot_tn(ghi, ind16) + _dot_tn(glo, ind16)
        mids = spread(gc, [c * c_ + c_ // 2 - 1 for c in range(n_chunks)])
        tots = spread(gc, [c * c_ + (c_ - 1 if d == 0 else 0) for c in range(n_chunks)])
        q = q_ref[...] * DKB ** -0.5
        k = k_ref[...]
        v16 = v_ref[...].astype(BF16)
        qt = q * jnp.exp(gc - mids)
        kt16 = (k * jnp.exp(mids - gc)).astype(BF16)
        qg16s.append((q * jnp.exp(gc)).astype(BF16))
        kg16 = (k * jnp.exp(tots - gc)).astype(BF16)
        heads = []
        for h in range(HB):
            in_head = (klane >= h * DKB) & (klane < (h + 1) * DKB)
            a = _dot_nt(jnp.where(in_head, qt, 0.0).astype(BF16), kt16)
            a16 = jnp.where(incl, a, 0.0).astype(BF16)
            heads.append(_dot(a16, v16[:, h * DVB:(h + 1) * DVB]))
        o_intra.append(heads)
        xs.append([jnp.where(same_head, _dot_tn(kg16[r, :], v16[r, :]), 0.0) for r in rows])
        decays.append([jnp.exp(tot_cols[:, c * LANES:(c + 1) * LANES]) for c in range(n_chunks)])

    states = [s_scr[0], s_scr[1]]
    for step in range(n_chunks):
        for d in range(2):
            c = step if d == 0 else n_chunks - 1 - step
            r = rows[c]
            o_inter = _dot(qg16s[d][r, :], states[d].astype(BF16))
            for h in range(HB):
                vs = slice(h * DVB, (h + 1) * DVB)
                refs[d][4][r, vs] = o_inter[:, vs] + o_intra[d][h][r, :]
            states[d] = jnp.concatenate([decays[d][c]] * (vw // LANES), axis=1) * states[d] + xs[d][c]
    s_scr[0] = states[0]
    s_scr[1] = states[1]


def _gla(proj, w_gate2, b_gate, n_b, lc, l):
    t = proj.shape[0]
    n_lat = l // lc
    n_g = n_lat + 1
    n_chunks = lc // CHUNK
    kw, vw = HB * DKB, HB * DVB
    rb0 = lambda b, g: _row_block(b, g, n_b, n_lat)
    rb1 = lambda b, g: _row_block(b, _rev_group(g, n_g), n_b, n_lat)

    def specs(rb):
        return [pl.BlockSpec((lc, kw), lambda b, g: (rb(b, g), COL_B // kw)),
                pl.BlockSpec((lc, kw), lambda b, g: (rb(b, g), COL_B // kw + 1)),
                pl.BlockSpec((lc, vw), lambda b, g: (rb(b, g), (COL_B + 2 * kw) // vw)),
                pl.BlockSpec((lc, LANES), lambda b, g: (rb(b, g), COL_S // LANES))]

    return pl.pallas_call(
        functools.partial(_gla_kernel, n_chunks=n_chunks),
        grid=(n_b, n_g),
        in_specs=specs(rb0) + specs(rb1)
        + [pl.BlockSpec((2, GATE_RANK, kw), lambda b, g: (0, 0, 0)),
           pl.BlockSpec((2, 1, kw), lambda b, g: (0, 0, 0))],
        out_specs=[pl.BlockSpec((lc, vw), lambda b, g: (rb0(b, g), 0)),
                   pl.BlockSpec((lc, vw), lambda b, g: (rb1(b, g), 0))],
        out_shape=[jax.ShapeDtypeStruct((t, vw), F32)] * 2,
        scratch_shapes=[pltpu.VMEM((2, kw, vw), F32)],
        compiler_params=_cparams(2),
        name="gla_bidir",
    )(proj, proj, proj, proj, proj, proj, proj, proj, w_gate2, b_gate.reshape(2, 1, kw))


def _attn_kernel(lam_ref, gn_ref, q_ref, kc_ref, kl_ref, vc_ref, vl_ref, cq_ref, sq_ref, ck_ref, sk_ref,
                 o_ref, k_scr, v_scr, *, lc, lam_init, ctx_out):
    i = pl.program_id(2)
    lane = lax.broadcasted_iota(jnp.int32, (1, LANES), 1)
    first_half = (lane & 31) < 16

    def rope(x, cos, sin):
        swapped = jnp.where(first_half, pltpu.roll(x, LANES - 16, 1), pltpu.roll(x, 16, 1))
        return x * cos + swapped * sin

    @pl.when(i == 0)
    def _():
        k_scr[0:lc, :] = kc_ref[...].astype(BF16)
        k_scr[lc:, :] = rope(kl_ref[...], ck_ref[...], sk_ref[...]).astype(BF16)
        v_scr[0:lc, :] = vc_ref[...].astype(BF16)
        v_scr[lc:, :] = vl_ref[...].astype(BF16)

    lam = lam_ref[...]
    lam_f = (jnp.exp(jnp.sum(lam[0:1] * lam[1:2], axis=1, keepdims=True))
             - jnp.exp(jnp.sum(lam[2:3] * lam[3:4], axis=1, keepdims=True)) + lam_init)

    def attend(q, keys, vals):
        q = q * DKC ** -0.5
        probs = []
        for m in range(2):
            qm = jnp.where((lane >= m * DKC) & (lane < (m + 1) * DKC), q, 0.0).astype(BF16)
            s = _dot_nt(qm, keys)
            e = jnp.exp(s - jnp.max(s, axis=-1, keepdims=True))
            probs.append(e * (1.0 / jnp.sum(e, axis=-1, keepdims=True)))
        p = (probs[0] - lam_f * probs[1]).astype(BF16)
        o = _dot(p, vals)
        o_ref[...] = _rms_rows(o, gn_ref[...]) * (1.0 - lam_init)

    @pl.when(i == 0)
    def _():
        if ctx_out:
            attend(q_ref[...], k_scr[0:lc, :], v_scr[0:lc, :])
        else:
            o_ref[...] = jnp.zeros_like(o_ref)

    @pl.when(i > 0)
    def _():
        attend(rope(q_ref[...], cq_ref[...], sq_ref[...]), k_scr[...], v_scr[...])


def _attention(proj, lam, gn, rope_cos, rope_sin, n_b, lc, l, lam_init, ctx_out):
    t = proj.shape[0]
    n_lat = l // lc
    rb = lambda b, h, i: _row_block(b, i, n_b, n_lat)
    cq, ck, cv = COL_C // LANES, COL_C // LANES + HC, COL_C // LANES + 2 * HC
    tab_q = pl.BlockSpec((lc, LANES), lambda b, h, i: (jnp.maximum(i - 1, 0), 0))
    tab_k = pl.BlockSpec((l, LANES), lambda b, h, i: (0, 0))
    return pl.pallas_call(
        functools.partial(_attn_kernel, lc=lc, lam_init=lam_init, ctx_out=ctx_out),
        grid=(n_b, HC, n_lat + 1),
        in_specs=[pl.BlockSpec((4, DKC), lambda b, h, i: (0, 0)),
                  pl.BlockSpec((1, DVC), lambda b, h, i: (0, 0)),
                  pl.BlockSpec((lc, LANES), lambda b, h, i: (rb(b, h, i), cq + h)),
                  pl.BlockSpec((lc, LANES), lambda b, h, i: (b, ck + h)),
                  pl.BlockSpec((l, LANES), lambda b, h, i: (b + 1, ck + h)),
                  pl.BlockSpec((lc, LANES), lambda b, h, i: (b, cv + h)),
                  pl.BlockSpec((l, LANES), lambda b, h, i: (b + 1, cv + h)),
                  tab_q, tab_q, tab_k, tab_k],
        out_specs=pl.BlockSpec((lc, LANES), lambda b, h, i: (rb(b, h, i), h)),
        out_shape=jax.ShapeDtypeStruct((t, HC * DVC), F32),
        scratch_shapes=[pltpu.VMEM((lc + l, LANES), BF16), pltpu.VMEM((lc + l, LANES), BF16)],
        compiler_params=_cparams(3),
        name="diff_attention",
    )(lam, gn.reshape(1, DVC), proj, proj, proj, proj, proj, rope_cos, rope_sin, rope_cos, rope_sin)


def _rope_tables(n_tokens):
    rows = n_tokens // GRID_W
    row = jnp.repeat(jnp.arange(rows, dtype=F32), GRID_W)
    col = jnp.tile(jnp.arange(GRID_W, dtype=F32), rows)
    n_freq = DKC // 4
    inv_freq = ROPE_THETA ** (-jnp.arange(n_freq, dtype=F32) / n_freq)
    ang_r = row[:, None] * inv_freq
    ang_c = col[:, None] * inv_freq
    cos = jnp.concatenate([jnp.cos(ang_r)] * 2 + [jnp.cos(ang_c)] * 2, axis=1)
    sin = jnp.concatenate([-jnp.sin(ang_r), jnp.sin(ang_r), -jnp.sin(ang_c), jnp.sin(ang_c)], axis=1)
    return jnp.concatenate([cos, cos], axis=1), jnp.concatenate([sin, sin], axis=1)


def _head_rms(o, gain, n_heads, width):
    parts = []
    for h in range(n_heads):
        parts.append(_rms_rows(o[:, h * width:(h + 1) * width], gain))
    return jnp.concatenate(parts, axis=1)


def _merge_kernel(oa0, oa1, z_ref, ob0, ob1, r_ref, od_ref, ga_ref, gb_ref, gd_ref, x_ref, mod_ref,
                  gna_ref, gnb_ref, woa_ref, wob_ref, woc_ref, wout_ref, o_ref):
    ya = _head_rms(oa0[...] + oa1[...], gna_ref[...], HA, DVA) * _silu(z_ref[...])
    yb = _head_rms(ob0[...] + ob1[...], gnb_ref[...], HB, DVB) * _silu(r_ref[...])
    acc = _sigmoid(ga_ref[...]) * _dot(ya.astype(BF16), woa_ref[...])
    acc = acc + _sigmoid(gb_ref[...]) * _dot(yb.astype(BF16), wob_ref[...])
    acc = acc + _sigmoid(gd_ref[...]) * _dot(od_ref[...].astype(BF16), woc_ref[...])
    y = _dot(acc.astype(BF16), wout_ref[...])
    o_ref[...] = x_ref[...] + mod_ref[0, 2:3, :] * y


def _merge(oa0, oa1, ob0, ob1, od, proj, x, mod, gn_a, gn_b, woa, wob, woc, wout, rows_per_mod, row_off, tm):
    t, d = x.shape
    off = row_off // tm
    n_tiles = (t - row_off) // tm
    per = rows_per_mod // tm
    w5 = HA * DVA
    row = lambda i: (i + off, 0)
    col = lambda c: (lambda i: (i + off, c))
    return pl.pallas_call(
        _merge_kernel,
        grid=(n_tiles,),
        in_specs=[pl.BlockSpec((tm, w5), row),
                  pl.BlockSpec((tm, w5), row),
                  pl.BlockSpec((tm, w5), col((COL_A + 3 * w5) // w5)),
                  pl.BlockSpec((tm, w5), row),
                  pl.BlockSpec((tm, w5), row),
                  pl.BlockSpec((tm, w5), col((COL_B + 2 * HB * DKB + w5) // w5)),
                  pl.BlockSpec((tm, w5), row),
                  pl.BlockSpec((tm, d), col(COL_G // d)),
                  pl.BlockSpec((tm, d), col(COL_G // d + 1)),
                  pl.BlockSpec((tm, d), col(COL_G // d + 2)),
                  pl.BlockSpec((tm, d), row),
                  pl.BlockSpec((1, 6, d), lambda i: ((i + off) // per, 0, 0)),
                  pl.BlockSpec((1, DVA), lambda i: (0, 0)),
                  pl.BlockSpec((1, DVB), lambda i: (0, 0)),
                  pl.BlockSpec((w5, d), lambda i: (0, 0)),
                  pl.BlockSpec((w5, d), lambda i: (0, 0)),
                  pl.BlockSpec((w5, d), lambda i: (0, 0)),
                  pl.BlockSpec((d, d), lambda i: (0, 0))],
        out_specs=pl.BlockSpec((tm, d), lambda i: (i, 0)),
        out_shape=jax.ShapeDtypeStruct((t - row_off, d), F32),
        compiler_params=_cparams(1),
        name="merge_outproj",
    )(oa0, oa1, proj, ob0, ob1, proj, od, proj, proj, proj, x, mod,
      gn_a.reshape(1, DVA), gn_b.reshape(1, DVB), woa, wob, woc, wout)


def _ffn_kernel(x_ref, mod_ref, g_ref, rw_ref, fg_ref, w1_ref, w3_ref, w2_ref, o_ref, h_scr, gate_scr, acc_scr,
                *, routed, final):
    e = pl.program_id(1)
    f = pl.program_id(2)
    first = (e == 0) & (f == 0)
    last = (e == pl.num_programs(1) - 1) & (f == pl.num_programs(2) - 1)
    lane = lax.broadcasted_iota(jnp.int32, (1, LANES), 1)

    @pl.when(first)
    def _():
        y = _rms_rows(x_ref[...], g_ref[...])
        hmod = y * (1.0 + mod_ref[0, 4:5, :]) + mod_ref[0, 3:4, :]
        h_scr[...] = hmod.astype(BF16)
        acc_scr[...] = jnp.zeros_like(acc_scr)
        if routed:
            logits = jnp.dot(hmod, rw_ref[...], precision=HIGHEST, preferred_element_type=F32)
            logits = jnp.where(lane < N_EXPERTS, logits, -jnp.inf)
            lanef = lane.astype(F32)
            m1 = jnp.max(logits, axis=1, keepdims=True)
            i1 = jnp.min(jnp.where(logits == m1, lanef, float(LANES)), axis=1, keepdims=True)
            hit1 = lanef == i1
            rest = jnp.where(hit1, -jnp.inf, logits)
            m2 = jnp.max(rest, axis=1, keepdims=True)
            i2 = jnp.min(jnp.where(rest == m2, lanef, float(LANES)), axis=1, keepdims=True)
            hit2 = lanef == i2
            ex = jnp.exp(m2 - m1)
            inv = 1.0 / (1.0 + ex)
            gate_scr[...] = jnp.where(hit1, inv, 0.0) + jnp.where(hit2, ex * inv, 0.0)

    h = h_scr[...]
    t = _silu(_dot(h, w1_ref[0])) * _dot(h, w3_ref[0])
    if routed:
        t = t * jnp.sum(jnp.where(lane == e, gate_scr[...], 0.0), axis=1, keepdims=True)
    acc_scr[...] += _dot(t.astype(BF16), w2_ref[0])

    @pl.when(last)
    def _():
        out = x_ref[...] + mod_ref[0, 5:6, :] * acc_scr[...]
        if final:
            out = _rms_rows(out, fg_ref[...])
        o_ref[...] = out


def _ffn(x, mod, g, router_w, final_g, w1, w3, w2, rows_per_mod, mod_off, tm, tf, routed, final):
    t, d = x.shape
    n_e, _, dff = w1.shape
    per = rows_per_mod // tm
    return pl.pallas_call(
        functools.partial(_ffn_kernel, routed=routed, final=final),
        grid=(t // tm, n_e, dff // tf),
        in_specs=[pl.BlockSpec((tm, d), lambda i, e, f: (i, 0)),
                  pl.BlockSpec((1, 6, d), lambda i, e, f: (i // per + mod_off, 0, 0)),
                  pl.BlockSpec((1, d), lambda i, e, f: (0, 0)),
                  pl.BlockSpec((d, LANES), lambda i, e, f: (0, 0)),
                  pl.BlockSpec((1, d), lambda i, e, f: (0, 0)),
                  pl.BlockSpec((1, d, tf), lambda i, e, f: (e, 0, f)),
                  pl.BlockSpec((1, d, tf), lambda i, e, f: (e, 0, f)),
                  pl.BlockSpec((1, tf, d), lambda i, e, f: (e, f, 0))],
        out_specs=pl.BlockSpec((tm, d), lambda i, e, f: (i, 0)),
        out_shape=jax.ShapeDtypeStruct((t, d), F32),
        scratch_shapes=[pltpu.VMEM((tm, d), BF16), pltpu.VMEM((tm, LANES), F32), pltpu.VMEM((tm, d), F32)],
        compiler_params=_cparams(3),
        name="routed_ffn" if routed else "dense_ffn",
    )(x, mod, g.reshape(1, d), router_w, final_g.reshape(1, d), w1, w3, w2)


def _reorder_w_in(w):
    d = w.shape[0]
    a_end = 4 * HA * DKA
    b_start = a_end + 4 * HA
    b_end = b_start + 2 * HB * DKB + 2 * HB * DVB
    c_start = b_end + 2 * GATE_RANK
    pad = N_COLS - COL_S - 4 * HA - 2 * GATE_RANK
    return jnp.concatenate([w[:, :a_end], w[:, b_start:b_end], w[:, c_start:], w[:, a_end:b_start],
                            w[:, b_end:c_start], jnp.zeros((d, pad), w.dtype)], axis=1).astype(BF16)


def kernel(x, c, ctx, c_ctx, w_mod, b_mod, norm1_g, norm2_g, w_in, conv_a, a_log, dt_bias, gn_a, w_gate2, b_gate, gn_b, lam_c, gn_c, w_o_a, w_o_b, w_o_c, w_out, ffn_w1, ffn_w3, ffn_w2, router_w, moe_w1, moe_w3, moe_w2, final_g):
    n_b, l, d = x.shape
    lc = ctx.shape[1]
    depth = w_mod.shape[0]
    assert n_b * lc == l and lc % CHUNK == 0 and l & (l - 1) == 0 and lc & (lc - 1) == 0
    t_ctx = n_b * lc
    tm = min(1024, l)
    tm_small = min(512, l)

    n_mod = 16
    cvec = jnp.concatenate([c_ctx[None, :], c, jnp.zeros((n_mod - 1 - n_b, d), F32)], axis=0)
    mod_all = _modulation(cvec, w_mod, b_mod).reshape(depth, n_mod, 6, d)

    rope_cos, rope_sin = _rope_tables(l)
    xs = jnp.concatenate([ctx.reshape(t_ctx, d), x.reshape(n_b * l, d)], axis=0)

    for layer in range(depth):
        ctx_out = layer < depth - 1
        lam_init = 0.8 - 0.6 * math.exp(-0.3 * layer)
        mod = mod_all[layer]
        proj = _in_projection(xs, mod, norm1_g[layer], _reorder_w_in(w_in[layer]), l, tm, 1664)

        qkvc = _short_conv(proj, conv_a[layer], lc, l)
        small = proj[:, COL_S:COL_S + 4 * HA].reshape(-1, 4, HA)
        ba_h = small.transpose(2, 0, 1)
        gt_h = small.transpose(2, 1, 0)
        oa0, oa1 = _dn_scan(_dn_factors(qkvc, ba_h, gt_h, a_log[layer], dt_bias[layer], n_b, lc, l), n_b, lc, l)
        ob0, ob1 = _gla(proj, w_gate2[layer], b_gate[layer], n_b, lc, l)
        od = _attention(proj, lam_c[layer], gn_c[layer], rope_cos, rope_sin, n_b, lc, l, lam_init, ctx_out)

        row_off = 0 if ctx_out else t_ctx
        xs_new = _merge(oa0, oa1, ob0, ob1, od, proj, xs, mod, gn_a[layer], gn_b[layer],
                        w_o_a[layer].astype(BF16), w_o_b[layer].astype(BF16), w_o_c[layer].astype(BF16),
                        w_out[layer].astype(BF16), l, row_off, tm_small)

        i = layer // 2
        final = layer == depth - 1
        mod_off = 0 if ctx_out else 1
        if layer % 2 == 0:
            rw = jnp.zeros((d, LANES), F32)
            xs = _ffn(xs_new, mod, norm2_g[layer], rw, final_g, ffn_w1[i:i + 1].astype(BF16),
                      ffn_w3[i:i + 1].astype(BF16), ffn_w2[i:i + 1].astype(BF16), l, mod_off, tm_small,
                      ffn_w1.shape[2] // 2, False, final)
        else:
            rw = jnp.pad(router_w[i], ((0, 0), (0, LANES - N_EXPERTS)))
            xs = _ffn(xs_new, mod, norm2_g[layer], rw, final_g, moe_w1[i].astype(BF16),
                      moe_w3[i].astype(BF16), moe_w2[i].astype(BF16), l, mod_off, tm_small,
                      moe_w1.shape[3] // 2, True, final)

    return xs.reshape(n_b, l, d)
```

```python
import functools
import math

import jax
import jax.numpy as jnp
from jax import lax
from jax.experimental import pallas as pl
from jax.experimental.pallas import tpu as pltpu

F32 = jnp.float32
BF16 = jnp.bfloat16
HIGHEST = lax.Precision.HIGHEST

EPS = 1e-6
HA, DKA, DVA = 4, 128, 128
HB, DKB, DVB = 4, 64, 128
GATE_RANK, GATE_NORM = 16, 16.0
HC, DKC, DVC = 4, 64, 128
GRID_W, ROPE_THETA = 64, 10000.0
N_EXPERTS, TOP_K = 8, 2
CHUNK = 64
LANES = 128
ATTN_SUB = 64
DN_HEADS = 4

COL_A = 0
COL_B = 2048
COL_C = 3584
COL_G = 5120
COL_S = 8192
N_COLS = 8320

VMEM_LIMIT = 56 * 1024 * 1024


def _cparams(n_axes):
    return pltpu.CompilerParams(dimension_semantics=("arbitrary",) * n_axes,
                                vmem_limit_bytes=VMEM_LIMIT)


def _sigmoid(x):
    return 1.0 / (1.0 + jnp.exp(-x))


def _silu(x):
    return x * _sigmoid(x)


def _softplus(x):
    return jnp.maximum(x, 0.0) + jnp.log(1.0 + jnp.exp(-jnp.abs(x)))


def _dot(a, b):
    return jnp.dot(a, b, preferred_element_type=F32)


def _dot_nt(a, b):
    return lax.dot_general(a, b, (((1,), (1,)), ((), ())), preferred_element_type=F32)


def _dot_tn(a, b):
    return lax.dot_general(a, b, (((0,), (0,)), ((), ())), preferred_element_type=F32)


def _split(a):
    hi = a.astype(BF16)
    lo = (a - hi.astype(F32)).astype(BF16)
    return hi, lo


def _dot3(a, b):
    ah, al = _split(a)
    bh, bl = _split(b)
    return _dot(ah, bh) + (_dot(ah, bl) + _dot(al, bh))


def _rms_rows(x, gain):
    ms = jnp.mean(x * x, axis=-1, keepdims=True)
    return x * lax.rsqrt(ms + EPS) * gain


def _mod_kernel(c_ref, w_ref, b_ref, o_ref):
    s = _silu(c_ref[...])
    o_ref[0] = jnp.dot(s, w_ref[0], precision=HIGHEST, preferred_element_type=F32) + b_ref[0]


def _modulation(cvec, w_mod, b_mod):
    depth, d, d6 = w_mod.shape
    n = cvec.shape[0]
    return pl.pallas_call(
        _mod_kernel,
        grid=(depth, d6 // d),
        in_specs=[pl.BlockSpec((n, d), lambda l, j: (0, 0)),
                  pl.BlockSpec((1, d, d), lambda l, j: (l, 0, j)),
                  pl.BlockSpec((1, 1, d), lambda l, j: (l, 0, j))],
        out_specs=pl.BlockSpec((1, n, d), lambda l, j: (l, 0, j)),
        out_shape=jax.ShapeDtypeStruct((depth, n, d6), F32),
        compiler_params=_cparams(2),
        name="adaln_mod",
    )(cvec, w_mod, b_mod.reshape(depth, 1, d6))


def _inproj_kernel(x_ref, mod_ref, g_ref, w_ref, o_ref, h_scr):
    @pl.when(pl.program_id(1) == 0)
    def _():
        y = _rms_rows(x_ref[...], g_ref[...])
        h_scr[...] = (y * (1.0 + mod_ref[0, 1:2, :]) + mod_ref[0, 0:1, :]).astype(BF16)

    o_ref[...] = _dot(h_scr[...], w_ref[...])


def _in_projection(x, mod, g, w, rows_per_mod, tm, tn):
    t, d = x.shape
    nc = w.shape[1]
    per = rows_per_mod // tm
    return pl.pallas_call(
        _inproj_kernel,
        grid=(t // tm, nc // tn),
        in_specs=[pl.BlockSpec((tm, d), lambda i, j: (i, 0)),
                  pl.BlockSpec((1, 6, d), lambda i, j: (i // per, 0, 0)),
                  pl.BlockSpec((1, d), lambda i, j: (0, 0)),
                  pl.BlockSpec((d, tn), lambda i, j: (0, j))],
        out_specs=pl.BlockSpec((tm, tn), lambda i, j: (i, j)),
        out_shape=jax.ShapeDtypeStruct((t, nc), F32),
        scratch_shapes=[pltpu.VMEM((tm, d), BF16)],
        compiler_params=_cparams(2),
        name="norm_mod_inproj",
    )(x, mod, g.reshape(1, d), w)


def _conv_kernel(x_ref, w_ref, o_ref, *, lc, l):
    p = pl.program_id(0)
    j = pl.program_id(1)
    x = x_ref[...]
    n = x.shape[0]
    row = lax.broadcasted_iota(jnp.int32, x.shape, 0)
    seq = jnp.where(p == 0, lc, l)
    pos = row & (seq - 1)
    prev = jnp.where(pos == 0, 0.0, pltpu.roll(x, 1, 0))
    nxt = jnp.where(pos == seq - 1, 0.0, pltpu.roll(x, n - 1, 0))
    w = w_ref[...]
    y = _silu(prev * w[0:1] + x * w[1:2] + nxt * w[2:3])
    inv = lax.rsqrt(jnp.sum(y * y, axis=-1, keepdims=True) + EPS)
    fac = jnp.where(j < HA, inv * DKA ** -0.5, jnp.where(j < 2 * HA, inv, 1.0))
    o_ref[...] = y * fac


def _short_conv(proj, conv_w, lc, l):
    t = proj.shape[0]
    ncol = conv_w.shape[1]
    return pl.pallas_call(
        functools.partial(_conv_kernel, lc=lc, l=l),
        grid=(t // l, ncol // LANES),
        in_specs=[pl.BlockSpec((l, LANES), lambda p, j: (p, j)),
                  pl.BlockSpec((conv_w.shape[0], LANES), lambda p, j: (0, j))],
        out_specs=pl.BlockSpec((l, LANES), lambda p, j: (p, j)),
        out_shape=jax.ShapeDtypeStruct((t, ncol), F32),
        compiler_params=_cparams(2),
        name="conv_silu_l2",
    )(proj, conv_w)


def _row_block(b, g, n_b, n_lat):
    return jnp.where(g == 0, b, n_b + b * n_lat + g - 1)


def _rev_group(g, n_g):
    return jnp.where(g == 0, 0, n_g - g)


def _tri_inverse_many(lms, ii, jj, uppers):
    c = lms[0].shape[0]
    eye = (ii == jj).astype(F32)
    n = len(lms)
    ts = None
    s, ls = 1, 0
    while s < c:
        same = (ii >> (ls + 1)) == (jj >> (ls + 1))
        io = (ii >> ls) & 1
        jo = (jj >> ls) & 1
        lower = same & (io == 1) & (jo == 0)
        upper = same & (io == 0) & (jo == 1)
        cms = [jnp.where(upper if uppers[i] else lower, lms[i], 0.0) for i in range(n)]
        if ts is None:
            ts = [eye - cm for cm in cms]
        else:
            t16 = [t.astype(BF16) for t in ts]
            ps = [_dot(t16[i], cms[i].astype(BF16)).astype(BF16) for i in range(n)]
            ts = [ts[i] - _dot(ps[i], t16[i]) for i in range(n)]
        s, ls = s * 2, ls + 1
    a_parts = [_split(eye + lm) for lm in lms]
    t_parts = [_split(t) for t in ts]
    res = [eye - (_dot(a_parts[i][0], t_parts[i][0])
                  + (_dot(a_parts[i][0], t_parts[i][1]) + _dot(a_parts[i][1], t_parts[i][0])))
           for i in range(n)]
    return [ts[i] + _dot(t_parts[i][0], res[i].astype(BF16)) for i in range(n)]


def _dn1_kernel(alog_ref, dtb_ref, q_ref, k_ref, v_ref, ba_ref, gt_ref,
                u_ref, w_ref, kg_ref, qg_ref, qk_ref, gl_ref, *, n_chunks):
    hp = pl.program_id(2)
    c_ = CHUNK
    ii = lax.broadcasted_iota(jnp.int32, (c_, c_), 0)
    jj = lax.broadcasted_iota(jnp.int32, (c_, c_), 1)
    rows = [slice(c * c_, (c + 1) * c_) for c in range(n_chunks)]
    lanes = [slice(j * LANES, (j + 1) * LANES) for j in range(DN_HEADS)]
    blocks = [(j, c) for j in range(DN_HEADS) for c in range(n_chunks)]
    qs = {jc: q_ref[rows[jc[1]], lanes[jc[0]]] for jc in blocks}
    ks = {jc: k_ref[rows[jc[1]], lanes[jc[0]]] for jc in blocks}
    vs = {jc: v_ref[rows[jc[1]], lanes[jc[0]]] for jc in blocks}
    k16 = {jc: ks[jc].astype(BF16) for jc in blocks}
    kks = {jc: _dot_nt(k16[jc], k16[jc]) for jc in blocks}
    qks = {jc: _dot_nt(qs[jc].astype(BF16), k16[jc]) for jc in blocks}

    items = [(j, c, d) for j, c in blocks for d in range(2)]
    betas, gcs, totals, gams, lms = [], [], [], [], []
    for j, c, d in items:
        r = rows[c]
        h = hp * DN_HEADS + j
        incl = (jj <= ii) if d == 0 else (jj >= ii)
        strict = (jj < ii) if d == 0 else (jj > ii)
        incl_t = (ii <= jj) if d == 0 else (ii >= jj)
        a_scale = -jnp.exp(jnp.full((1, 1), alog_ref[d * HA + h], F32))
        dtb = dtb_ref[d * HA + h]
        beta = _sigmoid(ba_ref[j, r, d:d + 1])
        g_col = a_scale * _softplus(ba_ref[j, r, 2 + d:3 + d] + dtb)
        g_row = a_scale * _softplus(gt_ref[j, 2 + d:3 + d, r] + dtb)
        gc_col = jnp.sum(jnp.where(incl, g_row, 0.0), axis=1, keepdims=True)
        gc_row = jnp.sum(jnp.where(incl_t, g_col, 0.0), axis=0, keepdims=True)
        gam = jnp.where(incl, jnp.exp(gc_col - gc_row), 0.0)
        betas.append(beta)
        gcs.append(gc_col)
        totals.append(jnp.sum(g_col, axis=0, keepdims=True))
        gams.append(gam)
        lms.append(jnp.where(strict, beta * kks[(j, c)] * gam, 0.0))

    tinvs = _tri_inverse_many(lms, ii, jj, [d == 1 for _, _, d in items])
    egs = [jnp.exp(gc) for gc in gcs]
    uws = []
    for i, (j, c, d) in enumerate(items):
        rhs = jnp.concatenate([vs[(j, c)] * betas[i], ks[(j, c)] * (betas[i] * egs[i])], axis=1).astype(BF16)
        uws.append(_dot(tinvs[i].astype(BF16), rhs))
    for i, (j, c, d) in enumerate(items):
        r = rows[c]
        u_ref[d, r, lanes[j]] = uws[i][:, :DVA]
        w_ref[d, r, lanes[j]] = uws[i][:, DVA:].astype(BF16)
        kg_ref[d, r, lanes[j]] = (ks[(j, c)] * jnp.exp(totals[i] - gcs[i])).astype(BF16)
        qg_ref[d, r, lanes[j]] = (qs[(j, c)] * egs[i]).astype(BF16)
        qk_ref[d, j, r, :] = (qks[(j, c)] * gams[i]).astype(BF16)
        gl_ref[0, 0, j, c * 2 + d:c * 2 + d + 1, :] = jnp.broadcast_to(jnp.exp(totals[i]), (1, LANES))


def _dn_factors(qkvc, ba_h, gt_h, a_log, dt_bias, n_b, lc, l):
    t = qkvc.shape[0]
    n_lat = l // lc
    n_g = n_lat + 1
    n_chunks = lc // CHUNK
    nh = DN_HEADS
    wide = nh * LANES
    steps = HA // nh
    rb = lambda b, g, h: _row_block(b, g, n_b, n_lat)
    tok = lambda dt: jax.ShapeDtypeStruct((2, t, HA * DVA), dt)
    smem = pl.BlockSpec(memory_space=pltpu.SMEM)
    return pl.pallas_call(
        functools.partial(_dn1_kernel, n_chunks=n_chunks),
        grid=(n_b, n_g, steps),
        in_specs=[smem, smem,
                  pl.BlockSpec((lc, wide), lambda b, g, h: (rb(b, g, h), h)),
                  pl.BlockSpec((lc, wide), lambda b, g, h: (rb(b, g, h), steps + h)),
                  pl.BlockSpec((lc, wide), lambda b, g, h: (rb(b, g, h), 2 * steps + h)),
                  pl.BlockSpec((nh, lc, 4), lambda b, g, h: (h, rb(b, g, h), 0)),
                  pl.BlockSpec((nh, 4, lc), lambda b, g, h: (h, 0, rb(b, g, h)))],
        out_specs=[pl.BlockSpec((2, lc, wide), lambda b, g, h: (0, rb(b, g, h), h))] * 4
        + [pl.BlockSpec((2, nh, lc, CHUNK), lambda b, g, h: (0, h, rb(b, g, h), 0)),
           pl.BlockSpec((1, 1, nh, 2 * n_chunks, LANES), lambda b, g, h: (b, g, h, 0, 0))],
        out_shape=[tok(F32), tok(BF16), tok(BF16), tok(BF16),
                   jax.ShapeDtypeStruct((2, HA, t, CHUNK), BF16),
                   jax.ShapeDtypeStruct((n_b, n_g, HA, 2 * n_chunks, LANES), F32)],
        compiler_params=_cparams(3),
        name="deltanet_chunk_factors",
    )(a_log.reshape(-1), dt_bias.reshape(-1), qkvc, qkvc, qkvc, ba_h, gt_h)


def _dn2_kernel(u0, w0, kg0, qg0, qk0, gl0, u1, w1, kg1, qg1, qk1, gl1, o0_ref, o1_ref, s_scr,
                *, n_chunks):
    @pl.when(pl.program_id(1) == 0)
    def _():
        s_scr[...] = jnp.zeros_like(s_scr)

    c_ = CHUNK
    refs = ((u0, w0, kg0, qg0, qk0, gl0, o0_ref), (u1, w1, kg1, qg1, qk1, gl1, o1_ref))
    chains = [(h, d) for h in range(HA) for d in range(2)]
    states = [s_scr[d * HA + h] for h, d in chains]
    for step in range(n_chunks):
        pos = []
        for h, d in chains:
            c = step if d == 0 else n_chunks - 1 - step
            pos.append((slice(c * c_, (c + 1) * c_), slice(h * DVA, (h + 1) * DVA), c))
        s16 = [s.astype(BF16) for s in states]
        prods = []
        for i, (h, d) in enumerate(chains):
            r, hs, _ = pos[i]
            wq = jnp.concatenate([refs[d][1][0, r, hs], refs[d][3][0, r, hs]], axis=0)
            prods.append(_dot(wq, s16[i]))
        vn16 = [(refs[d][0][0, pos[i][0], pos[i][1]] - prods[i][:c_]).astype(BF16)
                for i, (h, d) in enumerate(chains)]
        for i, (h, d) in enumerate(chains):
            r, hs, c = pos[i]
            u, w, kg, qg, qk, gl, o_ref = refs[d]
            o_ref[r, hs] = prods[i][c_:] + _dot(qk[0, h, r, :], vn16[i])
            states[i] = (gl[0, 0, h, c * 2 + d:c * 2 + d + 1, :] * states[i]
                         + _dot_tn(kg[0, r, hs], vn16[i]))
    for i, (h, d) in enumerate(chains):
        s_scr[d * HA + h] = states[i]


def _dn_scan(factors, n_b, lc, l):
    u, w, kg, qg, qk, gl = factors
    t = u.shape[1]
    n_lat = l // lc
    n_g = n_lat + 1
    n_chunks = lc // CHUNK
    width = HA * DVA
    rb0 = lambda b, g: _row_block(b, g, n_b, n_lat)
    rb1 = lambda b, g: _row_block(b, _rev_group(g, n_g), n_b, n_lat)

    def specs(d, rb, gsel):
        tok = pl.BlockSpec((1, lc, width), lambda b, g: (d, rb(b, g), 0))
        return [tok, tok, tok, tok,
                pl.BlockSpec((1, HA, lc, CHUNK), lambda b, g: (d, 0, rb(b, g), 0)),
                pl.BlockSpec((1, 1, HA, 2 * n_chunks, LANES), lambda b, g: (b, gsel(g), 0, 0, 0))]

    return pl.pallas_call(
        functools.partial(_dn2_kernel, n_chunks=n_chunks),
        grid=(n_b, n_g),
        in_specs=specs(0, rb0, lambda g: g) + specs(1, rb1, lambda g: _rev_group(g, n_g)),
        out_specs=[pl.BlockSpec((lc, width), lambda b, g: (rb0(b, g), 0)),
                   pl.BlockSpec((lc, width), lambda b, g: (rb1(b, g), 0))],
        out_shape=[jax.ShapeDtypeStruct((t, width), F32)] * 2,
        scratch_shapes=[pltpu.VMEM((2 * HA, DKA, DVA), F32)],
        compiler_params=_cparams(2),
        name="deltanet_scan",
    )(u, w, kg, qg, qk, gl, u, w, kg, qg, qk, gl)


def _gla_kernel(q0, k0, v0, sm0, q1, k1, v1, sm1, wg_ref, bg_ref, o0_ref, o1_ref, s_scr, *, n_chunks):
    @pl.when(pl.program_id(1) == 0)
    def _():
        s_scr[...] = jnp.zeros_like(s_scr)

    c_ = CHUNK
    sh = c_.bit_length() - 1
    g_ = n_chunks * c_
    kw = HB * DKB
    vw = HB * DVB
    ii = lax.broadcasted_iota(jnp.int32, (g_, g_), 0)
    jj = lax.broadcasted_iota(jnp.int32, (g_, g_), 1)
    same_chunk = (ii >> sh) == (jj >> sh)
    klane = lax.broadcasted_iota(jnp.int32, (1, kw), 1)
    srow = lax.broadcasted_iota(jnp.int32, (kw, vw), 0) >> (DKB.bit_length() - 1)
    scol = lax.broadcasted_iota(jnp.int32, (kw, vw), 1) >> (DVB.bit_length() - 1)
    same_head = srow == scol
    ind16 = jnp.where((lax.broadcasted_iota(jnp.int32, (g_, n_chunks * LANES), 0) >> sh)
                      == (lax.broadcasted_iota(jnp.int32, (g_, n_chunks * LANES), 1) >> (LANES.bit_length() - 1)),
                      1.0, 0.0).astype(BF16)
    rows = [slice(c * c_, (c + 1) * c_) for c in range(n_chunks)]
    refs = ((q0, k0, v0, sm0, o0_ref), (q1, k1, v1, sm1, o1_ref))

    def spread(x, picks):
        return jnp.concatenate([jnp.broadcast_to(x[p:p + 1, :], (c_, kw)) for p in picks], axis=0)

    o_intra, qg16s, xs, decays = [], [], [], []
    for d, (q_ref, k_ref, v_ref, sm_ref, o_ref) in enumerate(refs):
        incl = same_chunk & ((jj <= ii) if d == 0 else (jj >= ii))
        tri16 = jnp.where(incl, 1.0, 0.0).astype(BF16)
        lo = 4 * HA + d * GATE_RANK
        pre = jnp.dot(sm_ref[:, lo:lo + GATE_RANK], wg_ref[d], precision=HIGHEST,
                      preferred_element_type=F32) + bg_ref[d]
        glog = (jnp.minimum(pre, 0.0) - jnp.log(1.0 + jnp.exp(-jnp.abs(pre)))) * (1.0 / GATE_NORM)
        ghi, glo = _split(glog)
        gc = _dot(tri16, ghi) + _dot(tri16, glo)
        tot_cols = _dot_tn(ghi, ind16) + _dot_tn(glo, ind16)
        mids = spread(gc, [c * c_ + c_ // 2 - 1 for c in range(n_chunks)])
        tots = spread(gc, [c * c_ + (c_ - 1 if d == 0 else 0) for c in range(n_chunks)])
        q = q_ref[...] * DKB ** -0.5
        k = k_ref[...]
        v16 = v_ref[...].astype(BF16)
        qt = q * jnp.exp(gc - mids)
        kt16 = (k * jnp.exp(mids - gc)).astype(BF16)
        qg16s.append((q * jnp.exp(gc)).astype(BF16))
        kg16 = (k * jnp.exp(tots - gc)).astype(BF16)
        heads = []
        for h in range(HB):
            in_head = (klane >= h * DKB) & (klane < (h + 1) * DKB)
            a = _dot_nt(jnp.where(in_head, qt, 0.0).astype(BF16), kt16)
            a16 = jnp.where(incl, a, 0.0).astype(BF16)
            heads.append(_dot(a16, v16[:, h * DVB:(h + 1) * DVB]))
        o_intra.append(heads)
        xs.append([jnp.where(same_head, _dot_tn(kg16[r, :], v16[r, :]), 0.0) for r in rows])
        decays.append([jnp.exp(tot_cols[:, c * LANES:(c + 1) * LANES]) for c in range(n_chunks)])

    states = [s_scr[0], s_scr[1]]
    for step in range(n_chunks):
        for d in range(2):
            c = step if d == 0 else n_chunks - 1 - step
            r = rows[c]
            o_inter = _dot(qg16s[d][r, :], states[d].astype(BF16))
            for h in range(HB):
                vs = slice(h * DVB, (h + 1) * DVB)
                refs[d][4][r, vs] = o_inter[:, vs] + o_intra[d][h][r, :]
            states[d] = jnp.concatenate([decays[d][c]] * (vw // LANES), axis=1) * states[d] + xs[d][c]
    s_scr[0] = states[0]
    s_scr[1] = states[1]


def _gla(proj, w_gate2, b_gate, n_b, lc, l):
    t = proj.shape[0]
    n_lat = l // lc
    n_g = n_lat + 1
    n_chunks = lc // CHUNK
    kw, vw = HB * DKB, HB * DVB
    rb0 = lambda b, g: _row_block(b, g, n_b, n_lat)
    rb1 = lambda b, g: _row_block(b, _rev_group(g, n_g), n_b, n_lat)

    def specs(rb):
        return [pl.BlockSpec((lc, kw), lambda b, g: (rb(b, g), COL_B // kw)),
                pl.BlockSpec((lc, kw), lambda b, g: (rb(b, g), COL_B // kw + 1)),
                pl.BlockSpec((lc, vw), lambda b, g: (rb(b, g), (COL_B + 2 * kw) // vw)),
                pl.BlockSpec((lc, LANES), lambda b, g: (rb(b, g), COL_S // LANES))]

    return pl.pallas_call(
        functools.partial(_gla_kernel, n_chunks=n_chunks),
        grid=(n_b, n_g),
        in_specs=specs(rb0) + specs(rb1)
        + [pl.BlockSpec((2, GATE_RANK, kw), lambda b, g: (0, 0, 0)),
           pl.BlockSpec((2, 1, kw), lambda b, g: (0, 0, 0))],
        out_specs=[pl.BlockSpec((lc, vw), lambda b, g: (rb0(b, g), 0)),
                   pl.BlockSpec((lc, vw), lambda b, g: (rb1(b, g), 0))],
        out_shape=[jax.ShapeDtypeStruct((t, vw), F32)] * 2,
        scratch_shapes=[pltpu.VMEM((2, kw, vw), F32)],
        compiler_params=_cparams(2),
        name="gla_bidir",
    )(proj, proj, proj, proj, proj, proj, proj, proj, w_gate2, b_gate.reshape(2, 1, kw))


def _attn_kernel(lam_ref, gn_ref, q_ref, kc_ref, kl_ref, vc_ref, vl_ref, cq_ref, sq_ref, ck_ref, sk_ref,
                 o_ref, k_scr, v_scr, *, lc, lam_init, ctx_out):
    i = pl.program_id(2)
    lane = lax.broadcasted_iota(jnp.int32, (1, LANES), 1)
    first_half = (lane & 31) < 16

    def rope(x, cos, sin):
        swapped = jnp.where(first_half, pltpu.roll(x, LANES - 16, 1), pltpu.roll(x, 16, 1))
        return x * cos + swapped * sin

    @pl.when(i == 0)
    def _():
        k_scr[0:lc, :] = kc_ref[...].astype(BF16)
        k_scr[lc:, :] = rope(kl_ref[...], ck_ref[...], sk_ref[...]).astype(BF16)
        v_scr[0:lc, :] = vc_ref[...].astype(BF16)
        v_scr[lc:, :] = vl_ref[...].astype(BF16)

    lam = lam_ref[...]
    lam_f = (jnp.exp(jnp.sum(lam[0:1] * lam[1:2], axis=1, keepdims=True))
             - jnp.exp(jnp.sum(lam[2:3] * lam[3:4], axis=1, keepdims=True)) + lam_init)

    def attend(q, keys, vals):
        n = ATTN_SUB
        q = q * (DKC ** -0.5 * math.log2(math.e))
        subs = [q[r * n:(r + 1) * n] for r in range(q.shape[0] // n)]
        q2s = [jnp.concatenate([jnp.where(lane < DKC, x, 0.0), jnp.where(lane >= DKC, x, 0.0)],
                               axis=0).astype(BF16) for x in subs]
        ss = [_dot_nt(q2, keys) for q2 in q2s]
        es = [jnp.exp2(s - jnp.max(s, axis=-1, keepdims=True)) for s in ss]
        invs = [1.0 / jnp.sum(e, axis=-1, keepdims=True) for e in es]
        o2s = [_dot(es[r].astype(BF16), vals) * invs[r] for r in range(len(es))]
        for r, o2 in enumerate(o2s):
            o = o2[:n] - lam_f * o2[n:]
            o_ref[r * n:(r + 1) * n, :] = _rms_rows(o, gn_ref[...]) * (1.0 - lam_init)

    @pl.when(i == 0)
    def _():
        if ctx_out:
            attend(q_ref[...], k_scr[0:lc, :], v_scr[0:lc, :])
        else:
            o_ref[...] = jnp.zeros_like(o_ref)

    @pl.when(i > 0)
    def _():
        attend(rope(q_ref[...], cq_ref[...], sq_ref[...]), k_scr[...], v_scr[...])


def _attention(proj, lam, gn, rope_cos, rope_sin, n_b, lc, l, lam_init, ctx_out):
    t = proj.shape[0]
    n_lat = l // lc
    rb = lambda b, h, i: _row_block(b, i, n_b, n_lat)
    cq, ck, cv = COL_C // LANES, COL_C // LANES + HC, COL_C // LANES + 2 * HC
    tab_q = pl.BlockSpec((lc, LANES), lambda b, h, i: (jnp.maximum(i - 1, 0), 0))
    tab_k = pl.BlockSpec((l, LANES), lambda b, h, i: (0, 0))
    return pl.pallas_call(
        functools.partial(_attn_kernel, lc=lc, lam_init=lam_init, ctx_out=ctx_out),
        grid=(n_b, HC, n_lat + 1),
        in_specs=[pl.BlockSpec((4, DKC), lambda b, h, i: (0, 0)),
                  pl.BlockSpec((1, DVC), lambda b, h, i: (0, 0)),
                  pl.BlockSpec((lc, LANES), lambda b, h, i: (rb(b, h, i), cq + h)),
                  pl.BlockSpec((lc, LANES), lambda b, h, i: (b, ck + h)),
                  pl.BlockSpec((l, LANES), lambda b, h, i: (b + 1, ck + h)),
                  pl.BlockSpec((lc, LANES), lambda b, h, i: (b, cv + h)),
                  pl.BlockSpec((l, LANES), lambda b, h, i: (b + 1, cv + h)),
                  tab_q, tab_q, tab_k, tab_k],
        out_specs=pl.BlockSpec((lc, LANES), lambda b, h, i: (rb(b, h, i), h)),
        out_shape=jax.ShapeDtypeStruct((t, HC * DVC), F32),
        scratch_shapes=[pltpu.VMEM((lc + l, LANES), BF16), pltpu.VMEM((lc + l, LANES), BF16)],
        compiler_params=_cparams(3),
        name="diff_attention",
    )(lam, gn.reshape(1, DVC), proj, proj, proj, proj, proj, rope_cos, rope_sin, rope_cos, rope_sin)


def _rope_tables(n_tokens):
    rows = n_tokens // GRID_W
    row = jnp.repeat(jnp.arange(rows, dtype=F32), GRID_W)
    col = jnp.tile(jnp.arange(GRID_W, dtype=F32), rows)
    n_freq = DKC // 4
    inv_freq = ROPE_THETA ** (-jnp.arange(n_freq, dtype=F32) / n_freq)
    ang_r = row[:, None] * inv_freq
    ang_c = col[:, None] * inv_freq
    cos = jnp.concatenate([jnp.cos(ang_r)] * 2 + [jnp.cos(ang_c)] * 2, axis=1)
    sin = jnp.concatenate([-jnp.sin(ang_r), jnp.sin(ang_r), -jnp.sin(ang_c), jnp.sin(ang_c)], axis=1)
    return jnp.concatenate([cos, cos], axis=1), jnp.concatenate([sin, sin], axis=1)


def _head_rms(o, gain, n_heads, width):
    parts = []
    for h in range(n_heads):
        parts.append(_rms_rows(o[:, h * width:(h + 1) * width], gain))
    return jnp.concatenate(parts, axis=1)


def _merge_kernel(oa0, oa1, z_ref, ob0, ob1, r_ref, od_ref, ga_ref, gb_ref, gd_ref, x_ref, mod_ref,
                  gna_ref, gnb_ref, woa_ref, wob_ref, woc_ref, wout_ref, o_ref):
    ya = _head_rms(oa0[...] + oa1[...], gna_ref[...], HA, DVA) * _silu(z_ref[...])
    yb = _head_rms(ob0[...] + ob1[...], gnb_ref[...], HB, DVB) * _silu(r_ref[...])
    acc = _sigmoid(ga_ref[...]) * _dot(ya.astype(BF16), woa_ref[...])
    acc = acc + _sigmoid(gb_ref[...]) * _dot(yb.astype(BF16), wob_ref[...])
    acc = acc + _sigmoid(gd_ref[...]) * _dot(od_ref[...].astype(BF16), woc_ref[...])
    y = _dot(acc.astype(BF16), wout_ref[...])
    o_ref[...] = x_ref[...] + mod_ref[0, 2:3, :] * y


def _merge(oa0, oa1, ob0, ob1, od, proj, x, mod, gn_a, gn_b, woa, wob, woc, wout, rows_per_mod, row_off, tm):
    t, d = x.shape
    off = row_off // tm
    n_tiles = (t - row_off) // tm
    per = rows_per_mod // tm
    w5 = HA * DVA
    row = lambda i: (i + off, 0)
    col = lambda c: (lambda i: (i + off, c))
    return pl.pallas_call(
        _merge_kernel,
        grid=(n_tiles,),
        in_specs=[pl.BlockSpec((tm, w5), row),
                  pl.BlockSpec((tm, w5), row),
                  pl.BlockSpec((tm, w5), col((COL_A + 3 * w5) // w5)),
                  pl.BlockSpec((tm, w5), row),
                  pl.BlockSpec((tm, w5), row),
                  pl.BlockSpec((tm, w5), col((COL_B + 2 * HB * DKB + w5) // w5)),
                  pl.BlockSpec((tm, w5), row),
                  pl.BlockSpec((tm, d), col(COL_G // d)),
                  pl.BlockSpec((tm, d), col(COL_G // d + 1)),
                  pl.BlockSpec((tm, d), col(COL_G // d + 2)),
                  pl.BlockSpec((tm, d), row),
                  pl.BlockSpec((1, 6, d), lambda i: ((i + off) // per, 0, 0)),
                  pl.BlockSpec((1, DVA), lambda i: (0, 0)),
                  pl.BlockSpec((1, DVB), lambda i: (0, 0)),
                  pl.BlockSpec((w5, d), lambda i: (0, 0)),
                  pl.BlockSpec((w5, d), lambda i: (0, 0)),
                  pl.BlockSpec((w5, d), lambda i: (0, 0)),
                  pl.BlockSpec((d, d), lambda i: (0, 0))],
        out_specs=pl.BlockSpec((tm, d), lambda i: (i, 0)),
        out_shape=jax.ShapeDtypeStruct((t - row_off, d), F32),
        compiler_params=_cparams(1),
        name="merge_outproj",
    )(oa0, oa1, proj, ob0, ob1, proj, od, proj, proj, proj, x, mod,
      gn_a.reshape(1, DVA), gn_b.reshape(1, DVB), woa, wob, woc, wout)


def _ffn_kernel(x_ref, mod_ref, g_ref, fg_ref, w1_ref, w3_ref, w2_ref, o_ref, h_scr, *, final):
    f = pl.program_id(1)

    @pl.when(f == 0)
    def _():
        y = _rms_rows(x_ref[...], g_ref[...])
        h_scr[...] = (y * (1.0 + mod_ref[0, 4:5, :]) + mod_ref[0, 3:4, :]).astype(BF16)
        o_ref[...] = jnp.zeros_like(o_ref)

    h = h_scr[...]
    t = _silu(_dot(h, w1_ref[...])) * _dot(h, w3_ref[...])
    o_ref[...] += _dot(t.astype(BF16), w2_ref[...])

    @pl.when(f == pl.num_programs(1) - 1)
    def _():
        out = x_ref[...] + mod_ref[0, 5:6, :] * o_ref[...]
        if final:
            out = _rms_rows(out, fg_ref[...])
        o_ref[...] = out


def _ffn(x, mod, g, final_g, w1, w3, w2, rows_per_mod, mod_off, tm, tf, final):
    t, d = x.shape
    dff = w1.shape[1]
    per = rows_per_mod // tm
    return pl.pallas_call(
        functools.partial(_ffn_kernel, final=final),
        grid=(t // tm, dff // tf),
        in_specs=[pl.BlockSpec((tm, d), lambda i, f: (i, 0)),
                  pl.BlockSpec((1, 6, d), lambda i, f: (i // per + mod_off, 0, 0)),
                  pl.BlockSpec((1, d), lambda i, f: (0, 0)),
                  pl.BlockSpec((1, d), lambda i, f: (0, 0)),
                  pl.BlockSpec((d, tf), lambda i, f: (0, f)),
                  pl.BlockSpec((d, tf), lambda i, f: (0, f)),
                  pl.BlockSpec((tf, d), lambda i, f: (f, 0))],
        out_specs=pl.BlockSpec((tm, d), lambda i, f: (i, 0)),
        out_shape=jax.ShapeDtypeStruct((t, d), F32),
        scratch_shapes=[pltpu.VMEM((tm, d), BF16)],
        compiler_params=_cparams(2),
        name="dense_ffn",
    )(x, mod, g.reshape(1, d), final_g.reshape(1, d), w1, w3, w2)


MOE_ROWS = 256


def _moe_kernel(x_ref, mod_ref, g_ref, rw_ref, fg_ref, w1_ref, w3_ref, w2_ref, o_ref,
                h_scr, gate_scr, sel_scr, rank_scr, selr_scr, rankr_scr, xg_scr, yg_scr, nb_scr, *, final):
    e = pl.program_id(1)
    f = pl.program_id(2)
    n_f = pl.num_programs(2)
    tm, d = x_ref.shape
    cb = MOE_ROWS
    lane = lax.broadcasted_iota(jnp.int32, (1, LANES), 1)

    @pl.when((e == 0) & (f == 0))
    def _():
        y = _rms_rows(x_ref[...], g_ref[...])
        hmod = y * (1.0 + mod_ref[0, 4:5, :]) + mod_ref[0, 3:4, :]
        h_scr[...] = hmod.astype(BF16)
        o_ref[...] = jnp.zeros_like(o_ref)
        logits = jnp.dot(hmod, rw_ref[...], precision=HIGHEST, preferred_element_type=F32)
        logits = jnp.where(lane < N_EXPERTS, logits, -jnp.inf)
        lanef = lane.astype(F32)
        m1 = jnp.max(logits, axis=1, keepdims=True)
        i1 = jnp.min(jnp.where(logits == m1, lanef, float(LANES)), axis=1, keepdims=True)
        hit1 = lanef == i1
        rest = jnp.where(hit1, -jnp.inf, logits)
        m2 = jnp.max(rest, axis=1, keepdims=True)
        i2 = jnp.min(jnp.where(rest == m2, lanef, float(LANES)), axis=1, keepdims=True)
        hit2 = lanef == i2
        ex = jnp.exp(m2 - m1)
        inv = 1.0 / (1.0 + ex)
        gate_scr[...] = jnp.where(hit1, inv, 0.0) + jnp.where(hit2, ex * inv, 0.0)
        sel = jnp.where(hit1 | hit2, 1.0, 0.0)
        sel_scr[...] = sel
        sel16 = sel.astype(BF16)
        pick = jnp.where(lax.broadcasted_iota(jnp.int32, (8, LANES), 0)
                         == lax.broadcasted_iota(jnp.int32, (8, LANES), 1), 1.0, 0.0).astype(BF16)
        sel_r = _dot_nt(pick, sel16)
        selr_scr[...] = sel_r
        sel_r16 = sel_r.astype(BF16)
        for blk in range(tm // LANES):
            lo = blk * LANES
            before = (lax.broadcasted_iota(jnp.int32, (LANES, tm), 1)
                      < lax.broadcasted_iota(jnp.int32, (LANES, tm), 0) + lo)
            rank_scr[lo:lo + LANES, :] = _dot(jnp.where(before, 1.0, 0.0).astype(BF16), sel16)
            before_t = (lax.broadcasted_iota(jnp.int32, (tm, LANES), 0)
                        < lax.broadcasted_iota(jnp.int32, (tm, LANES), 1) + lo)
            rankr_scr[:, lo:lo + LANES] = _dot(sel_r16, jnp.where(before_t, 1.0, 0.0).astype(BF16))

    @pl.when(f == 0)
    def _():
        sel_row = selr_scr[pl.ds(e, 1), :]
        rank_row = rankr_scr[pl.ds(e, 1), :]
        n_e = jnp.sum(sel_row).astype(jnp.int32)
        nb = (n_e + (cb - 1)) // cb
        nb_scr[0] = nb
        slot = lax.broadcasted_iota(jnp.int32, (cb, tm), 0).astype(F32)

        def gather(j, carry):
            r0 = pl.multiple_of(j * cb, cb)
            onehot = jnp.where((rank_row == slot + (j * cb).astype(F32)) & (sel_row > 0.0), 1.0, 0.0)
            xg_scr[pl.ds(r0, cb), :] = _dot(onehot.astype(BF16), h_scr[...]).astype(BF16)
            yg_scr[pl.ds(r0, cb), :] = jnp.zeros((cb, d), F32)
            return carry

        lax.fori_loop(0, nb, gather, 0)

    nb = nb_scr[0]

    def expert(j, carry):
        r0 = pl.multiple_of(j * cb, cb)
        xb = xg_scr[pl.ds(r0, cb), :]
        t = _silu(_dot(xb, w1_ref[0])) * _dot(xb, w3_ref[0])
        yg_scr[pl.ds(r0, cb), :] += _dot(t.astype(BF16), w2_ref[0])
        return carry

    lax.fori_loop(0, nb, expert, 0)

    @pl.when(f == n_f - 1)
    def _():
        pick_e = lane == e
        rank_col = jnp.sum(jnp.where(pick_e, rank_scr[...], 0.0), axis=1, keepdims=True)
        sel_col = jnp.sum(jnp.where(pick_e, sel_scr[...], 0.0), axis=1, keepdims=True)
        gate_col = jnp.sum(jnp.where(pick_e, gate_scr[...], 0.0), axis=1, keepdims=True)
        slot = lax.broadcasted_iota(jnp.int32, (tm, cb), 1).astype(F32)

        def scatter(j, carry):
            r0 = pl.multiple_of(j * cb, cb)
            onehot = jnp.where((rank_col == slot + (j * cb).astype(F32)) & (sel_col > 0.0), 1.0, 0.0)
            o_ref[...] += gate_col * _dot(onehot.astype(BF16), yg_scr[pl.ds(r0, cb), :].astype(BF16))
            return carry

        lax.fori_loop(0, nb, scatter, 0)

    @pl.when((e == pl.num_programs(1) - 1) & (f == n_f - 1))
    def _():
        out = x_ref[...] + mod_ref[0, 5:6, :] * o_ref[...]
        if final:
            out = _rms_rows(out, fg_ref[...])
        o_ref[...] = out


def _moe(x, mod, g, router_w, final_g, w1, w3, w2, rows_per_mod, mod_off, tm, tf, final):
    t, d = x.shape
    n_e, _, dff = w1.shape
    per = rows_per_mod // tm
    return pl.pallas_call(
        functools.partial(_moe_kernel, final=final),
        grid=(t // tm, n_e, dff // tf),
        in_specs=[pl.BlockSpec((tm, d), lambda i, e, f: (i, 0)),
                  pl.BlockSpec((1, 6, d), lambda i, e, f: (i // per + mod_off, 0, 0)),
                  pl.BlockSpec((1, d), lambda i, e, f: (0, 0)),
                  pl.BlockSpec((d, LANES), lambda i, e, f: (0, 0)),
                  pl.BlockSpec((1, d), lambda i, e, f: (0, 0)),
                  pl.BlockSpec((1, d, tf), lambda i, e, f: (e, 0, f)),
                  pl.BlockSpec((1, d, tf), lambda i, e, f: (e, 0, f)),
                  pl.BlockSpec((1, tf, d), lambda i, e, f: (e, f, 0))],
        out_specs=pl.BlockSpec((tm, d), lambda i, e, f: (i, 0)),
        out_shape=jax.ShapeDtypeStruct((t, d), F32),
        scratch_shapes=[pltpu.VMEM((tm, d), BF16),
                        pltpu.VMEM((tm, LANES), F32),
                        pltpu.VMEM((tm, LANES), F32),
                        pltpu.VMEM((tm, LANES), F32),
                        pltpu.VMEM((8, tm), F32),
                        pltpu.VMEM((8, tm), F32),
                        pltpu.VMEM((tm, d), BF16),
                        pltpu.VMEM((tm, d), F32),
                        pltpu.SMEM((1,), jnp.int32)],
        compiler_params=_cparams(3),
        name="routed_ffn_sparse",
    )(x, mod, g.reshape(1, d), router_w, final_g.reshape(1, d), w1, w3, w2)


def _reorder_w_in(w):
    d = w.shape[0]
    a_end = 4 * HA * DKA
    b_start = a_end + 4 * HA
    b_end = b_start + 2 * HB * DKB + 2 * HB * DVB
    c_start = b_end + 2 * GATE_RANK
    pad = N_COLS - COL_S - 4 * HA - 2 * GATE_RANK
    return jnp.concatenate([w[:, :a_end], w[:, b_start:b_end], w[:, c_start:], w[:, a_end:b_start],
                            w[:, b_end:c_start], jnp.zeros((d, pad), w.dtype)], axis=1).astype(BF16)


def kernel(x, c, ctx, c_ctx, w_mod, b_mod, norm1_g, norm2_g, w_in, conv_a, a_log, dt_bias, gn_a, w_gate2, b_gate, gn_b, lam_c, gn_c, w_o_a, w_o_b, w_o_c, w_out, ffn_w1, ffn_w3, ffn_w2, router_w, moe_w1, moe_w3, moe_w2, final_g):
    n_b, l, d = x.shape
    lc = ctx.shape[1]
    depth = w_mod.shape[0]
    assert n_b * lc == l and lc % CHUNK == 0 and l & (l - 1) == 0 and lc & (lc - 1) == 0
    t_ctx = n_b * lc
    tm = min(1024, l)
    tm_small = min(512, l)

    n_mod = 16
    cvec = jnp.concatenate([c_ctx[None, :], c, jnp.zeros((n_mod - 1 - n_b, d), F32)], axis=0)
    mod_all = _modulation(cvec, w_mod, b_mod).reshape(depth, n_mod, 6, d)

    rope_cos, rope_sin = _rope_tables(l)
    xs = jnp.concatenate([ctx.reshape(t_ctx, d), x.reshape(n_b * l, d)], axis=0)

    for layer in range(depth):
        ctx_out = layer < depth - 1
        lam_init = 0.8 - 0.6 * math.exp(-0.3 * layer)
        mod = mod_all[layer]
        proj = _in_projection(xs, mod, norm1_g[layer], _reorder_w_in(w_in[layer]), l, tm, 1664)

        qkvc = _short_conv(proj, conv_a[layer], lc, l)
        small = proj[:, COL_S:COL_S + 4 * HA].reshape(-1, 4, HA)
        ba_h = small.transpose(2, 0, 1)
        gt_h = small.transpose(2, 1, 0)
        oa0, oa1 = _dn_scan(_dn_factors(qkvc, ba_h, gt_h, a_log[layer], dt_bias[layer], n_b, lc, l), n_b, lc, l)
        ob0, ob1 = _gla(proj, w_gate2[layer], b_gate[layer], n_b, lc, l)
        od = _attention(proj, lam_c[layer], gn_c[layer], rope_cos, rope_sin, n_b, lc, l, lam_init, ctx_out)

        row_off = 0 if ctx_out else t_ctx
        xs_new = _merge(oa0, oa1, ob0, ob1, od, proj, xs, mod, gn_a[layer], gn_b[layer],
                        w_o_a[layer].astype(BF16), w_o_b[layer].astype(BF16), w_o_c[layer].astype(BF16),
                        w_out[layer].astype(BF16), l, row_off, tm_small)

        i = layer // 2
        final = layer == depth - 1
        mod_off = 0 if ctx_out else 1
        if layer % 2 == 0:
            xs = _ffn(xs_new, mod, norm2_g[layer], final_g, ffn_w1[i].astype(BF16),
                      ffn_w3[i].astype(BF16), ffn_w2[i].astype(BF16), l, mod_off, tm_small,
                      ffn_w1.shape[2] // 2, final)
        else:
            rw = jnp.pad(router_w[i], ((0, 0), (0, LANES - N_EXPERTS)))
            xs = _moe(xs_new, mod, norm2_g[layer], rw, final_g, moe_w1[i].astype(BF16),
                      moe_w3[i].astype(BF16), moe_w2[i].astype(BF16), l, mod_off, tm,
                      moe_w1.shape[3] // 2, final)

    return xs.reshape(n_b, l, d)
```

```python
import functools
import math

import jax
import jax.numpy as jnp
from jax import lax
from jax.experimental import pallas as pl
from jax.experimental.pallas import tpu as pltpu

F32 = jnp.float32
BF16 = jnp.bfloat16
HIGHEST = lax.Precision.HIGHEST

EPS = 1e-6
HA, DKA, DVA = 4, 128, 128
HB, DKB, DVB = 4, 64, 128
GATE_RANK, GATE_NORM = 16, 16.0
HC, DKC, DVC = 4, 64, 128
GRID_W, ROPE_THETA = 64, 10000.0
N_EXPERTS, TOP_K = 8, 2
CHUNK = 64
LANES = 128
ATTN_SUB = 64
DN_HEADS = 4

F_QKV = 0
F_SMALL = 1536
N_F32 = 1664
H_GATES = 0
H_AZ = 3072
H_BV = 3584
H_BR = 4096
H_CQ = 4608
H_CK = 5120
H_CV = 5632
H_BQ = 6144
H_BK = 6400
N_B16 = 6656
PROJ_TN = 1664

VMEM_LIMIT = 56 * 1024 * 1024


def _cparams(n_axes):
    return pltpu.CompilerParams(dimension_semantics=("arbitrary",) * n_axes,
                                vmem_limit_bytes=VMEM_LIMIT)


def _sigmoid(x):
    return 1.0 / (1.0 + jnp.exp(-x))


def _silu(x):
    return x * _sigmoid(x)


def _softplus(x):
    return jnp.maximum(x, 0.0) + jnp.log(1.0 + jnp.exp(-jnp.abs(x)))


def _dot(a, b):
    return jnp.dot(a, b, preferred_element_type=F32)


def _dot_nt(a, b):
    return lax.dot_general(a, b, (((1,), (1,)), ((), ())), preferred_element_type=F32)


def _dot_tn(a, b):
    return lax.dot_general(a, b, (((0,), (0,)), ((), ())), preferred_element_type=F32)


def _split(a):
    hi = a.astype(BF16)
    lo = (a - hi.astype(F32)).astype(BF16)
    return hi, lo


def _dot3(a, b):
    ah, al = _split(a)
    bh, bl = _split(b)
    return _dot(ah, bh) + (_dot(ah, bl) + _dot(al, bh))


def _rms_rows(x, gain):
    ms = jnp.mean(x * x, axis=-1, keepdims=True)
    return x * lax.rsqrt(ms + EPS) * gain


def _mod_kernel(c_ref, w_ref, b_ref, o_ref):
    s = _silu(c_ref[...])
    o_ref[0] = jnp.dot(s, w_ref[0], precision=HIGHEST, preferred_element_type=F32) + b_ref[0]


def _modulation(cvec, w_mod, b_mod):
    depth, d, d6 = w_mod.shape
    n = cvec.shape[0]
    return pl.pallas_call(
        _mod_kernel,
        grid=(depth, d6 // d),
        in_specs=[pl.BlockSpec((n, d), lambda l, j: (0, 0)),
                  pl.BlockSpec((1, d, d), lambda l, j: (l, 0, j)),
                  pl.BlockSpec((1, 1, d), lambda l, j: (l, 0, j))],
        out_specs=pl.BlockSpec((1, n, d), lambda l, j: (l, 0, j)),
        out_shape=jax.ShapeDtypeStruct((depth, n, d6), F32),
        compiler_params=_cparams(2),
        name="adaln_mod",
    )(cvec, w_mod, b_mod.reshape(depth, 1, d6))


def _token_specs(tm, d, ctx_tiles, lat_first, off=0):
    return [pl.BlockSpec((tm, d), lambda i, *_: (jnp.minimum(i + off, ctx_tiles - 1), 0)),
            pl.BlockSpec((tm, d), lambda i, *_: (jnp.maximum(i + off - ctx_tiles, 0) + lat_first, 0))]


def _inproj_kernel(xc_ref, xl_ref, mod_ref, g_ref, w_ref, o32_ref, o16_ref, h_scr, *, ctx_tiles):
    j = pl.program_id(1)

    @pl.when(j == 0)
    def _():
        x = jnp.where(pl.program_id(0) < ctx_tiles, xc_ref[...], xl_ref[...])
        y = _rms_rows(x, g_ref[...])
        h_scr[...] = (y * (1.0 + mod_ref[0, 1:2, :]) + mod_ref[0, 0:1, :]).astype(BF16)
        o32_ref[...] = _dot(h_scr[...], w_ref[...])

    @pl.when(j > 0)
    def _():
        o16_ref[...] = _dot(h_scr[...], w_ref[...]).astype(BF16)


def _in_projection(x_ctx, x_lat, lat_first, t, mod, g, w, rows_per_mod, tm):
    d = x_ctx.shape[1]
    tn = PROJ_TN
    assert w.shape[1] == N_F32 + N_B16 and N_F32 == tn and N_B16 % tn == 0
    per = rows_per_mod // tm
    ctx_tiles = rows_per_mod // tm
    return pl.pallas_call(
        functools.partial(_inproj_kernel, ctx_tiles=ctx_tiles),
        grid=(t // tm, 1 + N_B16 // tn),
        in_specs=_token_specs(tm, d, ctx_tiles, lat_first // tm)
        + [pl.BlockSpec((1, 6, d), lambda i, j: (i // per, 0, 0)),
                  pl.BlockSpec((1, d), lambda i, j: (0, 0)),
                  pl.BlockSpec((d, tn), lambda i, j: (0, j))],
        out_specs=[pl.BlockSpec((tm, tn), lambda i, j: (i, 0)),
                   pl.BlockSpec((tm, tn), lambda i, j: (i, jnp.maximum(j - 1, 0)))],
        out_shape=[jax.ShapeDtypeStruct((t, N_F32), F32), jax.ShapeDtypeStruct((t, N_B16), BF16)],
        scratch_shapes=[pltpu.VMEM((tm, d), BF16)],
        compiler_params=_cparams(2),
        name="norm_mod_inproj",
    )(x_ctx, x_lat, mod, g.reshape(1, d), w)


def _conv_kernel(x_ref, w_ref, o_ref, *, lc, l):
    p = pl.program_id(0)
    j = pl.program_id(1)
    x = x_ref[...]
    n = x.shape[0]
    row = lax.broadcasted_iota(jnp.int32, x.shape, 0)
    seq = jnp.where(p == 0, lc, l)
    pos = row & (seq - 1)
    prev = jnp.where(pos == 0, 0.0, pltpu.roll(x, 1, 0))
    nxt = jnp.where(pos == seq - 1, 0.0, pltpu.roll(x, n - 1, 0))
    w = w_ref[...]
    y = _silu(prev * w[0:1] + x * w[1:2] + nxt * w[2:3])
    inv = lax.rsqrt(jnp.sum(y * y, axis=-1, keepdims=True) + EPS)
    fac = jnp.where(j < HA, inv * DKA ** -0.5, jnp.where(j < 2 * HA, inv, 1.0))
    o_ref[...] = y * fac


def _short_conv(proj, conv_w, lc, l):
    t = proj.shape[0]
    ncol = conv_w.shape[1]
    return pl.pallas_call(
        functools.partial(_conv_kernel, lc=lc, l=l),
        grid=(t // l, ncol // LANES),
        in_specs=[pl.BlockSpec((l, LANES), lambda p, j: (p, j)),
                  pl.BlockSpec((conv_w.shape[0], LANES), lambda p, j: (0, j))],
        out_specs=pl.BlockSpec((l, LANES), lambda p, j: (p, j)),
        out_shape=jax.ShapeDtypeStruct((t, ncol), F32),
        compiler_params=_cparams(2),
        name="conv_silu_l2",
    )(proj, conv_w)


def _row_block(b, g, n_b, n_lat):
    return jnp.where(g == 0, b, n_b + b * n_lat + g - 1)


def _rev_group(g, n_g):
    return jnp.where(g == 0, 0, n_g - g)


def _tri_inverse_many(lms, ii, jj, uppers):
    c = lms[0].shape[0]
    eye = (ii == jj).astype(F32)
    n = len(lms)
    ts = None
    s, ls = 1, 0
    while s < c:
        same = (ii >> (ls + 1)) == (jj >> (ls + 1))
        io = (ii >> ls) & 1
        jo = (jj >> ls) & 1
        lower = same & (io == 1) & (jo == 0)
        upper = same & (io == 0) & (jo == 1)
        cms = [jnp.where(upper if uppers[i] else lower, lms[i], 0.0) for i in range(n)]
        if ts is None:
            ts = [eye - cm for cm in cms]
        else:
            t16 = [t.astype(BF16) for t in ts]
            ps = [_dot(t16[i], cms[i].astype(BF16)).astype(BF16) for i in range(n)]
            ts = [ts[i] - _dot(ps[i], t16[i]) for i in range(n)]
        s, ls = s * 2, ls + 1
    a_parts = [_split(eye + lm) for lm in lms]
    t_parts = [_split(t) for t in ts]
    res = [eye - (_dot(a_parts[i][0], t_parts[i][0])
                  + (_dot(a_parts[i][0], t_parts[i][1]) + _dot(a_parts[i][1], t_parts[i][0])))
           for i in range(n)]
    return [ts[i] + _dot(t_parts[i][0], res[i].astype(BF16)) for i in range(n)]


def _dn1_kernel(alog_ref, dtb_ref, q_ref, k_ref, v_ref, sm_ref,
                u_ref, w_ref, kg_ref, qg_ref, qk_ref, gl_ref, *, n_chunks):
    hp = pl.program_id(2)
    c_ = CHUNK
    sm = sm_ref[...]
    sm_t = sm.T
    ii = lax.broadcasted_iota(jnp.int32, (c_, c_), 0)
    jj = lax.broadcasted_iota(jnp.int32, (c_, c_), 1)
    rows = [slice(c * c_, (c + 1) * c_) for c in range(n_chunks)]
    lanes = [slice(j * LANES, (j + 1) * LANES) for j in range(DN_HEADS)]
    blocks = [(j, c) for j in range(DN_HEADS) for c in range(n_chunks)]
    qs = {jc: q_ref[rows[jc[1]], lanes[jc[0]]] for jc in blocks}
    ks = {jc: k_ref[rows[jc[1]], lanes[jc[0]]] for jc in blocks}
    vs = {jc: v_ref[rows[jc[1]], lanes[jc[0]]] for jc in blocks}
    k16 = {jc: ks[jc].astype(BF16) for jc in blocks}
    kks = {jc: _dot_nt(k16[jc], k16[jc]) for jc in blocks}
    qks = {jc: _dot_nt(qs[jc].astype(BF16), k16[jc]) for jc in blocks}

    items = [(j, c, d) for j, c in blocks for d in range(2)]
    betas, gcs, totals, gams, lms = [], [], [], [], []
    for j, c, d in items:
        r = rows[c]
        h = hp * DN_HEADS + j
        incl = (jj <= ii) if d == 0 else (jj >= ii)
        strict = (jj < ii) if d == 0 else (jj > ii)
        incl_t = (ii <= jj) if d == 0 else (ii >= jj)
        a_scale = -jnp.exp(jnp.full((1, 1), alog_ref[d * HA + h], F32))
        dtb = dtb_ref[d * HA + h]
        col = d * HA + j
        beta = _sigmoid(sm[r, col:col + 1])
        g_col = a_scale * _softplus(sm[r, 2 * HA + col:2 * HA + col + 1] + dtb)
        g_row = a_scale * _softplus(sm_t[2 * HA + col:2 * HA + col + 1, r] + dtb)
        gc_col = jnp.sum(jnp.where(incl, g_row, 0.0), axis=1, keepdims=True)
        gc_row = jnp.sum(jnp.where(incl_t, g_col, 0.0), axis=0, keepdims=True)
        gam = jnp.where(incl, jnp.exp(gc_col - gc_row), 0.0)
        betas.append(beta)
        gcs.append(gc_col)
        totals.append(jnp.sum(g_col, axis=0, keepdims=True))
        gams.append(gam)
        lms.append(jnp.where(strict, beta * kks[(j, c)] * gam, 0.0))

    tinvs = _tri_inverse_many(lms, ii, jj, [d == 1 for _, _, d in items])
    egs = [jnp.exp(gc) for gc in gcs]
    uws = []
    for i, (j, c, d) in enumerate(items):
        rhs = jnp.concatenate([vs[(j, c)] * betas[i], ks[(j, c)] * (betas[i] * egs[i])], axis=1).astype(BF16)
        uws.append(_dot(tinvs[i].astype(BF16), rhs))
    for i, (j, c, d) in enumerate(items):
        r = rows[c]
        u_ref[d, r, lanes[j]] = uws[i][:, :DVA]
        w_ref[d, r, lanes[j]] = uws[i][:, DVA:].astype(BF16)
        kg_ref[d, r, lanes[j]] = (ks[(j, c)] * jnp.exp(totals[i] - gcs[i])).astype(BF16)
        qg_ref[d, r, lanes[j]] = (qs[(j, c)] * egs[i]).astype(BF16)
        qk_ref[d, j, r, :] = (qks[(j, c)] * gams[i]).astype(BF16)
        gl_ref[0, 0, j, c * 2 + d:c * 2 + d + 1, :] = jnp.broadcast_to(jnp.exp(totals[i]), (1, LANES))


def _dn_factors(qkvc, proj32, a_log, dt_bias, n_b, lc, l):
    t = qkvc.shape[0]
    n_lat = l // lc
    n_g = n_lat + 1
    n_chunks = lc // CHUNK
    assert DN_HEADS == HA
    nh = DN_HEADS
    wide = nh * LANES
    steps = HA // nh
    rb = lambda b, g, h: _row_block(b, g, n_b, n_lat)
    tok = lambda dt: jax.ShapeDtypeStruct((2, t, HA * DVA), dt)
    smem = pl.BlockSpec(memory_space=pltpu.SMEM)
    return pl.pallas_call(
        functools.partial(_dn1_kernel, n_chunks=n_chunks),
        grid=(n_b, n_g, steps),
        in_specs=[smem, smem,
                  pl.BlockSpec((lc, wide), lambda b, g, h: (rb(b, g, h), h)),
                  pl.BlockSpec((lc, wide), lambda b, g, h: (rb(b, g, h), steps + h)),
                  pl.BlockSpec((lc, wide), lambda b, g, h: (rb(b, g, h), 2 * steps + h)),
                  pl.BlockSpec((lc, LANES), lambda b, g, h: (rb(b, g, h), F_SMALL // LANES))],
        out_specs=[pl.BlockSpec((2, lc, wide), lambda b, g, h: (0, rb(b, g, h), h))] * 4
        + [pl.BlockSpec((2, nh, lc, CHUNK), lambda b, g, h: (0, h, rb(b, g, h), 0)),
           pl.BlockSpec((1, 1, nh, 2 * n_chunks, LANES), lambda b, g, h: (b, g, h, 0, 0))],
        out_shape=[tok(F32), tok(BF16), tok(BF16), tok(BF16),
                   jax.ShapeDtypeStruct((2, HA, t, CHUNK), BF16),
                   jax.ShapeDtypeStruct((n_b, n_g, HA, 2 * n_chunks, LANES), F32)],
        compiler_params=_cparams(3),
        name="deltanet_chunk_factors",
    )(a_log.reshape(-1), dt_bias.reshape(-1), qkvc, qkvc, qkvc, proj32)


def _dn2_kernel(u0, w0, kg0, qg0, qk0, gl0, u1, w1, kg1, qg1, qk1, gl1, o0_ref, o1_ref, s_scr,
                *, n_chunks):
    @pl.when(pl.program_id(1) == 0)
    def _():
        s_scr[...] = jnp.zeros_like(s_scr)

    c_ = CHUNK
    refs = ((u0, w0, kg0, qg0, qk0, gl0, o0_ref), (u1, w1, kg1, qg1, qk1, gl1, o1_ref))
    chains = [(h, d) for h in range(HA) for d in range(2)]
    states = [s_scr[d * HA + h] for h, d in chains]
    for step in range(n_chunks):
        pos = []
        for h, d in chains:
            c = step if d == 0 else n_chunks - 1 - step
            pos.append((slice(c * c_, (c + 1) * c_), slice(h * DVA, (h + 1) * DVA), c))
        s16 = [s.astype(BF16) for s in states]
        prods = []
        for i, (h, d) in enumerate(chains):
            r, hs, _ = pos[i]
            wq = jnp.concatenate([refs[d][1][0, r, hs], refs[d][3][0, r, hs]], axis=0)
            prods.append(_dot(wq, s16[i]))
        vn16 = [(refs[d][0][0, pos[i][0], pos[i][1]] - prods[i][:c_]).astype(BF16)
                for i, (h, d) in enumerate(chains)]
        for i, (h, d) in enumerate(chains):
            r, hs, c = pos[i]
            u, w, kg, qg, qk, gl, o_ref = refs[d]
            o_ref[r, hs] = (prods[i][c_:] + _dot(qk[0, h, r, :], vn16[i])).astype(BF16)
            states[i] = (gl[0, 0, h, c * 2 + d:c * 2 + d + 1, :] * states[i]
                         + _dot_tn(kg[0, r, hs], vn16[i]))
    for i, (h, d) in enumerate(chains):
        s_scr[d * HA + h] = states[i]


def _dn_scan(factors, n_b, lc, l):
    u, w, kg, qg, qk, gl = factors
    t = u.shape[1]
    n_lat = l // lc
    n_g = n_lat + 1
    n_chunks = lc // CHUNK
    width = HA * DVA
    rb0 = lambda b, g: _row_block(b, g, n_b, n_lat)
    rb1 = lambda b, g: _row_block(b, _rev_group(g, n_g), n_b, n_lat)

    def specs(d, rb, gsel):
        tok = pl.BlockSpec((1, lc, width), lambda b, g: (d, rb(b, g), 0))
        return [tok, tok, tok, tok,
                pl.BlockSpec((1, HA, lc, CHUNK), lambda b, g: (d, 0, rb(b, g), 0)),
                pl.BlockSpec((1, 1, HA, 2 * n_chunks, LANES), lambda b, g: (b, gsel(g), 0, 0, 0))]

    return pl.pallas_call(
        functools.partial(_dn2_kernel, n_chunks=n_chunks),
        grid=(n_b, n_g),
        in_specs=specs(0, rb0, lambda g: g) + specs(1, rb1, lambda g: _rev_group(g, n_g)),
        out_specs=[pl.BlockSpec((lc, width), lambda b, g: (rb0(b, g), 0)),
                   pl.BlockSpec((lc, width), lambda b, g: (rb1(b, g), 0))],
        out_shape=[jax.ShapeDtypeStruct((t, width), BF16)] * 2,
        scratch_shapes=[pltpu.VMEM((2 * HA, DKA, DVA), F32)],
        compiler_params=_cparams(2),
        name="deltanet_scan",
    )(u, w, kg, qg, qk, gl, u, w, kg, qg, qk, gl)


def _gla_kernel(q0, k0, v0, sm0, q1, k1, v1, sm1, wg_ref, bg_ref, o0_ref, o1_ref, s_scr, *, n_chunks):
    @pl.when(pl.program_id(1) == 0)
    def _():
        s_scr[...] = jnp.zeros_like(s_scr)

    c_ = CHUNK
    sh = c_.bit_length() - 1
    g_ = n_chunks * c_
    kw = HB * DKB
    vw = HB * DVB
    ii = lax.broadcasted_iota(jnp.int32, (g_, g_), 0)
    jj = lax.broadcasted_iota(jnp.int32, (g_, g_), 1)
    same_chunk = (ii >> sh) == (jj >> sh)
    klane = lax.broadcasted_iota(jnp.int32, (1, kw), 1)
    srow = lax.broadcasted_iota(jnp.int32, (kw, vw), 0) >> (DKB.bit_length() - 1)
    scol = lax.broadcasted_iota(jnp.int32, (kw, vw), 1) >> (DVB.bit_length() - 1)
    same_head = srow == scol
    ind16 = jnp.where((lax.broadcasted_iota(jnp.int32, (g_, n_chunks * LANES), 0) >> sh)
                      == (lax.broadcasted_iota(jnp.int32, (g_, n_chunks * LANES), 1) >> (LANES.bit_length() - 1)),
                      1.0, 0.0).astype(BF16)
    rows = [slice(c * c_, (c + 1) * c_) for c in range(n_chunks)]
    refs = ((q0, k0, v0, sm0, o0_ref), (q1, k1, v1, sm1, o1_ref))

    def spread(x, picks):
        return jnp.concatenate([jnp.broadcast_to(x[p:p + 1, :], (c_, kw)) for p in picks], axis=0)

    o_intra, qg16s, xs, decays = [], [], [], []
    for d, (q_ref, k_ref, v_ref, sm_ref, o_ref) in enumerate(refs):
        incl = same_chunk & ((jj <= ii) if d == 0 else (jj >= ii))
        tri16 = jnp.where(incl, 1.0, 0.0).astype(BF16)
        lo = 4 * HA + d * GATE_RANK
        pre = jnp.dot(sm_ref[:, lo:lo + GATE_RANK], wg_ref[d], precision=HIGHEST,
                      preferred_element_type=F32) + bg_ref[d]
        glog = (jnp.minimum(pre, 0.0) - jnp.log(1.0 + jnp.exp(-jnp.abs(pre)))) * (1.0 / GATE_NORM)
        ghi, glo = _split(glog)
        gc = _dot(tri16, ghi) + _dot(tri16, glo)
        tot_cols = _dot_tn(ghi, ind16) + _dot_tn(glo, ind16)
        mids = spread(gc, [c * c_ + c_ // 2 - 1 for c in range(n_chunks)])
        tots = spread(gc, [c * c_ + (c_ - 1 if d == 0 else 0) for c in range(n_chunks)])
        q = q_ref[...].astype(F32) * DKB ** -0.5
        k = k_ref[...].astype(F32)
        v16 = v_ref[...]
        qt = q * jnp.exp(gc - mids)
        kt16 = (k * jnp.exp(mids - gc)).astype(BF16)
        qg16s.append((q * jnp.exp(gc)).astype(BF16))
        kg16 = (k * jnp.exp(tots - gc)).astype(BF16)
        heads = []
        for h in range(HB):
            in_head = (klane >= h * DKB) & (klane < (h + 1) * DKB)
            a = _dot_nt(jnp.where(in_head, qt, 0.0).astype(BF16), kt16)
            a16 = jnp.where(incl, a, 0.0).astype(BF16)
            heads.append(_dot(a16, v16[:, h * DVB:(h + 1) * DVB]))
        o_intra.append(heads)
        xs.append([jnp.where(same_head, _dot_tn(kg16[r, :], v16[r, :]), 0.0) for r in rows])
        decays.append([jnp.exp(tot_cols[:, c * LANES:(c + 1) * LANES]) for c in range(n_chunks)])

    states = [s_scr[0], s_scr[1]]
    for step in range(n_chunks):
        for d in range(2):
            c = step if d == 0 else n_chunks - 1 - step
            r = rows[c]
            o_inter = _dot(qg16s[d][r, :], states[d].astype(BF16))
            for h in range(HB):
                vs = slice(h * DVB, (h + 1) * DVB)
                refs[d][4][r, vs] = (o_inter[:, vs] + o_intra[d][h][r, :]).astype(BF16)
            states[d] = jnp.concatenate([decays[d][c]] * (vw // LANES), axis=1) * states[d] + xs[d][c]
    s_scr[0] = states[0]
    s_scr[1] = states[1]


def _gla(proj32, proj16, w_gate2, b_gate, n_b, lc, l):
    t = proj16.shape[0]
    n_lat = l // lc
    n_g = n_lat + 1
    n_chunks = lc // CHUNK
    kw, vw = HB * DKB, HB * DVB
    rb0 = lambda b, g: _row_block(b, g, n_b, n_lat)
    rb1 = lambda b, g: _row_block(b, _rev_group(g, n_g), n_b, n_lat)

    def specs(rb):
        return [pl.BlockSpec((lc, kw), lambda b, g: (rb(b, g), H_BQ // kw)),
                pl.BlockSpec((lc, kw), lambda b, g: (rb(b, g), H_BK // kw)),
                pl.BlockSpec((lc, vw), lambda b, g: (rb(b, g), H_BV // vw)),
                pl.BlockSpec((lc, LANES), lambda b, g: (rb(b, g), F_SMALL // LANES))]

    return pl.pallas_call(
        functools.partial(_gla_kernel, n_chunks=n_chunks),
        grid=(n_b, n_g),
        in_specs=specs(rb0) + specs(rb1)
        + [pl.BlockSpec((2, GATE_RANK, kw), lambda b, g: (0, 0, 0)),
           pl.BlockSpec((2, 1, kw), lambda b, g: (0, 0, 0))],
        out_specs=[pl.BlockSpec((lc, vw), lambda b, g: (rb0(b, g), 0)),
                   pl.BlockSpec((lc, vw), lambda b, g: (rb1(b, g), 0))],
        out_shape=[jax.ShapeDtypeStruct((t, vw), BF16)] * 2,
        scratch_shapes=[pltpu.VMEM((2, kw, vw), F32)],
        compiler_params=_cparams(2),
        name="gla_bidir",
    )(proj16, proj16, proj16, proj32, proj16, proj16, proj16, proj32, w_gate2, b_gate.reshape(2, 1, kw))


def _attn_kernel(lam_ref, gn_ref, q_ref, kc_ref, kl_ref, vc_ref, vl_ref, cq_ref, sq_ref, ck_ref, sk_ref,
                 o_ref, k_scr, v_scr, *, lc, lam_init, ctx_out):
    i = pl.program_id(2)
    lane = lax.broadcasted_iota(jnp.int32, (1, LANES), 1)
    first_half = (lane & 31) < 16

    def rope(x, cos, sin):
        swapped = jnp.where(first_half, pltpu.roll(x, LANES - 16, 1), pltpu.roll(x, 16, 1))
        return x * cos + swapped * sin

    @pl.when(i == 0)
    def _():
        k_scr[0:lc, :] = kc_ref[...]
        k_scr[lc:, :] = rope(kl_ref[...].astype(F32), ck_ref[...], sk_ref[...]).astype(BF16)
        v_scr[0:lc, :] = vc_ref[...]
        v_scr[lc:, :] = vl_ref[...]

    lam = lam_ref[...]
    lam_f = (jnp.exp(jnp.sum(lam[0:1] * lam[1:2], axis=1, keepdims=True))
             - jnp.exp(jnp.sum(lam[2:3] * lam[3:4], axis=1, keepdims=True)) + lam_init)

    def attend(q, keys, vals):
        n = ATTN_SUB
        q = q * (DKC ** -0.5 * math.log2(math.e))
        subs = [q[r * n:(r + 1) * n] for r in range(q.shape[0] // n)]
        q2s = [jnp.concatenate([jnp.where(lane < DKC, x, 0.0), jnp.where(lane >= DKC, x, 0.0)],
                               axis=0).astype(BF16) for x in subs]
        ss = [_dot_nt(q2, keys) for q2 in q2s]
        es = [jnp.exp2(s - jnp.max(s, axis=-1, keepdims=True)) for s in ss]
        invs = [1.0 / jnp.sum(e, axis=-1, keepdims=True) for e in es]
        o2s = [_dot(es[r].astype(BF16), vals) * invs[r] for r in range(len(es))]
        for r, o2 in enumerate(o2s):
            o = o2[:n] - lam_f * o2[n:]
            o_ref[r * n:(r + 1) * n, :] = (_rms_rows(o, gn_ref[...]) * (1.0 - lam_init)).astype(BF16)

    @pl.when(i == 0)
    def _():
        if ctx_out:
            attend(q_ref[...].astype(F32), k_scr[0:lc, :], v_scr[0:lc, :])
        else:
            o_ref[...] = jnp.zeros_like(o_ref)

    @pl.when(i > 0)
    def _():
        attend(rope(q_ref[...].astype(F32), cq_ref[...], sq_ref[...]), k_scr[...], v_scr[...])


def _attention(proj, lam, gn, rope_cos, rope_sin, n_b, lc, l, lam_init, ctx_out):
    t = proj.shape[0]
    n_lat = l // lc
    rb = lambda b, h, i: _row_block(b, i, n_b, n_lat)
    cq, ck, cv = H_CQ // LANES, H_CK // LANES, H_CV // LANES
    tab_q = pl.BlockSpec((lc, LANES), lambda b, h, i: (jnp.maximum(i - 1, 0), 0))
    tab_k = pl.BlockSpec((l, LANES), lambda b, h, i: (0, 0))
    return pl.pallas_call(
        functools.partial(_attn_kernel, lc=lc, lam_init=lam_init, ctx_out=ctx_out),
        grid=(n_b, HC, n_lat + 1),
        in_specs=[pl.BlockSpec((4, DKC), lambda b, h, i: (0, 0)),
                  pl.BlockSpec((1, DVC), lambda b, h, i: (0, 0)),
                  pl.BlockSpec((lc, LANES), lambda b, h, i: (rb(b, h, i), cq + h)),
                  pl.BlockSpec((lc, LANES), lambda b, h, i: (b, ck + h)),
                  pl.BlockSpec((l, LANES), lambda b, h, i: (b + 1, ck + h)),
                  pl.BlockSpec((lc, LANES), lambda b, h, i: (b, cv + h)),
                  pl.BlockSpec((l, LANES), lambda b, h, i: (b + 1, cv + h)),
                  tab_q, tab_q, tab_k, tab_k],
        out_specs=pl.BlockSpec((lc, LANES), lambda b, h, i: (rb(b, h, i), h)),
        out_shape=jax.ShapeDtypeStruct((t, HC * DVC), BF16),
        scratch_shapes=[pltpu.VMEM((lc + l, LANES), BF16), pltpu.VMEM((lc + l, LANES), BF16)],
        compiler_params=_cparams(3),
        name="diff_attention",
    )(lam, gn.reshape(1, DVC), proj, proj, proj, proj, proj, rope_cos, rope_sin, rope_cos, rope_sin)


def _rope_tables(n_tokens):
    rows = n_tokens // GRID_W
    row = jnp.repeat(jnp.arange(rows, dtype=F32), GRID_W)
    col = jnp.tile(jnp.arange(GRID_W, dtype=F32), rows)
    n_freq = DKC // 4
    inv_freq = ROPE_THETA ** (-jnp.arange(n_freq, dtype=F32) / n_freq)
    ang_r = row[:, None] * inv_freq
    ang_c = col[:, None] * inv_freq
    cos = jnp.concatenate([jnp.cos(ang_r)] * 2 + [jnp.cos(ang_c)] * 2, axis=1)
    sin = jnp.concatenate([-jnp.sin(ang_r), jnp.sin(ang_r), -jnp.sin(ang_c), jnp.sin(ang_c)], axis=1)
    return jnp.concatenate([cos, cos], axis=1), jnp.concatenate([sin, sin], axis=1)


def _head_rms(o, gain, n_heads, width):
    parts = []
    for h in range(n_heads):
        parts.append(_rms_rows(o[:, h * width:(h + 1) * width], gain))
    return jnp.concatenate(parts, axis=1)


def _merge_kernel(oa0, oa1, z_ref, ob0, ob1, r_ref, od_ref, ga_ref, gb_ref, gd_ref, xc_ref, xl_ref, mod_ref,
                  gna_ref, gnb_ref, woa_ref, wob_ref, woc_ref, wout_ref, o_ref, *, ctx_tiles):
    f32 = lambda ref: ref[...].astype(F32)
    ya = _head_rms(f32(oa0) + f32(oa1), gna_ref[...], HA, DVA) * _silu(f32(z_ref))
    yb = _head_rms(f32(ob0) + f32(ob1), gnb_ref[...], HB, DVB) * _silu(f32(r_ref))
    acc = _sigmoid(f32(ga_ref)) * _dot(ya.astype(BF16), woa_ref[...])
    acc = acc + _sigmoid(f32(gb_ref)) * _dot(yb.astype(BF16), wob_ref[...])
    acc = acc + _sigmoid(f32(gd_ref)) * _dot(od_ref[...], woc_ref[...])
    y = _dot(acc.astype(BF16), wout_ref[...])
    x = jnp.where(pl.program_id(0) < ctx_tiles, xc_ref[...], xl_ref[...])
    o_ref[...] = x + mod_ref[0, 2:3, :] * y


def _merge(oa0, oa1, ob0, ob1, od, proj, x_ctx, x_lat, lat_first, t, mod, gn_a, gn_b, woa, wob, woc, wout,
           rows_per_mod, row_off, tm):
    d = x_ctx.shape[1]
    off = row_off // tm
    ctx_tiles = rows_per_mod // tm
    n_tiles = (t - row_off) // tm
    per = rows_per_mod // tm
    w5 = HA * DVA
    row = lambda i: (i + off, 0)
    col = lambda c: (lambda i: (i + off, c))
    return pl.pallas_call(
        functools.partial(_merge_kernel, ctx_tiles=ctx_tiles - off),
        grid=(n_tiles,),
        in_specs=[pl.BlockSpec((tm, w5), row),
                  pl.BlockSpec((tm, w5), row),
                  pl.BlockSpec((tm, w5), col(H_AZ // w5)),
                  pl.BlockSpec((tm, w5), row),
                  pl.BlockSpec((tm, w5), row),
                  pl.BlockSpec((tm, w5), col(H_BR // w5)),
                  pl.BlockSpec((tm, w5), row),
                  pl.BlockSpec((tm, d), col(H_GATES // d)),
                  pl.BlockSpec((tm, d), col(H_GATES // d + 1)),
                  pl.BlockSpec((tm, d), col(H_GATES // d + 2))]
        + _token_specs(tm, d, ctx_tiles, lat_first // tm, off)
        + [pl.BlockSpec((1, 6, d), lambda i: ((i + off) // per, 0, 0)),
                  pl.BlockSpec((1, DVA), lambda i: (0, 0)),
                  pl.BlockSpec((1, DVB), lambda i: (0, 0)),
                  pl.BlockSpec((w5, d), lambda i: (0, 0)),
                  pl.BlockSpec((w5, d), lambda i: (0, 0)),
                  pl.BlockSpec((w5, d), lambda i: (0, 0)),
                  pl.BlockSpec((d, d), lambda i: (0, 0))],
        out_specs=pl.BlockSpec((tm, d), lambda i: (i, 0)),
        out_shape=jax.ShapeDtypeStruct((t - row_off, d), F32),
        compiler_params=_cparams(1),
        name="merge_outproj",
    )(oa0, oa1, proj, ob0, ob1, proj, od, proj, proj, proj, x_ctx, x_lat, mod,
      gn_a.reshape(1, DVA), gn_b.reshape(1, DVB), woa, wob, woc, wout)


def _ffn_kernel(x_ref, mod_ref, g_ref, fg_ref, w1_ref, w3_ref, w2_ref, o_ref, h_scr, *, final):
    f = pl.program_id(1)

    @pl.when(f == 0)
    def _():
        y = _rms_rows(x_ref[...], g_ref[...])
        h_scr[...] = (y * (1.0 + mod_ref[0, 4:5, :]) + mod_ref[0, 3:4, :]).astype(BF16)
        o_ref[...] = jnp.zeros_like(o_ref)

    h = h_scr[...]
    t = _silu(_dot(h, w1_ref[...])) * _dot(h, w3_ref[...])
    o_ref[...] += _dot(t.astype(BF16), w2_ref[...])

    @pl.when(f == pl.num_programs(1) - 1)
    def _():
        out = x_ref[...] + mod_ref[0, 5:6, :] * o_ref[...]
        if final:
            out = _rms_rows(out, fg_ref[...])
        o_ref[...] = out


def _ffn(x, mod, g, final_g, w1, w3, w2, rows_per_mod, mod_off, tm, tf, final):
    t, d = x.shape
    dff = w1.shape[1]
    per = rows_per_mod // tm
    return pl.pallas_call(
        functools.partial(_ffn_kernel, final=final),
        grid=(t // tm, dff // tf),
        in_specs=[pl.BlockSpec((tm, d), lambda i, f: (i, 0)),
                  pl.BlockSpec((1, 6, d), lambda i, f: (i // per + mod_off, 0, 0)),
                  pl.BlockSpec((1, d), lambda i, f: (0, 0)),
                  pl.BlockSpec((1, d), lambda i, f: (0, 0)),
                  pl.BlockSpec((d, tf), lambda i, f: (0, f)),
                  pl.BlockSpec((d, tf), lambda i, f: (0, f)),
                  pl.BlockSpec((tf, d), lambda i, f: (f, 0))],
        out_specs=pl.BlockSpec((tm, d), lambda i, f: (i, 0)),
        out_shape=jax.ShapeDtypeStruct((t, d), F32),
        scratch_shapes=[pltpu.VMEM((tm, d), BF16)],
        compiler_params=_cparams(2),
        name="dense_ffn",
    )(x, mod, g.reshape(1, d), final_g.reshape(1, d), w1, w3, w2)


MOE_ROWS = 256


def _moe_kernel(x_ref, mod_ref, g_ref, rw_ref, fg_ref, w1_ref, w3_ref, w2_ref, o_ref,
                h_scr, gate_scr, sel_scr, rank_scr, selr_scr, rankr_scr, xg_scr, yg_scr, nb_scr, *, final):
    e = pl.program_id(1)
    f = pl.program_id(2)
    n_f = pl.num_programs(2)
    tm, d = x_ref.shape
    cb = MOE_ROWS
    half = cb // 2
    lane = lax.broadcasted_iota(jnp.int32, (1, LANES), 1)

    def for_blocks(body):
        nb = nb_scr[0]

        def full(j, carry):
            body(pl.multiple_of(j * cb, cb), cb)
            return carry

        lax.fori_loop(0, nb, full, 0)

        @pl.when(nb_scr[1] == 1)
        def _():
            body(pl.multiple_of(nb * cb, cb), half)

    @pl.when((e == 0) & (f == 0))
    def _():
        y = _rms_rows(x_ref[...], g_ref[...])
        hmod = y * (1.0 + mod_ref[0, 4:5, :]) + mod_ref[0, 3:4, :]
        h_scr[...] = hmod.astype(BF16)
        o_ref[...] = jnp.zeros_like(o_ref)
        logits = jnp.dot(hmod, rw_ref[...], precision=HIGHEST, preferred_element_type=F32)
        logits = jnp.where(lane < N_EXPERTS, logits, -jnp.inf)
        lanef = lane.astype(F32)
        m1 = jnp.max(logits, axis=1, keepdims=True)
        i1 = jnp.min(jnp.where(logits == m1, lanef, float(LANES)), axis=1, keepdims=True)
        hit1 = lanef == i1
        rest = jnp.where(hit1, -jnp.inf, logits)
        m2 = jnp.max(rest, axis=1, keepdims=True)
        i2 = jnp.min(jnp.where(rest == m2, lanef, float(LANES)), axis=1, keepdims=True)
        hit2 = lanef == i2
        ex = jnp.exp(m2 - m1)
        inv = 1.0 / (1.0 + ex)
        gate_scr[...] = jnp.where(hit1, inv, 0.0) + jnp.where(hit2, ex * inv, 0.0)
        sel = jnp.where(hit1 | hit2, 1.0, 0.0)
        sel_scr[...] = sel
        sel16 = sel.astype(BF16)
        pick = jnp.where(lax.broadcasted_iota(jnp.int32, (8, LANES), 0)
                         == lax.broadcasted_iota(jnp.int32, (8, LANES), 1), 1.0, 0.0).astype(BF16)
        sel_r = _dot_nt(pick, sel16)
        selr_scr[...] = sel_r
        sel_r16 = sel_r.astype(BF16)
        for blk in range(tm // LANES):
            lo = blk * LANES
            before = (lax.broadcasted_iota(jnp.int32, (LANES, tm), 1)
                      < lax.broadcasted_iota(jnp.int32, (LANES, tm), 0) + lo)
            rank_scr[lo:lo + LANES, :] = _dot(jnp.where(before, 1.0, 0.0).astype(BF16), sel16)
            before_t = (lax.broadcasted_iota(jnp.int32, (tm, LANES), 0)
                        < lax.broadcasted_iota(jnp.int32, (tm, LANES), 1) + lo)
            rankr_scr[:, lo:lo + LANES] = _dot(sel_r16, jnp.where(before_t, 1.0, 0.0).astype(BF16))

    @pl.when(f == 0)
    def _():
        sel_row = selr_scr[pl.ds(e, 1), :]
        rank_row = rankr_scr[pl.ds(e, 1), :]
        n_e = jnp.sum(sel_row).astype(jnp.int32)
        n_half = (n_e + (half - 1)) // half
        nb_scr[0] = n_half // 2
        nb_scr[1] = n_half % 2

        def gather(r0, bs):
            slot = lax.broadcasted_iota(jnp.int32, (bs, tm), 0).astype(F32) + r0.astype(F32)
            onehot = jnp.where((rank_row == slot) & (sel_row > 0.0), 1.0, 0.0)
            xg_scr[pl.ds(r0, bs), :] = _dot(onehot.astype(BF16), h_scr[...]).astype(BF16)
            yg_scr[pl.ds(r0, bs), :] = jnp.zeros((bs, d), F32)

        for_blocks(gather)

    def expert(r0, bs):
        xb = xg_scr[pl.ds(r0, bs), :]
        t = _silu(_dot(xb, w1_ref[0])) * _dot(xb, w3_ref[0])
        yg_scr[pl.ds(r0, bs), :] += _dot(t.astype(BF16), w2_ref[0])

    for_blocks(expert)

    @pl.when(f == n_f - 1)
    def _():
        pick_e = lane == e
        rank_col = jnp.sum(jnp.where(pick_e, rank_scr[...], 0.0), axis=1, keepdims=True)
        sel_col = jnp.sum(jnp.where(pick_e, sel_scr[...], 0.0), axis=1, keepdims=True)
        gate_col = jnp.sum(jnp.where(pick_e, gate_scr[...], 0.0), axis=1, keepdims=True)

        def scatter(r0, bs):
            slot = lax.broadcasted_iota(jnp.int32, (tm, bs), 1).astype(F32) + r0.astype(F32)
            onehot = jnp.where((rank_col == slot) & (sel_col > 0.0), 1.0, 0.0)
            o_ref[...] += gate_col * _dot(onehot.astype(BF16), yg_scr[pl.ds(r0, bs), :].astype(BF16))

        for_blocks(scatter)

    @pl.when((e == pl.num_programs(1) - 1) & (f == n_f - 1))
    def _():
        out = x_ref[...] + mod_ref[0, 5:6, :] * o_ref[...]
        if final:
            out = _rms_rows(out, fg_ref[...])
        o_ref[...] = out


def _moe(x, mod, g, router_w, final_g, w1, w3, w2, rows_per_mod, mod_off, tm, tf, final):
    t, d = x.shape
    n_e, _, dff = w1.shape
    per = rows_per_mod // tm
    return pl.pallas_call(
        functools.partial(_moe_kernel, final=final),
        grid=(t // tm, n_e, dff // tf),
        in_specs=[pl.BlockSpec((tm, d), lambda i, e, f: (i, 0)),
                  pl.BlockSpec((1, 6, d), lambda i, e, f: (i // per + mod_off, 0, 0)),
                  pl.BlockSpec((1, d), lambda i, e, f: (0, 0)),
                  pl.BlockSpec((d, LANES), lambda i, e, f: (0, 0)),
                  pl.BlockSpec((1, d), lambda i, e, f: (0, 0)),
                  pl.BlockSpec((1, d, tf), lambda i, e, f: (e, 0, f)),
                  pl.BlockSpec((1, d, tf), lambda i, e, f: (e, 0, f)),
                  pl.BlockSpec((1, tf, d), lambda i, e, f: (e, f, 0))],
        out_specs=pl.BlockSpec((tm, d), lambda i, e, f: (i, 0)),
        out_shape=jax.ShapeDtypeStruct((t, d), F32),
        scratch_shapes=[pltpu.VMEM((tm, d), BF16),
                        pltpu.VMEM((tm, LANES), F32),
                        pltpu.VMEM((tm, LANES), F32),
                        pltpu.VMEM((tm, LANES), F32),
                        pltpu.VMEM((8, tm), F32),
                        pltpu.VMEM((8, tm), F32),
                        pltpu.VMEM((tm, d), BF16),
                        pltpu.VMEM((tm, d), F32),
                        pltpu.SMEM((2,), jnp.int32)],
        compiler_params=_cparams(3),
        name="routed_ffn_sparse",
    )(x, mod, g.reshape(1, d), router_w, final_g.reshape(1, d), w1, w3, w2)


def _reorder_w_in(w):
    d = w.shape[0]
    a_qkv = 2 * HA * DKA + HA * DVA
    a_z = a_qkv + HA * DVA
    b_q = a_z + 4 * HA
    b_v = b_q + 2 * HB * DKB
    b_glr = b_v + 2 * HB * DVB
    c_q = b_glr + 2 * GATE_RANK
    gates = c_q + 4 * HC * DKC + HC * DVC
    pad = N_F32 - F_SMALL - 4 * HA - 2 * GATE_RANK
    w = w.astype(BF16)
    return jnp.concatenate([w[:, :a_qkv], w[:, a_z:b_q], w[:, b_glr:c_q], jnp.zeros((d, pad), BF16),
                            w[:, gates:], w[:, a_qkv:a_z], w[:, b_v:b_glr], w[:, c_q:gates], w[:, b_q:b_v]], axis=1)


def kernel(x, c, ctx, c_ctx, w_mod, b_mod, norm1_g, norm2_g, w_in, conv_a, a_log, dt_bias, gn_a, w_gate2, b_gate, gn_b, lam_c, gn_c, w_o_a, w_o_b, w_o_c, w_out, ffn_w1, ffn_w3, ffn_w2, router_w, moe_w1, moe_w3, moe_w2, final_g):
    n_b, l, d = x.shape
    lc = ctx.shape[1]
    depth = w_mod.shape[0]
    assert n_b * lc == l and lc % CHUNK == 0 and l & (l - 1) == 0 and lc & (lc - 1) == 0
    t_ctx = n_b * lc
    tm = min(1024, l)
    tm_small = min(512, l)

    n_mod = 16
    cvec = jnp.concatenate([c_ctx[None, :], c, jnp.zeros((n_mod - 1 - n_b, d), F32)], axis=0)
    mod_all = _modulation(cvec, w_mod, b_mod).reshape(depth, n_mod, 6, d)

    rope_cos, rope_sin = _rope_tables(l)
    t_all = t_ctx + n_b * l
    tokens = (ctx.reshape(t_ctx, d), x.reshape(n_b * l, d), 0)

    for layer in range(depth):
        ctx_out = layer < depth - 1
        lam_init = 0.8 - 0.6 * math.exp(-0.3 * layer)
        mod = mod_all[layer]
        proj32, proj16 = _in_projection(*tokens, t_all, mod, norm1_g[layer], _reorder_w_in(w_in[layer]), l, tm)

        qkvc = _short_conv(proj32, conv_a[layer], lc, l)
        oa0, oa1 = _dn_scan(_dn_factors(qkvc, proj32, a_log[layer], dt_bias[layer], n_b, lc, l), n_b, lc, l)
        ob0, ob1 = _gla(proj32, proj16, w_gate2[layer], b_gate[layer], n_b, lc, l)
        od = _attention(proj16, lam_c[layer], gn_c[layer], rope_cos, rope_sin, n_b, lc, l, lam_init, ctx_out)

        row_off = 0 if ctx_out else t_ctx
        xs_new = _merge(oa0, oa1, ob0, ob1, od, proj16, *tokens, t_all, mod, gn_a[layer], gn_b[layer],
                        w_o_a[layer].astype(BF16), w_o_b[layer].astype(BF16), w_o_c[layer].astype(BF16),
                        w_out[layer].astype(BF16), l, row_off, tm_small)

        i = layer // 2
        final = layer == depth - 1
        mod_off = 0 if ctx_out else 1
        if layer % 2 == 0:
            xs = _ffn(xs_new, mod, norm2_g[layer], final_g, ffn_w1[i].astype(BF16),
                      ffn_w3[i].astype(BF16), ffn_w2[i].astype(BF16), l, mod_off, tm_small,
                      ffn_w1.shape[2] // 2, final)
        else:
            rw = jnp.pad(router_w[i], ((0, 0), (0, LANES - N_EXPERTS)))
            xs = _moe(xs_new, mod, norm2_g[layer], rw, final_g, moe_w1[i].astype(BF16),
                      moe_w3[i].astype(BF16), moe_w2[i].astype(BF16), l, mod_off, tm,
                      moe_w1.shape[3] // 2, final)
        tokens = (xs, xs, t_ctx)

    return xs.reshape(n_b, l, d)
```

```python
import functools
import math

import jax
import jax.numpy as jnp
from jax import lax
from jax.experimental import pallas as pl
from jax.experimental.pallas import tpu as pltpu

F32 = jnp.float32
BF16 = jnp.bfloat16
HIGHEST = lax.Precision.HIGHEST

EPS = 1e-6
HA, DKA, DVA = 4, 128, 128
HB, DKB, DVB = 4, 64, 128
GATE_RANK, GATE_NORM = 16, 16.0
HC, DKC, DVC = 4, 64, 128
GRID_W, ROPE_THETA = 64, 10000.0
N_EXPERTS, TOP_K = 8, 2
CHUNK = 64
LANES = 128
ATTN_SUB = 64
DN_HEADS = 4

F_QKV = 0
F_SMALL = 1536
N_F32 = 1664
H_GATES = 0
H_AZ = 3072
H_BV = 3584
H_BR = 4096
H_CQ = 4608
H_CK = 5120
H_CV = 5632
H_BQ = 6144
H_BK = 6400
N_B16 = 6656
PROJ_TN = 1664

VMEM_LIMIT = 56 * 1024 * 1024


def _cparams(n_axes):
    return pltpu.CompilerParams(dimension_semantics=("arbitrary",) * n_axes,
                                vmem_limit_bytes=VMEM_LIMIT)


def _sigmoid(x):
    return 1.0 / (1.0 + jnp.exp(-x))


def _silu(x):
    return x * _sigmoid(x)


def _softplus(x):
    return jnp.maximum(x, 0.0) + jnp.log(1.0 + jnp.exp(-jnp.abs(x)))


def _dot(a, b):
    return jnp.dot(a, b, preferred_element_type=F32)


def _dot_nt(a, b):
    return lax.dot_general(a, b, (((1,), (1,)), ((), ())), preferred_element_type=F32)


def _dot_tn(a, b):
    return lax.dot_general(a, b, (((0,), (0,)), ((), ())), preferred_element_type=F32)


def _split(a):
    hi = a.astype(BF16)
    lo = (a - hi.astype(F32)).astype(BF16)
    return hi, lo


def _dot3(a, b):
    ah, al = _split(a)
    bh, bl = _split(b)
    return _dot(ah, bh) + (_dot(ah, bl) + _dot(al, bh))


def _rms_rows(x, gain):
    ms = jnp.mean(x * x, axis=-1, keepdims=True)
    return x * lax.rsqrt(ms + EPS) * gain


def _mod_kernel(c_ref, w_ref, b_ref, o_ref):
    s = _silu(c_ref[...])
    o_ref[0] = jnp.dot(s, w_ref[0], precision=HIGHEST, preferred_element_type=F32) + b_ref[0]


def _modulation(cvec, w_mod, b_mod):
    depth, d, d6 = w_mod.shape
    n = cvec.shape[0]
    return pl.pallas_call(
        _mod_kernel,
        grid=(depth, d6 // d),
        in_specs=[pl.BlockSpec((n, d), lambda l, j: (0, 0)),
                  pl.BlockSpec((1, d, d), lambda l, j: (l, 0, j)),
                  pl.BlockSpec((1, 1, d), lambda l, j: (l, 0, j))],
        out_specs=pl.BlockSpec((1, n, d), lambda l, j: (l, 0, j)),
        out_shape=jax.ShapeDtypeStruct((depth, n, d6), F32),
        compiler_params=_cparams(2),
        name="adaln_mod",
    )(cvec, w_mod, b_mod.reshape(depth, 1, d6))


def _token_specs(tm, d, ctx_tiles, lat_first, off=0):
    return [pl.BlockSpec((tm, d), lambda i, *_: (jnp.minimum(i + off, ctx_tiles - 1), 0)),
            pl.BlockSpec((tm, d), lambda i, *_: (jnp.maximum(i + off - ctx_tiles, 0) + lat_first, 0))]


def _token_inputs(x_ctx, x_lat, lat_first, tm, ctx_tiles, off=0):
    d = x_ctx.shape[1]
    if x_lat is x_ctx:
        return [pl.BlockSpec((tm, d), lambda i, *_: (i + off, 0))], [x_ctx]
    return _token_specs(tm, d, ctx_tiles, lat_first // tm, off), [x_ctx, x_lat]


def _token_tile(x_refs, ctx_tiles):
    if len(x_refs) == 1:
        return x_refs[0][...]
    return jnp.where(pl.program_id(0) < ctx_tiles, x_refs[0][...], x_refs[1][...])


def _inproj_kernel(*refs, ctx_tiles, n_tok):
    x_refs, (mod_ref, g_ref, w_ref, o32_ref, o16_ref, h_scr) = refs[:n_tok], refs[n_tok:]
    j = pl.program_id(1)

    @pl.when(j == 0)
    def _():
        y = _rms_rows(_token_tile(x_refs, ctx_tiles), g_ref[...])
        h_scr[...] = (y * (1.0 + mod_ref[0, 1:2, :]) + mod_ref[0, 0:1, :]).astype(BF16)
        o32_ref[...] = _dot(h_scr[...], w_ref[...])

    @pl.when(j > 0)
    def _():
        o16_ref[...] = _dot(h_scr[...], w_ref[...]).astype(BF16)


def _in_projection(x_ctx, x_lat, lat_first, t, mod, g, w, rows_per_mod, tm):
    d = x_ctx.shape[1]
    tn = PROJ_TN
    assert w.shape[1] == N_F32 + N_B16 and N_F32 == tn and N_B16 % tn == 0
    per = rows_per_mod // tm
    ctx_tiles = rows_per_mod // tm
    tok_specs, tok_args = _token_inputs(x_ctx, x_lat, lat_first, tm, ctx_tiles)
    return pl.pallas_call(
        functools.partial(_inproj_kernel, ctx_tiles=ctx_tiles, n_tok=len(tok_args)),
        grid=(t // tm, 1 + N_B16 // tn),
        in_specs=tok_specs
        + [pl.BlockSpec((1, 6, d), lambda i, j: (i // per, 0, 0)),
           pl.BlockSpec((1, d), lambda i, j: (0, 0)),
           pl.BlockSpec((d, tn), lambda i, j: (0, j))],
        out_specs=[pl.BlockSpec((tm, tn), lambda i, j: (i, 0)),
                   pl.BlockSpec((tm, tn), lambda i, j: (i, jnp.maximum(j - 1, 0)))],
        out_shape=[jax.ShapeDtypeStruct((t, N_F32), F32), jax.ShapeDtypeStruct((t, N_B16), BF16)],
        scratch_shapes=[pltpu.VMEM((tm, d), BF16)],
        compiler_params=_cparams(2),
        name="norm_mod_inproj",
    )(*tok_args, mod, g.reshape(1, d), w)


def _conv_kernel(x_ref, w_ref, o_ref, *, lc, l):
    p = pl.program_id(0)
    j = pl.program_id(1)
    x = x_ref[...]
    n = x.shape[0]
    row = lax.broadcasted_iota(jnp.int32, x.shape, 0)
    seq = jnp.where(p == 0, lc, l)
    pos = row & (seq - 1)
    prev = jnp.where(pos == 0, 0.0, pltpu.roll(x, 1, 0))
    nxt = jnp.where(pos == seq - 1, 0.0, pltpu.roll(x, n - 1, 0))
    w = w_ref[...]
    y = _silu(prev * w[0:1] + x * w[1:2] + nxt * w[2:3])
    inv = lax.rsqrt(jnp.sum(y * y, axis=-1, keepdims=True) + EPS)
    fac = jnp.where(j < HA, inv * DKA ** -0.5, jnp.where(j < 2 * HA, inv, 1.0))
    o_ref[...] = y * fac


def _short_conv(proj, conv_w, lc, l):
    t = proj.shape[0]
    ncol = conv_w.shape[1]
    return pl.pallas_call(
        functools.partial(_conv_kernel, lc=lc, l=l),
        grid=(t // l, ncol // LANES),
        in_specs=[pl.BlockSpec((l, LANES), lambda p, j: (p, j)),
                  pl.BlockSpec((conv_w.shape[0], LANES), lambda p, j: (0, j))],
        out_specs=pl.BlockSpec((l, LANES), lambda p, j: (p, j)),
        out_shape=jax.ShapeDtypeStruct((t, ncol), F32),
        compiler_params=_cparams(2),
        name="conv_silu_l2",
    )(proj, conv_w)


def _row_block(b, g, n_b, n_lat):
    return jnp.where(g == 0, b, n_b + b * n_lat + g - 1)


def _rev_group(g, n_g):
    return jnp.where(g == 0, 0, n_g - g)


def _tri_inverse_many(lms, ii, jj, uppers):
    c = lms[0].shape[0]
    eye = (ii == jj).astype(F32)
    n = len(lms)
    ts = None
    s, ls = 1, 0
    while s < c:
        same = (ii >> (ls + 1)) == (jj >> (ls + 1))
        io = (ii >> ls) & 1
        jo = (jj >> ls) & 1
        lower = same & (io == 1) & (jo == 0)
        upper = same & (io == 0) & (jo == 1)
        cms = [jnp.where(upper if uppers[i] else lower, lms[i], 0.0) for i in range(n)]
        if ts is None:
            ts = [eye - cm for cm in cms]
        else:
            t16 = [t.astype(BF16) for t in ts]
            ps = [_dot(t16[i], cms[i].astype(BF16)).astype(BF16) for i in range(n)]
            ts = [ts[i] - _dot(ps[i], t16[i]) for i in range(n)]
        s, ls = s * 2, ls + 1
    a_parts = [_split(eye + lm) for lm in lms]
    t_parts = [_split(t) for t in ts]
    res = [eye - (_dot(a_parts[i][0], t_parts[i][0])
                  + (_dot(a_parts[i][0], t_parts[i][1]) + _dot(a_parts[i][1], t_parts[i][0])))
           for i in range(n)]
    return [ts[i] + _dot(t_parts[i][0], res[i].astype(BF16)) for i in range(n)]


def _dn1_kernel(alog_ref, dtb_ref, q_ref, k_ref, v_ref, sm_ref,
                u_ref, w_ref, kg_ref, qg_ref, qk_ref, gl_ref, *, n_chunks):
    hp = pl.program_id(2)
    c_ = CHUNK
    sm = sm_ref[...]
    sm_t = sm.T
    ii = lax.broadcasted_iota(jnp.int32, (c_, c_), 0)
    jj = lax.broadcasted_iota(jnp.int32, (c_, c_), 1)
    rows = [slice(c * c_, (c + 1) * c_) for c in range(n_chunks)]
    lanes = [slice(j * LANES, (j + 1) * LANES) for j in range(DN_HEADS)]
    blocks = [(j, c) for j in range(DN_HEADS) for c in range(n_chunks)]
    qs = {jc: q_ref[rows[jc[1]], lanes[jc[0]]] for jc in blocks}
    ks = {jc: k_ref[rows[jc[1]], lanes[jc[0]]] for jc in blocks}
    vs = {jc: v_ref[rows[jc[1]], lanes[jc[0]]] for jc in blocks}
    k16 = {jc: ks[jc].astype(BF16) for jc in blocks}
    kks = {jc: _dot_nt(k16[jc], k16[jc]) for jc in blocks}
    qks = {jc: _dot_nt(qs[jc].astype(BF16), k16[jc]) for jc in blocks}

    items = [(j, c, d) for j, c in blocks for d in range(2)]
    betas, gcs, totals, gams, lms = [], [], [], [], []
    for j, c, d in items:
        r = rows[c]
        h = hp * DN_HEADS + j
        incl = (jj <= ii) if d == 0 else (jj >= ii)
        strict = (jj < ii) if d == 0 else (jj > ii)
        incl_t = (ii <= jj) if d == 0 else (ii >= jj)
        a_scale = -jnp.exp(jnp.full((1, 1), alog_ref[d * HA + h], F32))
        dtb = dtb_ref[d * HA + h]
        col = d * HA + j
        beta = _sigmoid(sm[r, col:col + 1])
        g_col = a_scale * _softplus(sm[r, 2 * HA + col:2 * HA + col + 1] + dtb)
        g_row = a_scale * _softplus(sm_t[2 * HA + col:2 * HA + col + 1, r] + dtb)
        gc_col = jnp.sum(jnp.where(incl, g_row, 0.0), axis=1, keepdims=True)
        gc_row = jnp.sum(jnp.where(incl_t, g_col, 0.0), axis=0, keepdims=True)
        gam = jnp.where(incl, jnp.exp(gc_col - gc_row), 0.0)
        betas.append(beta)
        gcs.append(gc_col)
        totals.append(jnp.sum(g_col, axis=0, keepdims=True))
        gams.append(gam)
        lms.append(jnp.where(strict, beta * kks[(j, c)] * gam, 0.0))

    tinvs = _tri_inverse_many(lms, ii, jj, [d == 1 for _, _, d in items])
    egs = [jnp.exp(gc) for gc in gcs]
    uws = []
    for i, (j, c, d) in enumerate(items):
        rhs = jnp.concatenate([vs[(j, c)] * betas[i], ks[(j, c)] * (betas[i] * egs[i])], axis=1).astype(BF16)
        uws.append(_dot(tinvs[i].astype(BF16), rhs))
    for i, (j, c, d) in enumerate(items):
        r = rows[c]
        u_ref[d, r, lanes[j]] = uws[i][:, :DVA]
        w_ref[d, r, lanes[j]] = uws[i][:, DVA:].astype(BF16)
        kg_ref[d, r, lanes[j]] = (ks[(j, c)] * jnp.exp(totals[i] - gcs[i])).astype(BF16)
        qg_ref[d, r, lanes[j]] = (qs[(j, c)] * egs[i]).astype(BF16)
        qk_ref[d, j, r, :] = (qks[(j, c)] * gams[i]).astype(BF16)
        gl_ref[0, 0, j, c * 2 + d:c * 2 + d + 1, :] = jnp.broadcast_to(jnp.exp(totals[i]), (1, LANES))


def _dn_factors(qkvc, proj32, a_log, dt_bias, n_b, lc, l):
    t = qkvc.shape[0]
    n_lat = l // lc
    n_g = n_lat + 1
    n_chunks = lc // CHUNK
    assert DN_HEADS == HA
    nh = DN_HEADS
    wide = nh * LANES
    steps = HA // nh
    rb = lambda b, g, h: _row_block(b, g, n_b, n_lat)
    tok = lambda dt: jax.ShapeDtypeStruct((2, t, HA * DVA), dt)
    smem = pl.BlockSpec(memory_space=pltpu.SMEM)
    return pl.pallas_call(
        functools.partial(_dn1_kernel, n_chunks=n_chunks),
        grid=(n_b, n_g, steps),
        in_specs=[smem, smem,
                  pl.BlockSpec((lc, wide), lambda b, g, h: (rb(b, g, h), h)),
                  pl.BlockSpec((lc, wide), lambda b, g, h: (rb(b, g, h), steps + h)),
                  pl.BlockSpec((lc, wide), lambda b, g, h: (rb(b, g, h), 2 * steps + h)),
                  pl.BlockSpec((lc, LANES), lambda b, g, h: (rb(b, g, h), F_SMALL // LANES))],
        out_specs=[pl.BlockSpec((2, lc, wide), lambda b, g, h: (0, rb(b, g, h), h))] * 4
        + [pl.BlockSpec((2, nh, lc, CHUNK), lambda b, g, h: (0, h, rb(b, g, h), 0)),
           pl.BlockSpec((1, 1, nh, 2 * n_chunks, LANES), lambda b, g, h: (b, g, h, 0, 0))],
        out_shape=[tok(F32), tok(BF16), tok(BF16), tok(BF16),
                   jax.ShapeDtypeStruct((2, HA, t, CHUNK), BF16),
                   jax.ShapeDtypeStruct((n_b, n_g, HA, 2 * n_chunks, LANES), F32)],
        compiler_params=_cparams(3),
        name="deltanet_chunk_factors",
    )(a_log.reshape(-1), dt_bias.reshape(-1), qkvc, qkvc, qkvc, proj32)


def _dn2_kernel(u0, w0, kg0, qg0, qk0, gl0, u1, w1, kg1, qg1, qk1, gl1, o0_ref, o1_ref, s_scr,
                *, n_chunks):
    @pl.when(pl.program_id(1) == 0)
    def _():
        s_scr[...] = jnp.zeros_like(s_scr)

    c_ = CHUNK
    refs = ((u0, w0, kg0, qg0, qk0, gl0, o0_ref), (u1, w1, kg1, qg1, qk1, gl1, o1_ref))
    chains = [(h, d) for h in range(HA) for d in range(2)]
    states = [s_scr[d * HA + h] for h, d in chains]
    for step in range(n_chunks):
        pos = []
        for h, d in chains:
            c = step if d == 0 else n_chunks - 1 - step
            pos.append((slice(c * c_, (c + 1) * c_), slice(h * DVA, (h + 1) * DVA), c))
        s16 = [s.astype(BF16) for s in states]
        prods = []
        for i, (h, d) in enumerate(chains):
            r, hs, _ = pos[i]
            wq = jnp.concatenate([refs[d][1][0, r, hs], refs[d][3][0, r, hs]], axis=0)
            prods.append(_dot(wq, s16[i]))
        vn16 = [(refs[d][0][0, pos[i][0], pos[i][1]] - prods[i][:c_]).astype(BF16)
                for i, (h, d) in enumerate(chains)]
        for i, (h, d) in enumerate(chains):
            r, hs, c = pos[i]
            u, w, kg, qg, qk, gl, o_ref = refs[d]
            o_ref[r, hs] = (prods[i][c_:] + _dot(qk[0, h, r, :], vn16[i])).astype(BF16)
            states[i] = (gl[0, 0, h, c * 2 + d:c * 2 + d + 1, :] * states[i]
                         + _dot_tn(kg[0, r, hs], vn16[i]))
    for i, (h, d) in enumerate(chains):
        s_scr[d * HA + h] = states[i]


def _dn_scan(factors, n_b, lc, l):
    u, w, kg, qg, qk, gl = factors
    t = u.shape[1]
    n_lat = l // lc
    n_g = n_lat + 1
    n_chunks = lc // CHUNK
    width = HA * DVA
    rb0 = lambda b, g: _row_block(b, g, n_b, n_lat)
    rb1 = lambda b, g: _row_block(b, _rev_group(g, n_g), n_b, n_lat)

    def specs(d, rb, gsel):
        tok = pl.BlockSpec((1, lc, width), lambda b, g: (d, rb(b, g), 0))
        return [tok, tok, tok, tok,
                pl.BlockSpec((1, HA, lc, CHUNK), lambda b, g: (d, 0, rb(b, g), 0)),
                pl.BlockSpec((1, 1, HA, 2 * n_chunks, LANES), lambda b, g: (b, gsel(g), 0, 0, 0))]

    return pl.pallas_call(
        functools.partial(_dn2_kernel, n_chunks=n_chunks),
        grid=(n_b, n_g),
        in_specs=specs(0, rb0, lambda g: g) + specs(1, rb1, lambda g: _rev_group(g, n_g)),
        out_specs=[pl.BlockSpec((lc, width), lambda b, g: (rb0(b, g), 0)),
                   pl.BlockSpec((lc, width), lambda b, g: (rb1(b, g), 0))],
        out_shape=[jax.ShapeDtypeStruct((t, width), BF16)] * 2,
        scratch_shapes=[pltpu.VMEM((2 * HA, DKA, DVA), F32)],
        compiler_params=_cparams(2),
        name="deltanet_scan",
    )(u, w, kg, qg, qk, gl, u, w, kg, qg, qk, gl)


def _gla_kernel(q0, k0, v0, sm0, q1, k1, v1, sm1, wg_ref, bg_ref, o0_ref, o1_ref, s_scr, *, n_chunks):
    @pl.when(pl.program_id(1) == 0)
    def _():
        s_scr[...] = jnp.zeros_like(s_scr)

    c_ = CHUNK
    sh = c_.bit_length() - 1
    g_ = n_chunks * c_
    kw = HB * DKB
    vw = HB * DVB
    ii = lax.broadcasted_iota(jnp.int32, (g_, g_), 0)
    jj = lax.broadcasted_iota(jnp.int32, (g_, g_), 1)
    same_chunk = (ii >> sh) == (jj >> sh)
    klane = lax.broadcasted_iota(jnp.int32, (1, kw), 1)
    srow = lax.broadcasted_iota(jnp.int32, (kw, vw), 0) >> (DKB.bit_length() - 1)
    scol = lax.broadcasted_iota(jnp.int32, (kw, vw), 1) >> (DVB.bit_length() - 1)
    same_head = srow == scol
    ind16 = jnp.where((lax.broadcasted_iota(jnp.int32, (g_, n_chunks * LANES), 0) >> sh)
                      == (lax.broadcasted_iota(jnp.int32, (g_, n_chunks * LANES), 1) >> (LANES.bit_length() - 1)),
                      1.0, 0.0).astype(BF16)
    rows = [slice(c * c_, (c + 1) * c_) for c in range(n_chunks)]
    refs = ((q0, k0, v0, sm0, o0_ref), (q1, k1, v1, sm1, o1_ref))

    def spread(x, picks):
        return jnp.concatenate([jnp.broadcast_to(x[p:p + 1, :], (c_, kw)) for p in picks], axis=0)

    dirs = (0, 1)
    incls = [same_chunk & ((jj <= ii) if d == 0 else (jj >= ii)) for d in dirs]
    tri16 = [jnp.where(m, 1.0, 0.0).astype(BF16) for m in incls]
    pres = [jnp.dot(refs[d][3][:, 4 * HA + d * GATE_RANK:4 * HA + (d + 1) * GATE_RANK], wg_ref[d],
                    precision=HIGHEST, preferred_element_type=F32) + bg_ref[d] for d in dirs]
    glogs = [(jnp.minimum(p, 0.0) - jnp.log(1.0 + jnp.exp(-jnp.abs(p)))) * (1.0 / GATE_NORM) for p in pres]
    parts = [_split(g) for g in glogs]
    gcs = [_dot(tri16[d], parts[d][0]) + _dot(tri16[d], parts[d][1]) for d in dirs]
    tot_cols = [_dot_tn(parts[d][0], ind16) + _dot_tn(parts[d][1], ind16) for d in dirs]
    mids = [spread(gcs[d], [c * c_ + c_ // 2 - 1 for c in range(n_chunks)]) for d in dirs]
    tots = [spread(gcs[d], [c * c_ + (c_ - 1 if d == 0 else 0) for c in range(n_chunks)]) for d in dirs]
    qs = [refs[d][0][...].astype(F32) * DKB ** -0.5 for d in dirs]
    ks = [refs[d][1][...].astype(F32) for d in dirs]
    v16s = [refs[d][2][...] for d in dirs]
    qts = [qs[d] * jnp.exp(gcs[d] - mids[d]) for d in dirs]
    kt16s = [(ks[d] * jnp.exp(mids[d] - gcs[d])).astype(BF16) for d in dirs]
    qg16s = [(qs[d] * jnp.exp(gcs[d])).astype(BF16) for d in dirs]
    kg16s = [(ks[d] * jnp.exp(tots[d] - gcs[d])).astype(BF16) for d in dirs]
    a16s = {}
    for h in range(HB):
        in_head = (klane >= h * DKB) & (klane < (h + 1) * DKB)
        for d in dirs:
            a = _dot_nt(jnp.where(in_head, qts[d], 0.0).astype(BF16), kt16s[d])
            a16s[(d, h)] = jnp.where(incls[d], a, 0.0).astype(BF16)
    o_intra = [[_dot(a16s[(d, h)], v16s[d][:, h * DVB:(h + 1) * DVB]) for h in range(HB)] for d in dirs]
    xs = [[jnp.where(same_head, _dot_tn(kg16s[d][r, :], v16s[d][r, :]), 0.0) for r in rows] for d in dirs]
    decays = [[jnp.exp(tot_cols[d][:, c * LANES:(c + 1) * LANES]) for c in range(n_chunks)] for d in dirs]

    states = [s_scr[0], s_scr[1]]
    for step in range(n_chunks):
        for d in range(2):
            c = step if d == 0 else n_chunks - 1 - step
            r = rows[c]
            o_inter = _dot(qg16s[d][r, :], states[d].astype(BF16))
            for h in range(HB):
                vs = slice(h * DVB, (h + 1) * DVB)
                refs[d][4][r, vs] = (o_inter[:, vs] + o_intra[d][h][r, :]).astype(BF16)
            states[d] = jnp.concatenate([decays[d][c]] * (vw // LANES), axis=1) * states[d] + xs[d][c]
    s_scr[0] = states[0]
    s_scr[1] = states[1]


def _gla(proj32, proj16, w_gate2, b_gate, n_b, lc, l):
    t = proj16.shape[0]
    n_lat = l // lc
    n_g = n_lat + 1
    n_chunks = lc // CHUNK
    kw, vw = HB * DKB, HB * DVB
    rb0 = lambda b, g: _row_block(b, g, n_b, n_lat)
    rb1 = lambda b, g: _row_block(b, _rev_group(g, n_g), n_b, n_lat)

    def specs(rb):
        return [pl.BlockSpec((lc, kw), lambda b, g: (rb(b, g), H_BQ // kw)),
                pl.BlockSpec((lc, kw), lambda b, g: (rb(b, g), H_BK // kw)),
                pl.BlockSpec((lc, vw), lambda b, g: (rb(b, g), H_BV // vw)),
                pl.BlockSpec((lc, LANES), lambda b, g: (rb(b, g), F_SMALL // LANES))]

    return pl.pallas_call(
        functools.partial(_gla_kernel, n_chunks=n_chunks),
        grid=(n_b, n_g),
        in_specs=specs(rb0) + specs(rb1)
        + [pl.BlockSpec((2, GATE_RANK, kw), lambda b, g: (0, 0, 0)),
           pl.BlockSpec((2, 1, kw), lambda b, g: (0, 0, 0))],
        out_specs=[pl.BlockSpec((lc, vw), lambda b, g: (rb0(b, g), 0)),
                   pl.BlockSpec((lc, vw), lambda b, g: (rb1(b, g), 0))],
        out_shape=[jax.ShapeDtypeStruct((t, vw), BF16)] * 2,
        scratch_shapes=[pltpu.VMEM((2, kw, vw), F32)],
        compiler_params=_cparams(2),
        name="gla_bidir",
    )(proj16, proj16, proj16, proj32, proj16, proj16, proj16, proj32, w_gate2, b_gate.reshape(2, 1, kw))


def _attn_kernel(lam_ref, gn_ref, q_ref, kc_ref, kl_ref, vc_ref, vl_ref, cq_ref, sq_ref, ck_ref, sk_ref,
                 o_ref, k_scr, v_scr, *, lc, lam_init, ctx_out):
    i = pl.program_id(2)
    lane = lax.broadcasted_iota(jnp.int32, (1, LANES), 1)
    first_half = (lane & 31) < 16

    def rope(x, cos, sin):
        swapped = jnp.where(first_half, pltpu.roll(x, LANES - 16, 1), pltpu.roll(x, 16, 1))
        return x * cos + swapped * sin

    @pl.when(i == 0)
    def _():
        k_scr[0:lc, :] = kc_ref[...]
        k_scr[lc:, :] = rope(kl_ref[...].astype(F32), ck_ref[...], sk_ref[...]).astype(BF16)
        v_scr[0:lc, :] = vc_ref[...]
        v_scr[lc:, :] = vl_ref[...]

    lam = lam_ref[...]
    lam_f = (jnp.exp(jnp.sum(lam[0:1] * lam[1:2], axis=1, keepdims=True))
             - jnp.exp(jnp.sum(lam[2:3] * lam[3:4], axis=1, keepdims=True)) + lam_init)

    def attend(q, keys, vals):
        n = ATTN_SUB
        q = q * (DKC ** -0.5 * math.log2(math.e))
        subs = [q[r * n:(r + 1) * n] for r in range(q.shape[0] // n)]
        q2s = [jnp.concatenate([jnp.where(lane < DKC, x, 0.0), jnp.where(lane >= DKC, x, 0.0)],
                               axis=0).astype(BF16) for x in subs]
        ss = [_dot_nt(q2, keys) for q2 in q2s]
        es = [jnp.exp2(s - jnp.max(s, axis=-1, keepdims=True)) for s in ss]
        invs = [1.0 / jnp.sum(e, axis=-1, keepdims=True) for e in es]
        o2s = [_dot(es[r].astype(BF16), vals) * invs[r] for r in range(len(es))]
        for r, o2 in enumerate(o2s):
            o = o2[:n] - lam_f * o2[n:]
            o_ref[r * n:(r + 1) * n, :] = (_rms_rows(o, gn_ref[...]) * (1.0 - lam_init)).astype(BF16)

    @pl.when(i == 0)
    def _():
        if ctx_out:
            attend(q_ref[...].astype(F32), k_scr[0:lc, :], v_scr[0:lc, :])
        else:
            o_ref[...] = jnp.zeros_like(o_ref)

    @pl.when(i > 0)
    def _():
        attend(rope(q_ref[...].astype(F32), cq_ref[...], sq_ref[...]), k_scr[...], v_scr[...])


def _attention(proj, lam, gn, rope_cos, rope_sin, n_b, lc, l, lam_init, ctx_out):
    t = proj.shape[0]
    n_lat = l // lc
    rb = lambda b, h, i: _row_block(b, i, n_b, n_lat)
    cq, ck, cv = H_CQ // LANES, H_CK // LANES, H_CV // LANES
    tab_q = pl.BlockSpec((lc, LANES), lambda b, h, i: (jnp.maximum(i - 1, 0), 0))
    tab_k = pl.BlockSpec((l, LANES), lambda b, h, i: (0, 0))
    return pl.pallas_call(
        functools.partial(_attn_kernel, lc=lc, lam_init=lam_init, ctx_out=ctx_out),
        grid=(n_b, HC, n_lat + 1),
        in_specs=[pl.BlockSpec((4, DKC), lambda b, h, i: (0, 0)),
                  pl.BlockSpec((1, DVC), lambda b, h, i: (0, 0)),
                  pl.BlockSpec((lc, LANES), lambda b, h, i: (rb(b, h, i), cq + h)),
                  pl.BlockSpec((lc, LANES), lambda b, h, i: (b, ck + h)),
                  pl.BlockSpec((l, LANES), lambda b, h, i: (b + 1, ck + h)),
                  pl.BlockSpec((lc, LANES), lambda b, h, i: (b, cv + h)),
                  pl.BlockSpec((l, LANES), lambda b, h, i: (b + 1, cv + h)),
                  tab_q, tab_q, tab_k, tab_k],
        out_specs=pl.BlockSpec((lc, LANES), lambda b, h, i: (rb(b, h, i), h)),
        out_shape=jax.ShapeDtypeStruct((t, HC * DVC), BF16),
        scratch_shapes=[pltpu.VMEM((lc + l, LANES), BF16), pltpu.VMEM((lc + l, LANES), BF16)],
        compiler_params=_cparams(3),
        name="diff_attention",
    )(lam, gn.reshape(1, DVC), proj, proj, proj, proj, proj, rope_cos, rope_sin, rope_cos, rope_sin)


def _rope_tables(n_tokens):
    rows = n_tokens // GRID_W
    row = jnp.repeat(jnp.arange(rows, dtype=F32), GRID_W)
    col = jnp.tile(jnp.arange(GRID_W, dtype=F32), rows)
    n_freq = DKC // 4
    inv_freq = ROPE_THETA ** (-jnp.arange(n_freq, dtype=F32) / n_freq)
    ang_r = row[:, None] * inv_freq
    ang_c = col[:, None] * inv_freq
    cos = jnp.concatenate([jnp.cos(ang_r)] * 2 + [jnp.cos(ang_c)] * 2, axis=1)
    sin = jnp.concatenate([-jnp.sin(ang_r), jnp.sin(ang_r), -jnp.sin(ang_c), jnp.sin(ang_c)], axis=1)
    return jnp.concatenate([cos, cos], axis=1), jnp.concatenate([sin, sin], axis=1)


def _head_rms(o, gain, n_heads, width):
    parts = []
    for h in range(n_heads):
        parts.append(_rms_rows(o[:, h * width:(h + 1) * width], gain))
    return jnp.concatenate(parts, axis=1)


def _merge_kernel(*refs, ctx_tiles, n_tok):
    x_refs = refs[:n_tok]
    (oa0, oa1, z_ref, ob0, ob1, r_ref, od_ref, ga_ref, gb_ref, gd_ref, mod_ref,
     gna_ref, gnb_ref, woa_ref, wob_ref, woc_ref, wout_ref, o_ref) = refs[n_tok:]
    f32 = lambda ref: ref[...].astype(F32)
    ya = _head_rms(f32(oa0) + f32(oa1), gna_ref[...], HA, DVA) * _silu(f32(z_ref))
    yb = _head_rms(f32(ob0) + f32(ob1), gnb_ref[...], HB, DVB) * _silu(f32(r_ref))
    acc = _sigmoid(f32(ga_ref)) * _dot(ya.astype(BF16), woa_ref[...])
    acc = acc + _sigmoid(f32(gb_ref)) * _dot(yb.astype(BF16), wob_ref[...])
    acc = acc + _sigmoid(f32(gd_ref)) * _dot(od_ref[...], woc_ref[...])
    y = _dot(acc.astype(BF16), wout_ref[...])
    o_ref[...] = _token_tile(x_refs, ctx_tiles) + mod_ref[0, 2:3, :] * y


def _merge(oa0, oa1, ob0, ob1, od, proj, x_ctx, x_lat, lat_first, t, mod, gn_a, gn_b, woa, wob, woc, wout,
           rows_per_mod, row_off, tm):
    d = x_ctx.shape[1]
    off = row_off // tm
    ctx_tiles = rows_per_mod // tm
    n_tiles = (t - row_off) // tm
    per = rows_per_mod // tm
    w5 = HA * DVA
    row = lambda i: (i + off, 0)
    col = lambda c: (lambda i: (i + off, c))
    tok_specs, tok_args = _token_inputs(x_ctx, x_lat, lat_first, tm, ctx_tiles, off)
    return pl.pallas_call(
        functools.partial(_merge_kernel, ctx_tiles=ctx_tiles - off, n_tok=len(tok_args)),
        grid=(n_tiles,),
        in_specs=tok_specs
        + [pl.BlockSpec((tm, w5), row),
                  pl.BlockSpec((tm, w5), row),
                  pl.BlockSpec((tm, w5), col(H_AZ // w5)),
                  pl.BlockSpec((tm, w5), row),
                  pl.BlockSpec((tm, w5), row),
                  pl.BlockSpec((tm, w5), col(H_BR // w5)),
                  pl.BlockSpec((tm, w5), row),
                  pl.BlockSpec((tm, d), col(H_GATES // d)),
                  pl.BlockSpec((tm, d), col(H_GATES // d + 1)),
                  pl.BlockSpec((tm, d), col(H_GATES // d + 2)),
                  pl.BlockSpec((1, 6, d), lambda i: ((i + off) // per, 0, 0)),
                  pl.BlockSpec((1, DVA), lambda i: (0, 0)),
                  pl.BlockSpec((1, DVB), lambda i: (0, 0)),
                  pl.BlockSpec((w5, d), lambda i: (0, 0)),
                  pl.BlockSpec((w5, d), lambda i: (0, 0)),
                  pl.BlockSpec((w5, d), lambda i: (0, 0)),
                  pl.BlockSpec((d, d), lambda i: (0, 0))],
        out_specs=pl.BlockSpec((tm, d), lambda i: (i, 0)),
        out_shape=jax.ShapeDtypeStruct((t - row_off, d), F32),
        compiler_params=_cparams(1),
        name="merge_outproj",
    )(*tok_args, oa0, oa1, proj, ob0, ob1, proj, od, proj, proj, proj, mod,
      gn_a.reshape(1, DVA), gn_b.reshape(1, DVB), woa, wob, woc, wout)


def _ffn_kernel(x_ref, mod_ref, g_ref, fg_ref, w1_ref, w3_ref, w2_ref, o_ref, h_scr, *, final):
    f = pl.program_id(1)

    @pl.when(f == 0)
    def _():
        y = _rms_rows(x_ref[...], g_ref[...])
        h_scr[...] = (y * (1.0 + mod_ref[0, 4:5, :]) + mod_ref[0, 3:4, :]).astype(BF16)
        o_ref[...] = jnp.zeros_like(o_ref)

    h = h_scr[...]
    t = _silu(_dot(h, w1_ref[...])) * _dot(h, w3_ref[...])
    o_ref[...] += _dot(t.astype(BF16), w2_ref[...])

    @pl.when(f == pl.num_programs(1) - 1)
    def _():
        out = x_ref[...] + mod_ref[0, 5:6, :] * o_ref[...]
        if final:
            out = _rms_rows(out, fg_ref[...])
        o_ref[...] = out


def _ffn(x, mod, g, final_g, w1, w3, w2, rows_per_mod, mod_off, tm, tf, final):
    t, d = x.shape
    dff = w1.shape[1]
    per = rows_per_mod // tm
    return pl.pallas_call(
        functools.partial(_ffn_kernel, final=final),
        grid=(t // tm, dff // tf),
        in_specs=[pl.BlockSpec((tm, d), lambda i, f: (i, 0)),
                  pl.BlockSpec((1, 6, d), lambda i, f: (i // per + mod_off, 0, 0)),
                  pl.BlockSpec((1, d), lambda i, f: (0, 0)),
                  pl.BlockSpec((1, d), lambda i, f: (0, 0)),
                  pl.BlockSpec((d, tf), lambda i, f: (0, f)),
                  pl.BlockSpec((d, tf), lambda i, f: (0, f)),
                  pl.BlockSpec((tf, d), lambda i, f: (f, 0))],
        out_specs=pl.BlockSpec((tm, d), lambda i, f: (i, 0)),
        out_shape=jax.ShapeDtypeStruct((t, d), F32),
        scratch_shapes=[pltpu.VMEM((tm, d), BF16)],
        compiler_params=_cparams(2),
        name="dense_ffn",
    )(x, mod, g.reshape(1, d), final_g.reshape(1, d), w1, w3, w2)


MOE_ROWS = 256


def _moe_kernel(x_ref, mod_ref, g_ref, rw_ref, tri_ref, fg_ref, w1_ref, w3_ref, w2_ref, o_ref,
                h_scr, gate_scr, sel_scr, rank_scr, selr_scr, rankr_scr, xg_scr, yg_scr, nb_scr, *, final):
    e = pl.program_id(1)
    f = pl.program_id(2)
    n_f = pl.num_programs(2)
    tm, d = x_ref.shape
    cb = MOE_ROWS
    half = cb // 2
    lane = lax.broadcasted_iota(jnp.int32, (1, LANES), 1)

    def for_blocks(body):
        nb = nb_scr[0]

        def full(j, carry):
            body(pl.multiple_of(j * cb, cb), cb)
            return carry

        lax.fori_loop(0, nb, full, 0)

        @pl.when(nb_scr[1] == 1)
        def _():
            body(pl.multiple_of(nb * cb, cb), half)

    @pl.when((e == 0) & (f == 0))
    def _():
        y = _rms_rows(x_ref[...], g_ref[...])
        hmod = y * (1.0 + mod_ref[0, 4:5, :]) + mod_ref[0, 3:4, :]
        h_scr[...] = hmod.astype(BF16)
        o_ref[...] = jnp.zeros_like(o_ref)
        logits = _dot3(hmod, rw_ref[...])
        logits = jnp.where(lane < N_EXPERTS, logits, -jnp.inf)
        lanef = lane.astype(F32)
        m1 = jnp.max(logits, axis=1, keepdims=True)
        i1 = jnp.min(jnp.where(logits == m1, lanef, float(LANES)), axis=1, keepdims=True)
        hit1 = lanef == i1
        rest = jnp.where(hit1, -jnp.inf, logits)
        m2 = jnp.max(rest, axis=1, keepdims=True)
        i2 = jnp.min(jnp.where(rest == m2, lanef, float(LANES)), axis=1, keepdims=True)
        hit2 = lanef == i2
        ex = jnp.exp(m2 - m1)
        inv = 1.0 / (1.0 + ex)
        gate_scr[...] = jnp.where(hit1, inv, 0.0) + jnp.where(hit2, ex * inv, 0.0)
        sel = jnp.where(hit1 | hit2, 1.0, 0.0)
        sel_scr[...] = sel
        sel16 = sel.astype(BF16)
        rank = _dot(tri_ref[...], sel16)
        rank_scr[...] = rank
        pick = jnp.where(lax.broadcasted_iota(jnp.int32, (8, LANES), 0)
                         == lax.broadcasted_iota(jnp.int32, (8, LANES), 1), 1.0, 0.0).astype(BF16)
        selr_scr[...] = _dot_nt(pick, sel16)
        hi = jnp.floor(rank * (1.0 / 32.0))
        lo = rank - 32.0 * hi
        rankr_scr[...] = 32.0 * _dot_nt(pick, hi.astype(BF16)) + _dot_nt(pick, lo.astype(BF16))

    @pl.when(f == 0)
    def _():
        sel_row = selr_scr[pl.ds(e, 1), :]
        rank_row = rankr_scr[pl.ds(e, 1), :]
        n_e = jnp.sum(sel_row).astype(jnp.int32)
        n_half = (n_e + (half - 1)) // half
        nb_scr[0] = n_half // 2
        nb_scr[1] = n_half % 2

        def gather(r0, bs):
            slot = lax.broadcasted_iota(jnp.int32, (bs, tm), 0).astype(F32) + r0.astype(F32)
            onehot = jnp.where((rank_row == slot) & (sel_row > 0.0), 1.0, 0.0)
            xg_scr[pl.ds(r0, bs), :] = _dot(onehot.astype(BF16), h_scr[...]).astype(BF16)
            yg_scr[pl.ds(r0, bs), :] = jnp.zeros((bs, d), F32)

        for_blocks(gather)

    def expert(r0, bs):
        xb = xg_scr[pl.ds(r0, bs), :]
        t = _silu(_dot(xb, w1_ref[0])) * _dot(xb, w3_ref[0])
        yg_scr[pl.ds(r0, bs), :] += _dot(t.astype(BF16), w2_ref[0])

    for_blocks(expert)

    @pl.when(f == n_f - 1)
    def _():
        pick_e = lane == e
        rank_col = jnp.sum(jnp.where(pick_e, rank_scr[...], 0.0), axis=1, keepdims=True)
        sel_col = jnp.sum(jnp.where(pick_e, sel_scr[...], 0.0), axis=1, keepdims=True)
        gate_col = jnp.sum(jnp.where(pick_e, gate_scr[...], 0.0), axis=1, keepdims=True)

        def scatter(r0, bs):
            slot = lax.broadcasted_iota(jnp.int32, (tm, bs), 1).astype(F32) + r0.astype(F32)
            onehot = jnp.where((rank_col == slot) & (sel_col > 0.0), 1.0, 0.0)
            o_ref[...] += gate_col * _dot(onehot.astype(BF16), yg_scr[pl.ds(r0, bs), :].astype(BF16))

        for_blocks(scatter)

    @pl.when((e == pl.num_programs(1) - 1) & (f == n_f - 1))
    def _():
        out = x_ref[...] + mod_ref[0, 5:6, :] * o_ref[...]
        if final:
            out = _rms_rows(out, fg_ref[...])
        o_ref[...] = out


def _moe(x, mod, g, router_w, final_g, w1, w3, w2, rows_per_mod, mod_off, tm, tf, final):
    t, d = x.shape
    n_e, _, dff = w1.shape
    per = rows_per_mod // tm
    return pl.pallas_call(
        functools.partial(_moe_kernel, final=final),
        grid=(t // tm, n_e, dff // tf),
        in_specs=[pl.BlockSpec((tm, d), lambda i, e, f: (i, 0)),
                  pl.BlockSpec((1, 6, d), lambda i, e, f: (i // per + mod_off, 0, 0)),
                  pl.BlockSpec((1, d), lambda i, e, f: (0, 0)),
                  pl.BlockSpec((d, LANES), lambda i, e, f: (0, 0)),
                  pl.BlockSpec((tm, tm), lambda i, e, f: (0, 0)),
                  pl.BlockSpec((1, d), lambda i, e, f: (0, 0)),
                  pl.BlockSpec((1, d, tf), lambda i, e, f: (e, 0, f)),
                  pl.BlockSpec((1, d, tf), lambda i, e, f: (e, 0, f)),
                  pl.BlockSpec((1, tf, d), lambda i, e, f: (e, f, 0))],
        out_specs=pl.BlockSpec((tm, d), lambda i, e, f: (i, 0)),
        out_shape=jax.ShapeDtypeStruct((t, d), F32),
        scratch_shapes=[pltpu.VMEM((tm, d), BF16),
                        pltpu.VMEM((tm, LANES), F32),
                        pltpu.VMEM((tm, LANES), F32),
                        pltpu.VMEM((tm, LANES), F32),
                        pltpu.VMEM((8, tm), F32),
                        pltpu.VMEM((8, tm), F32),
                        pltpu.VMEM((tm, d), BF16),
                        pltpu.VMEM((tm, d), F32),
                        pltpu.SMEM((2,), jnp.int32)],
        compiler_params=_cparams(3),
        name="routed_ffn_sparse",
    )(x, mod, g.reshape(1, d), router_w, jnp.tril(jnp.ones((tm, tm), BF16), -1),
      final_g.reshape(1, d), w1, w3, w2)


def _reorder_w_in(w):
    d = w.shape[0]
    a_qkv = 2 * HA * DKA + HA * DVA
    a_z = a_qkv + HA * DVA
    b_q = a_z + 4 * HA
    b_v = b_q + 2 * HB * DKB
    b_glr = b_v + 2 * HB * DVB
    c_q = b_glr + 2 * GATE_RANK
    gates = c_q + 4 * HC * DKC + HC * DVC
    pad = N_F32 - F_SMALL - 4 * HA - 2 * GATE_RANK
    w = w.astype(BF16)
    return jnp.concatenate([w[:, :a_qkv], w[:, a_z:b_q], w[:, b_glr:c_q], jnp.zeros((d, pad), BF16),
                            w[:, gates:], w[:, a_qkv:a_z], w[:, b_v:b_glr], w[:, c_q:gates], w[:, b_q:b_v]], axis=1)


def kernel(x, c, ctx, c_ctx, w_mod, b_mod, norm1_g, norm2_g, w_in, conv_a, a_log, dt_bias, gn_a, w_gate2, b_gate, gn_b, lam_c, gn_c, w_o_a, w_o_b, w_o_c, w_out, ffn_w1, ffn_w3, ffn_w2, router_w, moe_w1, moe_w3, moe_w2, final_g):
    n_b, l, d = x.shape
    lc = ctx.shape[1]
    depth = w_mod.shape[0]
    assert n_b * lc == l and lc % CHUNK == 0 and l & (l - 1) == 0 and lc & (lc - 1) == 0
    t_ctx = n_b * lc
    tm = min(1024, l)
    tm_small = min(512, l)

    n_mod = 16
    cvec = jnp.concatenate([c_ctx[None, :], c, jnp.zeros((n_mod - 1 - n_b, d), F32)], axis=0)
    mod_all = _modulation(cvec, w_mod, b_mod).reshape(depth, n_mod, 6, d)

    rope_cos, rope_sin = _rope_tables(l)
    t_all = t_ctx + n_b * l
    tokens = (ctx.reshape(t_ctx, d), x.reshape(n_b * l, d), 0)

    for layer in range(depth):
        ctx_out = layer < depth - 1
        lam_init = 0.8 - 0.6 * math.exp(-0.3 * layer)
        mod = mod_all[layer]
        proj32, proj16 = _in_projection(*tokens, t_all, mod, norm1_g[layer], _reorder_w_in(w_in[layer]), l, tm)

        qkvc = _short_conv(proj32, conv_a[layer], lc, l)
        oa0, oa1 = _dn_scan(_dn_factors(qkvc, proj32, a_log[layer], dt_bias[layer], n_b, lc, l), n_b, lc, l)
        ob0, ob1 = _gla(proj32, proj16, w_gate2[layer], b_gate[layer], n_b, lc, l)
        od = _attention(proj16, lam_c[layer], gn_c[layer], rope_cos, rope_sin, n_b, lc, l, lam_init, ctx_out)

        row_off = 0 if ctx_out else t_ctx
        xs_new = _merge(oa0, oa1, ob0, ob1, od, proj16, *tokens, t_all, mod, gn_a[layer], gn_b[layer],
                        w_o_a[layer].astype(BF16), w_o_b[layer].astype(BF16), w_o_c[layer].astype(BF16),
                        w_out[layer].astype(BF16), l, row_off, tm_small)

        i = layer // 2
        final = layer == depth - 1
        mod_off = 0 if ctx_out else 1
        if layer % 2 == 0:
            xs = _ffn(xs_new, mod, norm2_g[layer], final_g, ffn_w1[i].astype(BF16),
                      ffn_w3[i].astype(BF16), ffn_w2[i].astype(BF16), l, mod_off, tm_small,
                      ffn_w1.shape[2] // 2, final)
        else:
            rw = jnp.pad(router_w[i], ((0, 0), (0, LANES - N_EXPERTS)))
            xs = _moe(xs_new, mod, norm2_g[layer], rw, final_g, moe_w1[i].astype(BF16),
                      moe_w3[i].astype(BF16), moe_w2[i].astype(BF16), l, mod_off, tm,
                      moe_w1.shape[3] // 2, final)
        tokens = (xs, xs, t_ctx)

    return xs.reshape(n_b, l, d)
```

```python
import functools
import math

import jax
import jax.numpy as jnp
from jax import lax
from jax.experimental import pallas as pl
from jax.experimental.pallas import tpu as pltpu

F32 = jnp.float32
BF16 = jnp.bfloat16
HIGHEST = lax.Precision.HIGHEST

EPS = 1e-6
HA, DKA, DVA = 4, 128, 128
HB, DKB, DVB = 4, 64, 128
GATE_RANK, GATE_NORM = 16, 16.0
HC, DKC, DVC = 4, 64, 128
GRID_W, ROPE_THETA = 64, 10000.0
N_EXPERTS, TOP_K = 8, 2
CHUNK = 64
LANES = 128
ATTN_SUB = 64
ATTN_TQ = 512
CONV_HEADS = 2
DN_HEADS = 4

F_QKV = 0
F_SMALL = 1536
N_F32 = 1664
H_GATES = 0
H_AZ = 3072
H_BV = 3584
H_BR = 4096
H_CQ = 4608
H_CK = 5120
H_CV = 5632
H_BQ = 6144
H_BK = 6400
N_B16 = 6656
PROJ_TN = 1664

VMEM_LIMIT = 56 * 1024 * 1024


def _cparams(n_axes):
    return pltpu.CompilerParams(dimension_semantics=("arbitrary",) * n_axes,
                                vmem_limit_bytes=VMEM_LIMIT)


def _sigmoid(x):
    return 1.0 / (1.0 + jnp.exp(-x))


def _silu(x):
    return x * _sigmoid(x)


def _softplus(x):
    return jnp.maximum(x, 0.0) + jnp.log(1.0 + jnp.exp(-jnp.abs(x)))


def _dot(a, b):
    return jnp.dot(a, b, preferred_element_type=F32)


def _dot_nt(a, b):
    return lax.dot_general(a, b, (((1,), (1,)), ((), ())), preferred_element_type=F32)


def _dot_tn(a, b):
    return lax.dot_general(a, b, (((0,), (0,)), ((), ())), preferred_element_type=F32)


def _split(a):
    hi = a.astype(BF16)
    lo = (a - hi.astype(F32)).astype(BF16)
    return hi, lo


def _dot3(a, b):
    ah, al = _split(a)
    bh, bl = _split(b)
    return _dot(ah, bh) + (_dot(ah, bl) + _dot(al, bh))


def _rms_rows(x, gain):
    ms = jnp.mean(x * x, axis=-1, keepdims=True)
    return x * lax.rsqrt(ms + EPS) * gain


def _mod_kernel(c_ref, w_ref, b_ref, o_ref):
    s = _silu(c_ref[...])
    o_ref[0] = jnp.dot(s, w_ref[0], precision=HIGHEST, preferred_element_type=F32) + b_ref[0]


def _modulation(cvec, w_mod, b_mod):
    depth, d, d6 = w_mod.shape
    n = cvec.shape[0]
    return pl.pallas_call(
        _mod_kernel,
        grid=(depth, d6 // d),
        in_specs=[pl.BlockSpec((n, d), lambda l, j: (0, 0)),
                  pl.BlockSpec((1, d, d), lambda l, j: (l, 0, j)),
                  pl.BlockSpec((1, 1, d), lambda l, j: (l, 0, j))],
        out_specs=pl.BlockSpec((1, n, d), lambda l, j: (l, 0, j)),
        out_shape=jax.ShapeDtypeStruct((depth, n, d6), F32),
        compiler_params=_cparams(2),
        name="adaln_mod",
    )(cvec, w_mod, b_mod.reshape(depth, 1, d6))


def _token_specs(tm, d, ctx_tiles, lat_first, off=0):
    return [pl.BlockSpec((tm, d), lambda i, *_: (jnp.minimum(i + off, ctx_tiles - 1), 0)),
            pl.BlockSpec((tm, d), lambda i, *_: (jnp.maximum(i + off - ctx_tiles, 0) + lat_first, 0))]


def _token_inputs(x_ctx, x_lat, lat_first, tm, ctx_tiles, off=0):
    d = x_ctx.shape[1]
    if x_lat is x_ctx:
        return [pl.BlockSpec((tm, d), lambda i, *_: (i + off, 0))], [x_ctx]
    return _token_specs(tm, d, ctx_tiles, lat_first // tm, off), [x_ctx, x_lat]


def _token_tile(x_refs, ctx_tiles):
    if len(x_refs) == 1:
        return x_refs[0][...]
    return jnp.where(pl.program_id(0) < ctx_tiles, x_refs[0][...], x_refs[1][...])


def _inproj_kernel(*refs, ctx_tiles, n_tok):
    x_refs, (mod_ref, g_ref, w_ref, o32_ref, o16_ref, h_scr) = refs[:n_tok], refs[n_tok:]
    j = pl.program_id(1)

    @pl.when(j == 0)
    def _():
        y = _rms_rows(_token_tile(x_refs, ctx_tiles), g_ref[...])
        h_scr[...] = (y * (1.0 + mod_ref[0, 1:2, :]) + mod_ref[0, 0:1, :]).astype(BF16)
        o32_ref[...] = _dot(h_scr[...], w_ref[0])

    @pl.when(j > 0)
    def _():
        o16_ref[...] = _dot(h_scr[...], w_ref[0]).astype(BF16)


def _in_projection(x_ctx, x_lat, lat_first, t, mod, g, w, layer, rows_per_mod, tm):
    d = x_ctx.shape[1]
    tn = PROJ_TN
    assert w.shape[2] == N_F32 + N_B16 and N_F32 == tn and N_B16 % tn == 0
    per = rows_per_mod // tm
    ctx_tiles = rows_per_mod // tm
    tok_specs, tok_args = _token_inputs(x_ctx, x_lat, lat_first, tm, ctx_tiles)
    return pl.pallas_call(
        functools.partial(_inproj_kernel, ctx_tiles=ctx_tiles, n_tok=len(tok_args)),
        grid=(t // tm, 1 + N_B16 // tn),
        in_specs=tok_specs
        + [pl.BlockSpec((1, 6, d), lambda i, j: (i // per, 0, 0)),
           pl.BlockSpec((1, d), lambda i, j: (0, 0)),
           pl.BlockSpec((1, d, tn), lambda i, j: (layer, 0, j))],
        out_specs=[pl.BlockSpec((tm, tn), lambda i, j: (i, 0)),
                   pl.BlockSpec((tm, tn), lambda i, j: (i, jnp.maximum(j - 1, 0)))],
        out_shape=[jax.ShapeDtypeStruct((t, N_F32), F32), jax.ShapeDtypeStruct((t, N_B16), BF16)],
        scratch_shapes=[pltpu.VMEM((tm, d), BF16)],
        compiler_params=_cparams(2),
        name="norm_mod_inproj",
    )(*tok_args, mod, g.reshape(1, d), w)


def _conv_kernel(x_ref, w_ref, o_ref, *, lc, l):
    p = pl.program_id(0)
    j = pl.program_id(1)
    x = x_ref[...]
    n = x.shape[0]
    row = lax.broadcasted_iota(jnp.int32, x.shape, 0)
    seq = jnp.where(p == 0, lc, l)
    pos = row & (seq - 1)
    prev = jnp.where(pos == 0, 0.0, pltpu.roll(x, 1, 0))
    nxt = jnp.where(pos == seq - 1, 0.0, pltpu.roll(x, n - 1, 0))
    w = w_ref[...]
    y = _silu(prev * w[0:1] + x * w[1:2] + nxt * w[2:3])
    heads_per_block = x.shape[1] // DKA
    for hh in range(heads_per_block):
        cols = slice(hh * DKA, (hh + 1) * DKA)
        yh = y[:, cols]
        inv = lax.rsqrt(jnp.sum(yh * yh, axis=-1, keepdims=True) + EPS)
        head = j * heads_per_block + hh
        fac = jnp.where(head < HA, inv * DKA ** -0.5, jnp.where(head < 2 * HA, inv, 1.0))
        o_ref[:, cols] = yh * fac


def _short_conv(proj, conv_w, lc, l):
    t = proj.shape[0]
    ncol = conv_w.shape[1]
    wide = CONV_HEADS * DKA
    return pl.pallas_call(
        functools.partial(_conv_kernel, lc=lc, l=l),
        grid=(t // l, ncol // wide),
        in_specs=[pl.BlockSpec((l, wide), lambda p, j: (p, j)),
                  pl.BlockSpec((conv_w.shape[0], wide), lambda p, j: (0, j))],
        out_specs=pl.BlockSpec((l, wide), lambda p, j: (p, j)),
        out_shape=jax.ShapeDtypeStruct((t, ncol), F32),
        compiler_params=_cparams(2),
        name="conv_silu_l2",
    )(proj, conv_w)


def _row_block(b, g, n_b, n_lat):
    return jnp.where(g == 0, b, n_b + b * n_lat + g - 1)


def _rev_group(g, n_g):
    return jnp.where(g == 0, 0, n_g - g)


def _tri_inverse_many(lms, ii, jj, uppers):
    c = lms[0].shape[0]
    eye = (ii == jj).astype(F32)
    n = len(lms)
    ts = None
    s, ls = 1, 0
    while s < c:
        same = (ii >> (ls + 1)) == (jj >> (ls + 1))
        io = (ii >> ls) & 1
        jo = (jj >> ls) & 1
        lower = same & (io == 1) & (jo == 0)
        upper = same & (io == 0) & (jo == 1)
        cms = [jnp.where(upper if uppers[i] else lower, lms[i], 0.0) for i in range(n)]
        if ts is None:
            ts = [eye - cm for cm in cms]
        else:
            t16 = [t.astype(BF16) for t in ts]
            ps = [_dot(t16[i], cms[i].astype(BF16)).astype(BF16) for i in range(n)]
            ts = [ts[i] - _dot(ps[i], t16[i]) for i in range(n)]
        s, ls = s * 2, ls + 1
    a_parts = [_split(eye + lm) for lm in lms]
    t_parts = [_split(t) for t in ts]
    res = [eye - (_dot(a_parts[i][0], t_parts[i][0])
                  + (_dot(a_parts[i][0], t_parts[i][1]) + _dot(a_parts[i][1], t_parts[i][0])))
           for i in range(n)]
    return [ts[i] + _dot(t_parts[i][0], res[i].astype(BF16)) for i in range(n)]


def _dn1_kernel(alog_ref, dtb_ref, q_ref, k_ref, v_ref, sm_ref,
                u_ref, w_ref, kg_ref, qg_ref, qk_ref, gl_ref, *, n_chunks):
    hp = pl.program_id(2)
    c_ = CHUNK
    sm = sm_ref[...]
    sm_t = sm.T
    ii = lax.broadcasted_iota(jnp.int32, (c_, c_), 0)
    jj = lax.broadcasted_iota(jnp.int32, (c_, c_), 1)
    rows = [slice(c * c_, (c + 1) * c_) for c in range(n_chunks)]
    lanes = [slice(j * LANES, (j + 1) * LANES) for j in range(DN_HEADS)]
    blocks = [(j, c) for j in range(DN_HEADS) for c in range(n_chunks)]
    qs = {jc: q_ref[rows[jc[1]], lanes[jc[0]]] for jc in blocks}
    ks = {jc: k_ref[rows[jc[1]], lanes[jc[0]]] for jc in blocks}
    vs = {jc: v_ref[rows[jc[1]], lanes[jc[0]]] for jc in blocks}
    k16 = {jc: ks[jc].astype(BF16) for jc in blocks}
    kks = {jc: _dot_nt(k16[jc], k16[jc]) for jc in blocks}
    qks = {jc: _dot_nt(qs[jc].astype(BF16), k16[jc]) for jc in blocks}

    items = [(j, c, d) for j, c in blocks for d in range(2)]
    betas, gcs, totals, gams, lms = [], [], [], [], []
    for j, c, d in items:
        r = rows[c]
        h = hp * DN_HEADS + j
        incl = (jj <= ii) if d == 0 else (jj >= ii)
        strict = (jj < ii) if d == 0 else (jj > ii)
        incl_t = (ii <= jj) if d == 0 else (ii >= jj)
        a_scale = -jnp.exp(jnp.full((1, 1), alog_ref[d * HA + h], F32))
        dtb = dtb_ref[d * HA + h]
        col = d * HA + j
        beta = _sigmoid(sm[r, col:col + 1])
        g_col = a_scale * _softplus(sm[r, 2 * HA + col:2 * HA + col + 1] + dtb)
        g_row = a_scale * _softplus(sm_t[2 * HA + col:2 * HA + col + 1, r] + dtb)
        gc_col = jnp.sum(jnp.where(incl, g_row, 0.0), axis=1, keepdims=True)
        gc_row = jnp.sum(jnp.where(incl_t, g_col, 0.0), axis=0, keepdims=True)
        gam = jnp.where(incl, jnp.exp(gc_col - gc_row), 0.0)
        betas.append(beta)
        gcs.append(gc_col)
        totals.append(jnp.sum(g_col, axis=0, keepdims=True))
        gams.append(gam)
        lms.append(jnp.where(strict, beta * kks[(j, c)] * gam, 0.0))

    tinvs = _tri_inverse_many(lms, ii, jj, [d == 1 for _, _, d in items])
    egs = [jnp.exp(gc) for gc in gcs]
    uws = []
    for i, (j, c, d) in enumerate(items):
        rhs = jnp.concatenate([vs[(j, c)] * betas[i], ks[(j, c)] * (betas[i] * egs[i])], axis=1).astype(BF16)
        uws.append(_dot(tinvs[i].astype(BF16), rhs))
    for i, (j, c, d) in enumerate(items):
        r = rows[c]
        u_ref[d, r, lanes[j]] = uws[i][:, :DVA]
        w_ref[d, r, lanes[j]] = uws[i][:, DVA:].astype(BF16)
        kg_ref[d, r, lanes[j]] = (ks[(j, c)] * jnp.exp(totals[i] - gcs[i])).astype(BF16)
        qg_ref[d, r, lanes[j]] = (qs[(j, c)] * egs[i]).astype(BF16)
        qk_ref[d, j, r, :] = (qks[(j, c)] * gams[i]).astype(BF16)
        gl_ref[0, 0, j, c * 2 + d:c * 2 + d + 1, :] = jnp.broadcast_to(jnp.exp(totals[i]), (1, LANES))


def _dn_factors(qkvc, proj32, a_log, dt_bias, n_b, lc, l):
    t = qkvc.shape[0]
    n_lat = l // lc
    n_g = n_lat + 1
    n_chunks = lc // CHUNK
    assert DN_HEADS == HA
    nh = DN_HEADS
    wide = nh * LANES
    steps = HA // nh
    rb = lambda b, g, h: _row_block(b, g, n_b, n_lat)
    tok = lambda dt: jax.ShapeDtypeStruct((2, t, HA * DVA), dt)
    smem = pl.BlockSpec(memory_space=pltpu.SMEM)
    return pl.pallas_call(
        functools.partial(_dn1_kernel, n_chunks=n_chunks),
        grid=(n_b, n_g, steps),
        in_specs=[smem, smem,
                  pl.BlockSpec((lc, wide), lambda b, g, h: (rb(b, g, h), h)),
                  pl.BlockSpec((lc, wide), lambda b, g, h: (rb(b, g, h), steps + h)),
                  pl.BlockSpec((lc, wide), lambda b, g, h: (rb(b, g, h), 2 * steps + h)),
                  pl.BlockSpec((lc, LANES), lambda b, g, h: (rb(b, g, h), F_SMALL // LANES))],
        out_specs=[pl.BlockSpec((2, lc, wide), lambda b, g, h: (0, rb(b, g, h), h))] * 4
        + [pl.BlockSpec((2, nh, lc, CHUNK), lambda b, g, h: (0, h, rb(b, g, h), 0)),
           pl.BlockSpec((1, 1, nh, 2 * n_chunks, LANES), lambda b, g, h: (b, g, h, 0, 0))],
        out_shape=[tok(F32), tok(BF16), tok(BF16), tok(BF16),
                   jax.ShapeDtypeStruct((2, HA, t, CHUNK), BF16),
                   jax.ShapeDtypeStruct((n_b, n_g, HA, 2 * n_chunks, LANES), F32)],
        compiler_params=_cparams(3),
        name="deltanet_chunk_factors",
    )(a_log.reshape(-1), dt_bias.reshape(-1), qkvc, qkvc, qkvc, proj32)


def _dn2_kernel(u0, w0, kg0, qg0, qk0, gl0, u1, w1, kg1, qg1, qk1, gl1, o0_ref, o1_ref, s_scr,
                *, n_chunks):
    @pl.when(pl.program_id(1) == 0)
    def _():
        s_scr[...] = jnp.zeros_like(s_scr)

    c_ = CHUNK
    refs = ((u0, w0, kg0, qg0, qk0, gl0, o0_ref), (u1, w1, kg1, qg1, qk1, gl1, o1_ref))
    chains = [(h, d) for h in range(HA) for d in range(2)]
    states = [s_scr[d * HA + h] for h, d in chains]
    for step in range(n_chunks):
        pos = []
        for h, d in chains:
            c = step if d == 0 else n_chunks - 1 - step
            pos.append((slice(c * c_, (c + 1) * c_), slice(h * DVA, (h + 1) * DVA), c))
        s16 = [s.astype(BF16) for s in states]
        prods = []
        for i, (h, d) in enumerate(chains):
            r, hs, _ = pos[i]
            wq = jnp.concatenate([refs[d][1][0, r, hs], refs[d][3][0, r, hs]], axis=0)
            prods.append(_dot(wq, s16[i]))
        vn16 = [(refs[d][0][0, pos[i][0], pos[i][1]] - prods[i][:c_]).astype(BF16)
                for i, (h, d) in enumerate(chains)]
        for i, (h, d) in enumerate(chains):
            r, hs, c = pos[i]
            u, w, kg, qg, qk, gl, o_ref = refs[d]
            o_ref[r, hs] = (prods[i][c_:] + _dot(qk[0, h, r, :], vn16[i])).astype(BF16)
            states[i] = (gl[0, 0, h, c * 2 + d:c * 2 + d + 1, :] * states[i]
                         + _dot_tn(kg[0, r, hs], vn16[i]))
    for i, (h, d) in enumerate(chains):
        s_scr[d * HA + h] = states[i]


def _dn_scan(factors, n_b, lc, l):
    u, w, kg, qg, qk, gl = factors
    t = u.shape[1]
    n_lat = l // lc
    n_g = n_lat + 1
    n_chunks = lc // CHUNK
    width = HA * DVA
    rb0 = lambda b, g: _row_block(b, g, n_b, n_lat)
    rb1 = lambda b, g: _row_block(b, _rev_group(g, n_g), n_b, n_lat)

    def specs(d, rb, gsel):
        tok = pl.BlockSpec((1, lc, width), lambda b, g: (d, rb(b, g), 0))
        return [tok, tok, tok, tok,
                pl.BlockSpec((1, HA, lc, CHUNK), lambda b, g: (d, 0, rb(b, g), 0)),
                pl.BlockSpec((1, 1, HA, 2 * n_chunks, LANES), lambda b, g: (b, gsel(g), 0, 0, 0))]

    return pl.pallas_call(
        functools.partial(_dn2_kernel, n_chunks=n_chunks),
        grid=(n_b, n_g),
        in_specs=specs(0, rb0, lambda g: g) + specs(1, rb1, lambda g: _rev_group(g, n_g)),
        out_specs=[pl.BlockSpec((lc, width), lambda b, g: (rb0(b, g), 0)),
                   pl.BlockSpec((lc, width), lambda b, g: (rb1(b, g), 0))],
        out_shape=[jax.ShapeDtypeStruct((t, width), BF16)] * 2,
        scratch_shapes=[pltpu.VMEM((2 * HA, DKA, DVA), F32)],
        compiler_params=_cparams(2),
        name="deltanet_scan",
    )(u, w, kg, qg, qk, gl, u, w, kg, qg, qk, gl)


def _gla_kernel(q0, k0, v0, sm0, q1, k1, v1, sm1, wg_ref, bg_ref, o0_ref, o1_ref, s_scr, *, n_chunks):
    @pl.when(pl.program_id(1) == 0)
    def _():
        s_scr[...] = jnp.zeros_like(s_scr)

    c_ = CHUNK
    sh = c_.bit_length() - 1
    g_ = n_chunks * c_
    kw = HB * DKB
    vw = HB * DVB
    ii = lax.broadcasted_iota(jnp.int32, (g_, g_), 0)
    jj = lax.broadcasted_iota(jnp.int32, (g_, g_), 1)
    same_chunk = (ii >> sh) == (jj >> sh)
    klane = lax.broadcasted_iota(jnp.int32, (1, kw), 1)
    srow = lax.broadcasted_iota(jnp.int32, (kw, vw), 0) >> (DKB.bit_length() - 1)
    scol = lax.broadcasted_iota(jnp.int32, (kw, vw), 1) >> (DVB.bit_length() - 1)
    same_head = srow == scol
    ind16 = jnp.where((lax.broadcasted_iota(jnp.int32, (g_, n_chunks * LANES), 0) >> sh)
                      == (lax.broadcasted_iota(jnp.int32, (g_, n_chunks * LANES), 1) >> (LANES.bit_length() - 1)),
                      1.0, 0.0).astype(BF16)
    rows = [slice(c * c_, (c + 1) * c_) for c in range(n_chunks)]
    refs = ((q0, k0, v0, sm0, o0_ref), (q1, k1, v1, sm1, o1_ref))

    def spread(x, picks):
        return jnp.concatenate([jnp.broadcast_to(x[p:p + 1, :], (c_, kw)) for p in picks], axis=0)

    dirs = (0, 1)
    incls = [same_chunk & ((jj <= ii) if d == 0 else (jj >= ii)) for d in dirs]
    tri16 = [jnp.where(m, 1.0, 0.0).astype(BF16) for m in incls]
    pres = [jnp.dot(refs[d][3][:, 4 * HA + d * GATE_RANK:4 * HA + (d + 1) * GATE_RANK], wg_ref[d],
                    precision=HIGHEST, preferred_element_type=F32) + bg_ref[d] for d in dirs]
    glogs = [(jnp.minimum(p, 0.0) - jnp.log(1.0 + jnp.exp(-jnp.abs(p)))) * (1.0 / GATE_NORM) for p in pres]
    parts = [_split(g) for g in glogs]
    gcs = [_dot(tri16[d], parts[d][0]) + _dot(tri16[d], parts[d][1]) for d in dirs]
    tot_cols = [_dot_tn(parts[d][0], ind16) + _dot_tn(parts[d][1], ind16) for d in dirs]
    mids = [spread(gcs[d], [c * c_ + c_ // 2 - 1 for c in range(n_chunks)]) for d in dirs]
    tots = [spread(gcs[d], [c * c_ + (c_ - 1 if d == 0 else 0) for c in range(n_chunks)]) for d in dirs]
    qs = [refs[d][0][...].astype(F32) * DKB ** -0.5 for d in dirs]
    ks = [refs[d][1][...].astype(F32) for d in dirs]
    v16s = [refs[d][2][...] for d in dirs]
    qts = [qs[d] * jnp.exp(gcs[d] - mids[d]) for d in dirs]
    kt16s = [(ks[d] * jnp.exp(mids[d] - gcs[d])).astype(BF16) for d in dirs]
    qg16s = [(qs[d] * jnp.exp(gcs[d])).astype(BF16) for d in dirs]
    kg16s = [(ks[d] * jnp.exp(tots[d] - gcs[d])).astype(BF16) for d in dirs]
    a16s = {}
    for h in range(HB):
        in_head = (klane >= h * DKB) & (klane < (h + 1) * DKB)
        for d in dirs:
            a = _dot_nt(jnp.where(in_head, qts[d], 0.0).astype(BF16), kt16s[d])
            a16s[(d, h)] = jnp.where(incls[d], a, 0.0).astype(BF16)
    o_intra = [[_dot(a16s[(d, h)], v16s[d][:, h * DVB:(h + 1) * DVB]) for h in range(HB)] for d in dirs]
    xs = [[jnp.where(same_head, _dot_tn(kg16s[d][r, :], v16s[d][r, :]), 0.0) for r in rows] for d in dirs]
    decays = [[jnp.exp(tot_cols[d][:, c * LANES:(c + 1) * LANES]) for c in range(n_chunks)] for d in dirs]

    states = [s_scr[0], s_scr[1]]
    for step in range(n_chunks):
        for d in range(2):
            c = step if d == 0 else n_chunks - 1 - step
            r = rows[c]
            o_inter = _dot(qg16s[d][r, :], states[d].astype(BF16))
            for h in range(HB):
                vs = slice(h * DVB, (h + 1) * DVB)
                refs[d][4][r, vs] = (o_inter[:, vs] + o_intra[d][h][r, :]).astype(BF16)
            states[d] = jnp.concatenate([decays[d][c]] * (vw // LANES), axis=1) * states[d] + xs[d][c]
    s_scr[0] = states[0]
    s_scr[1] = states[1]


def _gla(proj32, proj16, w_gate2, b_gate, n_b, lc, l):
    t = proj16.shape[0]
    n_lat = l // lc
    n_g = n_lat + 1
    n_chunks = lc // CHUNK
    kw, vw = HB * DKB, HB * DVB
    rb0 = lambda b, g: _row_block(b, g, n_b, n_lat)
    rb1 = lambda b, g: _row_block(b, _rev_group(g, n_g), n_b, n_lat)

    def specs(rb):
        return [pl.BlockSpec((lc, kw), lambda b, g: (rb(b, g), H_BQ // kw)),
                pl.BlockSpec((lc, kw), lambda b, g: (rb(b, g), H_BK // kw)),
                pl.BlockSpec((lc, vw), lambda b, g: (rb(b, g), H_BV // vw)),
                pl.BlockSpec((lc, LANES), lambda b, g: (rb(b, g), F_SMALL // LANES))]

    return pl.pallas_call(
        functools.partial(_gla_kernel, n_chunks=n_chunks),
        grid=(n_b, n_g),
        in_specs=specs(rb0) + specs(rb1)
        + [pl.BlockSpec((2, GATE_RANK, kw), lambda b, g: (0, 0, 0)),
           pl.BlockSpec((2, 1, kw), lambda b, g: (0, 0, 0))],
        out_specs=[pl.BlockSpec((lc, vw), lambda b, g: (rb0(b, g), 0)),
                   pl.BlockSpec((lc, vw), lambda b, g: (rb1(b, g), 0))],
        out_shape=[jax.ShapeDtypeStruct((t, vw), BF16)] * 2,
        scratch_shapes=[pltpu.VMEM((2, kw, vw), F32)],
        compiler_params=_cparams(2),
        name="gla_bidir",
    )(proj16, proj16, proj16, proj32, proj16, proj16, proj16, proj32, w_gate2, b_gate.reshape(2, 1, kw))


def _lambda_full(lam, lam_init):
    return (jnp.exp(jnp.sum(lam[0:1] * lam[1:2], axis=1, keepdims=True))
            - jnp.exp(jnp.sum(lam[2:3] * lam[3:4], axis=1, keepdims=True)) + lam_init)


def _attend(q, keys, vals, lam_f, gain, lam_init, o_ref):
    lane = lax.broadcasted_iota(jnp.int32, (1, LANES), 1)
    n = ATTN_SUB
    q = q * (DKC ** -0.5 * math.log2(math.e))
    subs = [q[r * n:(r + 1) * n] for r in range(q.shape[0] // n)]
    q2s = [jnp.concatenate([jnp.where(lane < DKC, x, 0.0), jnp.where(lane >= DKC, x, 0.0)],
                           axis=0).astype(BF16) for x in subs]
    ss = [_dot_nt(q2, keys) for q2 in q2s]
    es = [jnp.exp2(s - jnp.max(s, axis=-1, keepdims=True)) for s in ss]
    invs = [1.0 / jnp.sum(e, axis=-1, keepdims=True) for e in es]
    o2s = [_dot(es[r].astype(BF16), vals) * invs[r] for r in range(len(es))]
    for r, o2 in enumerate(o2s):
        o = o2[:n] - lam_f * o2[n:]
        o_ref[r * n:(r + 1) * n, :] = (_rms_rows(o, gain) * (1.0 - lam_init)).astype(BF16)


def _attn_lat_kernel(lam_ref, gn_ref, q_ref, kc_ref, kl_ref, vc_ref, vl_ref, cq_ref, sq_ref, ck_ref, sk_ref,
                     o_ref, k_scr, v_scr, *, lc, lam_init):
    lane = lax.broadcasted_iota(jnp.int32, (1, LANES), 1)
    first_half = (lane & 31) < 16

    def rope(x, cos, sin):
        swapped = jnp.where(first_half, pltpu.roll(x, LANES - 16, 1), pltpu.roll(x, 16, 1))
        return x * cos + swapped * sin

    @pl.when(pl.program_id(2) == 0)
    def _():
        k_scr[0:lc, :] = kc_ref[...]
        k_scr[lc:, :] = rope(kl_ref[...].astype(F32), ck_ref[...], sk_ref[...]).astype(BF16)
        v_scr[0:lc, :] = vc_ref[...]
        v_scr[lc:, :] = vl_ref[...]

    _attend(rope(q_ref[...].astype(F32), cq_ref[...], sq_ref[...]), k_scr[...], v_scr[...],
            _lambda_full(lam_ref[...], lam_init), gn_ref[...], lam_init, o_ref)


def _attn_ctx_kernel(lam_ref, gn_ref, q_ref, k_ref, v_ref, prev_ref, o_ref, *, lam_init):
    del prev_ref
    _attend(q_ref[...].astype(F32), k_ref[...], v_ref[...], _lambda_full(lam_ref[...], lam_init),
            gn_ref[...], lam_init, o_ref)


def _attention(proj, lam, gn, rope_cos, rope_sin, n_b, lc, l, lam_init, ctx_out):
    t = proj.shape[0]
    tq = min(ATTN_TQ, l)
    per = l // tq
    cq, ck, cv = H_CQ // LANES, H_CK // LANES, H_CV // LANES
    small = [pl.BlockSpec((4, DKC), lambda b, h, *_: (0, 0)), pl.BlockSpec((1, DVC), lambda b, h, *_: (0, 0))]
    tab_q = pl.BlockSpec((tq, LANES), lambda b, h, i: (i, 0))
    tab_k = pl.BlockSpec((l, LANES), lambda b, h, i: (0, 0))
    od = pl.pallas_call(
        functools.partial(_attn_lat_kernel, lc=lc, lam_init=lam_init),
        grid=(n_b, HC, per),
        in_specs=small
        + [pl.BlockSpec((tq, LANES), lambda b, h, i: ((b + 1) * per + i, cq + h)),
           pl.BlockSpec((lc, LANES), lambda b, h, i: (b, ck + h)),
           pl.BlockSpec((l, LANES), lambda b, h, i: (b + 1, ck + h)),
           pl.BlockSpec((lc, LANES), lambda b, h, i: (b, cv + h)),
           pl.BlockSpec((l, LANES), lambda b, h, i: (b + 1, cv + h)),
           tab_q, tab_q, tab_k, tab_k],
        out_specs=pl.BlockSpec((tq, LANES), lambda b, h, i: ((b + 1) * per + i, h)),
        out_shape=jax.ShapeDtypeStruct((t, HC * DVC), BF16),
        scratch_shapes=[pltpu.VMEM((lc + l, LANES), BF16), pltpu.VMEM((lc + l, LANES), BF16)],
        compiler_params=_cparams(3),
        name="diff_attention",
    )(lam, gn.reshape(1, DVC), proj, proj, proj, proj, proj, rope_cos, rope_sin, rope_cos, rope_sin)
    if not ctx_out:
        return od
    return pl.pallas_call(
        functools.partial(_attn_ctx_kernel, lam_init=lam_init),
        grid=(n_b, HC),
        in_specs=small
        + [pl.BlockSpec((lc, LANES), lambda b, h: (b, cq + h)),
           pl.BlockSpec((lc, LANES), lambda b, h: (b, ck + h)),
           pl.BlockSpec((lc, LANES), lambda b, h: (b, cv + h)),
           pl.BlockSpec(memory_space=pl.ANY)],
        out_specs=pl.BlockSpec((lc, LANES), lambda b, h: (b, h)),
        out_shape=jax.ShapeDtypeStruct((t, HC * DVC), BF16),
        input_output_aliases={5: 0},
        compiler_params=_cparams(2),
        name="diff_attention_ctx",
    )(lam, gn.reshape(1, DVC), proj, proj, proj, od)


def _rope_tables(n_tokens):
    rows = n_tokens // GRID_W
    row = jnp.repeat(jnp.arange(rows, dtype=F32), GRID_W)
    col = jnp.tile(jnp.arange(GRID_W, dtype=F32), rows)
    n_freq = DKC // 4
    inv_freq = ROPE_THETA ** (-jnp.arange(n_freq, dtype=F32) / n_freq)
    ang_r = row[:, None] * inv_freq
    ang_c = col[:, None] * inv_freq
    cos = jnp.concatenate([jnp.cos(ang_r)] * 2 + [jnp.cos(ang_c)] * 2, axis=1)
    sin = jnp.concatenate([-jnp.sin(ang_r), jnp.sin(ang_r), -jnp.sin(ang_c), jnp.sin(ang_c)], axis=1)
    return jnp.concatenate([cos, cos], axis=1), jnp.concatenate([sin, sin], axis=1)


def _head_rms(o, gain, n_heads, width):
    parts = []
    for h in range(n_heads):
        parts.append(_rms_rows(o[:, h * width:(h + 1) * width], gain))
    return jnp.concatenate(parts, axis=1)


def _merge_kernel(*refs, ctx_tiles, n_tok):
    x_refs = refs[:n_tok]
    (oa0, oa1, z_ref, ob0, ob1, r_ref, od_ref, ga_ref, gb_ref, gd_ref, mod_ref,
     gna_ref, gnb_ref, woa_ref, wob_ref, woc_ref, wout_ref, o_ref) = refs[n_tok:]
    f32 = lambda ref: ref[...].astype(F32)
    ya = _head_rms(f32(oa0) + f32(oa1), gna_ref[...], HA, DVA) * _silu(f32(z_ref))
    yb = _head_rms(f32(ob0) + f32(ob1), gnb_ref[...], HB, DVB) * _silu(f32(r_ref))
    acc = _sigmoid(f32(ga_ref)) * _dot(ya.astype(BF16), woa_ref[...])
    acc = acc + _sigmoid(f32(gb_ref)) * _dot(yb.astype(BF16), wob_ref[...])
    acc = acc + _sigmoid(f32(gd_ref)) * _dot(od_ref[...], woc_ref[...])
    y = _dot(acc.astype(BF16), wout_ref[...])
    o_ref[...] = _token_tile(x_refs, ctx_tiles) + mod_ref[0, 2:3, :] * y


def _merge(oa0, oa1, ob0, ob1, od, proj, x_ctx, x_lat, lat_first, t, mod, gn_a, gn_b, woa, wob, woc, wout,
           rows_per_mod, row_off, tm):
    d = x_ctx.shape[1]
    off = row_off // tm
    ctx_tiles = rows_per_mod // tm
    n_tiles = (t - row_off) // tm
    per = rows_per_mod // tm
    w5 = HA * DVA
    row = lambda i: (i + off, 0)
    col = lambda c: (lambda i: (i + off, c))
    tok_specs, tok_args = _token_inputs(x_ctx, x_lat, lat_first, tm, ctx_tiles, off)
    return pl.pallas_call(
        functools.partial(_merge_kernel, ctx_tiles=ctx_tiles - off, n_tok=len(tok_args)),
        grid=(n_tiles,),
        in_specs=tok_specs
        + [pl.BlockSpec((tm, w5), row),
                  pl.BlockSpec((tm, w5), row),
                  pl.BlockSpec((tm, w5), col(H_AZ // w5)),
                  pl.BlockSpec((tm, w5), row),
                  pl.BlockSpec((tm, w5), row),
                  pl.BlockSpec((tm, w5), col(H_BR // w5)),
                  pl.BlockSpec((tm, w5), row),
                  pl.BlockSpec((tm, d), col(H_GATES // d)),
                  pl.BlockSpec((tm, d), col(H_GATES // d + 1)),
                  pl.BlockSpec((tm, d), col(H_GATES // d + 2)),
                  pl.BlockSpec((1, 6, d), lambda i: ((i + off) // per, 0, 0)),
                  pl.BlockSpec((1, DVA), lambda i: (0, 0)),
                  pl.BlockSpec((1, DVB), lambda i: (0, 0)),
                  pl.BlockSpec((w5, d), lambda i: (0, 0)),
                  pl.BlockSpec((w5, d), lambda i: (0, 0)),
                  pl.BlockSpec((w5, d), lambda i: (0, 0)),
                  pl.BlockSpec((d, d), lambda i: (0, 0))],
        out_specs=pl.BlockSpec((tm, d), lambda i: (i, 0)),
        out_shape=jax.ShapeDtypeStruct((t - row_off, d), F32),
        compiler_params=_cparams(1),
        name="merge_outproj",
    )(*tok_args, oa0, oa1, proj, ob0, ob1, proj, od, proj, proj, proj, mod,
      gn_a.reshape(1, DVA), gn_b.reshape(1, DVB), woa, wob, woc, wout)


def _ffn_kernel(x_ref, mod_ref, g_ref, fg_ref, w1_ref, w3_ref, w2_ref, o_ref, h_scr, *, final):
    f = pl.program_id(1)

    @pl.when(f == 0)
    def _():
        y = _rms_rows(x_ref[...], g_ref[...])
        h_scr[...] = (y * (1.0 + mod_ref[0, 4:5, :]) + mod_ref[0, 3:4, :]).astype(BF16)
        o_ref[...] = jnp.zeros_like(o_ref)

    h = h_scr[...]
    t = _silu(_dot(h, w1_ref[...])) * _dot(h, w3_ref[...])
    o_ref[...] += _dot(t.astype(BF16), w2_ref[...])

    @pl.when(f == pl.num_programs(1) - 1)
    def _():
        out = x_ref[...] + mod_ref[0, 5:6, :] * o_ref[...]
        if final:
            out = _rms_rows(out, fg_ref[...])
        o_ref[...] = out


def _ffn(x, mod, g, final_g, w1, w3, w2, rows_per_mod, mod_off, tm, tf, final):
    t, d = x.shape
    dff = w1.shape[1]
    per = rows_per_mod // tm
    return pl.pallas_call(
        functools.partial(_ffn_kernel, final=final),
        grid=(t // tm, dff // tf),
        in_specs=[pl.BlockSpec((tm, d), lambda i, f: (i, 0)),
                  pl.BlockSpec((1, 6, d), lambda i, f: (i // per + mod_off, 0, 0)),
                  pl.BlockSpec((1, d), lambda i, f: (0, 0)),
                  pl.BlockSpec((1, d), lambda i, f: (0, 0)),
                  pl.BlockSpec((d, tf), lambda i, f: (0, f)),
                  pl.BlockSpec((d, tf), lambda i, f: (0, f)),
                  pl.BlockSpec((tf, d), lambda i, f: (f, 0))],
        out_specs=pl.BlockSpec((tm, d), lambda i, f: (i, 0)),
        out_shape=jax.ShapeDtypeStruct((t, d), F32),
        scratch_shapes=[pltpu.VMEM((tm, d), BF16)],
        compiler_params=_cparams(2),
        name="dense_ffn",
    )(x, mod, g.reshape(1, d), final_g.reshape(1, d), w1, w3, w2)


MOE_ROWS = 256
MOE_MOVE_UNIT = 128
MOE_FFN_UNIT = 128


def _moe_kernel(x_ref, mod_ref, g_ref, rw_ref, tri_ref, fg_ref, w1_ref, w3_ref, w2_ref, o_ref,
                h_scr, gate_scr, sel_scr, rank_scr, selr_scr, rankr_scr, xg_scr, yg_scr, nb_scr, *, final):
    e = pl.program_id(1)
    f = pl.program_id(2)
    n_f = pl.num_programs(2)
    tm, d = x_ref.shape
    cb = MOE_ROWS
    lane = lax.broadcasted_iota(jnp.int32, (1, LANES), 1)

    def for_blocks(body, slot, unit):
        units = nb_scr[slot]
        per = cb // unit
        nb = units // per

        def full(j, carry):
            body(pl.multiple_of(j * cb, cb), cb)
            return carry

        lax.fori_loop(0, nb, full, 0)
        rem = units - nb * per
        r0 = nb * cb
        size = cb // 2
        while size >= unit:
            has = (rem // (size // unit)) % 2 == 1

            @pl.when(has)
            def _():
                body(pl.multiple_of(r0, unit), size)

            r0 = r0 + jnp.where(has, size, 0)
            size //= 2

    @pl.when((e == 0) & (f == 0))
    def _():
        y = _rms_rows(x_ref[...], g_ref[...])
        hmod = y * (1.0 + mod_ref[0, 4:5, :]) + mod_ref[0, 3:4, :]
        h_scr[...] = hmod.astype(BF16)
        o_ref[...] = jnp.zeros_like(o_ref)
        logits = _dot3(hmod, rw_ref[...])
        logits = jnp.where(lane < N_EXPERTS, logits, -jnp.inf)
        lanef = lane.astype(F32)
        m1 = jnp.max(logits, axis=1, keepdims=True)
        i1 = jnp.min(jnp.where(logits == m1, lanef, float(LANES)), axis=1, keepdims=True)
        hit1 = lanef == i1
        rest = jnp.where(hit1, -jnp.inf, logits)
        m2 = jnp.max(rest, axis=1, keepdims=True)
        i2 = jnp.min(jnp.where(rest == m2, lanef, float(LANES)), axis=1, keepdims=True)
        hit2 = lanef == i2
        ex = jnp.exp(m2 - m1)
        inv = 1.0 / (1.0 + ex)
        gate_scr[...] = jnp.where(hit1, inv, 0.0) + jnp.where(hit2, ex * inv, 0.0)
        sel = jnp.where(hit1 | hit2, 1.0, 0.0)
        sel_scr[...] = sel
        sel16 = sel.astype(BF16)
        rank = _dot(tri_ref[...], sel16)
        rank_scr[...] = rank
        pick = jnp.where(lax.broadcasted_iota(jnp.int32, (8, LANES), 0)
                         == lax.broadcasted_iota(jnp.int32, (8, LANES), 1), 1.0, 0.0).astype(BF16)
        selr_scr[...] = _dot_nt(pick, sel16)
        hi = jnp.floor(rank * (1.0 / 32.0))
        lo = rank - 32.0 * hi
        rankr_scr[...] = 32.0 * _dot_nt(pick, hi.astype(BF16)) + _dot_nt(pick, lo.astype(BF16))

    @pl.when(f == 0)
    def _():
        sel_row = selr_scr[pl.ds(e, 1), :]
        rank_row = rankr_scr[pl.ds(e, 1), :]
        n_e = jnp.sum(sel_row).astype(jnp.int32)
        nb_scr[0] = (n_e + (MOE_MOVE_UNIT - 1)) // MOE_MOVE_UNIT
        nb_scr[1] = (n_e + (MOE_FFN_UNIT - 1)) // MOE_FFN_UNIT

        def gather(r0, bs):
            slot = lax.broadcasted_iota(jnp.int32, (bs, tm), 0).astype(F32) + r0.astype(F32)
            onehot = jnp.where((rank_row == slot) & (sel_row > 0.0), 1.0, 0.0)
            xg_scr[pl.ds(r0, bs), :] = _dot(onehot.astype(BF16), h_scr[...]).astype(BF16)
            yg_scr[pl.ds(r0, bs), :] = jnp.zeros((bs, d), F32)

        for_blocks(gather, 0, MOE_MOVE_UNIT)

    def expert(r0, bs):
        xb = xg_scr[pl.ds(r0, bs), :]
        t = _silu(_dot(xb, w1_ref[0])) * _dot(xb, w3_ref[0])
        yg_scr[pl.ds(r0, bs), :] += _dot(t.astype(BF16), w2_ref[0])

    for_blocks(expert, 1, MOE_FFN_UNIT)

    @pl.when(f == n_f - 1)
    def _():
        pick_e = lane == e
        rank_col = jnp.sum(jnp.where(pick_e, rank_scr[...], 0.0), axis=1, keepdims=True)
        sel_col = jnp.sum(jnp.where(pick_e, sel_scr[...], 0.0), axis=1, keepdims=True)
        gate_col = jnp.sum(jnp.where(pick_e, gate_scr[...], 0.0), axis=1, keepdims=True)

        def scatter(r0, bs):
            slot = lax.broadcasted_iota(jnp.int32, (tm, bs), 1).astype(F32) + r0.astype(F32)
            onehot = jnp.where((rank_col == slot) & (sel_col > 0.0), 1.0, 0.0)
            o_ref[...] += gate_col * _dot(onehot.astype(BF16), yg_scr[pl.ds(r0, bs), :].astype(BF16))

        for_blocks(scatter, 0, MOE_MOVE_UNIT)

    @pl.when((e == pl.num_programs(1) - 1) & (f == n_f - 1))
    def _():
        out = x_ref[...] + mod_ref[0, 5:6, :] * o_ref[...]
        if final:
            out = _rms_rows(out, fg_ref[...])
        o_ref[...] = out


def _moe(x, mod, g, router_w, final_g, w1, w3, w2, rows_per_mod, mod_off, tm, tf, final):
    t, d = x.shape
    n_e, _, dff = w1.shape
    per = rows_per_mod // tm
    return pl.pallas_call(
        functools.partial(_moe_kernel, final=final),
        grid=(t // tm, n_e, dff // tf),
        in_specs=[pl.BlockSpec((tm, d), lambda i, e, f: (i, 0)),
                  pl.BlockSpec((1, 6, d), lambda i, e, f: (i // per + mod_off, 0, 0)),
                  pl.BlockSpec((1, d), lambda i, e, f: (0, 0)),
                  pl.BlockSpec((d, LANES), lambda i, e, f: (0, 0)),
                  pl.BlockSpec((tm, tm), lambda i, e, f: (0, 0)),
                  pl.BlockSpec((1, d), lambda i, e, f: (0, 0)),
                  pl.BlockSpec((1, d, tf), lambda i, e, f: (e, 0, f)),
                  pl.BlockSpec((1, d, tf), lambda i, e, f: (e, 0, f)),
                  pl.BlockSpec((1, tf, d), lambda i, e, f: (e, f, 0))],
        out_specs=pl.BlockSpec((tm, d), lambda i, e, f: (i, 0)),
        out_shape=jax.ShapeDtypeStruct((t, d), F32),
        scratch_shapes=[pltpu.VMEM((tm, d), BF16),
                        pltpu.VMEM((tm, LANES), F32),
                        pltpu.VMEM((tm, LANES), F32),
                        pltpu.VMEM((tm, LANES), F32),
                        pltpu.VMEM((8, tm), F32),
                        pltpu.VMEM((8, tm), F32),
                        pltpu.VMEM((tm, d), BF16),
                        pltpu.VMEM((tm, d), F32),
                        pltpu.SMEM((2,), jnp.int32)],
        compiler_params=_cparams(3),
        name="routed_ffn_sparse",
    )(x, mod, g.reshape(1, d), router_w, jnp.tril(jnp.ones((tm, tm), BF16), -1),
      final_g.reshape(1, d), w1, w3, w2)


def _w_in_pieces(end):
    a_qkv = 2 * HA * DKA + HA * DVA
    a_z = a_qkv + HA * DVA
    b_q = a_z + 4 * HA
    b_v = b_q + 2 * HB * DKB
    b_glr = b_v + 2 * HB * DVB
    c_q = b_glr + 2 * GATE_RANK
    gates = c_q + 4 * HC * DKC + HC * DVC
    return [(0, a_qkv), (a_z, b_q), (b_glr, c_q), None,
            (gates, end), (a_qkv, a_z), (b_v, b_glr), (c_q, gates), (b_q, b_v)]


def _reorder_kernel(w_ref, o_ref):
    w = w_ref[0]
    at = 0
    for piece in _w_in_pieces(w.shape[1]):
        if piece is None:
            width = N_F32 - at
            o_ref[0, :, at:at + width] = jnp.zeros((w.shape[0], width), BF16)
        else:
            width = piece[1] - piece[0]
            o_ref[0, :, at:at + width] = w[:, piece[0]:piece[1]].astype(BF16)
        at += width
    assert at == N_F32 + N_B16


def _reorder_w_in(w_in):
    depth, d, cols = w_in.shape
    rb = 128
    return pl.pallas_call(
        _reorder_kernel,
        grid=(depth, d // rb),
        in_specs=[pl.BlockSpec((1, rb, cols), lambda l, i: (l, i, 0))],
        out_specs=pl.BlockSpec((1, rb, N_F32 + N_B16), lambda l, i: (l, i, 0)),
        out_shape=jax.ShapeDtypeStruct((depth, d, N_F32 + N_B16), BF16),
        compiler_params=_cparams(2),
        name="reorder_w_in",
    )(w_in)


def kernel(x, c, ctx, c_ctx, w_mod, b_mod, norm1_g, norm2_g, w_in, conv_a, a_log, dt_bias, gn_a, w_gate2, b_gate, gn_b, lam_c, gn_c, w_o_a, w_o_b, w_o_c, w_out, ffn_w1, ffn_w3, ffn_w2, router_w, moe_w1, moe_w3, moe_w2, final_g):
    n_b, l, d = x.shape
    lc = ctx.shape[1]
    depth = w_mod.shape[0]
    assert n_b * lc == l and lc % CHUNK == 0 and l & (l - 1) == 0 and lc & (lc - 1) == 0
    t_ctx = n_b * lc
    tm = min(1024, l)
    tm_small = min(512, l)

    n_mod = 16
    cvec = jnp.concatenate([c_ctx[None, :], c, jnp.zeros((n_mod - 1 - n_b, d), F32)], axis=0)
    mod_all = _modulation(cvec, w_mod, b_mod).reshape(depth, n_mod, 6, d)

    rope_cos, rope_sin = _rope_tables(l)
    w_in16 = _reorder_w_in(w_in)
    t_all = t_ctx + n_b * l
    tokens = (ctx.reshape(t_ctx, d), x.reshape(n_b * l, d), 0)

    for layer in range(depth):
        ctx_out = layer < depth - 1
        lam_init = 0.8 - 0.6 * math.exp(-0.3 * layer)
        mod = mod_all[layer]
        proj32, proj16 = _in_projection(*tokens, t_all, mod, norm1_g[layer], w_in16, layer, l, tm)

        qkvc = _short_conv(proj32, conv_a[layer], lc, l)
        oa0, oa1 = _dn_scan(_dn_factors(qkvc, proj32, a_log[layer], dt_bias[layer], n_b, lc, l), n_b, lc, l)
        ob0, ob1 = _gla(proj32, proj16, w_gate2[layer], b_gate[layer], n_b, lc, l)
        od = _attention(proj16, lam_c[layer], gn_c[layer], rope_cos, rope_sin, n_b, lc, l, lam_init, ctx_out)

        row_off = 0 if ctx_out else t_ctx
        xs_new = _merge(oa0, oa1, ob0, ob1, od, proj16, *tokens, t_all, mod, gn_a[layer], gn_b[layer],
                        w_o_a[layer].astype(BF16), w_o_b[layer].astype(BF16), w_o_c[layer].astype(BF16),
                        w_out[layer].astype(BF16), l, row_off, tm_small)

        i = layer // 2
        final = layer == depth - 1
        mod_off = 0 if ctx_out else 1
        if layer % 2 == 0:
            xs = _ffn(xs_new, mod, norm2_g[layer], final_g, ffn_w1[i].astype(BF16),
                      ffn_w3[i].astype(BF16), ffn_w2[i].astype(BF16), l, mod_off, tm_small,
                      ffn_w1.shape[2] // 2, final)
        else:
            rw = jnp.pad(router_w[i], ((0, 0), (0, LANES - N_EXPERTS)))
            xs = _moe(xs_new, mod, norm2_g[layer], rw, final_g, moe_w1[i].astype(BF16),
                      moe_w3[i].astype(BF16), moe_w2[i].astype(BF16), l, mod_off, tm,
                      moe_w1.shape[3] // 2, final)
        tokens = (xs, xs, t_ctx)

    return xs.reshape(n_b, l, d)
```

```python
import functools
import math

import jax
import jax.numpy as jnp
from jax import lax
from jax.experimental import pallas as pl
from jax.experimental.pallas import tpu as pltpu

F32 = jnp.float32
BF16 = jnp.bfloat16
HIGHEST = lax.Precision.HIGHEST

EPS = 1e-6
HA, DKA, DVA = 4, 128, 128
HB, DKB, DVB = 4, 64, 128
GATE_RANK, GATE_NORM = 16, 16.0
HC, DKC, DVC = 4, 64, 128
GRID_W, ROPE_THETA = 64, 10000.0
N_EXPERTS, TOP_K = 8, 2
CHUNK = 64
LANES = 128
ATTN_SUB = 64
ATTN_TQ = 512
CONV_HEADS = 2

F_QKV = 0
F_SMALL = 1536
N_F32 = 1664
H_GATES = 0
H_AZ = 3072
H_BV = 3584
H_BR = 4096
H_CQ = 4608
H_CK = 5120
H_CV = 5632
H_BQ = 6144
H_BK = 6400
N_B16 = 6656
PROJ_TN = 1664

VMEM_LIMIT = 56 * 1024 * 1024


def _cparams(n_axes):
    return pltpu.CompilerParams(dimension_semantics=("arbitrary",) * n_axes,
                                vmem_limit_bytes=VMEM_LIMIT)


def _sigmoid(x):
    return 1.0 / (1.0 + jnp.exp(-x))


def _silu(x):
    return x * _sigmoid(x)


def _softplus(x):
    return jnp.maximum(x, 0.0) + jnp.log(1.0 + jnp.exp(-jnp.abs(x)))


def _dot(a, b):
    return jnp.dot(a, b, preferred_element_type=F32)


def _dot_nt(a, b):
    return lax.dot_general(a, b, (((1,), (1,)), ((), ())), preferred_element_type=F32)


def _dot_tn(a, b):
    return lax.dot_general(a, b, (((0,), (0,)), ((), ())), preferred_element_type=F32)


def _split(a):
    hi = a.astype(BF16)
    lo = (a - hi.astype(F32)).astype(BF16)
    return hi, lo


def _dot3(a, b):
    ah, al = _split(a)
    bh, bl = _split(b)
    return _dot(ah, bh) + (_dot(ah, bl) + _dot(al, bh))


def _rms_rows(x, gain):
    ms = jnp.mean(x * x, axis=-1, keepdims=True)
    return x * lax.rsqrt(ms + EPS) * gain


def _mod_kernel(c_ref, w_ref, b_ref, o_ref):
    s = _silu(c_ref[...])
    o_ref[0] = jnp.dot(s, w_ref[0], precision=HIGHEST, preferred_element_type=F32) + b_ref[0]


def _modulation(cvec, w_mod, b_mod):
    depth, d, d6 = w_mod.shape
    n = cvec.shape[0]
    return pl.pallas_call(
        _mod_kernel,
        grid=(depth, d6 // d),
        in_specs=[pl.BlockSpec((n, d), lambda l, j: (0, 0)),
                  pl.BlockSpec((1, d, d), lambda l, j: (l, 0, j)),
                  pl.BlockSpec((1, 1, d), lambda l, j: (l, 0, j))],
        out_specs=pl.BlockSpec((1, n, d), lambda l, j: (l, 0, j)),
        out_shape=jax.ShapeDtypeStruct((depth, n, d6), F32),
        compiler_params=_cparams(2),
        name="adaln_mod",
    )(cvec, w_mod, b_mod.reshape(depth, 1, d6))


def _token_specs(tm, d, ctx_tiles, lat_first, off=0):
    return [pl.BlockSpec((tm, d), lambda i, *_: (jnp.minimum(i + off, ctx_tiles - 1), 0)),
            pl.BlockSpec((tm, d), lambda i, *_: (jnp.maximum(i + off - ctx_tiles, 0) + lat_first, 0))]


def _token_inputs(x_ctx, x_lat, lat_first, tm, ctx_tiles, off=0):
    d = x_ctx.shape[1]
    if x_lat is x_ctx:
        return [pl.BlockSpec((tm, d), lambda i, *_: (i + off, 0))], [x_ctx]
    return _token_specs(tm, d, ctx_tiles, lat_first // tm, off), [x_ctx, x_lat]


def _token_tile(x_refs, ctx_tiles):
    if len(x_refs) == 1:
        return x_refs[0][...]
    return jnp.where(pl.program_id(0) < ctx_tiles, x_refs[0][...], x_refs[1][...])


def _inproj_kernel(*refs, ctx_tiles, n_tok):
    x_refs, (mod_ref, g_ref, w_ref, o32_ref, o16_ref, h_scr) = refs[:n_tok], refs[n_tok:]
    j = pl.program_id(1)

    @pl.when(j == 0)
    def _():
        y = _rms_rows(_token_tile(x_refs, ctx_tiles), g_ref[...])
        h_scr[...] = (y * (1.0 + mod_ref[0, 1:2, :]) + mod_ref[0, 0:1, :]).astype(BF16)
        o32_ref[...] = _dot_nt(h_scr[...], w_ref[0])

    @pl.when(j > 0)
    def _():
        o16_ref[...] = _dot_nt(h_scr[...], w_ref[0]).astype(BF16)


def _in_projection(x_ctx, x_lat, lat_first, t, mod, g, w, layer, rows_per_mod, tm):
    d = x_ctx.shape[1]
    tn = PROJ_TN
    assert w.shape[1] == N_F32 + N_B16 and N_F32 == tn and N_B16 % tn == 0
    per = rows_per_mod // tm
    ctx_tiles = rows_per_mod // tm
    tok_specs, tok_args = _token_inputs(x_ctx, x_lat, lat_first, tm, ctx_tiles)
    return pl.pallas_call(
        functools.partial(_inproj_kernel, ctx_tiles=ctx_tiles, n_tok=len(tok_args)),
        grid=(t // tm, 1 + N_B16 // tn),
        in_specs=tok_specs
        + [pl.BlockSpec((1, 6, d), lambda i, j: (i // per, 0, 0)),
           pl.BlockSpec((1, d), lambda i, j: (0, 0)),
           pl.BlockSpec((1, tn, d), lambda i, j: (layer, j, 0))],
        out_specs=[pl.BlockSpec((tm, tn), lambda i, j: (i, 0)),
                   pl.BlockSpec((tm, tn), lambda i, j: (i, jnp.maximum(j - 1, 0)))],
        out_shape=[jax.ShapeDtypeStruct((t, N_F32), F32), jax.ShapeDtypeStruct((t, N_B16), BF16)],
        scratch_shapes=[pltpu.VMEM((tm, d), BF16)],
        compiler_params=_cparams(2),
        name="norm_mod_inproj",
    )(*tok_args, mod, g.reshape(1, d), w)


def _conv_kernel(x_ref, w_ref, o_ref, *, lc, l):
    p = pl.program_id(0)
    j = pl.program_id(1)
    x = x_ref[...]
    n = x.shape[0]
    row = lax.broadcasted_iota(jnp.int32, x.shape, 0)
    seq = jnp.where(p == 0, lc, l)
    pos = row & (seq - 1)
    prev = jnp.where(pos == 0, 0.0, pltpu.roll(x, 1, 0))
    nxt = jnp.where(pos == seq - 1, 0.0, pltpu.roll(x, n - 1, 0))
    w = w_ref[...]
    y = _silu(prev * w[0:1] + x * w[1:2] + nxt * w[2:3])
    heads_per_block = x.shape[1] // DKA
    for hh in range(heads_per_block):
        cols = slice(hh * DKA, (hh + 1) * DKA)
        yh = y[:, cols]
        inv = lax.rsqrt(jnp.sum(yh * yh, axis=-1, keepdims=True) + EPS)
        head = j * heads_per_block + hh
        fac = jnp.where(head < HA, inv * DKA ** -0.5, jnp.where(head < 2 * HA, inv, 1.0))
        o_ref[:, cols] = yh * fac


def _short_conv(proj, conv_w, lc, l):
    t = proj.shape[0]
    ncol = conv_w.shape[1]
    wide = CONV_HEADS * DKA
    return pl.pallas_call(
        functools.partial(_conv_kernel, lc=lc, l=l),
        grid=(t // l, ncol // wide),
        in_specs=[pl.BlockSpec((l, wide), lambda p, j: (p, j)),
                  pl.BlockSpec((conv_w.shape[0], wide), lambda p, j: (0, j))],
        out_specs=pl.BlockSpec((l, wide), lambda p, j: (p, j)),
        out_shape=jax.ShapeDtypeStruct((t, ncol), F32),
        compiler_params=_cparams(2),
        name="conv_silu_l2",
    )(proj, conv_w)


def _row_block(b, g, n_b, n_lat):
    return jnp.where(g == 0, b, n_b + b * n_lat + g - 1)


def _rev_group(g, n_g):
    return jnp.where(g == 0, 0, n_g - g)


def _tri_inverse_many(lms, ii, jj, uppers):
    c = lms[0].shape[0]
    eye = (ii == jj).astype(F32)
    n = len(lms)
    ts = None
    s, ls = 1, 0
    while s < c:
        same = (ii >> (ls + 1)) == (jj >> (ls + 1))
        io = (ii >> ls) & 1
        jo = (jj >> ls) & 1
        lower = same & (io == 1) & (jo == 0)
        upper = same & (io == 0) & (jo == 1)
        cms = [jnp.where(upper if uppers[i] else lower, lms[i], 0.0) for i in range(n)]
        if ts is None:
            ts = [eye - cm for cm in cms]
        else:
            t16 = [t.astype(BF16) for t in ts]
            ps = [_dot(t16[i], cms[i].astype(BF16)).astype(BF16) for i in range(n)]
            ts = [ts[i] - _dot(ps[i], t16[i]) for i in range(n)]
        s, ls = s * 2, ls + 1
    a_parts = [_split(eye + lm) for lm in lms]
    t_parts = [_split(t) for t in ts]
    res = [eye - (_dot(a_parts[i][0], t_parts[i][0])
                  + (_dot(a_parts[i][0], t_parts[i][1]) + _dot(a_parts[i][1], t_parts[i][0])))
           for i in range(n)]
    return [ts[i] + _dot(t_parts[i][0], res[i].astype(BF16)) for i in range(n)]


def _dn_kernel(alog_ref, dtb_ref, q0, k0, v0, sm0, q1, k1, v1, sm1, o0_ref, o1_ref, s_scr, *, n_chunks):
    @pl.when(pl.program_id(1) == 0)
    def _():
        s_scr[...] = jnp.zeros_like(s_scr)

    c_ = CHUNK
    refs = ((q0, k0, v0, o0_ref), (q1, k1, v1, o1_ref))
    sms = (sm0[...], sm1[...])
    sm_ts = tuple(s.T for s in sms)
    ii = lax.broadcasted_iota(jnp.int32, (c_, c_), 0)
    jj = lax.broadcasted_iota(jnp.int32, (c_, c_), 1)
    rows = [slice(c * c_, (c + 1) * c_) for c in range(n_chunks)]
    lanes = [slice(j * LANES, (j + 1) * LANES) for j in range(HA)]
    items = [(j, c, d) for j in range(HA) for c in range(n_chunks) for d in range(2)]
    qs = [refs[d][0][rows[c], lanes[j]] for j, c, d in items]
    ks = [refs[d][1][rows[c], lanes[j]] for j, c, d in items]
    vs = [refs[d][2][rows[c], lanes[j]] for j, c, d in items]
    k16 = [k.astype(BF16) for k in ks]
    kks = [_dot_nt(k, k) for k in k16]
    qks = [_dot_nt(qs[i].astype(BF16), k16[i]) for i in range(len(items))]

    betas, gcs, totals, gams, lms = [], [], [], [], []
    for i, (j, c, d) in enumerate(items):
        r = rows[c]
        incl = (jj <= ii) if d == 0 else (jj >= ii)
        strict = (jj < ii) if d == 0 else (jj > ii)
        incl_t = (ii <= jj) if d == 0 else (ii >= jj)
        a_scale = -jnp.exp(jnp.full((1, 1), alog_ref[d * HA + j], F32))
        dtb = dtb_ref[d * HA + j]
        col = d * HA + j
        beta = _sigmoid(sms[d][r, col:col + 1])
        g_col = a_scale * _softplus(sms[d][r, 2 * HA + col:2 * HA + col + 1] + dtb)
        g_row = a_scale * _softplus(sm_ts[d][2 * HA + col:2 * HA + col + 1, r] + dtb)
        gc_col = jnp.sum(jnp.where(incl, g_row, 0.0), axis=1, keepdims=True)
        gc_row = jnp.sum(jnp.where(incl_t, g_col, 0.0), axis=0, keepdims=True)
        gam = jnp.where(incl, jnp.exp(gc_col - gc_row), 0.0)
        betas.append(beta)
        gcs.append(gc_col)
        totals.append(jnp.sum(g_col, axis=0, keepdims=True))
        gams.append(gam)
        lms.append(jnp.where(strict, beta * kks[i] * gam, 0.0))

    tinvs = _tri_inverse_many(lms, ii, jj, [d == 1 for _, _, d in items])
    egs = [jnp.exp(gc) for gc in gcs]
    uws = []
    for i in range(len(items)):
        rhs = jnp.concatenate([vs[i] * betas[i], ks[i] * (betas[i] * egs[i])], axis=1).astype(BF16)
        uws.append(_dot(tinvs[i].astype(BF16), rhs))
    us = [uw[:, :DVA] for uw in uws]
    wq16 = [jnp.concatenate([uws[i][:, DVA:], qs[i] * egs[i]], axis=0).astype(BF16) for i in range(len(items))]
    kg16 = [(ks[i] * jnp.exp(totals[i] - gcs[i])).astype(BF16) for i in range(len(items))]
    qkg16 = [(qks[i] * gams[i]).astype(BF16) for i in range(len(items))]
    carry = [jnp.exp(t) for t in totals]

    index = {item: i for i, item in enumerate(items)}
    chains = [(j, d) for j in range(HA) for d in range(2)]
    states = [s_scr[d * HA + j] for j, d in chains]
    for step in range(n_chunks):
        at = [index[(j, step if d == 0 else n_chunks - 1 - step, d)] for j, d in chains]
        s16 = [s.astype(BF16) for s in states]
        prods = [_dot(wq16[i], s16[n]) for n, i in enumerate(at)]
        vn16 = [(us[i] - prods[n][:c_]).astype(BF16) for n, i in enumerate(at)]
        for n, i in enumerate(at):
            j, c, d = items[i]
            refs[d][3][rows[c], lanes[j]] = (prods[n][c_:] + _dot(qkg16[i], vn16[n])).astype(BF16)
            states[n] = carry[i] * states[n] + _dot_tn(kg16[i], vn16[n])
    for n, (j, d) in enumerate(chains):
        s_scr[d * HA + j] = states[n]


def _deltanet(qkvc, proj32, a_log, dt_bias, n_b, lc, l):
    t = qkvc.shape[0]
    n_lat = l // lc
    n_g = n_lat + 1
    wide = HA * DVA
    rb0 = lambda b, g: _row_block(b, g, n_b, n_lat)
    rb1 = lambda b, g: _row_block(b, _rev_group(g, n_g), n_b, n_lat)
    smem = pl.BlockSpec(memory_space=pltpu.SMEM)

    def specs(rb):
        return [pl.BlockSpec((lc, wide), lambda b, g: (rb(b, g), 0)),
                pl.BlockSpec((lc, wide), lambda b, g: (rb(b, g), 1)),
                pl.BlockSpec((lc, wide), lambda b, g: (rb(b, g), 2)),
                pl.BlockSpec((lc, LANES), lambda b, g: (rb(b, g), F_SMALL // LANES))]

    return pl.pallas_call(
        functools.partial(_dn_kernel, n_chunks=lc // CHUNK),
        grid=(n_b, n_g),
        in_specs=[smem, smem] + specs(rb0) + specs(rb1),
        out_specs=[pl.BlockSpec((lc, wide), lambda b, g: (rb0(b, g), 0)),
                   pl.BlockSpec((lc, wide), lambda b, g: (rb1(b, g), 0))],
        out_shape=[jax.ShapeDtypeStruct((t, wide), BF16)] * 2,
        scratch_shapes=[pltpu.VMEM((2 * HA, DKA, DVA), F32)],
        compiler_params=_cparams(2),
        name="deltanet",
    )(a_log.reshape(-1), dt_bias.reshape(-1), qkvc, qkvc, qkvc, proj32, qkvc, qkvc, qkvc, proj32)


def _gla_kernel(q0, k0, v0, sm0, q1, k1, v1, sm1, wg_ref, bg_ref, o0_ref, o1_ref, s_scr, *, n_chunks):
    @pl.when(pl.program_id(1) == 0)
    def _():
        s_scr[...] = jnp.zeros_like(s_scr)

    c_ = CHUNK
    sh = c_.bit_length() - 1
    g_ = n_chunks * c_
    kw = HB * DKB
    vw = HB * DVB
    ii = lax.broadcasted_iota(jnp.int32, (g_, g_), 0)
    jj = lax.broadcasted_iota(jnp.int32, (g_, g_), 1)
    same_chunk = (ii >> sh) == (jj >> sh)
    klane = lax.broadcasted_iota(jnp.int32, (1, kw), 1)
    srow = lax.broadcasted_iota(jnp.int32, (kw, vw), 0) >> (DKB.bit_length() - 1)
    scol = lax.broadcasted_iota(jnp.int32, (kw, vw), 1) >> (DVB.bit_length() - 1)
    same_head = srow == scol
    ind16 = jnp.where((lax.broadcasted_iota(jnp.int32, (g_, n_chunks * LANES), 0) >> sh)
                      == (lax.broadcasted_iota(jnp.int32, (g_, n_chunks * LANES), 1) >> (LANES.bit_length() - 1)),
                      1.0, 0.0).astype(BF16)
    rows = [slice(c * c_, (c + 1) * c_) for c in range(n_chunks)]
    refs = ((q0, k0, v0, sm0, o0_ref), (q1, k1, v1, sm1, o1_ref))

    def spread(x, picks):
        return jnp.concatenate([jnp.broadcast_to(x[p:p + 1, :], (c_, kw)) for p in picks], axis=0)

    dirs = (0, 1)
    incls = [same_chunk & ((jj <= ii) if d == 0 else (jj >= ii)) for d in dirs]
    tri16 = [jnp.where(m, 1.0, 0.0).astype(BF16) for m in incls]
    pres = [jnp.dot(refs[d][3][:, 4 * HA + d * GATE_RANK:4 * HA + (d + 1) * GATE_RANK], wg_ref[d],
                    precision=HIGHEST, preferred_element_type=F32) + bg_ref[d] for d in dirs]
    glogs = [(jnp.minimum(p, 0.0) - jnp.log(1.0 + jnp.exp(-jnp.abs(p)))) * (1.0 / GATE_NORM) for p in pres]
    parts = [_split(g) for g in glogs]
    gcs = [_dot(tri16[d], parts[d][0]) + _dot(tri16[d], parts[d][1]) for d in dirs]
    tot_cols = [_dot_tn(parts[d][0], ind16) + _dot_tn(parts[d][1], ind16) for d in dirs]
    mids = [spread(gcs[d], [c * c_ + c_ // 2 - 1 for c in range(n_chunks)]) for d in dirs]
    tots = [spread(gcs[d], [c * c_ + (c_ - 1 if d == 0 else 0) for c in range(n_chunks)]) for d in dirs]
    qs = [refs[d][0][...].astype(F32) * DKB ** -0.5 for d in dirs]
    ks = [refs[d][1][...].astype(F32) for d in dirs]
    v16s = [refs[d][2][...] for d in dirs]
    qts = [qs[d] * jnp.exp(gcs[d] - mids[d]) for d in dirs]
    kt16s = [(ks[d] * jnp.exp(mids[d] - gcs[d])).astype(BF16) for d in dirs]
    qg16s = [(qs[d] * jnp.exp(gcs[d])).astype(BF16) for d in dirs]
    kg16s = [(ks[d] * jnp.exp(tots[d] - gcs[d])).astype(BF16) for d in dirs]
    a16s = {}
    for h in range(HB):
        in_head = (klane >= h * DKB) & (klane < (h + 1) * DKB)
        for d in dirs:
            a = _dot_nt(jnp.where(in_head, qts[d], 0.0).astype(BF16), kt16s[d])
            a16s[(d, h)] = jnp.where(incls[d], a, 0.0).astype(BF16)
    o_intra = [[_dot(a16s[(d, h)], v16s[d][:, h * DVB:(h + 1) * DVB]) for h in range(HB)] for d in dirs]
    xs = [[jnp.where(same_head, _dot_tn(kg16s[d][r, :], v16s[d][r, :]), 0.0) for r in rows] for d in dirs]
    decays = [[jnp.exp(tot_cols[d][:, c * LANES:(c + 1) * LANES]) for c in range(n_chunks)] for d in dirs]

    states = [s_scr[0], s_scr[1]]
    for step in range(n_chunks):
        for d in range(2):
            c = step if d == 0 else n_chunks - 1 - step
            r = rows[c]
            o_inter = _dot(qg16s[d][r, :], states[d].astype(BF16))
            for h in range(HB):
                vs = slice(h * DVB, (h + 1) * DVB)
                refs[d][4][r, vs] = (o_inter[:, vs] + o_intra[d][h][r, :]).astype(BF16)
            states[d] = jnp.concatenate([decays[d][c]] * (vw // LANES), axis=1) * states[d] + xs[d][c]
    s_scr[0] = states[0]
    s_scr[1] = states[1]


def _gla(proj32, proj16, w_gate2, b_gate, n_b, lc, l):
    t = proj16.shape[0]
    n_lat = l // lc
    n_g = n_lat + 1
    n_chunks = lc // CHUNK
    kw, vw = HB * DKB, HB * DVB
    rb0 = lambda b, g: _row_block(b, g, n_b, n_lat)
    rb1 = lambda b, g: _row_block(b, _rev_group(g, n_g), n_b, n_lat)

    def specs(rb):
        return [pl.BlockSpec((lc, kw), lambda b, g: (rb(b, g), H_BQ // kw)),
                pl.BlockSpec((lc, kw), lambda b, g: (rb(b, g), H_BK // kw)),
                pl.BlockSpec((lc, vw), lambda b, g: (rb(b, g), H_BV // vw)),
                pl.BlockSpec((lc, LANES), lambda b, g: (rb(b, g), F_SMALL // LANES))]

    return pl.pallas_call(
        functools.partial(_gla_kernel, n_chunks=n_chunks),
        grid=(n_b, n_g),
        in_specs=specs(rb0) + specs(rb1)
        + [pl.BlockSpec((2, GATE_RANK, kw), lambda b, g: (0, 0, 0)),
           pl.BlockSpec((2, 1, kw), lambda b, g: (0, 0, 0))],
        out_specs=[pl.BlockSpec((lc, vw), lambda b, g: (rb0(b, g), 0)),
                   pl.BlockSpec((lc, vw), lambda b, g: (rb1(b, g), 0))],
        out_shape=[jax.ShapeDtypeStruct((t, vw), BF16)] * 2,
        scratch_shapes=[pltpu.VMEM((2, kw, vw), F32)],
        compiler_params=_cparams(2),
        name="gla_bidir",
    )(proj16, proj16, proj16, proj32, proj16, proj16, proj16, proj32, w_gate2, b_gate.reshape(2, 1, kw))


def _lambda_full(lam, lam_init):
    return (jnp.exp(jnp.sum(lam[0:1] * lam[1:2], axis=1, keepdims=True))
            - jnp.exp(jnp.sum(lam[2:3] * lam[3:4], axis=1, keepdims=True)) + lam_init)


def _attend(q, keys, vals, lam_f, gain, lam_init, o_ref):
    lane = lax.broadcasted_iota(jnp.int32, (1, LANES), 1)
    n = ATTN_SUB
    q = q * (DKC ** -0.5 * math.log2(math.e))
    subs = [q[r * n:(r + 1) * n] for r in range(q.shape[0] // n)]
    q2s = [jnp.concatenate([jnp.where(lane < DKC, x, 0.0), jnp.where(lane >= DKC, x, 0.0)],
                           axis=0).astype(BF16) for x in subs]
    ss = [_dot_nt(q2, keys) for q2 in q2s]
    es = [jnp.exp2(s - jnp.max(s, axis=-1, keepdims=True)) for s in ss]
    invs = [1.0 / jnp.sum(e, axis=-1, keepdims=True) for e in es]
    o2s = [_dot(es[r].astype(BF16), vals) * invs[r] for r in range(len(es))]
    for r, o2 in enumerate(o2s):
        o = o2[:n] - lam_f * o2[n:]
        o_ref[r * n:(r + 1) * n, :] = (_rms_rows(o, gain) * (1.0 - lam_init)).astype(BF16)


def _attn_lat_kernel(lam_ref, gn_ref, q_ref, kc_ref, kl_ref, vc_ref, vl_ref, cq_ref, sq_ref, ck_ref, sk_ref,
                     o_ref, k_scr, v_scr, *, lc, lam_init):
    lane = lax.broadcasted_iota(jnp.int32, (1, LANES), 1)
    first_half = (lane & 31) < 16

    def rope(x, cos, sin):
        swapped = jnp.where(first_half, pltpu.roll(x, LANES - 16, 1), pltpu.roll(x, 16, 1))
        return x * cos + swapped * sin

    @pl.when(pl.program_id(2) == 0)
    def _():
        k_scr[0:lc, :] = kc_ref[...]
        k_scr[lc:, :] = rope(kl_ref[...].astype(F32), ck_ref[...], sk_ref[...]).astype(BF16)
        v_scr[0:lc, :] = vc_ref[...]
        v_scr[lc:, :] = vl_ref[...]

    _attend(rope(q_ref[...].astype(F32), cq_ref[...], sq_ref[...]), k_scr[...], v_scr[...],
            _lambda_full(lam_ref[...], lam_init), gn_ref[...], lam_init, o_ref)


def _attn_ctx_kernel(lam_ref, gn_ref, q_ref, k_ref, v_ref, prev_ref, o_ref, *, lam_init):
    del prev_ref
    _attend(q_ref[...].astype(F32), k_ref[...], v_ref[...], _lambda_full(lam_ref[...], lam_init),
            gn_ref[...], lam_init, o_ref)


def _attention(proj, lam, gn, rope_cos, rope_sin, n_b, lc, l, lam_init, ctx_out):
    t = proj.shape[0]
    tq = min(ATTN_TQ, l)
    per = l // tq
    cq, ck, cv = H_CQ // LANES, H_CK // LANES, H_CV // LANES
    small = [pl.BlockSpec((4, DKC), lambda b, h, *_: (0, 0)), pl.BlockSpec((1, DVC), lambda b, h, *_: (0, 0))]
    tab_q = pl.BlockSpec((tq, LANES), lambda b, h, i: (i, 0))
    tab_k = pl.BlockSpec((l, LANES), lambda b, h, i: (0, 0))
    od = pl.pallas_call(
        functools.partial(_attn_lat_kernel, lc=lc, lam_init=lam_init),
        grid=(n_b, HC, per),
        in_specs=small
        + [pl.BlockSpec((tq, LANES), lambda b, h, i: ((b + 1) * per + i, cq + h)),
           pl.BlockSpec((lc, LANES), lambda b, h, i: (b, ck + h)),
           pl.BlockSpec((l, LANES), lambda b, h, i: (b + 1, ck + h)),
           pl.BlockSpec((lc, LANES), lambda b, h, i: (b, cv + h)),
           pl.BlockSpec((l, LANES), lambda b, h, i: (b + 1, cv + h)),
           tab_q, tab_q, tab_k, tab_k],
        out_specs=pl.BlockSpec((tq, LANES), lambda b, h, i: ((b + 1) * per + i, h)),
        out_shape=jax.ShapeDtypeStruct((t, HC * DVC), BF16),
        scratch_shapes=[pltpu.VMEM((lc + l, LANES), BF16), pltpu.VMEM((lc + l, LANES), BF16)],
        compiler_params=_cparams(3),
        name="diff_attention",
    )(lam, gn.reshape(1, DVC), proj, proj, proj, proj, proj, rope_cos, rope_sin, rope_cos, rope_sin)
    if not ctx_out:
        return od
    return pl.pallas_call(
        functools.partial(_attn_ctx_kernel, lam_init=lam_init),
        grid=(n_b, HC),
        in_specs=small
        + [pl.BlockSpec((lc, LANES), lambda b, h: (b, cq + h)),
           pl.BlockSpec((lc, LANES), lambda b, h: (b, ck + h)),
           pl.BlockSpec((lc, LANES), lambda b, h: (b, cv + h)),
           pl.BlockSpec(memory_space=pl.ANY)],
        out_specs=pl.BlockSpec((lc, LANES), lambda b, h: (b, h)),
        out_shape=jax.ShapeDtypeStruct((t, HC * DVC), BF16),
        input_output_aliases={5: 0},
        compiler_params=_cparams(2),
        name="diff_attention_ctx",
    )(lam, gn.reshape(1, DVC), proj, proj, proj, od)


def _rope_tables(n_tokens):
    rows = n_tokens // GRID_W
    row = jnp.repeat(jnp.arange(rows, dtype=F32), GRID_W)
    col = jnp.tile(jnp.arange(GRID_W, dtype=F32), rows)
    n_freq = DKC // 4
    inv_freq = ROPE_THETA ** (-jnp.arange(n_freq, dtype=F32) / n_freq)
    ang_r = row[:, None] * inv_freq
    ang_c = col[:, None] * inv_freq
    cos = jnp.concatenate([jnp.cos(ang_r)] * 2 + [jnp.cos(ang_c)] * 2, axis=1)
    sin = jnp.concatenate([-jnp.sin(ang_r), jnp.sin(ang_r), -jnp.sin(ang_c), jnp.sin(ang_c)], axis=1)
    return jnp.concatenate([cos, cos], axis=1), jnp.concatenate([sin, sin], axis=1)


def _head_rms(o, gain, n_heads, width):
    parts = []
    for h in range(n_heads):
        parts.append(_rms_rows(o[:, h * width:(h + 1) * width], gain))
    return jnp.concatenate(parts, axis=1)


def _merge_kernel(*refs, ctx_tiles, n_tok):
    x_refs = refs[:n_tok]
    (oa0, oa1, z_ref, ob0, ob1, r_ref, od_ref, ga_ref, gb_ref, gd_ref, mod_ref,
     gna_ref, gnb_ref, woa_ref, wob_ref, woc_ref, wout_ref, o_ref) = refs[n_tok:]
    f32 = lambda ref: ref[...].astype(F32)
    ya = _head_rms(f32(oa0) + f32(oa1), gna_ref[...], HA, DVA) * _silu(f32(z_ref))
    yb = _head_rms(f32(ob0) + f32(ob1), gnb_ref[...], HB, DVB) * _silu(f32(r_ref))
    acc = _sigmoid(f32(ga_ref)) * _dot(ya.astype(BF16), woa_ref[...])
    acc = acc + _sigmoid(f32(gb_ref)) * _dot(yb.astype(BF16), wob_ref[...])
    acc = acc + _sigmoid(f32(gd_ref)) * _dot(od_ref[...], woc_ref[...])
    y = _dot(acc.astype(BF16), wout_ref[...])
    o_ref[...] = _token_tile(x_refs, ctx_tiles) + mod_ref[0, 2:3, :] * y


def _merge(oa0, oa1, ob0, ob1, od, proj, x_ctx, x_lat, lat_first, t, mod, gn_a, gn_b, woa, wob, woc, wout,
           rows_per_mod, row_off, tm):
    d = x_ctx.shape[1]
    off = row_off // tm
    ctx_tiles = rows_per_mod // tm
    n_tiles = (t - row_off) // tm
    per = rows_per_mod // tm
    w5 = HA * DVA
    row = lambda i: (i + off, 0)
    col = lambda c: (lambda i: (i + off, c))
    tok_specs, tok_args = _token_inputs(x_ctx, x_lat, lat_first, tm, ctx_tiles, off)
    return pl.pallas_call(
        functools.partial(_merge_kernel, ctx_tiles=ctx_tiles - off, n_tok=len(tok_args)),
        grid=(n_tiles,),
        in_specs=tok_specs
        + [pl.BlockSpec((tm, w5), row),
                  pl.BlockSpec((tm, w5), row),
                  pl.BlockSpec((tm, w5), col(H_AZ // w5)),
                  pl.BlockSpec((tm, w5), row),
                  pl.BlockSpec((tm, w5), row),
                  pl.BlockSpec((tm, w5), col(H_BR // w5)),
                  pl.BlockSpec((tm, w5), row),
                  pl.BlockSpec((tm, d), col(H_GATES // d)),
                  pl.BlockSpec((tm, d), col(H_GATES // d + 1)),
                  pl.BlockSpec((tm, d), col(H_GATES // d + 2)),
                  pl.BlockSpec((1, 6, d), lambda i: ((i + off) // per, 0, 0)),
                  pl.BlockSpec((1, DVA), lambda i: (0, 0)),
                  pl.BlockSpec((1, DVB), lambda i: (0, 0)),
                  pl.BlockSpec((w5, d), lambda i: (0, 0)),
                  pl.BlockSpec((w5, d), lambda i: (0, 0)),
                  pl.BlockSpec((w5, d), lambda i: (0, 0)),
                  pl.BlockSpec((d, d), lambda i: (0, 0))],
        out_specs=pl.BlockSpec((tm, d), lambda i: (i, 0)),
        out_shape=jax.ShapeDtypeStruct((t - row_off, d), F32),
        compiler_params=_cparams(1),
        name="merge_outproj",
    )(*tok_args, oa0, oa1, proj, ob0, ob1, proj, od, proj, proj, proj, mod,
      gn_a.reshape(1, DVA), gn_b.reshape(1, DVB), woa, wob, woc, wout)


def _ffn_kernel(x_ref, mod_ref, g_ref, fg_ref, w1_ref, w3_ref, w2_ref, o_ref, h_scr, *, final):
    f = pl.program_id(1)

    @pl.when(f == 0)
    def _():
        y = _rms_rows(x_ref[...], g_ref[...])
        h_scr[...] = (y * (1.0 + mod_ref[0, 4:5, :]) + mod_ref[0, 3:4, :]).astype(BF16)
        o_ref[...] = jnp.zeros_like(o_ref)

    h = h_scr[...]
    t = _silu(_dot(h, w1_ref[...])) * _dot(h, w3_ref[...])
    o_ref[...] += _dot(t.astype(BF16), w2_ref[...])

    @pl.when(f == pl.num_programs(1) - 1)
    def _():
        out = x_ref[...] + mod_ref[0, 5:6, :] * o_ref[...]
        if final:
            out = _rms_rows(out, fg_ref[...])
        o_ref[...] = out


def _ffn(x, mod, g, final_g, w1, w3, w2, rows_per_mod, mod_off, tm, tf, final):
    t, d = x.shape
    dff = w1.shape[1]
    per = rows_per_mod // tm
    return pl.pallas_call(
        functools.partial(_ffn_kernel, final=final),
        grid=(t // tm, dff // tf),
        in_specs=[pl.BlockSpec((tm, d), lambda i, f: (i, 0)),
                  pl.BlockSpec((1, 6, d), lambda i, f: (i // per + mod_off, 0, 0)),
                  pl.BlockSpec((1, d), lambda i, f: (0, 0)),
                  pl.BlockSpec((1, d), lambda i, f: (0, 0)),
                  pl.BlockSpec((d, tf), lambda i, f: (0, f)),
                  pl.BlockSpec((d, tf), lambda i, f: (0, f)),
                  pl.BlockSpec((tf, d), lambda i, f: (f, 0))],
        out_specs=pl.BlockSpec((tm, d), lambda i, f: (i, 0)),
        out_shape=jax.ShapeDtypeStruct((t, d), F32),
        scratch_shapes=[pltpu.VMEM((tm, d), BF16)],
        compiler_params=_cparams(2),
        name="dense_ffn",
    )(x, mod, g.reshape(1, d), final_g.reshape(1, d), w1, w3, w2)


MOE_ROWS = 256
MOE_MOVE_UNIT = 128
MOE_FFN_UNIT = 128


def _moe_kernel(x_ref, mod_ref, g_ref, rw_ref, tri_ref, fg_ref, w1_ref, w3_ref, w2_ref, o_ref,
                h_scr, gate_scr, sel_scr, rank_scr, selr_scr, rankr_scr, xg_scr, yg_scr, nb_scr, *, final):
    e = pl.program_id(1)
    f = pl.program_id(2)
    n_f = pl.num_programs(2)
    tm, d = x_ref.shape
    cb = MOE_ROWS
    lane = lax.broadcasted_iota(jnp.int32, (1, LANES), 1)

    def for_blocks(body, slot, unit):
        units = nb_scr[slot]
        per = cb // unit
        nb = units // per

        def full(j, carry):
            body(pl.multiple_of(j * cb, cb), cb)
            return carry

        lax.fori_loop(0, nb, full, 0)
        rem = units - nb * per
        r0 = nb * cb
        size = cb // 2
        while size >= unit:
            has = (rem // (size // unit)) % 2 == 1

            @pl.when(has)
            def _():
                body(pl.multiple_of(r0, unit), size)

            r0 = r0 + jnp.where(has, size, 0)
            size //= 2

    @pl.when((e == 0) & (f == 0))
    def _():
        y = _rms_rows(x_ref[...], g_ref[...])
        hmod = y * (1.0 + mod_ref[0, 4:5, :]) + mod_ref[0, 3:4, :]
        h_scr[...] = hmod.astype(BF16)
        o_ref[...] = jnp.zeros_like(o_ref)
        logits = _dot3(hmod, rw_ref[...])
        logits = jnp.where(lane < N_EXPERTS, logits, -jnp.inf)
        lanef = lane.astype(F32)
        m1 = jnp.max(logits, axis=1, keepdims=True)
        i1 = jnp.min(jnp.where(logits == m1, lanef, float(LANES)), axis=1, keepdims=True)
        hit1 = lanef == i1
        rest = jnp.where(hit1, -jnp.inf, logits)
        m2 = jnp.max(rest, axis=1, keepdims=True)
        i2 = jnp.min(jnp.where(rest == m2, lanef, float(LANES)), axis=1, keepdims=True)
        hit2 = lanef == i2
        ex = jnp.exp(m2 - m1)
        inv = 1.0 / (1.0 + ex)
        gate_scr[...] = jnp.where(hit1, inv, 0.0) + jnp.where(hit2, ex * inv, 0.0)
        sel = jnp.where(hit1 | hit2, 1.0, 0.0)
        sel_scr[...] = sel
        sel16 = sel.astype(BF16)
        rank = _dot(tri_ref[...], sel16)
        rank_scr[...] = rank
        pick = jnp.where(lax.broadcasted_iota(jnp.int32, (8, LANES), 0)
                         == lax.broadcasted_iota(jnp.int32, (8, LANES), 1), 1.0, 0.0).astype(BF16)
        selr_scr[...] = _dot_nt(pick, sel16)
        hi = jnp.floor(rank * (1.0 / 32.0))
        lo = rank - 32.0 * hi
        rankr_scr[...] = 32.0 * _dot_nt(pick, hi.astype(BF16)) + _dot_nt(pick, lo.astype(BF16))

    @pl.when(f == 0)
    def _():
        sel_row = selr_scr[pl.ds(e, 1), :]
        rank_row = rankr_scr[pl.ds(e, 1), :]
        n_e = jnp.sum(sel_row).astype(jnp.int32)
        nb_scr[0] = (n_e + (MOE_MOVE_UNIT - 1)) // MOE_MOVE_UNIT
        nb_scr[1] = (n_e + (MOE_FFN_UNIT - 1)) // MOE_FFN_UNIT

        def gather(r0, bs):
            slot = lax.broadcasted_iota(jnp.int32, (bs, tm), 0).astype(F32) + r0.astype(F32)
            onehot = jnp.where((rank_row == slot) & (sel_row > 0.0), 1.0, 0.0)
            xg_scr[pl.ds(r0, bs), :] = _dot(onehot.astype(BF16), h_scr[...]).astype(BF16)
            yg_scr[pl.ds(r0, bs), :] = jnp.zeros((bs, d), F32)

        for_blocks(gather, 0, MOE_MOVE_UNIT)

    def expert(r0, bs):
        xb = xg_scr[pl.ds(r0, bs), :]
        t = _silu(_dot(xb, w1_ref[0])) * _dot(xb, w3_ref[0])
        yg_scr[pl.ds(r0, bs), :] += _dot(t.astype(BF16), w2_ref[0])

    for_blocks(expert, 1, MOE_FFN_UNIT)

    @pl.when(f == n_f - 1)
    def _():
        pick_e = lane == e
        rank_col = jnp.sum(jnp.where(pick_e, rank_scr[...], 0.0), axis=1, keepdims=True)
        sel_col = jnp.sum(jnp.where(pick_e, sel_scr[...], 0.0), axis=1, keepdims=True)
        gate_col = jnp.sum(jnp.where(pick_e, gate_scr[...], 0.0), axis=1, keepdims=True)

        def scatter(r0, bs):
            slot = lax.broadcasted_iota(jnp.int32, (tm, bs), 1).astype(F32) + r0.astype(F32)
            onehot = jnp.where((rank_col == slot) & (sel_col > 0.0), 1.0, 0.0)
            o_ref[...] += gate_col * _dot(onehot.astype(BF16), yg_scr[pl.ds(r0, bs), :].astype(BF16))

        for_blocks(scatter, 0, MOE_MOVE_UNIT)

    @pl.when((e == pl.num_programs(1) - 1) & (f == n_f - 1))
    def _():
        out = x_ref[...] + mod_ref[0, 5:6, :] * o_ref[...]
        if final:
            out = _rms_rows(out, fg_ref[...])
        o_ref[...] = out


def _moe(x, mod, g, router_w, final_g, w1, w3, w2, rows_per_mod, mod_off, tm, tf, final):
    t, d = x.shape
    n_e, _, dff = w1.shape
    per = rows_per_mod // tm
    return pl.pallas_call(
        functools.partial(_moe_kernel, final=final),
        grid=(t // tm, n_e, dff // tf),
        in_specs=[pl.BlockSpec((tm, d), lambda i, e, f: (i, 0)),
                  pl.BlockSpec((1, 6, d), lambda i, e, f: (i // per + mod_off, 0, 0)),
                  pl.BlockSpec((1, d), lambda i, e, f: (0, 0)),
                  pl.BlockSpec((d, LANES), lambda i, e, f: (0, 0)),
                  pl.BlockSpec((tm, tm), lambda i, e, f: (0, 0)),
                  pl.BlockSpec((1, d), lambda i, e, f: (0, 0)),
                  pl.BlockSpec((1, d, tf), lambda i, e, f: (e, 0, f)),
                  pl.BlockSpec((1, d, tf), lambda i, e, f: (e, 0, f)),
                  pl.BlockSpec((1, tf, d), lambda i, e, f: (e, f, 0))],
        out_specs=pl.BlockSpec((tm, d), lambda i, e, f: (i, 0)),
        out_shape=jax.ShapeDtypeStruct((t, d), F32),
        scratch_shapes=[pltpu.VMEM((tm, d), BF16),
                        pltpu.VMEM((tm, LANES), F32),
                        pltpu.VMEM((tm, LANES), F32),
                        pltpu.VMEM((tm, LANES), F32),
                        pltpu.VMEM((8, tm), F32),
                        pltpu.VMEM((8, tm), F32),
                        pltpu.VMEM((tm, d), BF16),
                        pltpu.VMEM((tm, d), F32),
                        pltpu.SMEM((2,), jnp.int32)],
        compiler_params=_cparams(3),
        name="routed_ffn_sparse",
    )(x, mod, g.reshape(1, d), router_w, jnp.tril(jnp.ones((tm, tm), BF16), -1),
      final_g.reshape(1, d), w1, w3, w2)


def _w_in_pieces(end):
    a_qkv = 2 * HA * DKA + HA * DVA
    a_z = a_qkv + HA * DVA
    b_q = a_z + 4 * HA
    b_v = b_q + 2 * HB * DKB
    b_glr = b_v + 2 * HB * DVB
    c_q = b_glr + 2 * GATE_RANK
    gates = c_q + 4 * HC * DKC + HC * DVC
    return [(0, a_qkv), (a_z, b_q), (b_glr, c_q), None,
            (gates, end), (a_qkv, a_z), (b_v, b_glr), (c_q, gates), (b_q, b_v)]


def _reorder_kernel(w_ref, o_ref):
    at = 0
    for piece in _w_in_pieces(w_ref.shape[1]):
        if piece is None:
            width = N_F32 - at
            o_ref[0, at:at + width, :] = jnp.zeros((width, o_ref.shape[2]), BF16)
        else:
            width = piece[1] - piece[0]
            o_ref[0, at:at + width, :] = w_ref[0, piece[0]:piece[1], :].astype(BF16)
        at += width
    assert at == N_F32 + N_B16


def _reorder_w_in(w_in):
    w_t = jnp.swapaxes(w_in, 1, 2)
    depth, cols, d = w_t.shape
    kb = LANES
    return pl.pallas_call(
        _reorder_kernel,
        grid=(depth, d // kb),
        in_specs=[pl.BlockSpec((1, cols, kb), lambda l, i: (l, 0, i))],
        out_specs=pl.BlockSpec((1, N_F32 + N_B16, kb), lambda l, i: (l, 0, i)),
        out_shape=jax.ShapeDtypeStruct((depth, N_F32 + N_B16, d), BF16),
        compiler_params=_cparams(2),
        name="reorder_w_in",
    )(w_t)


def kernel(x, c, ctx, c_ctx, w_mod, b_mod, norm1_g, norm2_g, w_in, conv_a, a_log, dt_bias, gn_a, w_gate2, b_gate, gn_b, lam_c, gn_c, w_o_a, w_o_b, w_o_c, w_out, ffn_w1, ffn_w3, ffn_w2, router_w, moe_w1, moe_w3, moe_w2, final_g):
    n_b, l, d = x.shape
    lc = ctx.shape[1]
    depth = w_mod.shape[0]
    assert n_b * lc == l and lc % CHUNK == 0 and l & (l - 1) == 0 and lc & (lc - 1) == 0
    t_ctx = n_b * lc
    tm = min(1024, l)
    tm_small = min(512, l)

    n_mod = 16
    cvec = jnp.concatenate([c_ctx[None, :], c, jnp.zeros((n_mod - 1 - n_b, d), F32)], axis=0)
    mod_all = _modulation(cvec, w_mod, b_mod).reshape(depth, n_mod, 6, d)

    rope_cos, rope_sin = _rope_tables(l)
    w_in16 = _reorder_w_in(w_in)
    t_all = t_ctx + n_b * l
    tokens = (ctx.reshape(t_ctx, d), x.reshape(n_b * l, d), 0)

    for layer in range(depth):
        ctx_out = layer < depth - 1
        lam_init = 0.8 - 0.6 * math.exp(-0.3 * layer)
        mod = mod_all[layer]
        proj32, proj16 = _in_projection(*tokens, t_all, mod, norm1_g[layer], w_in16, layer, l, tm)

        qkvc = _short_conv(proj32, conv_a[layer], lc, l)
        oa0, oa1 = _deltanet(qkvc, proj32, a_log[layer], dt_bias[layer], n_b, lc, l)
        ob0, ob1 = _gla(proj32, proj16, w_gate2[layer], b_gate[layer], n_b, lc, l)
        od = _attention(proj16, lam_c[layer], gn_c[layer], rope_cos, rope_sin, n_b, lc, l, lam_init, ctx_out)

        row_off = 0 if ctx_out else t_ctx
        xs_new = _merge(oa0, oa1, ob0, ob1, od, proj16, *tokens, t_all, mod, gn_a[layer], gn_b[layer],
                        w_o_a[layer].astype(BF16), w_o_b[layer].astype(BF16), w_o_c[layer].astype(BF16),
                        w_out[layer].astype(BF16), l, row_off, tm_small)

        i = layer // 2
        final = layer == depth - 1
        mod_off = 0 if ctx_out else 1
        if layer % 2 == 0:
            xs = _ffn(xs_new, mod, norm2_g[layer], final_g, ffn_w1[i].astype(BF16),
                      ffn_w3[i].astype(BF16), ffn_w2[i].astype(BF16), l, mod_off, tm_small,
                      ffn_w1.shape[2] // 2, final)
        else:
            rw = jnp.pad(router_w[i], ((0, 0), (0, LANES - N_EXPERTS)))
            xs = _moe(xs_new, mod, norm2_g[layer], rw, final_g, moe_w1[i].astype(BF16),
                      moe_w3[i].astype(BF16), moe_w2[i].astype(BF16), l, mod_off, tm,
                      moe_w1.shape[3] // 2, final)
        tokens = (xs, xs, t_ctx)

    return xs.reshape(n_b, l, d)
```

```python
import functools
import math

import jax
import jax.numpy as jnp
from jax import lax
from jax.experimental import pallas as pl
from jax.experimental.pallas import tpu as pltpu

F32 = jnp.float32
BF16 = jnp.bfloat16
HIGHEST = lax.Precision.HIGHEST

EPS = 1e-6
HA, DKA, DVA = 4, 128, 128
HB, DKB, DVB = 4, 64, 128
GATE_RANK, GATE_NORM = 16, 16.0
HC, DKC, DVC = 4, 64, 128
GRID_W, ROPE_THETA = 64, 10000.0
N_EXPERTS, TOP_K = 8, 2
CHUNK = 64
LANES = 128
ATTN_SUB = 64
ATTN_TQ = 512
CONV_HEADS = 2

F_QKV = 0
F_SMALL = 1536
N_F32 = 1664
H_GATES = 0
H_AZ = 3072
H_BV = 3584
H_BR = 4096
H_CQ = 4608
H_CK = 5120
H_CV = 5632
H_BQ = 6144
H_BK = 6400
N_B16 = 6656
PROJ_TN = 1664

VMEM_LIMIT = 56 * 1024 * 1024


def _cparams(n_axes):
    return pltpu.CompilerParams(dimension_semantics=("arbitrary",) * n_axes,
                                vmem_limit_bytes=VMEM_LIMIT)


def _sigmoid(x):
    return 1.0 / (1.0 + jnp.exp2(x * -math.log2(math.e)))


def _silu(x):
    return x * _sigmoid(x)


def _softplus(x):
    return jnp.maximum(x, 0.0) + jnp.log(1.0 + jnp.exp(-jnp.abs(x)))


def _dot(a, b):
    return jnp.dot(a, b, preferred_element_type=F32)


def _dot_nt(a, b):
    return lax.dot_general(a, b, (((1,), (1,)), ((), ())), preferred_element_type=F32)


def _dot_tn(a, b):
    return lax.dot_general(a, b, (((0,), (0,)), ((), ())), preferred_element_type=F32)


def _split(a):
    hi = a.astype(BF16)
    lo = (a - hi.astype(F32)).astype(BF16)
    return hi, lo


def _dot3(a, b):
    ah, al = _split(a)
    bh, bl = _split(b)
    return _dot(ah, bh) + (_dot(ah, bl) + _dot(al, bh))


def _rms_rows(x, gain):
    ms = jnp.mean(x * x, axis=-1, keepdims=True)
    return x * lax.rsqrt(ms + EPS) * gain


def _mod_kernel(c_ref, w_ref, b_ref, o_ref):
    s = _silu(c_ref[...])
    o_ref[0] = jnp.dot(s, w_ref[0], precision=HIGHEST, preferred_element_type=F32) + b_ref[0]


def _modulation(cvec, w_mod, b_mod):
    depth, d, d6 = w_mod.shape
    n = cvec.shape[0]
    return pl.pallas_call(
        _mod_kernel,
        grid=(depth, d6 // d),
        in_specs=[pl.BlockSpec((n, d), lambda l, j: (0, 0)),
                  pl.BlockSpec((1, d, d), lambda l, j: (l, 0, j)),
                  pl.BlockSpec((1, 1, d), lambda l, j: (l, 0, j))],
        out_specs=pl.BlockSpec((1, n, d), lambda l, j: (l, 0, j)),
        out_shape=jax.ShapeDtypeStruct((depth, n, d6), F32),
        compiler_params=_cparams(2),
        name="adaln_mod",
    )(cvec, w_mod, b_mod.reshape(depth, 1, d6))


def _token_specs(tm, d, ctx_tiles, lat_first, off=0):
    return [pl.BlockSpec((tm, d), lambda i, *_: (jnp.minimum(i + off, ctx_tiles - 1), 0)),
            pl.BlockSpec((tm, d), lambda i, *_: (jnp.maximum(i + off - ctx_tiles, 0) + lat_first, 0))]


def _token_inputs(x_ctx, x_lat, lat_first, tm, ctx_tiles, off=0):
    d = x_ctx.shape[1]
    if x_lat is x_ctx:
        return [pl.BlockSpec((tm, d), lambda i, *_: (i + off, 0))], [x_ctx]
    return _token_specs(tm, d, ctx_tiles, lat_first // tm, off), [x_ctx, x_lat]


def _token_tile(x_refs, ctx_tiles):
    if len(x_refs) == 1:
        return x_refs[0][...]
    return jnp.where(pl.program_id(0) < ctx_tiles, x_refs[0][...], x_refs[1][...])


def _inproj_kernel(*refs, ctx_tiles, n_tok):
    x_refs, (mod_ref, g_ref, w_ref, o32_ref, o16_ref, h_scr) = refs[:n_tok], refs[n_tok:]
    j = pl.program_id(1)

    @pl.when(j == 0)
    def _():
        y = _rms_rows(_token_tile(x_refs, ctx_tiles), g_ref[...])
        h_scr[...] = (y * (1.0 + mod_ref[0, 1:2, :]) + mod_ref[0, 0:1, :]).astype(BF16)
        o32_ref[...] = _dot_nt(h_scr[...], w_ref[0])

    @pl.when(j > 0)
    def _():
        o16_ref[...] = _dot_nt(h_scr[...], w_ref[0]).astype(BF16)


def _in_projection(x_ctx, x_lat, lat_first, t, mod, g, w, layer, rows_per_mod, tm):
    d = x_ctx.shape[1]
    tn = PROJ_TN
    assert w.shape[1] == N_F32 + N_B16 and N_F32 == tn and N_B16 % tn == 0
    per = rows_per_mod // tm
    ctx_tiles = rows_per_mod // tm
    tok_specs, tok_args = _token_inputs(x_ctx, x_lat, lat_first, tm, ctx_tiles)
    return pl.pallas_call(
        functools.partial(_inproj_kernel, ctx_tiles=ctx_tiles, n_tok=len(tok_args)),
        grid=(t // tm, 1 + N_B16 // tn),
        in_specs=tok_specs
        + [pl.BlockSpec((1, 6, d), lambda i, j: (i // per, 0, 0)),
           pl.BlockSpec((1, d), lambda i, j: (0, 0)),
           pl.BlockSpec((1, tn, d), lambda i, j: (layer, j, 0))],
        out_specs=[pl.BlockSpec((tm, tn), lambda i, j: (i, 0)),
                   pl.BlockSpec((tm, tn), lambda i, j: (i, jnp.maximum(j - 1, 0)))],
        out_shape=[jax.ShapeDtypeStruct((t, N_F32), F32), jax.ShapeDtypeStruct((t, N_B16), BF16)],
        scratch_shapes=[pltpu.VMEM((tm, d), BF16)],
        compiler_params=_cparams(2),
        name="norm_mod_inproj",
    )(*tok_args, mod, g.reshape(1, d), w)


def _conv_kernel(x_ref, w_ref, o_ref, *, lc, l):
    p = pl.program_id(0)
    j = pl.program_id(1)
    x = x_ref[...]
    n = x.shape[0]
    row = lax.broadcasted_iota(jnp.int32, x.shape, 0)
    seq = jnp.where(p == 0, lc, l)
    pos = row & (seq - 1)
    prev = jnp.where(pos == 0, 0.0, pltpu.roll(x, 1, 0))
    nxt = jnp.where(pos == seq - 1, 0.0, pltpu.roll(x, n - 1, 0))
    w = w_ref[...]
    y = _silu(prev * w[0:1] + x * w[1:2] + nxt * w[2:3])
    heads_per_block = x.shape[1] // DKA
    for hh in range(heads_per_block):
        cols = slice(hh * DKA, (hh + 1) * DKA)
        yh = y[:, cols]
        inv = lax.rsqrt(jnp.sum(yh * yh, axis=-1, keepdims=True) + EPS)
        head = j * heads_per_block + hh
        fac = jnp.where(head < HA, inv * DKA ** -0.5, jnp.where(head < 2 * HA, inv, 1.0))
        o_ref[:, cols] = yh * fac


def _short_conv(proj, conv_w, lc, l):
    t = proj.shape[0]
    ncol = conv_w.shape[1]
    wide = CONV_HEADS * DKA
    return pl.pallas_call(
        functools.partial(_conv_kernel, lc=lc, l=l),
        grid=(t // l, ncol // wide),
        in_specs=[pl.BlockSpec((l, wide), lambda p, j: (p, j)),
                  pl.BlockSpec((conv_w.shape[0], wide), lambda p, j: (0, j))],
        out_specs=pl.BlockSpec((l, wide), lambda p, j: (p, j)),
        out_shape=jax.ShapeDtypeStruct((t, ncol), F32),
        compiler_params=_cparams(2),
        name="conv_silu_l2",
    )(proj, conv_w)


def _row_block(b, g, n_b, n_lat):
    return jnp.where(g == 0, b, n_b + b * n_lat + g - 1)


def _rev_group(g, n_g):
    return jnp.where(g == 0, 0, n_g - g)


def _tri_inverse_many(lms, ii, jj, uppers):
    c = lms[0].shape[0]
    eye = (ii == jj).astype(F32)
    n = len(lms)
    ts = None
    s, ls = 1, 0
    while s < c:
        same = (ii >> (ls + 1)) == (jj >> (ls + 1))
        io = (ii >> ls) & 1
        jo = (jj >> ls) & 1
        lower = same & (io == 1) & (jo == 0)
        upper = same & (io == 0) & (jo == 1)
        cms = [jnp.where(upper if uppers[i] else lower, lms[i], 0.0) for i in range(n)]
        if ts is None:
            ts = [eye - cm for cm in cms]
        else:
            t16 = [t.astype(BF16) for t in ts]
            ps = [_dot(t16[i], cms[i].astype(BF16)).astype(BF16) for i in range(n)]
            ts = [ts[i] - _dot(ps[i], t16[i]) for i in range(n)]
        s, ls = s * 2, ls + 1
    a_parts = [_split(eye + lm) for lm in lms]
    t_parts = [_split(t) for t in ts]
    res = [eye - (_dot(a_parts[i][0], t_parts[i][0])
                  + (_dot(a_parts[i][0], t_parts[i][1]) + _dot(a_parts[i][1], t_parts[i][0])))
           for i in range(n)]
    return [ts[i] + _dot(t_parts[i][0], res[i].astype(BF16)) for i in range(n)]


def _dn_kernel(alog_ref, dtb_ref, q0, k0, v0, sm0, q1, k1, v1, sm1, o0_ref, o1_ref, s_scr, *, n_chunks):
    @pl.when(pl.program_id(1) == 0)
    def _():
        s_scr[...] = jnp.zeros_like(s_scr)

    c_ = CHUNK
    refs = ((q0, k0, v0, o0_ref), (q1, k1, v1, o1_ref))
    sms = (sm0[...], sm1[...])
    sm_ts = tuple(s.T for s in sms)
    ii = lax.broadcasted_iota(jnp.int32, (c_, c_), 0)
    jj = lax.broadcasted_iota(jnp.int32, (c_, c_), 1)
    rows = [slice(c * c_, (c + 1) * c_) for c in range(n_chunks)]
    lanes = [slice(j * LANES, (j + 1) * LANES) for j in range(HA)]
    items = [(j, c, d) for j in range(HA) for c in range(n_chunks) for d in range(2)]
    qs = [refs[d][0][rows[c], lanes[j]] for j, c, d in items]
    ks = [refs[d][1][rows[c], lanes[j]] for j, c, d in items]
    vs = [refs[d][2][rows[c], lanes[j]] for j, c, d in items]
    k16 = [k.astype(BF16) for k in ks]
    kks = [_dot_nt(k, k) for k in k16]
    qks = [_dot_nt(qs[i].astype(BF16), k16[i]) for i in range(len(items))]

    betas, gcs, totals, gams, lms = [], [], [], [], []
    for i, (j, c, d) in enumerate(items):
        r = rows[c]
        incl = (jj <= ii) if d == 0 else (jj >= ii)
        strict = (jj < ii) if d == 0 else (jj > ii)
        incl_t = (ii <= jj) if d == 0 else (ii >= jj)
        a_scale = -jnp.exp(jnp.full((1, 1), alog_ref[d * HA + j], F32))
        dtb = dtb_ref[d * HA + j]
        col = d * HA + j
        beta = _sigmoid(sms[d][r, col:col + 1])
        g_col = a_scale * _softplus(sms[d][r, 2 * HA + col:2 * HA + col + 1] + dtb)
        g_row = a_scale * _softplus(sm_ts[d][2 * HA + col:2 * HA + col + 1, r] + dtb)
        gc_col = jnp.sum(jnp.where(incl, g_row, 0.0), axis=1, keepdims=True)
        gc_row = jnp.sum(jnp.where(incl_t, g_col, 0.0), axis=0, keepdims=True)
        gam = jnp.where(incl, jnp.exp(gc_col - gc_row), 0.0)
        betas.append(beta)
        gcs.append(gc_col)
        totals.append(jnp.sum(g_col, axis=0, keepdims=True))
        gams.append(gam)
        lms.append(jnp.where(strict, beta * kks[i] * gam, 0.0))

    tinvs = _tri_inverse_many(lms, ii, jj, [d == 1 for _, _, d in items])
    egs = [jnp.exp(gc) for gc in gcs]
    uws = []
    for i in range(len(items)):
        rhs = jnp.concatenate([vs[i] * betas[i], ks[i] * (betas[i] * egs[i])], axis=1).astype(BF16)
        uws.append(_dot(tinvs[i].astype(BF16), rhs))
    us = [uw[:, :DVA] for uw in uws]
    wq16 = [jnp.concatenate([uws[i][:, DVA:], qs[i] * egs[i]], axis=0).astype(BF16) for i in range(len(items))]
    kg16 = [(ks[i] * jnp.exp(totals[i] - gcs[i])).astype(BF16) for i in range(len(items))]
    qkg16 = [(qks[i] * gams[i]).astype(BF16) for i in range(len(items))]
    carry = [jnp.exp(t) for t in totals]

    index = {item: i for i, item in enumerate(items)}
    chains = [(j, d) for j in range(HA) for d in range(2)]
    states = [s_scr[d * HA + j] for j, d in chains]
    for step in range(n_chunks):
        at = [index[(j, step if d == 0 else n_chunks - 1 - step, d)] for j, d in chains]
        s16 = [s.astype(BF16) for s in states]
        prods = [_dot(wq16[i], s16[n]) for n, i in enumerate(at)]
        vn16 = [(us[i] - prods[n][:c_]).astype(BF16) for n, i in enumerate(at)]
        for n, i in enumerate(at):
            j, c, d = items[i]
            refs[d][3][rows[c], lanes[j]] = (prods[n][c_:] + _dot(qkg16[i], vn16[n])).astype(BF16)
            states[n] = carry[i] * states[n] + _dot_tn(kg16[i], vn16[n])
    for n, (j, d) in enumerate(chains):
        s_scr[d * HA + j] = states[n]


def _deltanet(qkvc, proj32, a_log, dt_bias, n_b, lc, l):
    t = qkvc.shape[0]
    n_lat = l // lc
    n_g = n_lat + 1
    wide = HA * DVA
    rb0 = lambda b, g: _row_block(b, g, n_b, n_lat)
    rb1 = lambda b, g: _row_block(b, _rev_group(g, n_g), n_b, n_lat)
    smem = pl.BlockSpec(memory_space=pltpu.SMEM)

    def specs(rb):
        return [pl.BlockSpec((lc, wide), lambda b, g: (rb(b, g), 0)),
                pl.BlockSpec((lc, wide), lambda b, g: (rb(b, g), 1)),
                pl.BlockSpec((lc, wide), lambda b, g: (rb(b, g), 2)),
                pl.BlockSpec((lc, LANES), lambda b, g: (rb(b, g), F_SMALL // LANES))]

    return pl.pallas_call(
        functools.partial(_dn_kernel, n_chunks=lc // CHUNK),
        grid=(n_b, n_g),
        in_specs=[smem, smem] + specs(rb0) + specs(rb1),
        out_specs=[pl.BlockSpec((lc, wide), lambda b, g: (rb0(b, g), 0)),
                   pl.BlockSpec((lc, wide), lambda b, g: (rb1(b, g), 0))],
        out_shape=[jax.ShapeDtypeStruct((t, wide), BF16)] * 2,
        scratch_shapes=[pltpu.VMEM((2 * HA, DKA, DVA), F32)],
        compiler_params=_cparams(2),
        name="deltanet",
    )(a_log.reshape(-1), dt_bias.reshape(-1), qkvc, qkvc, qkvc, proj32, qkvc, qkvc, qkvc, proj32)


def _gla_kernel(q0, k0, v0, sm0, q1, k1, v1, sm1, wg_ref, bg_ref, o0_ref, o1_ref, s_scr, *, n_chunks):
    @pl.when(pl.program_id(1) == 0)
    def _():
        s_scr[...] = jnp.zeros_like(s_scr)

    c_ = CHUNK
    sh = c_.bit_length() - 1
    g_ = n_chunks * c_
    kw = HB * DKB
    vw = HB * DVB
    ii = lax.broadcasted_iota(jnp.int32, (g_, g_), 0)
    jj = lax.broadcasted_iota(jnp.int32, (g_, g_), 1)
    same_chunk = (ii >> sh) == (jj >> sh)
    klane = lax.broadcasted_iota(jnp.int32, (1, kw), 1)
    ind16 = jnp.where((lax.broadcasted_iota(jnp.int32, (g_, n_chunks * LANES), 0) >> sh)
                      == (lax.broadcasted_iota(jnp.int32, (g_, n_chunks * LANES), 1) >> (LANES.bit_length() - 1)),
                      1.0, 0.0).astype(BF16)
    rows = [slice(c * c_, (c + 1) * c_) for c in range(n_chunks)]
    refs = ((q0, k0, v0, sm0, o0_ref), (q1, k1, v1, sm1, o1_ref))

    def spread(x, picks):
        return jnp.concatenate([jnp.broadcast_to(x[p:p + 1, :], (c_, kw)) for p in picks], axis=0)

    dirs = (0, 1)
    incls = [same_chunk & ((jj <= ii) if d == 0 else (jj >= ii)) for d in dirs]
    tri16 = [jnp.where(m, 1.0, 0.0).astype(BF16) for m in incls]
    pres = [jnp.dot(refs[d][3][:, 4 * HA + d * GATE_RANK:4 * HA + (d + 1) * GATE_RANK], wg_ref[d],
                    precision=HIGHEST, preferred_element_type=F32) + bg_ref[d] for d in dirs]
    glogs = [(jnp.minimum(p, 0.0) - jnp.log(1.0 + jnp.exp(-jnp.abs(p)))) * (1.0 / GATE_NORM) for p in pres]
    parts = [_split(g) for g in glogs]
    gcs = [_dot(tri16[d], parts[d][0]) + _dot(tri16[d], parts[d][1]) for d in dirs]
    tot_cols = [_dot_tn(parts[d][0], ind16) + _dot_tn(parts[d][1], ind16) for d in dirs]
    mids = [spread(gcs[d], [c * c_ + c_ // 2 - 1 for c in range(n_chunks)]) for d in dirs]
    tots = [spread(gcs[d], [c * c_ + (c_ - 1 if d == 0 else 0) for c in range(n_chunks)]) for d in dirs]
    qs = [refs[d][0][...].astype(F32) * DKB ** -0.5 for d in dirs]
    ks = [refs[d][1][...].astype(F32) for d in dirs]
    v16s = [refs[d][2][...] for d in dirs]
    qts = [qs[d] * jnp.exp(gcs[d] - mids[d]) for d in dirs]
    kt16s = [(ks[d] * jnp.exp(mids[d] - gcs[d])).astype(BF16) for d in dirs]
    qg16s = [(qs[d] * jnp.exp(gcs[d])).astype(BF16) for d in dirs]
    kg16s = [(ks[d] * jnp.exp(tots[d] - gcs[d])).astype(BF16) for d in dirs]
    a16s = {}
    for h in range(HB):
        in_head = (klane >= h * DKB) & (klane < (h + 1) * DKB)
        for d in dirs:
            a = _dot_nt(jnp.where(in_head, qts[d], 0.0).astype(BF16), kt16s[d])
            a16s[(d, h)] = jnp.where(incls[d], a, 0.0).astype(BF16)
    o_intra = [[_dot(a16s[(d, h)], v16s[d][:, h * DVB:(h + 1) * DVB]) for h in range(HB)] for d in dirs]
    xs = [[_dot_tn(kg16s[d][r, :], v16s[d][r, :]) for r in rows] for d in dirs]
    decays = [[jnp.exp(tot_cols[d][:, c * LANES:(c + 1) * LANES]) for c in range(n_chunks)] for d in dirs]

    states = [[s_scr[d, h] for h in range(HB)] for d in dirs]
    zero16 = jnp.zeros((DKB, DVB), BF16)
    for step in range(n_chunks):
        for d in range(2):
            c = step if d == 0 else n_chunks - 1 - step
            r = rows[c]
            s16 = [s.astype(BF16) for s in states[d]]
            s_all = jnp.concatenate(
                [jnp.concatenate([s16[h] if hv == h else zero16 for hv in range(HB)], axis=1) for h in range(HB)],
                axis=0)
            o_inter = _dot(qg16s[d][r, :], s_all)
            for h in range(HB):
                vs = slice(h * DVB, (h + 1) * DVB)
                kr = slice(h * DKB, (h + 1) * DKB)
                refs[d][4][r, vs] = (o_inter[:, vs] + o_intra[d][h][r, :]).astype(BF16)
                states[d][h] = decays[d][c][kr, :] * states[d][h] + xs[d][c][kr, vs]
    for d in dirs:
        for h in range(HB):
            s_scr[d, h] = states[d][h]


def _gla(proj32, proj16, w_gate2, b_gate, n_b, lc, l):
    t = proj16.shape[0]
    n_lat = l // lc
    n_g = n_lat + 1
    n_chunks = lc // CHUNK
    kw, vw = HB * DKB, HB * DVB
    rb0 = lambda b, g: _row_block(b, g, n_b, n_lat)
    rb1 = lambda b, g: _row_block(b, _rev_group(g, n_g), n_b, n_lat)

    def specs(rb):
        return [pl.BlockSpec((lc, kw), lambda b, g: (rb(b, g), H_BQ // kw)),
                pl.BlockSpec((lc, kw), lambda b, g: (rb(b, g), H_BK // kw)),
                pl.BlockSpec((lc, vw), lambda b, g: (rb(b, g), H_BV // vw)),
                pl.BlockSpec((lc, LANES), lambda b, g: (rb(b, g), F_SMALL // LANES))]

    return pl.pallas_call(
        functools.partial(_gla_kernel, n_chunks=n_chunks),
        grid=(n_b, n_g),
        in_specs=specs(rb0) + specs(rb1)
        + [pl.BlockSpec((2, GATE_RANK, kw), lambda b, g: (0, 0, 0)),
           pl.BlockSpec((2, 1, kw), lambda b, g: (0, 0, 0))],
        out_specs=[pl.BlockSpec((lc, vw), lambda b, g: (rb0(b, g), 0)),
                   pl.BlockSpec((lc, vw), lambda b, g: (rb1(b, g), 0))],
        out_shape=[jax.ShapeDtypeStruct((t, vw), BF16)] * 2,
        scratch_shapes=[pltpu.VMEM((2, HB, DKB, DVB), F32)],
        compiler_params=_cparams(2),
        name="gla_bidir",
    )(proj16, proj16, proj16, proj32, proj16, proj16, proj16, proj32, w_gate2, b_gate.reshape(2, 1, kw))


def _lambda_full(lam, lam_init):
    return (jnp.exp(jnp.sum(lam[0:1] * lam[1:2], axis=1, keepdims=True))
            - jnp.exp(jnp.sum(lam[2:3] * lam[3:4], axis=1, keepdims=True)) + lam_init)


def _attend(q, keys, vals, lam_f, gain, lam_init, o_ref):
    lane = lax.broadcasted_iota(jnp.int32, (1, LANES), 1)
    n = ATTN_SUB
    q = q * (DKC ** -0.5 * math.log2(math.e))
    subs = [q[r * n:(r + 1) * n] for r in range(q.shape[0] // n)]
    q2s = [jnp.concatenate([jnp.where(lane < DKC, x, 0.0), jnp.where(lane >= DKC, x, 0.0)],
                           axis=0).astype(BF16) for x in subs]
    ss = [_dot_nt(q2, keys) for q2 in q2s]
    es = [jnp.exp2(s - jnp.max(s, axis=-1, keepdims=True)) for s in ss]
    invs = [1.0 / jnp.sum(e, axis=-1, keepdims=True) for e in es]
    o2s = [_dot(es[r].astype(BF16), vals) * invs[r] for r in range(len(es))]
    for r, o2 in enumerate(o2s):
        o = o2[:n] - lam_f * o2[n:]
        o_ref[r * n:(r + 1) * n, :] = (_rms_rows(o, gain) * (1.0 - lam_init)).astype(BF16)


def _attn_lat_kernel(lam_ref, gn_ref, q_ref, kc_ref, kl_ref, vc_ref, vl_ref, cq_ref, sq_ref, ck_ref, sk_ref,
                     o_ref, k_scr, v_scr, *, lc, lam_init):
    lane = lax.broadcasted_iota(jnp.int32, (1, LANES), 1)
    first_half = (lane & 31) < 16

    def rope(x, cos, sin):
        swapped = jnp.where(first_half, pltpu.roll(x, LANES - 16, 1), pltpu.roll(x, 16, 1))
        return x * cos + swapped * sin

    @pl.when(pl.program_id(2) == 0)
    def _():
        k_scr[0:lc, :] = kc_ref[...]
        k_scr[lc:, :] = rope(kl_ref[...].astype(F32), ck_ref[...], sk_ref[...]).astype(BF16)
        v_scr[0:lc, :] = vc_ref[...]
        v_scr[lc:, :] = vl_ref[...]

    _attend(rope(q_ref[...].astype(F32), cq_ref[...], sq_ref[...]), k_scr[...], v_scr[...],
            _lambda_full(lam_ref[...], lam_init), gn_ref[...], lam_init, o_ref)


def _attn_ctx_kernel(lam_ref, gn_ref, q_ref, k_ref, v_ref, prev_ref, o_ref, *, lam_init):
    del prev_ref
    _attend(q_ref[...].astype(F32), k_ref[...], v_ref[...], _lambda_full(lam_ref[...], lam_init),
            gn_ref[...], lam_init, o_ref)


def _attention(proj, lam, gn, rope_cos, rope_sin, n_b, lc, l, lam_init, ctx_out):
    t = proj.shape[0]
    tq = min(ATTN_TQ, l)
    per = l // tq
    cq, ck, cv = H_CQ // LANES, H_CK // LANES, H_CV // LANES
    small = [pl.BlockSpec((4, DKC), lambda b, h, *_: (0, 0)), pl.BlockSpec((1, DVC), lambda b, h, *_: (0, 0))]
    tab_q = pl.BlockSpec((tq, LANES), lambda b, h, i: (i, 0))
    tab_k = pl.BlockSpec((l, LANES), lambda b, h, i: (0, 0))
    od = pl.pallas_call(
        functools.partial(_attn_lat_kernel, lc=lc, lam_init=lam_init),
        grid=(n_b, HC, per),
        in_specs=small
        + [pl.BlockSpec((tq, LANES), lambda b, h, i: ((b + 1) * per + i, cq + h)),
           pl.BlockSpec((lc, LANES), lambda b, h, i: (b, ck + h)),
           pl.BlockSpec((l, LANES), lambda b, h, i: (b + 1, ck + h)),
           pl.BlockSpec((lc, LANES), lambda b, h, i: (b, cv + h)),
           pl.BlockSpec((l, LANES), lambda b, h, i: (b + 1, cv + h)),
           tab_q, tab_q, tab_k, tab_k],
        out_specs=pl.BlockSpec((tq, LANES), lambda b, h, i: ((b + 1) * per + i, h)),
        out_shape=jax.ShapeDtypeStruct((t, HC * DVC), BF16),
        scratch_shapes=[pltpu.VMEM((lc + l, LANES), BF16), pltpu.VMEM((lc + l, LANES), BF16)],
        compiler_params=_cparams(3),
        name="diff_attention",
    )(lam, gn.reshape(1, DVC), proj, proj, proj, proj, proj, rope_cos, rope_sin, rope_cos, rope_sin)
    if not ctx_out:
        return od
    return pl.pallas_call(
        functools.partial(_attn_ctx_kernel, lam_init=lam_init),
        grid=(n_b, HC),
        in_specs=small
        + [pl.BlockSpec((lc, LANES), lambda b, h: (b, cq + h)),
           pl.BlockSpec((lc, LANES), lambda b, h: (b, ck + h)),
           pl.BlockSpec((lc, LANES), lambda b, h: (b, cv + h)),
           pl.BlockSpec(memory_space=pl.ANY)],
        out_specs=pl.BlockSpec((lc, LANES), lambda b, h: (b, h)),
        out_shape=jax.ShapeDtypeStruct((t, HC * DVC), BF16),
        input_output_aliases={5: 0},
        compiler_params=_cparams(2),
        name="diff_attention_ctx",
    )(lam, gn.reshape(1, DVC), proj, proj, proj, od)


def _rope_tables(n_tokens):
    rows = n_tokens // GRID_W
    row = jnp.repeat(jnp.arange(rows, dtype=F32), GRID_W)
    col = jnp.tile(jnp.arange(GRID_W, dtype=F32), rows)
    n_freq = DKC // 4
    inv_freq = ROPE_THETA ** (-jnp.arange(n_freq, dtype=F32) / n_freq)
    ang_r = row[:, None] * inv_freq
    ang_c = col[:, None] * inv_freq
    cos = jnp.concatenate([jnp.cos(ang_r)] * 2 + [jnp.cos(ang_c)] * 2, axis=1)
    sin = jnp.concatenate([-jnp.sin(ang_r), jnp.sin(ang_r), -jnp.sin(ang_c), jnp.sin(ang_c)], axis=1)
    return jnp.concatenate([cos, cos], axis=1), jnp.concatenate([sin, sin], axis=1)


def _head_rms(o, gain, n_heads, width):
    parts = []
    for h in range(n_heads):
        parts.append(_rms_rows(o[:, h * width:(h + 1) * width], gain))
    return jnp.concatenate(parts, axis=1)


def _merge_kernel(*refs, ctx_tiles, n_tok):
    x_refs = refs[:n_tok]
    (oa0, oa1, z_ref, ob0, ob1, r_ref, od_ref, ga_ref, gb_ref, gd_ref, mod_ref,
     gna_ref, gnb_ref, woa_ref, wob_ref, woc_ref, wout_ref, o_ref) = refs[n_tok:]
    f32 = lambda ref: ref[...].astype(F32)
    ya = _head_rms(f32(oa0) + f32(oa1), gna_ref[...], HA, DVA) * _silu(f32(z_ref))
    yb = _head_rms(f32(ob0) + f32(ob1), gnb_ref[...], HB, DVB) * _silu(f32(r_ref))
    acc = _sigmoid(f32(ga_ref)) * _dot(ya.astype(BF16), woa_ref[...])
    acc = acc + _sigmoid(f32(gb_ref)) * _dot(yb.astype(BF16), wob_ref[...])
    acc = acc + _sigmoid(f32(gd_ref)) * _dot(od_ref[...], woc_ref[...])
    y = _dot(acc.astype(BF16), wout_ref[...])
    o_ref[...] = _token_tile(x_refs, ctx_tiles) + mod_ref[0, 2:3, :] * y


def _merge(oa0, oa1, ob0, ob1, od, proj, x_ctx, x_lat, lat_first, t, mod, gn_a, gn_b, woa, wob, woc, wout,
           rows_per_mod, row_off, tm):
    d = x_ctx.shape[1]
    off = row_off // tm
    ctx_tiles = rows_per_mod // tm
    n_tiles = (t - row_off) // tm
    per = rows_per_mod // tm
    w5 = HA * DVA
    row = lambda i: (i + off, 0)
    col = lambda c: (lambda i: (i + off, c))
    tok_specs, tok_args = _token_inputs(x_ctx, x_lat, lat_first, tm, ctx_tiles, off)
    return pl.pallas_call(
        functools.partial(_merge_kernel, ctx_tiles=ctx_tiles - off, n_tok=len(tok_args)),
        grid=(n_tiles,),
        in_specs=tok_specs
        + [pl.BlockSpec((tm, w5), row),
                  pl.BlockSpec((tm, w5), row),
                  pl.BlockSpec((tm, w5), col(H_AZ // w5)),
                  pl.BlockSpec((tm, w5), row),
                  pl.BlockSpec((tm, w5), row),
                  pl.BlockSpec((tm, w5), col(H_BR // w5)),
                  pl.BlockSpec((tm, w5), row),
                  pl.BlockSpec((tm, d), col(H_GATES // d)),
                  pl.BlockSpec((tm, d), col(H_GATES // d + 1)),
                  pl.BlockSpec((tm, d), col(H_GATES // d + 2)),
                  pl.BlockSpec((1, 6, d), lambda i: ((i + off) // per, 0, 0)),
                  pl.BlockSpec((1, DVA), lambda i: (0, 0)),
                  pl.BlockSpec((1, DVB), lambda i: (0, 0)),
                  pl.BlockSpec((w5, d), lambda i: (0, 0)),
                  pl.BlockSpec((w5, d), lambda i: (0, 0)),
                  pl.BlockSpec((w5, d), lambda i: (0, 0)),
                  pl.BlockSpec((d, d), lambda i: (0, 0))],
        out_specs=pl.BlockSpec((tm, d), lambda i: (i, 0)),
        out_shape=jax.ShapeDtypeStruct((t - row_off, d), F32),
        compiler_params=_cparams(1),
        name="merge_outproj",
    )(*tok_args, oa0, oa1, proj, ob0, ob1, proj, od, proj, proj, proj, mod,
      gn_a.reshape(1, DVA), gn_b.reshape(1, DVB), woa, wob, woc, wout)


def _ffn_kernel(x_ref, mod_ref, g_ref, fg_ref, w1_ref, w3_ref, w2_ref, o_ref, h_scr, *, final):
    f = pl.program_id(1)

    @pl.when(f == 0)
    def _():
        y = _rms_rows(x_ref[...], g_ref[...])
        h_scr[...] = (y * (1.0 + mod_ref[0, 4:5, :]) + mod_ref[0, 3:4, :]).astype(BF16)
        o_ref[...] = jnp.zeros_like(o_ref)

    h = h_scr[...]
    t = _silu(_dot(h, w1_ref[...])) * _dot(h, w3_ref[...])
    o_ref[...] += _dot(t.astype(BF16), w2_ref[...])

    @pl.when(f == pl.num_programs(1) - 1)
    def _():
        out = x_ref[...] + mod_ref[0, 5:6, :] * o_ref[...]
        if final:
            out = _rms_rows(out, fg_ref[...])
        o_ref[...] = out


def _ffn(x, mod, g, final_g, w1, w3, w2, rows_per_mod, mod_off, tm, tf, final):
    t, d = x.shape
    dff = w1.shape[1]
    per = rows_per_mod // tm
    return pl.pallas_call(
        functools.partial(_ffn_kernel, final=final),
        grid=(t // tm, dff // tf),
        in_specs=[pl.BlockSpec((tm, d), lambda i, f: (i, 0)),
                  pl.BlockSpec((1, 6, d), lambda i, f: (i // per + mod_off, 0, 0)),
                  pl.BlockSpec((1, d), lambda i, f: (0, 0)),
                  pl.BlockSpec((1, d), lambda i, f: (0, 0)),
                  pl.BlockSpec((d, tf), lambda i, f: (0, f)),
                  pl.BlockSpec((d, tf), lambda i, f: (0, f)),
                  pl.BlockSpec((tf, d), lambda i, f: (f, 0))],
        out_specs=pl.BlockSpec((tm, d), lambda i, f: (i, 0)),
        out_shape=jax.ShapeDtypeStruct((t, d), F32),
        scratch_shapes=[pltpu.VMEM((tm, d), BF16)],
        compiler_params=_cparams(2),
        name="dense_ffn",
    )(x, mod, g.reshape(1, d), final_g.reshape(1, d), w1, w3, w2)


MOE_ROWS = 256
MOE_MOVE_UNIT = 128
MOE_FFN_UNIT = 128


def _moe_kernel(x_ref, mod_ref, g_ref, rw_ref, tri_ref, fg_ref, w1_ref, w3_ref, w2_ref, o_ref,
                h_scr, gate_scr, sel_scr, rank_scr, selr_scr, rankr_scr, xg_scr, yg_scr, nb_scr, *, final):
    e = pl.program_id(1)
    f = pl.program_id(2)
    n_f = pl.num_programs(2)
    tm, d = x_ref.shape
    cb = MOE_ROWS
    lane = lax.broadcasted_iota(jnp.int32, (1, LANES), 1)

    def for_blocks(body, slot, unit):
        units = nb_scr[slot]
        per = cb // unit
        nb = units // per

        def full(j, carry):
            body(pl.multiple_of(j * cb, cb), cb)
            return carry

        lax.fori_loop(0, nb, full, 0)
        rem = units - nb * per
        r0 = nb * cb
        size = cb // 2
        while size >= unit:
            has = (rem // (size // unit)) % 2 == 1

            @pl.when(has)
            def _():
                body(pl.multiple_of(r0, unit), size)

            r0 = r0 + jnp.where(has, size, 0)
            size //= 2

    @pl.when((e == 0) & (f == 0))
    def _():
        y = _rms_rows(x_ref[...], g_ref[...])
        hmod = y * (1.0 + mod_ref[0, 4:5, :]) + mod_ref[0, 3:4, :]
        h_scr[...] = hmod.astype(BF16)
        o_ref[...] = jnp.zeros_like(o_ref)
        logits = _dot3(hmod, rw_ref[...])
        logits = jnp.where(lane < N_EXPERTS, logits, -jnp.inf)
        lanef = lane.astype(F32)
        m1 = jnp.max(logits, axis=1, keepdims=True)
        i1 = jnp.min(jnp.where(logits == m1, lanef, float(LANES)), axis=1, keepdims=True)
        hit1 = lanef == i1
        rest = jnp.where(hit1, -jnp.inf, logits)
        m2 = jnp.max(rest, axis=1, keepdims=True)
        i2 = jnp.min(jnp.where(rest == m2, lanef, float(LANES)), axis=1, keepdims=True)
        hit2 = lanef == i2
        ex = jnp.exp(m2 - m1)
        inv = 1.0 / (1.0 + ex)
        gate_scr[...] = jnp.where(hit1, inv, 0.0) + jnp.where(hit2, ex * inv, 0.0)
        sel = jnp.where(hit1 | hit2, 1.0, 0.0)
        sel_scr[...] = sel
        sel16 = sel.astype(BF16)
        rank = _dot(tri_ref[...], sel16)
        rank_scr[...] = rank
        pick = jnp.where(lax.broadcasted_iota(jnp.int32, (8, LANES), 0)
                         == lax.broadcasted_iota(jnp.int32, (8, LANES), 1), 1.0, 0.0).astype(BF16)
        selr_scr[...] = _dot_nt(pick, sel16)
        hi = jnp.floor(rank * (1.0 / 32.0))
        lo = rank - 32.0 * hi
        rankr_scr[...] = 32.0 * _dot_nt(pick, hi.astype(BF16)) + _dot_nt(pick, lo.astype(BF16))

    @pl.when(f == 0)
    def _():
        sel_row = selr_scr[pl.ds(e, 1), :]
        rank_row = rankr_scr[pl.ds(e, 1), :]
        n_e = jnp.sum(sel_row).astype(jnp.int32)
        nb_scr[0] = (n_e + (MOE_MOVE_UNIT - 1)) // MOE_MOVE_UNIT
        nb_scr[1] = (n_e + (MOE_FFN_UNIT - 1)) // MOE_FFN_UNIT

        def gather(r0, bs):
            slot = lax.broadcasted_iota(jnp.int32, (bs, tm), 0).astype(F32) + r0.astype(F32)
            onehot = jnp.where((rank_row == slot) & (sel_row > 0.0), 1.0, 0.0)
            xg_scr[pl.ds(r0, bs), :] = _dot(onehot.astype(BF16), h_scr[...]).astype(BF16)
            yg_scr[pl.ds(r0, bs), :] = jnp.zeros((bs, d), F32)

        for_blocks(gather, 0, MOE_MOVE_UNIT)

    def expert(r0, bs):
        xb = xg_scr[pl.ds(r0, bs), :]
        t = _silu(_dot(xb, w1_ref[0])) * _dot(xb, w3_ref[0])
        yg_scr[pl.ds(r0, bs), :] += _dot(t.astype(BF16), w2_ref[0])

    for_blocks(expert, 1, MOE_FFN_UNIT)

    @pl.when(f == n_f - 1)
    def _():
        pick_e = lane == e
        rank_col = jnp.sum(jnp.where(pick_e, rank_scr[...], 0.0), axis=1, keepdims=True)
        sel_col = jnp.sum(jnp.where(pick_e, sel_scr[...], 0.0), axis=1, keepdims=True)
        gate_col = jnp.sum(jnp.where(pick_e, gate_scr[...], 0.0), axis=1, keepdims=True)

        def scatter(r0, bs):
            slot = lax.broadcasted_iota(jnp.int32, (tm, bs), 1).astype(F32) + r0.astype(F32)
            onehot = jnp.where((rank_col == slot) & (sel_col > 0.0), 1.0, 0.0)
            o_ref[...] += gate_col * _dot(onehot.astype(BF16), yg_scr[pl.ds(r0, bs), :].astype(BF16))

        for_blocks(scatter, 0, MOE_MOVE_UNIT)

    @pl.when((e == pl.num_programs(1) - 1) & (f == n_f - 1))
    def _():
        out = x_ref[...] + mod_ref[0, 5:6, :] * o_ref[...]
        if final:
            out = _rms_rows(out, fg_ref[...])
        o_ref[...] = out


def _moe(x, mod, g, router_w, final_g, w1, w3, w2, rows_per_mod, mod_off, tm, tf, final):
    t, d = x.shape
    n_e, _, dff = w1.shape
    per = rows_per_mod // tm
    return pl.pallas_call(
        functools.partial(_moe_kernel, final=final),
        grid=(t // tm, n_e, dff // tf),
        in_specs=[pl.BlockSpec((tm, d), lambda i, e, f: (i, 0)),
                  pl.BlockSpec((1, 6, d), lambda i, e, f: (i // per + mod_off, 0, 0)),
                  pl.BlockSpec((1, d), lambda i, e, f: (0, 0)),
                  pl.BlockSpec((d, LANES), lambda i, e, f: (0, 0)),
                  pl.BlockSpec((tm, tm), lambda i, e, f: (0, 0)),
                  pl.BlockSpec((1, d), lambda i, e, f: (0, 0)),
                  pl.BlockSpec((1, d, tf), lambda i, e, f: (e, 0, f)),
                  pl.BlockSpec((1, d, tf), lambda i, e, f: (e, 0, f)),
                  pl.BlockSpec((1, tf, d), lambda i, e, f: (e, f, 0))],
        out_specs=pl.BlockSpec((tm, d), lambda i, e, f: (i, 0)),
        out_shape=jax.ShapeDtypeStruct((t, d), F32),
        scratch_shapes=[pltpu.VMEM((tm, d), BF16),
                        pltpu.VMEM((tm, LANES), F32),
                        pltpu.VMEM((tm, LANES), F32),
                        pltpu.VMEM((tm, LANES), F32),
                        pltpu.VMEM((8, tm), F32),
                        pltpu.VMEM((8, tm), F32),
                        pltpu.VMEM((tm, d), BF16),
                        pltpu.VMEM((tm, d), F32),
                        pltpu.SMEM((2,), jnp.int32)],
        compiler_params=_cparams(3),
        name="routed_ffn_sparse",
    )(x, mod, g.reshape(1, d), router_w, jnp.tril(jnp.ones((tm, tm), BF16), -1),
      final_g.reshape(1, d), w1, w3, w2)


def _w_in_pieces(end):
    a_qkv = 2 * HA * DKA + HA * DVA
    a_z = a_qkv + HA * DVA
    b_q = a_z + 4 * HA
    b_v = b_q + 2 * HB * DKB
    b_glr = b_v + 2 * HB * DVB
    c_q = b_glr + 2 * GATE_RANK
    gates = c_q + 4 * HC * DKC + HC * DVC
    return [(0, a_qkv), (a_z, b_q), (b_glr, c_q), None,
            (gates, end), (a_qkv, a_z), (b_v, b_glr), (c_q, gates), (b_q, b_v)]


def _reorder_kernel(w_ref, o_ref):
    at = 0
    for piece in _w_in_pieces(w_ref.shape[1]):
        if piece is None:
            width = N_F32 - at
            o_ref[0, at:at + width, :] = jnp.zeros((width, o_ref.shape[2]), BF16)
        else:
            width = piece[1] - piece[0]
            o_ref[0, at:at + width, :] = w_ref[0, piece[0]:piece[1], :].astype(BF16)
        at += width
    assert at == N_F32 + N_B16


def _reorder_w_in(w_in):
    w_t = jnp.swapaxes(w_in, 1, 2)
    depth, cols, d = w_t.shape
    kb = LANES
    return pl.pallas_call(
        _reorder_kernel,
        grid=(depth, d // kb),
        in_specs=[pl.BlockSpec((1, cols, kb), lambda l, i: (l, 0, i))],
        out_specs=pl.BlockSpec((1, N_F32 + N_B16, kb), lambda l, i: (l, 0, i)),
        out_shape=jax.ShapeDtypeStruct((depth, N_F32 + N_B16, d), BF16),
        compiler_params=_cparams(2),
        name="reorder_w_in",
    )(w_t)


def kernel(x, c, ctx, c_ctx, w_mod, b_mod, norm1_g, norm2_g, w_in, conv_a, a_log, dt_bias, gn_a, w_gate2, b_gate, gn_b, lam_c, gn_c, w_o_a, w_o_b, w_o_c, w_out, ffn_w1, ffn_w3, ffn_w2, router_w, moe_w1, moe_w3, moe_w2, final_g):
    n_b, l, d = x.shape
    lc = ctx.shape[1]
    depth = w_mod.shape[0]
    assert n_b * lc == l and lc % CHUNK == 0 and l & (l - 1) == 0 and lc & (lc - 1) == 0
    t_ctx = n_b * lc
    tm = min(1024, l)
    tm_small = min(512, l)

    n_mod = 16
    cvec = jnp.concatenate([c_ctx[None, :], c, jnp.zeros((n_mod - 1 - n_b, d), F32)], axis=0)
    mod_all = _modulation(cvec, w_mod, b_mod).reshape(depth, n_mod, 6, d)

    rope_cos, rope_sin = _rope_tables(l)
    w_in16 = _reorder_w_in(w_in)
    t_all = t_ctx + n_b * l
    tokens = (ctx.reshape(t_ctx, d), x.reshape(n_b * l, d), 0)

    for layer in range(depth):
        ctx_out = layer < depth - 1
        lam_init = 0.8 - 0.6 * math.exp(-0.3 * layer)
        mod = mod_all[layer]
        proj32, proj16 = _in_projection(*tokens, t_all, mod, norm1_g[layer], w_in16, layer, l, tm)

        qkvc = _short_conv(proj32, conv_a[layer], lc, l)
        oa0, oa1 = _deltanet(qkvc, proj32, a_log[layer], dt_bias[layer], n_b, lc, l)
        ob0, ob1 = _gla(proj32, proj16, w_gate2[layer], b_gate[layer], n_b, lc, l)
        od = _attention(proj16, lam_c[layer], gn_c[layer], rope_cos, rope_sin, n_b, lc, l, lam_init, ctx_out)

        row_off = 0 if ctx_out else t_ctx
        xs_new = _merge(oa0, oa1, ob0, ob1, od, proj16, *tokens, t_all, mod, gn_a[layer], gn_b[layer],
                        w_o_a[layer].astype(BF16), w_o_b[layer].astype(BF16), w_o_c[layer].astype(BF16),
                        w_out[layer].astype(BF16), l, row_off, tm_small)

        i = layer // 2
        final = layer == depth - 1
        mod_off = 0 if ctx_out else 1
        if layer % 2 == 0:
            xs = _ffn(xs_new, mod, norm2_g[layer], final_g, ffn_w1[i].astype(BF16),
                      ffn_w3[i].astype(BF16), ffn_w2[i].astype(BF16), l, mod_off, tm_small,
                      ffn_w1.shape[2] // 2, final)
        else:
            rw = jnp.pad(router_w[i], ((0, 0), (0, LANES - N_EXPERTS)))
            xs = _moe(xs_new, mod, norm2_g[layer], rw, final_g, moe_w1[i].astype(BF16),
                      moe_w3[i].astype(BF16), moe_w2[i].astype(BF16), l, mod_off, tm,
                      moe_w1.shape[3] // 2, final)
        tokens = (xs, xs, t_ctx)

    return xs.reshape(n_b, l, d)
```

```python
import functools
import math

import jax
import jax.numpy as jnp
from jax import lax
from jax.experimental import pallas as pl
from jax.experimental.pallas import tpu as pltpu

F32 = jnp.float32
BF16 = jnp.bfloat16
HIGHEST = lax.Precision.HIGHEST

EPS = 1e-6
HA, DKA, DVA = 4, 128, 128
HB, DKB, DVB = 4, 64, 128
GATE_RANK, GATE_NORM = 16, 16.0
HC, DKC, DVC = 4, 64, 128
GRID_W, ROPE_THETA = 64, 10000.0
N_EXPERTS, TOP_K = 8, 2
CHUNK = 64
LANES = 128
ATTN_SUB = 64
ATTN_TQ = 512
CONV_HEADS = 2

F_QKV = 0
F_SMALL = 1536
N_F32 = 1664
H_GATES = 0
H_AZ = 3072
H_BV = 3584
H_BR = 4096
H_CQ = 4608
H_CK = 5120
H_CV = 5632
H_BQ = 6144
H_BK = 6400
N_B16 = 6656
PROJ_TN = 1664

VMEM_LIMIT = 56 * 1024 * 1024


def _cparams(n_axes):
    return pltpu.CompilerParams(dimension_semantics=("arbitrary",) * n_axes,
                                vmem_limit_bytes=VMEM_LIMIT)


def _sigmoid(x):
    return 1.0 / (1.0 + jnp.exp2(x * -math.log2(math.e)))


def _silu(x):
    return x * _sigmoid(x)


def _softplus(x):
    return jnp.maximum(x, 0.0) + jnp.log(1.0 + jnp.exp(-jnp.abs(x)))


def _dot(a, b):
    return jnp.dot(a, b, preferred_element_type=F32)


def _dot_nt(a, b):
    return lax.dot_general(a, b, (((1,), (1,)), ((), ())), preferred_element_type=F32)


def _dot_tn(a, b):
    return lax.dot_general(a, b, (((0,), (0,)), ((), ())), preferred_element_type=F32)


def _split(a):
    hi = a.astype(BF16)
    lo = (a - hi.astype(F32)).astype(BF16)
    return hi, lo


def _dot3(a, b):
    ah, al = _split(a)
    bh, bl = _split(b)
    return _dot(ah, bh) + (_dot(ah, bl) + _dot(al, bh))


def _rms_rows(x, gain):
    ms = jnp.mean(x * x, axis=-1, keepdims=True)
    return x * lax.rsqrt(ms + EPS) * gain


def _mod_kernel(c_ref, w_ref, b_ref, o_ref):
    s = _silu(c_ref[...])
    o_ref[0] = jnp.dot(s, w_ref[0], precision=HIGHEST, preferred_element_type=F32) + b_ref[0]


def _modulation(cvec, w_mod, b_mod):
    depth, d, d6 = w_mod.shape
    n = cvec.shape[0]
    return pl.pallas_call(
        _mod_kernel,
        grid=(depth, d6 // d),
        in_specs=[pl.BlockSpec((n, d), lambda l, j: (0, 0)),
                  pl.BlockSpec((1, d, d), lambda l, j: (l, 0, j)),
                  pl.BlockSpec((1, 1, d), lambda l, j: (l, 0, j))],
        out_specs=pl.BlockSpec((1, n, d), lambda l, j: (l, 0, j)),
        out_shape=jax.ShapeDtypeStruct((depth, n, d6), F32),
        compiler_params=_cparams(2),
        name="adaln_mod",
    )(cvec, w_mod, b_mod.reshape(depth, 1, d6))


def _token_specs(tm, d, ctx_tiles, lat_first, off=0):
    return [pl.BlockSpec((tm, d), lambda i, *_: (jnp.minimum(i + off, ctx_tiles - 1), 0)),
            pl.BlockSpec((tm, d), lambda i, *_: (jnp.maximum(i + off - ctx_tiles, 0) + lat_first, 0))]


def _token_inputs(x_ctx, x_lat, lat_first, tm, ctx_tiles, off=0):
    d = x_ctx.shape[1]
    if x_lat is x_ctx:
        return [pl.BlockSpec((tm, d), lambda i, *_: (i + off, 0))], [x_ctx]
    return _token_specs(tm, d, ctx_tiles, lat_first // tm, off), [x_ctx, x_lat]


def _token_tile(x_refs, ctx_tiles):
    if len(x_refs) == 1:
        return x_refs[0][...]
    return jnp.where(pl.program_id(0) < ctx_tiles, x_refs[0][...], x_refs[1][...])


def _inproj_kernel(*refs, ctx_tiles, n_tok):
    x_refs, (mod_ref, g_ref, w_ref, o32_ref, o16_ref, h_scr) = refs[:n_tok], refs[n_tok:]
    j = pl.program_id(1)

    @pl.when(j == 0)
    def _():
        y = _rms_rows(_token_tile(x_refs, ctx_tiles), g_ref[...])
        h_scr[...] = (y * (1.0 + mod_ref[0, 1:2, :]) + mod_ref[0, 0:1, :]).astype(BF16)
        o32_ref[...] = _dot_nt(h_scr[...], w_ref[0])

    @pl.when(j > 0)
    def _():
        o16_ref[...] = _dot_nt(h_scr[...], w_ref[0]).astype(BF16)


def _in_projection(x_ctx, x_lat, lat_first, t, mod, g, w, layer, rows_per_mod, tm):
    d = x_ctx.shape[1]
    tn = PROJ_TN
    assert w.shape[1] == N_F32 + N_B16 and N_F32 == tn and N_B16 % tn == 0
    per = rows_per_mod // tm
    ctx_tiles = rows_per_mod // tm
    tok_specs, tok_args = _token_inputs(x_ctx, x_lat, lat_first, tm, ctx_tiles)
    return pl.pallas_call(
        functools.partial(_inproj_kernel, ctx_tiles=ctx_tiles, n_tok=len(tok_args)),
        grid=(t // tm, 1 + N_B16 // tn),
        in_specs=tok_specs
        + [pl.BlockSpec((1, 6, d), lambda i, j: (i // per, 0, 0)),
           pl.BlockSpec((1, d), lambda i, j: (0, 0)),
           pl.BlockSpec((1, tn, d), lambda i, j: (layer, j, 0))],
        out_specs=[pl.BlockSpec((tm, tn), lambda i, j: (i, 0)),
                   pl.BlockSpec((tm, tn), lambda i, j: (i, jnp.maximum(j - 1, 0)))],
        out_shape=[jax.ShapeDtypeStruct((t, N_F32), F32), jax.ShapeDtypeStruct((t, N_B16), BF16)],
        scratch_shapes=[pltpu.VMEM((tm, d), BF16)],
        compiler_params=_cparams(2),
        name="norm_mod_inproj",
    )(*tok_args, mod, g.reshape(1, d), w)


def _conv_kernel(x_ref, w_ref, o_ref, *, lc, l):
    p = pl.program_id(0)
    j = pl.program_id(1)
    x = x_ref[...]
    n = x.shape[0]
    row = lax.broadcasted_iota(jnp.int32, x.shape, 0)
    seq = jnp.where(p == 0, lc, l)
    pos = row & (seq - 1)
    prev = jnp.where(pos == 0, 0.0, pltpu.roll(x, 1, 0))
    nxt = jnp.where(pos == seq - 1, 0.0, pltpu.roll(x, n - 1, 0))
    w = w_ref[...]
    y = _silu(prev * w[0:1] + x * w[1:2] + nxt * w[2:3])
    heads_per_block = x.shape[1] // DKA
    for hh in range(heads_per_block):
        cols = slice(hh * DKA, (hh + 1) * DKA)
        yh = y[:, cols]
        inv = lax.rsqrt(jnp.sum(yh * yh, axis=-1, keepdims=True) + EPS)
        head = j * heads_per_block + hh
        fac = jnp.where(head < HA, inv * DKA ** -0.5, jnp.where(head < 2 * HA, inv, 1.0))
        o_ref[:, cols] = yh * fac


def _short_conv(proj, conv_w, lc, l):
    t = proj.shape[0]
    ncol = conv_w.shape[1]
    wide = CONV_HEADS * DKA
    return pl.pallas_call(
        functools.partial(_conv_kernel, lc=lc, l=l),
        grid=(t // l, ncol // wide),
        in_specs=[pl.BlockSpec((l, wide), lambda p, j: (p, j)),
                  pl.BlockSpec((conv_w.shape[0], wide), lambda p, j: (0, j))],
        out_specs=pl.BlockSpec((l, wide), lambda p, j: (p, j)),
        out_shape=jax.ShapeDtypeStruct((t, ncol), F32),
        compiler_params=_cparams(2),
        name="conv_silu_l2",
    )(proj, conv_w)


def _row_block(b, g, n_b, n_lat):
    return jnp.where(g == 0, b, n_b + b * n_lat + g - 1)


def _rev_group(g, n_g):
    return jnp.where(g == 0, 0, n_g - g)


def _tri_inverse_many(lms, ii, jj, uppers):
    c = lms[0].shape[0]
    eye = (ii == jj).astype(F32)
    n = len(lms)
    ts = None
    s, ls = 1, 0
    while s < c:
        same = (ii >> (ls + 1)) == (jj >> (ls + 1))
        io = (ii >> ls) & 1
        jo = (jj >> ls) & 1
        lower = same & (io == 1) & (jo == 0)
        upper = same & (io == 0) & (jo == 1)
        cms = [jnp.where(upper if uppers[i] else lower, lms[i], 0.0) for i in range(n)]
        if ts is None:
            ts = [eye - cm for cm in cms]
        else:
            t16 = [t.astype(BF16) for t in ts]
            ps = [_dot(t16[i], cms[i].astype(BF16)).astype(BF16) for i in range(n)]
            ts = [ts[i] - _dot(ps[i], t16[i]) for i in range(n)]
        s, ls = s * 2, ls + 1
    a_parts = [_split(eye + lm) for lm in lms]
    t_parts = [_split(t) for t in ts]
    res = [eye - (_dot(a_parts[i][0], t_parts[i][0])
                  + (_dot(a_parts[i][0], t_parts[i][1]) + _dot(a_parts[i][1], t_parts[i][0])))
           for i in range(n)]
    return [ts[i] + _dot(t_parts[i][0], res[i].astype(BF16)) for i in range(n)]


def _dn_kernel(alog_ref, dtb_ref, q0, k0, v0, sm0, q1, k1, v1, sm1, o0_ref, o1_ref, s_scr, *, n_chunks):
    @pl.when(pl.program_id(1) == 0)
    def _():
        s_scr[...] = jnp.zeros_like(s_scr)

    c_ = CHUNK
    refs = ((q0, k0, v0, o0_ref), (q1, k1, v1, o1_ref))
    sms = (sm0[...], sm1[...])
    sm_ts = tuple(s.T for s in sms)
    ii = lax.broadcasted_iota(jnp.int32, (c_, c_), 0)
    jj = lax.broadcasted_iota(jnp.int32, (c_, c_), 1)
    rows = [slice(c * c_, (c + 1) * c_) for c in range(n_chunks)]
    lanes = [slice(j * LANES, (j + 1) * LANES) for j in range(HA)]
    items = [(j, c, d) for j in range(HA) for c in range(n_chunks) for d in range(2)]
    qs = [refs[d][0][rows[c], lanes[j]] for j, c, d in items]
    ks = [refs[d][1][rows[c], lanes[j]] for j, c, d in items]
    vs = [refs[d][2][rows[c], lanes[j]] for j, c, d in items]
    k16 = [k.astype(BF16) for k in ks]
    kks = [_dot_nt(k, k) for k in k16]
    qks = [_dot_nt(qs[i].astype(BF16), k16[i]) for i in range(len(items))]

    betas, gcs, totals, gams, lms = [], [], [], [], []
    for i, (j, c, d) in enumerate(items):
        r = rows[c]
        incl = (jj <= ii) if d == 0 else (jj >= ii)
        strict = (jj < ii) if d == 0 else (jj > ii)
        incl_t = (ii <= jj) if d == 0 else (ii >= jj)
        a_scale = -jnp.exp(jnp.full((1, 1), alog_ref[d * HA + j], F32))
        dtb = dtb_ref[d * HA + j]
        col = d * HA + j
        beta = _sigmoid(sms[d][r, col:col + 1])
        g_col = a_scale * _softplus(sms[d][r, 2 * HA + col:2 * HA + col + 1] + dtb)
        g_row = a_scale * _softplus(sm_ts[d][2 * HA + col:2 * HA + col + 1, r] + dtb)
        gc_col = jnp.sum(jnp.where(incl, g_row, 0.0), axis=1, keepdims=True)
        gc_row = jnp.sum(jnp.where(incl_t, g_col, 0.0), axis=0, keepdims=True)
        gam = jnp.where(incl, jnp.exp(gc_col - gc_row), 0.0)
        betas.append(beta)
        gcs.append(gc_col)
        totals.append(jnp.sum(g_col, axis=0, keepdims=True))
        gams.append(gam)
        lms.append(jnp.where(strict, beta * kks[i] * gam, 0.0))

    tinvs = _tri_inverse_many(lms, ii, jj, [d == 1 for _, _, d in items])
    egs = [jnp.exp(gc) for gc in gcs]
    uws = []
    for i in range(len(items)):
        rhs = jnp.concatenate([vs[i] * betas[i], ks[i] * (betas[i] * egs[i])], axis=1).astype(BF16)
        uws.append(_dot(tinvs[i].astype(BF16), rhs))
    us = [uw[:, :DVA] for uw in uws]
    wq16 = [jnp.concatenate([uws[i][:, DVA:], qs[i] * egs[i]], axis=0).astype(BF16) for i in range(len(items))]
    kg16 = [(ks[i] * jnp.exp(totals[i] - gcs[i])).astype(BF16) for i in range(len(items))]
    qkg16 = [(qks[i] * gams[i]).astype(BF16) for i in range(len(items))]
    carry = [jnp.exp(t) for t in totals]

    index = {item: i for i, item in enumerate(items)}
    chains = [(j, d) for j in range(HA) for d in range(2)]
    states = [s_scr[d * HA + j] for j, d in chains]
    for step in range(n_chunks):
        at = [index[(j, step if d == 0 else n_chunks - 1 - step, d)] for j, d in chains]
        s16 = [s.astype(BF16) for s in states]
        prods = [_dot(wq16[i], s16[n]) for n, i in enumerate(at)]
        vn16 = [(us[i] - prods[n][:c_]).astype(BF16) for n, i in enumerate(at)]
        for n, i in enumerate(at):
            j, c, d = items[i]
            refs[d][3][rows[c], lanes[j]] = (prods[n][c_:] + _dot(qkg16[i], vn16[n])).astype(BF16)
            states[n] = carry[i] * states[n] + _dot_tn(kg16[i], vn16[n])
    for n, (j, d) in enumerate(chains):
        s_scr[d * HA + j] = states[n]


def _deltanet(qkvc, proj32, a_log, dt_bias, n_b, lc, l):
    t = qkvc.shape[0]
    n_lat = l // lc
    n_g = n_lat + 1
    wide = HA * DVA
    rb0 = lambda b, g: _row_block(b, g, n_b, n_lat)
    rb1 = lambda b, g: _row_block(b, _rev_group(g, n_g), n_b, n_lat)
    smem = pl.BlockSpec(memory_space=pltpu.SMEM)

    def specs(rb):
        return [pl.BlockSpec((lc, wide), lambda b, g: (rb(b, g), 0)),
                pl.BlockSpec((lc, wide), lambda b, g: (rb(b, g), 1)),
                pl.BlockSpec((lc, wide), lambda b, g: (rb(b, g), 2)),
                pl.BlockSpec((lc, LANES), lambda b, g: (rb(b, g), F_SMALL // LANES))]

    return pl.pallas_call(
        functools.partial(_dn_kernel, n_chunks=lc // CHUNK),
        grid=(n_b, n_g),
        in_specs=[smem, smem] + specs(rb0) + specs(rb1),
        out_specs=[pl.BlockSpec((lc, wide), lambda b, g: (rb0(b, g), 0)),
                   pl.BlockSpec((lc, wide), lambda b, g: (rb1(b, g), 0))],
        out_shape=[jax.ShapeDtypeStruct((t, wide), BF16)] * 2,
        scratch_shapes=[pltpu.VMEM((2 * HA, DKA, DVA), F32)],
        compiler_params=_cparams(2),
        name="deltanet",
    )(a_log.reshape(-1), dt_bias.reshape(-1), qkvc, qkvc, qkvc, proj32, qkvc, qkvc, qkvc, proj32)


def _gla_kernel(q0, k0, v0, sm0, q1, k1, v1, sm1, wg_ref, bg_ref, o0_ref, o1_ref, s_scr, *, n_chunks):
    @pl.when(pl.program_id(1) == 0)
    def _():
        s_scr[...] = jnp.zeros_like(s_scr)

    c_ = CHUNK
    sh = c_.bit_length() - 1
    g_ = n_chunks * c_
    kw = HB * DKB
    vw = HB * DVB
    ii = lax.broadcasted_iota(jnp.int32, (g_, g_), 0)
    jj = lax.broadcasted_iota(jnp.int32, (g_, g_), 1)
    same_chunk = (ii >> sh) == (jj >> sh)
    klane = lax.broadcasted_iota(jnp.int32, (1, kw), 1)
    ind16 = jnp.where((lax.broadcasted_iota(jnp.int32, (g_, n_chunks * LANES), 0) >> sh)
                      == (lax.broadcasted_iota(jnp.int32, (g_, n_chunks * LANES), 1) >> (LANES.bit_length() - 1)),
                      1.0, 0.0).astype(BF16)
    rows = [slice(c * c_, (c + 1) * c_) for c in range(n_chunks)]
    refs = ((q0, k0, v0, sm0, o0_ref), (q1, k1, v1, sm1, o1_ref))

    def spread(x, picks):
        return jnp.concatenate([jnp.broadcast_to(x[p:p + 1, :], (c_, kw)) for p in picks], axis=0)

    dirs = (0, 1)
    incls = [same_chunk & ((jj <= ii) if d == 0 else (jj >= ii)) for d in dirs]
    tri16 = [jnp.where(m, 1.0, 0.0).astype(BF16) for m in incls]
    pres = [jnp.dot(refs[d][3][:, 4 * HA + d * GATE_RANK:4 * HA + (d + 1) * GATE_RANK], wg_ref[d],
                    precision=HIGHEST, preferred_element_type=F32) + bg_ref[d] for d in dirs]
    glogs = [(jnp.minimum(p, 0.0) - jnp.log(1.0 + jnp.exp(-jnp.abs(p)))) * (1.0 / GATE_NORM) for p in pres]
    parts = [_split(g) for g in glogs]
    gcs = [_dot(tri16[d], parts[d][0]) + _dot(tri16[d], parts[d][1]) for d in dirs]
    tot_cols = [_dot_tn(parts[d][0], ind16) + _dot_tn(parts[d][1], ind16) for d in dirs]
    mids = [spread(gcs[d], [c * c_ + c_ // 2 - 1 for c in range(n_chunks)]) for d in dirs]
    tots = [spread(gcs[d], [c * c_ + (c_ - 1 if d == 0 else 0) for c in range(n_chunks)]) for d in dirs]
    qs = [refs[d][0][...].astype(F32) * DKB ** -0.5 for d in dirs]
    ks = [refs[d][1][...].astype(F32) for d in dirs]
    v16s = [refs[d][2][...] for d in dirs]
    qts = [qs[d] * jnp.exp(gcs[d] - mids[d]) for d in dirs]
    kt16s = [(ks[d] * jnp.exp(mids[d] - gcs[d])).astype(BF16) for d in dirs]
    qg16s = [(qs[d] * jnp.exp(gcs[d])).astype(BF16) for d in dirs]
    kg16s = [(ks[d] * jnp.exp(tots[d] - gcs[d])).astype(BF16) for d in dirs]
    a16s = {}
    for h in range(HB):
        in_head = (klane >= h * DKB) & (klane < (h + 1) * DKB)
        for d in dirs:
            a = _dot_nt(jnp.where(in_head, qts[d], 0.0).astype(BF16), kt16s[d])
            a16s[(d, h)] = jnp.where(incls[d], a, 0.0).astype(BF16)
    o_intra = [[_dot(a16s[(d, h)], v16s[d][:, h * DVB:(h + 1) * DVB]) for h in range(HB)] for d in dirs]
    xs = [[_dot_tn(kg16s[d][r, :], v16s[d][r, :]) for r in rows] for d in dirs]
    decays = [[jnp.exp(tot_cols[d][:, c * LANES:(c + 1) * LANES]) for c in range(n_chunks)] for d in dirs]

    states = [[s_scr[d, h] for h in range(HB)] for d in dirs]
    zero16 = jnp.zeros((DKB, DVB), BF16)
    for step in range(n_chunks):
        for d in range(2):
            c = step if d == 0 else n_chunks - 1 - step
            r = rows[c]
            s16 = [s.astype(BF16) for s in states[d]]
            s_all = jnp.concatenate(
                [jnp.concatenate([s16[h] if hv == h else zero16 for hv in range(HB)], axis=1) for h in range(HB)],
                axis=0)
            o_inter = _dot(qg16s[d][r, :], s_all)
            for h in range(HB):
                vs = slice(h * DVB, (h + 1) * DVB)
                kr = slice(h * DKB, (h + 1) * DKB)
                refs[d][4][r, vs] = (o_inter[:, vs] + o_intra[d][h][r, :]).astype(BF16)
                states[d][h] = decays[d][c][kr, :] * states[d][h] + xs[d][c][kr, vs]
    for d in dirs:
        for h in range(HB):
            s_scr[d, h] = states[d][h]


def _gla(proj32, proj16, w_gate2, b_gate, n_b, lc, l):
    t = proj16.shape[0]
    n_lat = l // lc
    n_g = n_lat + 1
    n_chunks = lc // CHUNK
    kw, vw = HB * DKB, HB * DVB
    rb0 = lambda b, g: _row_block(b, g, n_b, n_lat)
    rb1 = lambda b, g: _row_block(b, _rev_group(g, n_g), n_b, n_lat)

    def specs(rb):
        return [pl.BlockSpec((lc, kw), lambda b, g: (rb(b, g), H_BQ // kw)),
                pl.BlockSpec((lc, kw), lambda b, g: (rb(b, g), H_BK // kw)),
                pl.BlockSpec((lc, vw), lambda b, g: (rb(b, g), H_BV // vw)),
                pl.BlockSpec((lc, LANES), lambda b, g: (rb(b, g), F_SMALL // LANES))]

    return pl.pallas_call(
        functools.partial(_gla_kernel, n_chunks=n_chunks),
        grid=(n_b, n_g),
        in_specs=specs(rb0) + specs(rb1)
        + [pl.BlockSpec((2, GATE_RANK, kw), lambda b, g: (0, 0, 0)),
           pl.BlockSpec((2, 1, kw), lambda b, g: (0, 0, 0))],
        out_specs=[pl.BlockSpec((lc, vw), lambda b, g: (rb0(b, g), 0)),
                   pl.BlockSpec((lc, vw), lambda b, g: (rb1(b, g), 0))],
        out_shape=[jax.ShapeDtypeStruct((t, vw), BF16)] * 2,
        scratch_shapes=[pltpu.VMEM((2, HB, DKB, DVB), F32)],
        compiler_params=_cparams(2),
        name="gla_bidir",
    )(proj16, proj16, proj16, proj32, proj16, proj16, proj16, proj32, w_gate2, b_gate.reshape(2, 1, kw))


def _lambda_full(lam, lam_init):
    return (jnp.exp(jnp.sum(lam[0:1] * lam[1:2], axis=1, keepdims=True))
            - jnp.exp(jnp.sum(lam[2:3] * lam[3:4], axis=1, keepdims=True)) + lam_init)


def _attend(q, keys, vals, lam_f, gain, lam_init, o_ref):
    lane = lax.broadcasted_iota(jnp.int32, (1, LANES), 1)
    n = ATTN_SUB
    q = q * (DKC ** -0.5 * math.log2(math.e))
    subs = [q[r * n:(r + 1) * n] for r in range(q.shape[0] // n)]
    q2s = [jnp.concatenate([jnp.where(lane < DKC, x, 0.0), jnp.where(lane >= DKC, x, 0.0)],
                           axis=0).astype(BF16) for x in subs]
    ss = [_dot_nt(q2, keys) for q2 in q2s]
    es = [jnp.exp2(s - jnp.max(s, axis=-1, keepdims=True)) for s in ss]
    invs = [1.0 / jnp.sum(e, axis=-1, keepdims=True) for e in es]
    o2s = [_dot(es[r].astype(BF16), vals) * invs[r] for r in range(len(es))]
    for r, o2 in enumerate(o2s):
        o = o2[:n] - lam_f * o2[n:]
        o_ref[r * n:(r + 1) * n, :] = (_rms_rows(o, gain) * (1.0 - lam_init)).astype(BF16)


def _attn_lat_kernel(lam_ref, gn_ref, q_ref, kc_ref, kl_ref, vc_ref, vl_ref, cq_ref, sq_ref, ck_ref, sk_ref,
                     o_ref, k_scr, v_scr, *, lc, lam_init):
    lane = lax.broadcasted_iota(jnp.int32, (1, LANES), 1)
    first_half = (lane & 31) < 16

    def rope(x, cos, sin):
        swapped = jnp.where(first_half, pltpu.roll(x, LANES - 16, 1), pltpu.roll(x, 16, 1))
        return x * cos + swapped * sin

    @pl.when(pl.program_id(2) == 0)
    def _():
        k_scr[0:lc, :] = kc_ref[...]
        k_scr[lc:, :] = rope(kl_ref[...].astype(F32), ck_ref[...], sk_ref[...]).astype(BF16)
        v_scr[0:lc, :] = vc_ref[...]
        v_scr[lc:, :] = vl_ref[...]

    _attend(rope(q_ref[...].astype(F32), cq_ref[...], sq_ref[...]), k_scr[...], v_scr[...],
            _lambda_full(lam_ref[...], lam_init), gn_ref[...], lam_init, o_ref)


def _attn_ctx_kernel(lam_ref, gn_ref, q_ref, k_ref, v_ref, o_ref, *, lam_init):
    _attend(q_ref[...].astype(F32), k_ref[...], v_ref[...], _lambda_full(lam_ref[...], lam_init),
            gn_ref[...], lam_init, o_ref)


def _attention(proj, lam, gn, rope_cos, rope_sin, n_b, lc, l, lam_init, ctx_out):
    tq = min(ATTN_TQ, l)
    per = l // tq
    cq, ck, cv = H_CQ // LANES, H_CK // LANES, H_CV // LANES
    small = [pl.BlockSpec((4, DKC), lambda b, h, *_: (0, 0)), pl.BlockSpec((1, DVC), lambda b, h, *_: (0, 0))]
    tab_q = pl.BlockSpec((tq, LANES), lambda b, h, i: (i, 0))
    tab_k = pl.BlockSpec((l, LANES), lambda b, h, i: (0, 0))
    od = pl.pallas_call(
        functools.partial(_attn_lat_kernel, lc=lc, lam_init=lam_init),
        grid=(n_b, HC, per),
        in_specs=small
        + [pl.BlockSpec((tq, LANES), lambda b, h, i: ((b + 1) * per + i, cq + h)),
           pl.BlockSpec((lc, LANES), lambda b, h, i: (b, ck + h)),
           pl.BlockSpec((l, LANES), lambda b, h, i: (b + 1, ck + h)),
           pl.BlockSpec((lc, LANES), lambda b, h, i: (b, cv + h)),
           pl.BlockSpec((l, LANES), lambda b, h, i: (b + 1, cv + h)),
           tab_q, tab_q, tab_k, tab_k],
        out_specs=pl.BlockSpec((tq, LANES), lambda b, h, i: (b * per + i, h)),
        out_shape=jax.ShapeDtypeStruct((n_b * l, HC * DVC), BF16),
        scratch_shapes=[pltpu.VMEM((lc + l, LANES), BF16), pltpu.VMEM((lc + l, LANES), BF16)],
        compiler_params=_cparams(3),
        name="diff_attention",
    )(lam, gn.reshape(1, DVC), proj, proj, proj, proj, proj, rope_cos, rope_sin, rope_cos, rope_sin)
    if not ctx_out:
        return None, od
    od_ctx = pl.pallas_call(
        functools.partial(_attn_ctx_kernel, lam_init=lam_init),
        grid=(n_b, HC),
        in_specs=small
        + [pl.BlockSpec((lc, LANES), lambda b, h: (b, cq + h)),
           pl.BlockSpec((lc, LANES), lambda b, h: (b, ck + h)),
           pl.BlockSpec((lc, LANES), lambda b, h: (b, cv + h))],
        out_specs=pl.BlockSpec((lc, LANES), lambda b, h: (b, h)),
        out_shape=jax.ShapeDtypeStruct((n_b * lc, HC * DVC), BF16),
        compiler_params=_cparams(2),
        name="diff_attention_ctx",
    )(lam, gn.reshape(1, DVC), proj, proj, proj)
    return od_ctx, od


def _rope_tables(n_tokens):
    rows = n_tokens // GRID_W
    row = jnp.repeat(jnp.arange(rows, dtype=F32), GRID_W)
    col = jnp.tile(jnp.arange(GRID_W, dtype=F32), rows)
    n_freq = DKC // 4
    inv_freq = ROPE_THETA ** (-jnp.arange(n_freq, dtype=F32) / n_freq)
    ang_r = row[:, None] * inv_freq
    ang_c = col[:, None] * inv_freq
    cos = jnp.concatenate([jnp.cos(ang_r)] * 2 + [jnp.cos(ang_c)] * 2, axis=1)
    sin = jnp.concatenate([-jnp.sin(ang_r), jnp.sin(ang_r), -jnp.sin(ang_c), jnp.sin(ang_c)], axis=1)
    return jnp.concatenate([cos, cos], axis=1), jnp.concatenate([sin, sin], axis=1)


def _head_rms(o, gain, n_heads, width):
    parts = []
    for h in range(n_heads):
        parts.append(_rms_rows(o[:, h * width:(h + 1) * width], gain))
    return jnp.concatenate(parts, axis=1)


def _merge_kernel(*refs, ctx_tiles, n_tok, n_od):
    x_refs, od_refs = refs[:n_tok], refs[n_tok:n_tok + n_od]
    (oa0, oa1, z_ref, ob0, ob1, r_ref, ga_ref, gb_ref, gd_ref, mod_ref,
     gna_ref, gnb_ref, woa_ref, wob_ref, woc_ref, wout_ref, o_ref) = refs[n_tok + n_od:]
    f32 = lambda ref: ref[...].astype(F32)
    ya = _head_rms(f32(oa0) + f32(oa1), gna_ref[...], HA, DVA) * _silu(f32(z_ref))
    yb = _head_rms(f32(ob0) + f32(ob1), gnb_ref[...], HB, DVB) * _silu(f32(r_ref))
    acc = _sigmoid(f32(ga_ref)) * _dot(ya.astype(BF16), woa_ref[...])
    acc = acc + _sigmoid(f32(gb_ref)) * _dot(yb.astype(BF16), wob_ref[...])
    acc = acc + _sigmoid(f32(gd_ref)) * _dot(_token_tile(od_refs, ctx_tiles), woc_ref[...])
    y = _dot(acc.astype(BF16), wout_ref[...])
    o_ref[...] = _token_tile(x_refs, ctx_tiles) + mod_ref[0, 2:3, :] * y


def _merge(oa0, oa1, ob0, ob1, od_ctx, od_lat, proj, x_ctx, x_lat, lat_first, t, mod, gn_a, gn_b,
           woa, wob, woc, wout, rows_per_mod, row_off, tm):
    d = x_ctx.shape[1]
    off = row_off // tm
    ctx_tiles = rows_per_mod // tm
    n_tiles = (t - row_off) // tm
    per = rows_per_mod // tm
    w5 = HA * DVA
    row = lambda i: (i + off, 0)
    col = lambda c: (lambda i: (i + off, c))
    tok_specs, tok_args = _token_inputs(x_ctx, x_lat, lat_first, tm, ctx_tiles, off)
    if od_ctx is None:
        assert off == ctx_tiles
        od_specs, od_args = [pl.BlockSpec((tm, w5), lambda i: (i, 0))], [od_lat]
    else:
        od_specs, od_args = _token_specs(tm, w5, ctx_tiles, 0, off), [od_ctx, od_lat]
    return pl.pallas_call(
        functools.partial(_merge_kernel, ctx_tiles=ctx_tiles - off, n_tok=len(tok_args), n_od=len(od_args)),
        grid=(n_tiles,),
        in_specs=tok_specs + od_specs
        + [pl.BlockSpec((tm, w5), row),
                  pl.BlockSpec((tm, w5), row),
                  pl.BlockSpec((tm, w5), col(H_AZ // w5)),
                  pl.BlockSpec((tm, w5), row),
                  pl.BlockSpec((tm, w5), row),
                  pl.BlockSpec((tm, w5), col(H_BR // w5)),
                  pl.BlockSpec((tm, d), col(H_GATES // d)),
                  pl.BlockSpec((tm, d), col(H_GATES // d + 1)),
                  pl.BlockSpec((tm, d), col(H_GATES // d + 2)),
                  pl.BlockSpec((1, 6, d), lambda i: ((i + off) // per, 0, 0)),
                  pl.BlockSpec((1, DVA), lambda i: (0, 0)),
                  pl.BlockSpec((1, DVB), lambda i: (0, 0)),
                  pl.BlockSpec((w5, d), lambda i: (0, 0)),
                  pl.BlockSpec((w5, d), lambda i: (0, 0)),
                  pl.BlockSpec((w5, d), lambda i: (0, 0)),
                  pl.BlockSpec((d, d), lambda i: (0, 0))],
        out_specs=pl.BlockSpec((tm, d), lambda i: (i, 0)),
        out_shape=jax.ShapeDtypeStruct((t - row_off, d), F32),
        compiler_params=_cparams(1),
        name="merge_outproj",
    )(*tok_args, *od_args, oa0, oa1, proj, ob0, ob1, proj, proj, proj, proj, mod,
      gn_a.reshape(1, DVA), gn_b.reshape(1, DVB), woa, wob, woc, wout)


def _ffn_kernel(x_ref, mod_ref, g_ref, fg_ref, w1_ref, w3_ref, w2_ref, o_ref, h_scr, *, final):
    f = pl.program_id(1)

    @pl.when(f == 0)
    def _():
        y = _rms_rows(x_ref[...], g_ref[...])
        h_scr[...] = (y * (1.0 + mod_ref[0, 4:5, :]) + mod_ref[0, 3:4, :]).astype(BF16)
        o_ref[...] = jnp.zeros_like(o_ref)

    h = h_scr[...]
    t = _silu(_dot(h, w1_ref[...])) * _dot(h, w3_ref[...])
    o_ref[...] += _dot(t.astype(BF16), w2_ref[...])

    @pl.when(f == pl.num_programs(1) - 1)
    def _():
        out = x_ref[...] + mod_ref[0, 5:6, :] * o_ref[...]
        if final:
            out = _rms_rows(out, fg_ref[...])
        o_ref[...] = out


def _ffn(x, mod, g, final_g, w1, w3, w2, rows_per_mod, mod_off, tm, tf, final):
    t, d = x.shape
    dff = w1.shape[1]
    per = rows_per_mod // tm
    return pl.pallas_call(
        functools.partial(_ffn_kernel, final=final),
        grid=(t // tm, dff // tf),
        in_specs=[pl.BlockSpec((tm, d), lambda i, f: (i, 0)),
                  pl.BlockSpec((1, 6, d), lambda i, f: (i // per + mod_off, 0, 0)),
                  pl.BlockSpec((1, d), lambda i, f: (0, 0)),
                  pl.BlockSpec((1, d), lambda i, f: (0, 0)),
                  pl.BlockSpec((d, tf), lambda i, f: (0, f)),
                  pl.BlockSpec((d, tf), lambda i, f: (0, f)),
                  pl.BlockSpec((tf, d), lambda i, f: (f, 0))],
        out_specs=pl.BlockSpec((tm, d), lambda i, f: (i, 0)),
        out_shape=jax.ShapeDtypeStruct((t, d), F32),
        scratch_shapes=[pltpu.VMEM((tm, d), BF16)],
        compiler_params=_cparams(2),
        name="dense_ffn",
    )(x, mod, g.reshape(1, d), final_g.reshape(1, d), w1, w3, w2)


MOE_ROWS = 256
MOE_MOVE_UNIT = 128
MOE_FFN_UNIT = 128


def _moe_kernel(x_ref, mod_ref, g_ref, rw_ref, tri_ref, fg_ref, w1_ref, w3_ref, w2_ref, o_ref,
                h_scr, gate_scr, sel_scr, rank_scr, selr_scr, rankr_scr, xg_scr, yg_scr, nb_scr, *, final):
    e = pl.program_id(1)
    f = pl.program_id(2)
    n_f = pl.num_programs(2)
    tm, d = x_ref.shape
    cb = MOE_ROWS
    lane = lax.broadcasted_iota(jnp.int32, (1, LANES), 1)

    def for_blocks(body, slot, unit):
        units = nb_scr[slot]
        per = cb // unit
        nb = units // per

        def full(j, carry):
            body(pl.multiple_of(j * cb, cb), cb)
            return carry

        lax.fori_loop(0, nb, full, 0)
        rem = units - nb * per
        r0 = nb * cb
        size = cb // 2
        while size >= unit:
            has = (rem // (size // unit)) % 2 == 1

            @pl.when(has)
            def _():
                body(pl.multiple_of(r0, unit), size)

            r0 = r0 + jnp.where(has, size, 0)
            size //= 2

    @pl.when((e == 0) & (f == 0))
    def _():
        y = _rms_rows(x_ref[...], g_ref[...])
        hmod = y * (1.0 + mod_ref[0, 4:5, :]) + mod_ref[0, 3:4, :]
        h_scr[...] = hmod.astype(BF16)
        o_ref[...] = jnp.zeros_like(o_ref)
        logits = _dot3(hmod, rw_ref[...])
        logits = jnp.where(lane < N_EXPERTS, logits, -jnp.inf)
        lanef = lane.astype(F32)
        m1 = jnp.max(logits, axis=1, keepdims=True)
        i1 = jnp.min(jnp.where(logits == m1, lanef, float(LANES)), axis=1, keepdims=True)
        hit1 = lanef == i1
        rest = jnp.where(hit1, -jnp.inf, logits)
        m2 = jnp.max(rest, axis=1, keepdims=True)
        i2 = jnp.min(jnp.where(rest == m2, lanef, float(LANES)), axis=1, keepdims=True)
        hit2 = lanef == i2
        ex = jnp.exp(m2 - m1)
        inv = 1.0 / (1.0 + ex)
        gate_scr[...] = jnp.where(hit1, inv, 0.0) + jnp.where(hit2, ex * inv, 0.0)
        sel = jnp.where(hit1 | hit2, 1.0, 0.0)
        sel_scr[...] = sel
        sel16 = sel.astype(BF16)
        rank = _dot(tri_ref[...], sel16)
        rank_scr[...] = rank
        pick = jnp.where(lax.broadcasted_iota(jnp.int32, (8, LANES), 0)
                         == lax.broadcasted_iota(jnp.int32, (8, LANES), 1), 1.0, 0.0).astype(BF16)
        selr_scr[...] = _dot_nt(pick, sel16)
        hi = jnp.floor(rank * (1.0 / 32.0))
        lo = rank - 32.0 * hi
        rankr_scr[...] = 32.0 * _dot_nt(pick, hi.astype(BF16)) + _dot_nt(pick, lo.astype(BF16))

    @pl.when(f == 0)
    def _():
        sel_row = selr_scr[pl.ds(e, 1), :]
        rank_row = rankr_scr[pl.ds(e, 1), :]
        n_e = jnp.sum(sel_row).astype(jnp.int32)
        nb_scr[0] = (n_e + (MOE_MOVE_UNIT - 1)) // MOE_MOVE_UNIT
        nb_scr[1] = (n_e + (MOE_FFN_UNIT - 1)) // MOE_FFN_UNIT

        def gather(r0, bs):
            slot = lax.broadcasted_iota(jnp.int32, (bs, tm), 0).astype(F32) + r0.astype(F32)
            onehot = jnp.where((rank_row == slot) & (sel_row > 0.0), 1.0, 0.0)
            xg_scr[pl.ds(r0, bs), :] = _dot(onehot.astype(BF16), h_scr[...]).astype(BF16)
            yg_scr[pl.ds(r0, bs), :] = jnp.zeros((bs, d), F32)

        for_blocks(gather, 0, MOE_MOVE_UNIT)

    def expert(r0, bs):
        xb = xg_scr[pl.ds(r0, bs), :]
        t = _silu(_dot(xb, w1_ref[0])) * _dot(xb, w3_ref[0])
        yg_scr[pl.ds(r0, bs), :] += _dot(t.astype(BF16), w2_ref[0])

    for_blocks(expert, 1, MOE_FFN_UNIT)

    @pl.when(f == n_f - 1)
    def _():
        pick_e = lane == e
        rank_col = jnp.sum(jnp.where(pick_e, rank_scr[...], 0.0), axis=1, keepdims=True)
        sel_col = jnp.sum(jnp.where(pick_e, sel_scr[...], 0.0), axis=1, keepdims=True)
        gate_col = jnp.sum(jnp.where(pick_e, gate_scr[...], 0.0), axis=1, keepdims=True)

        def scatter(r0, bs):
            slot = lax.broadcasted_iota(jnp.int32, (tm, bs), 1).astype(F32) + r0.astype(F32)
            onehot = jnp.where((rank_col == slot) & (sel_col > 0.0), 1.0, 0.0)
            o_ref[...] += gate_col * _dot(onehot.astype(BF16), yg_scr[pl.ds(r0, bs), :].astype(BF16))

        for_blocks(scatter, 0, MOE_MOVE_UNIT)

    @pl.when((e == pl.num_programs(1) - 1) & (f == n_f - 1))
    def _():
        out = x_ref[...] + mod_ref[0, 5:6, :] * o_ref[...]
        if final:
            out = _rms_rows(out, fg_ref[...])
        o_ref[...] = out


def _moe(x, mod, g, router_w, final_g, w1, w3, w2, rows_per_mod, mod_off, tm, tf, final):
    t, d = x.shape
    n_e, _, dff = w1.shape
    per = rows_per_mod // tm
    return pl.pallas_call(
        functools.partial(_moe_kernel, final=final),
        grid=(t // tm, n_e, dff // tf),
        in_specs=[pl.BlockSpec((tm, d), lambda i, e, f: (i, 0)),
                  pl.BlockSpec((1, 6, d), lambda i, e, f: (i // per + mod_off, 0, 0)),
                  pl.BlockSpec((1, d), lambda i, e, f: (0, 0)),
                  pl.BlockSpec((d, LANES), lambda i, e, f: (0, 0)),
                  pl.BlockSpec((tm, tm), lambda i, e, f: (0, 0)),
                  pl.BlockSpec((1, d), lambda i, e, f: (0, 0)),
                  pl.BlockSpec((1, d, tf), lambda i, e, f: (e, 0, f)),
                  pl.BlockSpec((1, d, tf), lambda i, e, f: (e, 0, f)),
                  pl.BlockSpec((1, tf, d), lambda i, e, f: (e, f, 0))],
        out_specs=pl.BlockSpec((tm, d), lambda i, e, f: (i, 0)),
        out_shape=jax.ShapeDtypeStruct((t, d), F32),
        scratch_shapes=[pltpu.VMEM((tm, d), BF16),
                        pltpu.VMEM((tm, LANES), F32),
                        pltpu.VMEM((tm, LANES), F32),
                        pltpu.VMEM((tm, LANES), F32),
                        pltpu.VMEM((8, tm), F32),
                        pltpu.VMEM((8, tm), F32),
                        pltpu.VMEM((tm, d), BF16),
                        pltpu.VMEM((tm, d), F32),
                        pltpu.SMEM((2,), jnp.int32)],
        compiler_params=_cparams(3),
        name="routed_ffn_sparse",
    )(x, mod, g.reshape(1, d), router_w, jnp.tril(jnp.ones((tm, tm), BF16), -1),
      final_g.reshape(1, d), w1, w3, w2)


def _w_in_pieces(end):
    a_qkv = 2 * HA * DKA + HA * DVA
    a_z = a_qkv + HA * DVA
    b_q = a_z + 4 * HA
    b_v = b_q + 2 * HB * DKB
    b_glr = b_v + 2 * HB * DVB
    c_q = b_glr + 2 * GATE_RANK
    gates = c_q + 4 * HC * DKC + HC * DVC
    return [(0, a_qkv), (a_z, b_q), (b_glr, c_q), None,
            (gates, end), (a_qkv, a_z), (b_v, b_glr), (c_q, gates), (b_q, b_v)]


def _reorder_kernel(w_ref, o_ref):
    at = 0
    for piece in _w_in_pieces(w_ref.shape[1]):
        if piece is None:
            width = N_F32 - at
            o_ref[0, at:at + width, :] = jnp.zeros((width, o_ref.shape[2]), BF16)
        else:
            width = piece[1] - piece[0]
            o_ref[0, at:at + width, :] = w_ref[0, piece[0]:piece[1], :].astype(BF16)
        at += width
    assert at == N_F32 + N_B16


def _reorder_w_in(w_in):
    w_t = jnp.swapaxes(w_in, 1, 2)
    depth, cols, d = w_t.shape
    kb = LANES
    return pl.pallas_call(
        _reorder_kernel,
        grid=(depth, d // kb),
        in_specs=[pl.BlockSpec((1, cols, kb), lambda l, i: (l, 0, i))],
        out_specs=pl.BlockSpec((1, N_F32 + N_B16, kb), lambda l, i: (l, 0, i)),
        out_shape=jax.ShapeDtypeStruct((depth, N_F32 + N_B16, d), BF16),
        compiler_params=_cparams(2),
        name="reorder_w_in",
    )(w_t)


def kernel(x, c, ctx, c_ctx, w_mod, b_mod, norm1_g, norm2_g, w_in, conv_a, a_log, dt_bias, gn_a, w_gate2, b_gate, gn_b, lam_c, gn_c, w_o_a, w_o_b, w_o_c, w_out, ffn_w1, ffn_w3, ffn_w2, router_w, moe_w1, moe_w3, moe_w2, final_g):
    n_b, l, d = x.shape
    lc = ctx.shape[1]
    depth = w_mod.shape[0]
    assert n_b * lc == l and lc % CHUNK == 0 and l & (l - 1) == 0 and lc & (lc - 1) == 0
    t_ctx = n_b * lc
    tm = min(1024, l)
    tm_small = min(512, l)

    n_mod = 16
    cvec = jnp.concatenate([c_ctx[None, :], c, jnp.zeros((n_mod - 1 - n_b, d), F32)], axis=0)
    mod_all = _modulation(cvec, w_mod, b_mod).reshape(depth, n_mod, 6, d)

    rope_cos, rope_sin = _rope_tables(l)
    w_in16 = _reorder_w_in(w_in)
    t_all = t_ctx + n_b * l
    tokens = (ctx.reshape(t_ctx, d), x.reshape(n_b * l, d), 0)

    for layer in range(depth):
        ctx_out = layer < depth - 1
        lam_init = 0.8 - 0.6 * math.exp(-0.3 * layer)
        mod = mod_all[layer]
        proj32, proj16 = _in_projection(*tokens, t_all, mod, norm1_g[layer], w_in16, layer, l, tm)

        qkvc = _short_conv(proj32, conv_a[layer], lc, l)
        oa0, oa1 = _deltanet(qkvc, proj32, a_log[layer], dt_bias[layer], n_b, lc, l)
        ob0, ob1 = _gla(proj32, proj16, w_gate2[layer], b_gate[layer], n_b, lc, l)
        od = _attention(proj16, lam_c[layer], gn_c[layer], rope_cos, rope_sin, n_b, lc, l, lam_init, ctx_out)

        row_off = 0 if ctx_out else t_ctx
        xs_new = _merge(oa0, oa1, ob0, ob1, *od, proj16, *tokens, t_all, mod, gn_a[layer], gn_b[layer],
                        w_o_a[layer].astype(BF16), w_o_b[layer].astype(BF16), w_o_c[layer].astype(BF16),
                        w_out[layer].astype(BF16), l, row_off, tm_small)

        i = layer // 2
        final = layer == depth - 1
        mod_off = 0 if ctx_out else 1
        if layer % 2 == 0:
            xs = _ffn(xs_new, mod, norm2_g[layer], final_g, ffn_w1[i].astype(BF16),
                      ffn_w3[i].astype(BF16), ffn_w2[i].astype(BF16), l, mod_off, tm_small,
                      ffn_w1.shape[2] // 2, final)
        else:
            rw = jnp.pad(router_w[i], ((0, 0), (0, LANES - N_EXPERTS)))
            xs = _moe(xs_new, mod, norm2_g[layer], rw, final_g, moe_w1[i].astype(BF16),
                      moe_w3[i].astype(BF16), moe_w2[i].astype(BF16), l, mod_off, tm,
                      moe_w1.shape[3] // 2, final)
        tokens = (xs, xs, t_ctx)

    return xs.reshape(n_b, l, d)
```

```python
import functools
import math

import jax
import jax.numpy as jnp
from jax import lax
from jax.experimental import pallas as pl
from jax.experimental.pallas import tpu as pltpu

F32 = jnp.float32
BF16 = jnp.bfloat16
HIGHEST = lax.Precision.HIGHEST

EPS = 1e-6
HA, DKA, DVA = 4, 128, 128
HB, DKB, DVB = 4, 64, 128
GATE_RANK, GATE_NORM = 16, 16.0
HC, DKC, DVC = 4, 64, 128
GRID_W, ROPE_THETA = 64, 10000.0
N_EXPERTS, TOP_K = 8, 2
CHUNK = 64
LANES = 128
ATTN_SUB = 64
ATTN_TQ = 512
CONV_HEADS = 4

F_QKV = 0
F_SMALL = 1536
N_F32 = 1664
H_GATES = 0
H_AZ = 3072
H_BV = 3584
H_BR = 4096
H_CQ = 4608
H_CK = 5120
H_CV = 5632
H_BQ = 6144
H_BK = 6400
N_B16 = 6656
PROJ_TN = 1664

VMEM_LIMIT = 56 * 1024 * 1024


def _cparams(n_axes):
    return pltpu.CompilerParams(dimension_semantics=("arbitrary",) * n_axes,
                                vmem_limit_bytes=VMEM_LIMIT)


def _sigmoid(x):
    return 1.0 / (1.0 + jnp.exp2(x * -math.log2(math.e)))


def _silu(x):
    return x * _sigmoid(x)


def _softplus(x):
    return jnp.maximum(x, 0.0) + jnp.log(1.0 + jnp.exp(-jnp.abs(x)))


def _dot(a, b):
    return jnp.dot(a, b, preferred_element_type=F32)


def _dot_nt(a, b):
    return lax.dot_general(a, b, (((1,), (1,)), ((), ())), preferred_element_type=F32)


def _dot_tn(a, b):
    return lax.dot_general(a, b, (((0,), (0,)), ((), ())), preferred_element_type=F32)


def _split(a):
    hi = a.astype(BF16)
    lo = (a - hi.astype(F32)).astype(BF16)
    return hi, lo


def _dot3(a, b):
    ah, al = _split(a)
    bh, bl = _split(b)
    return _dot(ah, bh) + (_dot(ah, bl) + _dot(al, bh))


def _rms_rows(x, gain):
    ms = jnp.mean(x * x, axis=-1, keepdims=True)
    return x * lax.rsqrt(ms + EPS) * gain


def _mod_kernel(c_ref, w_ref, b_ref, o_ref):
    s = _silu(c_ref[...])
    o_ref[0] = jnp.dot(s, w_ref[0], precision=HIGHEST, preferred_element_type=F32) + b_ref[0]


def _modulation(cvec, w_mod, b_mod):
    depth, d, d6 = w_mod.shape
    n = cvec.shape[0]
    return pl.pallas_call(
        _mod_kernel,
        grid=(depth, d6 // d),
        in_specs=[pl.BlockSpec((n, d), lambda l, j: (0, 0)),
                  pl.BlockSpec((1, d, d), lambda l, j: (l, 0, j)),
                  pl.BlockSpec((1, 1, d), lambda l, j: (l, 0, j))],
        out_specs=pl.BlockSpec((1, n, d), lambda l, j: (l, 0, j)),
        out_shape=jax.ShapeDtypeStruct((depth, n, d6), F32),
        compiler_params=_cparams(2),
        name="adaln_mod",
    )(cvec, w_mod, b_mod.reshape(depth, 1, d6))


def _token_specs(tm, d, ctx_tiles, lat_first, off=0):
    return [pl.BlockSpec((tm, d), lambda i, *_: (jnp.minimum(i + off, ctx_tiles - 1), 0)),
            pl.BlockSpec((tm, d), lambda i, *_: (jnp.maximum(i + off - ctx_tiles, 0) + lat_first, 0))]


def _token_inputs(x_ctx, x_lat, lat_first, tm, ctx_tiles, off=0):
    d = x_ctx.shape[1]
    if x_lat is x_ctx:
        return [pl.BlockSpec((tm, d), lambda i, *_: (i + off, 0))], [x_ctx]
    return _token_specs(tm, d, ctx_tiles, lat_first // tm, off), [x_ctx, x_lat]


def _token_tile(x_refs, ctx_tiles):
    if len(x_refs) == 1:
        return x_refs[0][...]
    return jnp.where(pl.program_id(0) < ctx_tiles, x_refs[0][...], x_refs[1][...])


def _inproj_kernel(*refs, ctx_tiles, n_tok):
    x_refs, (mod_ref, g_ref, w_ref, o32_ref, o16_ref, h_scr) = refs[:n_tok], refs[n_tok:]
    j = pl.program_id(1)

    @pl.when(j == 0)
    def _():
        y = _rms_rows(_token_tile(x_refs, ctx_tiles), g_ref[...])
        h_scr[...] = (y * (1.0 + mod_ref[0, 1:2, :]) + mod_ref[0, 0:1, :]).astype(BF16)
        o32_ref[...] = _dot_nt(h_scr[...], w_ref[0])

    @pl.when(j > 0)
    def _():
        o16_ref[...] = _dot_nt(h_scr[...], w_ref[0]).astype(BF16)


def _in_projection(x_ctx, x_lat, lat_first, t, mod, g, w, layer, rows_per_mod, tm):
    d = x_ctx.shape[1]
    tn = PROJ_TN
    assert w.shape[1] == N_F32 + N_B16 and N_F32 == tn and N_B16 % tn == 0
    per = rows_per_mod // tm
    ctx_tiles = rows_per_mod // tm
    tok_specs, tok_args = _token_inputs(x_ctx, x_lat, lat_first, tm, ctx_tiles)
    return pl.pallas_call(
        functools.partial(_inproj_kernel, ctx_tiles=ctx_tiles, n_tok=len(tok_args)),
        grid=(t // tm, 1 + N_B16 // tn),
        in_specs=tok_specs
        + [pl.BlockSpec((1, 6, d), lambda i, j: (i // per, 0, 0)),
           pl.BlockSpec((1, d), lambda i, j: (0, 0)),
           pl.BlockSpec((1, tn, d), lambda i, j: (layer, j, 0))],
        out_specs=[pl.BlockSpec((tm, tn), lambda i, j: (i, 0)),
                   pl.BlockSpec((tm, tn), lambda i, j: (i, jnp.maximum(j - 1, 0)))],
        out_shape=[jax.ShapeDtypeStruct((t, N_F32), F32), jax.ShapeDtypeStruct((t, N_B16), BF16)],
        scratch_shapes=[pltpu.VMEM((tm, d), BF16)],
        compiler_params=_cparams(2),
        name="norm_mod_inproj",
    )(*tok_args, mod, g.reshape(1, d), w)


def _conv_kernel(x_ref, w_ref, o_ref, *, lc, l):
    p = pl.program_id(0)
    j = pl.program_id(1)
    x = x_ref[...]
    n = x.shape[0]
    row = lax.broadcasted_iota(jnp.int32, x.shape, 0)
    seq = jnp.where(p == 0, lc, l)
    pos = row & (seq - 1)
    prev = jnp.where(pos == 0, 0.0, pltpu.roll(x, 1, 0))
    nxt = jnp.where(pos == seq - 1, 0.0, pltpu.roll(x, n - 1, 0))
    w = w_ref[...]
    y = _silu(prev * w[0:1] + x * w[1:2] + nxt * w[2:3])
    heads_per_block = x.shape[1] // DKA
    for hh in range(heads_per_block):
        cols = slice(hh * DKA, (hh + 1) * DKA)
        yh = y[:, cols]
        inv = lax.rsqrt(jnp.sum(yh * yh, axis=-1, keepdims=True) + EPS)
        head = j * heads_per_block + hh
        fac = jnp.where(head < HA, inv * DKA ** -0.5, jnp.where(head < 2 * HA, inv, 1.0))
        o_ref[:, cols] = yh * fac


def _short_conv(proj, conv_w, lc, l):
    t = proj.shape[0]
    ncol = conv_w.shape[1]
    wide = CONV_HEADS * DKA
    return pl.pallas_call(
        functools.partial(_conv_kernel, lc=lc, l=l),
        grid=(t // l, ncol // wide),
        in_specs=[pl.BlockSpec((l, wide), lambda p, j: (p, j)),
                  pl.BlockSpec((conv_w.shape[0], wide), lambda p, j: (0, j))],
        out_specs=pl.BlockSpec((l, wide), lambda p, j: (p, j)),
        out_shape=jax.ShapeDtypeStruct((t, ncol), F32),
        compiler_params=_cparams(2),
        name="conv_silu_l2",
    )(proj, conv_w)


def _row_block(b, g, n_b, n_lat):
    return jnp.where(g == 0, b, n_b + b * n_lat + g - 1)


def _rev_group(g, n_g):
    return jnp.where(g == 0, 0, n_g - g)


def _tri_inverse_many(lms, ii, jj, uppers):
    c = lms[0].shape[0]
    eye = (ii == jj).astype(F32)
    n = len(lms)
    ts = None
    s, ls = 1, 0
    while s < c:
        same = (ii >> (ls + 1)) == (jj >> (ls + 1))
        io = (ii >> ls) & 1
        jo = (jj >> ls) & 1
        lower = same & (io == 1) & (jo == 0)
        upper = same & (io == 0) & (jo == 1)
        cms = [jnp.where(upper if uppers[i] else lower, lms[i], 0.0) for i in range(n)]
        if ts is None:
            ts = [eye - cm for cm in cms]
        else:
            t16 = [t.astype(BF16) for t in ts]
            ps = [_dot(t16[i], cms[i].astype(BF16)).astype(BF16) for i in range(n)]
            yield
            ts = [ts[i] - _dot(ps[i], t16[i]) for i in range(n)]
            yield
        s, ls = s * 2, ls + 1
    a_parts = [_split(eye + lm) for lm in lms]
    t_parts = [_split(t) for t in ts]
    res = [eye - (_dot(a_parts[i][0], t_parts[i][0])
                  + (_dot(a_parts[i][0], t_parts[i][1]) + _dot(a_parts[i][1], t_parts[i][0])))
           for i in range(n)]
    yield
    return [ts[i] + _dot(t_parts[i][0], res[i].astype(BF16)) for i in range(n)]


def _interleave(*stage_generators):
    live = list(stage_generators)
    while live:
        for gen in list(live):
            try:
                next(gen)
            except StopIteration:
                live.remove(gen)


def _dn_stages(alog_ref, dtb_ref, q0, k0, v0, sm0, q1, k1, v1, sm1, o0_ref, o1_ref, s_scr, n_chunks):
    @pl.when(pl.program_id(1) == 0)
    def _():
        s_scr[...] = jnp.zeros_like(s_scr)

    c_ = CHUNK
    refs = ((q0, k0, v0, o0_ref), (q1, k1, v1, o1_ref))
    sms = (sm0[...], sm1[...])
    sm_ts = tuple(s.T for s in sms)
    ii = lax.broadcasted_iota(jnp.int32, (c_, c_), 0)
    jj = lax.broadcasted_iota(jnp.int32, (c_, c_), 1)
    rows = [slice(c * c_, (c + 1) * c_) for c in range(n_chunks)]
    lanes = [slice(j * LANES, (j + 1) * LANES) for j in range(HA)]
    items = [(j, c, d) for j in range(HA) for c in range(n_chunks) for d in range(2)]
    qs = [refs[d][0][rows[c], lanes[j]] for j, c, d in items]
    ks = [refs[d][1][rows[c], lanes[j]] for j, c, d in items]
    vs = [refs[d][2][rows[c], lanes[j]] for j, c, d in items]
    k16 = [k.astype(BF16) for k in ks]
    kks = [_dot_nt(k, k) for k in k16]
    qks = [_dot_nt(qs[i].astype(BF16), k16[i]) for i in range(len(items))]
    yield

    betas, gcs, totals, gams, lms = [], [], [], [], []
    for i, (j, c, d) in enumerate(items):
        r = rows[c]
        incl = (jj <= ii) if d == 0 else (jj >= ii)
        strict = (jj < ii) if d == 0 else (jj > ii)
        incl_t = (ii <= jj) if d == 0 else (ii >= jj)
        a_scale = -jnp.exp(jnp.full((1, 1), alog_ref[d * HA + j], F32))
        dtb = dtb_ref[d * HA + j]
        col = d * HA + j
        beta = _sigmoid(sms[d][r, col:col + 1])
        g_col = a_scale * _softplus(sms[d][r, 2 * HA + col:2 * HA + col + 1] + dtb)
        g_row = a_scale * _softplus(sm_ts[d][2 * HA + col:2 * HA + col + 1, r] + dtb)
        gc_col = jnp.sum(jnp.where(incl, g_row, 0.0), axis=1, keepdims=True)
        gc_row = jnp.sum(jnp.where(incl_t, g_col, 0.0), axis=0, keepdims=True)
        gam = jnp.where(incl, jnp.exp(gc_col - gc_row), 0.0)
        betas.append(beta)
        gcs.append(gc_col)
        totals.append(jnp.sum(g_col, axis=0, keepdims=True))
        gams.append(gam)
        lms.append(jnp.where(strict, beta * kks[i] * gam, 0.0))

    yield
    tinvs = yield from _tri_inverse_many(lms, ii, jj, [d == 1 for _, _, d in items])
    yield
    egs = [jnp.exp(gc) for gc in gcs]
    uws = []
    for i in range(len(items)):
        rhs = jnp.concatenate([vs[i] * betas[i], ks[i] * (betas[i] * egs[i])], axis=1).astype(BF16)
        uws.append(_dot(tinvs[i].astype(BF16), rhs))
    yield
    us = [uw[:, :DVA] for uw in uws]
    wq16 = [jnp.concatenate([uws[i][:, DVA:], qs[i] * egs[i]], axis=0).astype(BF16) for i in range(len(items))]
    kg16 = [(ks[i] * jnp.exp(totals[i] - gcs[i])).astype(BF16) for i in range(len(items))]
    qkg16 = [(qks[i] * gams[i]).astype(BF16) for i in range(len(items))]
    carry = [jnp.exp(t) for t in totals]
    yield

    index = {item: i for i, item in enumerate(items)}
    chains = [(j, d) for j in range(HA) for d in range(2)]
    states = [s_scr[d * HA + j] for j, d in chains]
    for step in range(n_chunks):
        at = [index[(j, step if d == 0 else n_chunks - 1 - step, d)] for j, d in chains]
        s16 = [s.astype(BF16) for s in states]
        prods = [_dot(wq16[i], s16[n]) for n, i in enumerate(at)]
        vn16 = [(us[i] - prods[n][:c_]).astype(BF16) for n, i in enumerate(at)]
        yield
        for n, i in enumerate(at):
            j, c, d = items[i]
            refs[d][3][rows[c], lanes[j]] = (prods[n][c_:] + _dot(qkg16[i], vn16[n])).astype(BF16)
            states[n] = carry[i] * states[n] + _dot_tn(kg16[i], vn16[n])
        yield
    for n, (j, d) in enumerate(chains):
        s_scr[d * HA + j] = states[n]


def _gla_stages(q0, k0, v0, sm0, q1, k1, v1, sm1, wg_ref, bg_ref, o0_ref, o1_ref, s_scr, n_chunks):
    @pl.when(pl.program_id(1) == 0)
    def _():
        s_scr[...] = jnp.zeros_like(s_scr)

    c_ = CHUNK
    sh = c_.bit_length() - 1
    g_ = n_chunks * c_
    kw = HB * DKB
    vw = HB * DVB
    ii = lax.broadcasted_iota(jnp.int32, (g_, g_), 0)
    jj = lax.broadcasted_iota(jnp.int32, (g_, g_), 1)
    same_chunk = (ii >> sh) == (jj >> sh)
    klane = lax.broadcasted_iota(jnp.int32, (1, kw), 1)
    ind16 = jnp.where((lax.broadcasted_iota(jnp.int32, (g_, n_chunks * LANES), 0) >> sh)
                      == (lax.broadcasted_iota(jnp.int32, (g_, n_chunks * LANES), 1) >> (LANES.bit_length() - 1)),
                      1.0, 0.0).astype(BF16)
    rows = [slice(c * c_, (c + 1) * c_) for c in range(n_chunks)]
    refs = ((q0, k0, v0, sm0, o0_ref), (q1, k1, v1, sm1, o1_ref))

    def spread(x, picks):
        return jnp.concatenate([jnp.broadcast_to(x[p:p + 1, :], (c_, kw)) for p in picks], axis=0)

    dirs = (0, 1)
    incls = [same_chunk & ((jj <= ii) if d == 0 else (jj >= ii)) for d in dirs]
    tri16 = [jnp.where(m, 1.0, 0.0).astype(BF16) for m in incls]
    pres = [jnp.dot(refs[d][3][:, 4 * HA + d * GATE_RANK:4 * HA + (d + 1) * GATE_RANK], wg_ref[d],
                    precision=HIGHEST, preferred_element_type=F32) + bg_ref[d] for d in dirs]
    yield
    glogs = [(jnp.minimum(p, 0.0) - jnp.log(1.0 + jnp.exp(-jnp.abs(p)))) * (1.0 / GATE_NORM) for p in pres]
    parts = [_split(g) for g in glogs]
    yield
    gcs = [_dot(tri16[d], parts[d][0]) + _dot(tri16[d], parts[d][1]) for d in dirs]
    tot_cols = [_dot_tn(parts[d][0], ind16) + _dot_tn(parts[d][1], ind16) for d in dirs]
    yield
    mids = [spread(gcs[d], [c * c_ + c_ // 2 - 1 for c in range(n_chunks)]) for d in dirs]
    tots = [spread(gcs[d], [c * c_ + (c_ - 1 if d == 0 else 0) for c in range(n_chunks)]) for d in dirs]
    qs = [refs[d][0][...].astype(F32) * DKB ** -0.5 for d in dirs]
    ks = [refs[d][1][...].astype(F32) for d in dirs]
    v16s = [refs[d][2][...] for d in dirs]
    qts = [qs[d] * jnp.exp(gcs[d] - mids[d]) for d in dirs]
    kt16s = [(ks[d] * jnp.exp(mids[d] - gcs[d])).astype(BF16) for d in dirs]
    qg16s = [(qs[d] * jnp.exp(gcs[d])).astype(BF16) for d in dirs]
    kg16s = [(ks[d] * jnp.exp(tots[d] - gcs[d])).astype(BF16) for d in dirs]
    yield
    a16s = {}
    for h in range(HB):
        in_head = (klane >= h * DKB) & (klane < (h + 1) * DKB)
        for d in dirs:
            a = _dot_nt(jnp.where(in_head, qts[d], 0.0).astype(BF16), kt16s[d])
            a16s[(d, h)] = jnp.where(incls[d], a, 0.0).astype(BF16)
        yield
    o_intra = [[_dot(a16s[(d, h)], v16s[d][:, h * DVB:(h + 1) * DVB]) for h in range(HB)] for d in dirs]
    yield
    xs = [[_dot_tn(kg16s[d][r, :], v16s[d][r, :]) for r in rows] for d in dirs]
    decays = [[jnp.exp(tot_cols[d][:, c * LANES:(c + 1) * LANES]) for c in range(n_chunks)] for d in dirs]
    yield

    states = [[s_scr[d, h] for h in range(HB)] for d in dirs]
    zero16 = jnp.zeros((DKB, DVB), BF16)
    for step in range(n_chunks):
        for d in range(2):
            c = step if d == 0 else n_chunks - 1 - step
            r = rows[c]
            s16 = [s.astype(BF16) for s in states[d]]
            s_all = jnp.concatenate(
                [jnp.concatenate([s16[h] if hv == h else zero16 for hv in range(HB)], axis=1) for h in range(HB)],
                axis=0)
            o_inter = _dot(qg16s[d][r, :], s_all)
            for h in range(HB):
                vs = slice(h * DVB, (h + 1) * DVB)
                kr = slice(h * DKB, (h + 1) * DKB)
                refs[d][4][r, vs] = (o_inter[:, vs] + o_intra[d][h][r, :]).astype(BF16)
                states[d][h] = decays[d][c][kr, :] * states[d][h] + xs[d][c][kr, vs]
        yield
    for d in dirs:
        for h in range(HB):
            s_scr[d, h] = states[d][h]


def _recurrent_kernel(alog_ref, dtb_ref, aq0, ak0, av0, sm0, aq1, ak1, av1, sm1,
                      bq0, bk0, bv0, bq1, bk1, bv1, wg_ref, bg_ref,
                      oa0_ref, oa1_ref, ob0_ref, ob1_ref, sa_scr, sb_scr, *, n_chunks):
    _interleave(
        _dn_stages(alog_ref, dtb_ref, aq0, ak0, av0, sm0, aq1, ak1, av1, sm1, oa0_ref, oa1_ref, sa_scr, n_chunks),
        _gla_stages(bq0, bk0, bv0, sm0, bq1, bk1, bv1, sm1, wg_ref, bg_ref, ob0_ref, ob1_ref, sb_scr, n_chunks))


def _recurrent_branches(qkvc, proj32, proj16, a_log, dt_bias, w_gate2, b_gate, n_b, lc, l):
    t = qkvc.shape[0]
    n_lat = l // lc
    n_g = n_lat + 1
    wide = HA * DVA
    kw, vw = HB * DKB, HB * DVB
    rb0 = lambda b, g: _row_block(b, g, n_b, n_lat)
    rb1 = lambda b, g: _row_block(b, _rev_group(g, n_g), n_b, n_lat)
    smem = pl.BlockSpec(memory_space=pltpu.SMEM)

    def dn_specs(rb):
        return [pl.BlockSpec((lc, wide), lambda b, g: (rb(b, g), 0)),
                pl.BlockSpec((lc, wide), lambda b, g: (rb(b, g), 1)),
                pl.BlockSpec((lc, wide), lambda b, g: (rb(b, g), 2)),
                pl.BlockSpec((lc, LANES), lambda b, g: (rb(b, g), F_SMALL // LANES))]

    def gla_specs(rb):
        return [pl.BlockSpec((lc, kw), lambda b, g: (rb(b, g), H_BQ // kw)),
                pl.BlockSpec((lc, kw), lambda b, g: (rb(b, g), H_BK // kw)),
                pl.BlockSpec((lc, vw), lambda b, g: (rb(b, g), H_BV // vw))]

    out = lambda rb, width: pl.BlockSpec((lc, width), lambda b, g: (rb(b, g), 0))
    return pl.pallas_call(
        functools.partial(_recurrent_kernel, n_chunks=lc // CHUNK),
        grid=(n_b, n_g),
        in_specs=[smem, smem] + dn_specs(rb0) + dn_specs(rb1) + gla_specs(rb0) + gla_specs(rb1)
        + [pl.BlockSpec((2, GATE_RANK, kw), lambda b, g: (0, 0, 0)),
           pl.BlockSpec((2, 1, kw), lambda b, g: (0, 0, 0))],
        out_specs=[out(rb0, wide), out(rb1, wide), out(rb0, vw), out(rb1, vw)],
        out_shape=[jax.ShapeDtypeStruct((t, wide), BF16)] * 2 + [jax.ShapeDtypeStruct((t, vw), BF16)] * 2,
        scratch_shapes=[pltpu.VMEM((2 * HA, DKA, DVA), F32), pltpu.VMEM((2, HB, DKB, DVB), F32)],
        compiler_params=_cparams(2),
        name="recurrent_branches",
    )(a_log.reshape(-1), dt_bias.reshape(-1), qkvc, qkvc, qkvc, proj32, qkvc, qkvc, qkvc, proj32,
      proj16, proj16, proj16, proj16, proj16, proj16, w_gate2, b_gate.reshape(2, 1, kw))


def _lambda_full(lam, lam_init):
    return (jnp.exp(jnp.sum(lam[0:1] * lam[1:2], axis=1, keepdims=True))
            - jnp.exp(jnp.sum(lam[2:3] * lam[3:4], axis=1, keepdims=True)) + lam_init)


def _attend(q, keys, vals, lam_f, gain, lam_init, o_ref):
    lane = lax.broadcasted_iota(jnp.int32, (1, LANES), 1)
    n = ATTN_SUB
    q = q * (DKC ** -0.5 * math.log2(math.e))
    subs = [q[r * n:(r + 1) * n] for r in range(q.shape[0] // n)]
    q2s = [jnp.concatenate([jnp.where(lane < DKC, x, 0.0), jnp.where(lane >= DKC, x, 0.0)],
                           axis=0).astype(BF16) for x in subs]
    ss = [_dot_nt(q2, keys) for q2 in q2s]
    es = [jnp.exp2(s - jnp.max(s, axis=-1, keepdims=True)) for s in ss]
    invs = [1.0 / jnp.sum(e, axis=-1, keepdims=True) for e in es]
    o2s = [_dot(es[r].astype(BF16), vals) * invs[r] for r in range(len(es))]
    for r, o2 in enumerate(o2s):
        o = o2[:n] - lam_f * o2[n:]
        o_ref[r * n:(r + 1) * n, :] = (_rms_rows(o, gain) * (1.0 - lam_init)).astype(BF16)


def _attn_lat_kernel(lam_ref, gn_ref, q_ref, kc_ref, kl_ref, vc_ref, vl_ref, cq_ref, sq_ref, ck_ref, sk_ref,
                     o_ref, k_scr, v_scr, *, lc, lam_init):
    lane = lax.broadcasted_iota(jnp.int32, (1, LANES), 1)
    first_half = (lane & 31) < 16

    def rope(x, cos, sin):
        swapped = jnp.where(first_half, pltpu.roll(x, LANES - 16, 1), pltpu.roll(x, 16, 1))
        return x * cos + swapped * sin

    @pl.when(pl.program_id(2) == 0)
    def _():
        k_scr[0:lc, :] = kc_ref[...]
        k_scr[lc:, :] = rope(kl_ref[...].astype(F32), ck_ref[...], sk_ref[...]).astype(BF16)
        v_scr[0:lc, :] = vc_ref[...]
        v_scr[lc:, :] = vl_ref[...]

    _attend(rope(q_ref[...].astype(F32), cq_ref[...], sq_ref[...]), k_scr[...], v_scr[...],
            _lambda_full(lam_ref[...], lam_init), gn_ref[...], lam_init, o_ref)


def _attn_ctx_kernel(lam_ref, gn_ref, q_ref, k_ref, v_ref, o_ref, *, lam_init):
    _attend(q_ref[...].astype(F32), k_ref[...], v_ref[...], _lambda_full(lam_ref[...], lam_init),
            gn_ref[...], lam_init, o_ref)


def _attention(proj, lam, gn, rope_cos, rope_sin, n_b, lc, l, lam_init, ctx_out):
    tq = min(ATTN_TQ, l)
    per = l // tq
    cq, ck, cv = H_CQ // LANES, H_CK // LANES, H_CV // LANES
    small = [pl.BlockSpec((4, DKC), lambda b, h, *_: (0, 0)), pl.BlockSpec((1, DVC), lambda b, h, *_: (0, 0))]
    tab_q = pl.BlockSpec((tq, LANES), lambda b, h, i: (i, 0))
    tab_k = pl.BlockSpec((l, LANES), lambda b, h, i: (0, 0))
    od = pl.pallas_call(
        functools.partial(_attn_lat_kernel, lc=lc, lam_init=lam_init),
        grid=(n_b, HC, per),
        in_specs=small
        + [pl.BlockSpec((tq, LANES), lambda b, h, i: ((b + 1) * per + i, cq + h)),
           pl.BlockSpec((lc, LANES), lambda b, h, i: (b, ck + h)),
           pl.BlockSpec((l, LANES), lambda b, h, i: (b + 1, ck + h)),
           pl.BlockSpec((lc, LANES), lambda b, h, i: (b, cv + h)),
           pl.BlockSpec((l, LANES), lambda b, h, i: (b + 1, cv + h)),
           tab_q, tab_q, tab_k, tab_k],
        out_specs=pl.BlockSpec((tq, LANES), lambda b, h, i: (b * per + i, h)),
        out_shape=jax.ShapeDtypeStruct((n_b * l, HC * DVC), BF16),
        scratch_shapes=[pltpu.VMEM((lc + l, LANES), BF16), pltpu.VMEM((lc + l, LANES), BF16)],
        compiler_params=_cparams(3),
        name="diff_attention",
    )(lam, gn.reshape(1, DVC), proj, proj, proj, proj, proj, rope_cos, rope_sin, rope_cos, rope_sin)
    if not ctx_out:
        return None, od
    od_ctx = pl.pallas_call(
        functools.partial(_attn_ctx_kernel, lam_init=lam_init),
        grid=(n_b, HC),
        in_specs=small
        + [pl.BlockSpec((lc, LANES), lambda b, h: (b, cq + h)),
           pl.BlockSpec((lc, LANES), lambda b, h: (b, ck + h)),
           pl.BlockSpec((lc, LANES), lambda b, h: (b, cv + h))],
        out_specs=pl.BlockSpec((lc, LANES), lambda b, h: (b, h)),
        out_shape=jax.ShapeDtypeStruct((n_b * lc, HC * DVC), BF16),
        compiler_params=_cparams(2),
        name="diff_attention_ctx",
    )(lam, gn.reshape(1, DVC), proj, proj, proj)
    return od_ctx, od


def _rope_tables(n_tokens):
    rows = n_tokens // GRID_W
    row = jnp.repeat(jnp.arange(rows, dtype=F32), GRID_W)
    col = jnp.tile(jnp.arange(GRID_W, dtype=F32), rows)
    n_freq = DKC // 4
    inv_freq = ROPE_THETA ** (-jnp.arange(n_freq, dtype=F32) / n_freq)
    ang_r = row[:, None] * inv_freq
    ang_c = col[:, None] * inv_freq
    cos = jnp.concatenate([jnp.cos(ang_r)] * 2 + [jnp.cos(ang_c)] * 2, axis=1)
    sin = jnp.concatenate([-jnp.sin(ang_r), jnp.sin(ang_r), -jnp.sin(ang_c), jnp.sin(ang_c)], axis=1)
    return jnp.concatenate([cos, cos], axis=1), jnp.concatenate([sin, sin], axis=1)


def _head_rms(o, gain, n_heads, width):
    parts = []
    for h in range(n_heads):
        parts.append(_rms_rows(o[:, h * width:(h + 1) * width], gain))
    return jnp.concatenate(parts, axis=1)


def _merge_kernel(*refs, ctx_tiles, n_tok, n_od):
    x_refs, od_refs = refs[:n_tok], refs[n_tok:n_tok + n_od]
    (oa0, oa1, z_ref, ob0, ob1, r_ref, ga_ref, gb_ref, gd_ref, mod_ref,
     gna_ref, gnb_ref, woa_ref, wob_ref, woc_ref, wout_ref, o_ref) = refs[n_tok + n_od:]
    f32 = lambda ref: ref[...].astype(F32)
    ya = _head_rms(f32(oa0) + f32(oa1), gna_ref[...], HA, DVA) * _silu(f32(z_ref))
    yb = _head_rms(f32(ob0) + f32(ob1), gnb_ref[...], HB, DVB) * _silu(f32(r_ref))
    acc = _sigmoid(f32(ga_ref)) * _dot(ya.astype(BF16), woa_ref[...])
    acc = acc + _sigmoid(f32(gb_ref)) * _dot(yb.astype(BF16), wob_ref[...])
    acc = acc + _sigmoid(f32(gd_ref)) * _dot(_token_tile(od_refs, ctx_tiles), woc_ref[...])
    y = _dot(acc.astype(BF16), wout_ref[...])
    o_ref[...] = _token_tile(x_refs, ctx_tiles) + mod_ref[0, 2:3, :] * y


def _merge(oa0, oa1, ob0, ob1, od_ctx, od_lat, proj, x_ctx, x_lat, lat_first, t, mod, gn_a, gn_b,
           woa, wob, woc, wout, rows_per_mod, row_off, tm):
    d = x_ctx.shape[1]
    off = row_off // tm
    ctx_tiles = rows_per_mod // tm
    n_tiles = (t - row_off) // tm
    per = rows_per_mod // tm
    w5 = HA * DVA
    row = lambda i: (i + off, 0)
    col = lambda c: (lambda i: (i + off, c))
    tok_specs, tok_args = _token_inputs(x_ctx, x_lat, lat_first, tm, ctx_tiles, off)
    if od_ctx is None:
        assert off == ctx_tiles
        od_specs, od_args = [pl.BlockSpec((tm, w5), lambda i: (i, 0))], [od_lat]
    else:
        od_specs, od_args = _token_specs(tm, w5, ctx_tiles, 0, off), [od_ctx, od_lat]
    return pl.pallas_call(
        functools.partial(_merge_kernel, ctx_tiles=ctx_tiles - off, n_tok=len(tok_args), n_od=len(od_args)),
        grid=(n_tiles,),
        in_specs=tok_specs + od_specs
        + [pl.BlockSpec((tm, w5), row),
                  pl.BlockSpec((tm, w5), row),
                  pl.BlockSpec((tm, w5), col(H_AZ // w5)),
                  pl.BlockSpec((tm, w5), row),
                  pl.BlockSpec((tm, w5), row),
                  pl.BlockSpec((tm, w5), col(H_BR // w5)),
                  pl.BlockSpec((tm, d), col(H_GATES // d)),
                  pl.BlockSpec((tm, d), col(H_GATES // d + 1)),
                  pl.BlockSpec((tm, d), col(H_GATES // d + 2)),
                  pl.BlockSpec((1, 6, d), lambda i: ((i + off) // per, 0, 0)),
                  pl.BlockSpec((1, DVA), lambda i: (0, 0)),
                  pl.BlockSpec((1, DVB), lambda i: (0, 0)),
                  pl.BlockSpec((w5, d), lambda i: (0, 0)),
                  pl.BlockSpec((w5, d), lambda i: (0, 0)),
                  pl.BlockSpec((w5, d), lambda i: (0, 0)),
                  pl.BlockSpec((d, d), lambda i: (0, 0))],
        out_specs=pl.BlockSpec((tm, d), lambda i: (i, 0)),
        out_shape=jax.ShapeDtypeStruct((t - row_off, d), F32),
        compiler_params=_cparams(1),
        name="merge_outproj",
    )(*tok_args, *od_args, oa0, oa1, proj, ob0, ob1, proj, proj, proj, proj, mod,
      gn_a.reshape(1, DVA), gn_b.reshape(1, DVB), woa, wob, woc, wout)


def _ffn_kernel(x_ref, mod_ref, g_ref, fg_ref, w1_ref, w3_ref, w2_ref, o_ref, h_scr, *, final):
    f = pl.program_id(1)

    @pl.when(f == 0)
    def _():
        y = _rms_rows(x_ref[...], g_ref[...])
        h_scr[...] = (y * (1.0 + mod_ref[0, 4:5, :]) + mod_ref[0, 3:4, :]).astype(BF16)
        o_ref[...] = jnp.zeros_like(o_ref)

    h = h_scr[...]
    t = _silu(_dot(h, w1_ref[...])) * _dot(h, w3_ref[...])
    o_ref[...] += _dot(t.astype(BF16), w2_ref[...])

    @pl.when(f == pl.num_programs(1) - 1)
    def _():
        out = x_ref[...] + mod_ref[0, 5:6, :] * o_ref[...]
        if final:
            out = _rms_rows(out, fg_ref[...])
        o_ref[...] = out


def _ffn(x, mod, g, final_g, w1, w3, w2, rows_per_mod, mod_off, tm, tf, final):
    t, d = x.shape
    dff = w1.shape[1]
    per = rows_per_mod // tm
    return pl.pallas_call(
        functools.partial(_ffn_kernel, final=final),
        grid=(t // tm, dff // tf),
        in_specs=[pl.BlockSpec((tm, d), lambda i, f: (i, 0)),
                  pl.BlockSpec((1, 6, d), lambda i, f: (i // per + mod_off, 0, 0)),
                  pl.BlockSpec((1, d), lambda i, f: (0, 0)),
                  pl.BlockSpec((1, d), lambda i, f: (0, 0)),
                  pl.BlockSpec((d, tf), lambda i, f: (0, f)),
                  pl.BlockSpec((d, tf), lambda i, f: (0, f)),
                  pl.BlockSpec((tf, d), lambda i, f: (f, 0))],
        out_specs=pl.BlockSpec((tm, d), lambda i, f: (i, 0)),
        out_shape=jax.ShapeDtypeStruct((t, d), F32),
        scratch_shapes=[pltpu.VMEM((tm, d), BF16)],
        compiler_params=_cparams(2),
        name="dense_ffn",
    )(x, mod, g.reshape(1, d), final_g.reshape(1, d), w1, w3, w2)


MOE_ROWS = 256
MOE_MOVE_UNIT = 128
MOE_FFN_UNIT = 128


def _moe_kernel(x_ref, mod_ref, g_ref, rw_ref, tri_ref, fg_ref, w1_ref, w3_ref, w2_ref, o_ref,
                h_scr, gate_scr, sel_scr, rank_scr, selr_scr, rankr_scr, xg_scr, yg_scr, nb_scr, *, final):
    e = pl.program_id(1)
    f = pl.program_id(2)
    n_f = pl.num_programs(2)
    tm, d = x_ref.shape
    cb = MOE_ROWS
    lane = lax.broadcasted_iota(jnp.int32, (1, LANES), 1)

    def for_blocks(body, slot, unit):
        units = nb_scr[slot]
        per = cb // unit
        nb = units // per

        def full(j, carry):
            body(pl.multiple_of(j * cb, cb), cb)
            return carry

        lax.fori_loop(0, nb, full, 0)
        rem = units - nb * per
        r0 = nb * cb
        size = cb // 2
        while size >= unit:
            has = (rem // (size // unit)) % 2 == 1

            @pl.when(has)
            def _():
                body(pl.multiple_of(r0, unit), size)

            r0 = r0 + jnp.where(has, size, 0)
            size //= 2

    @pl.when((e == 0) & (f == 0))
    def _():
        y = _rms_rows(x_ref[...], g_ref[...])
        hmod = y * (1.0 + mod_ref[0, 4:5, :]) + mod_ref[0, 3:4, :]
        h_scr[...] = hmod.astype(BF16)
        o_ref[...] = jnp.zeros_like(o_ref)
        logits = _dot3(hmod, rw_ref[...])
        logits = jnp.where(lane < N_EXPERTS, logits, -jnp.inf)
        lanef = lane.astype(F32)
        m1 = jnp.max(logits, axis=1, keepdims=True)
        i1 = jnp.min(jnp.where(logits == m1, lanef, float(LANES)), axis=1, keepdims=True)
        hit1 = lanef == i1
        rest = jnp.where(hit1, -jnp.inf, logits)
        m2 = jnp.max(rest, axis=1, keepdims=True)
        i2 = jnp.min(jnp.where(rest == m2, lanef, float(LANES)), axis=1, keepdims=True)
        hit2 = lanef == i2
        ex = jnp.exp(m2 - m1)
        inv = 1.0 / (1.0 + ex)
        gate_scr[...] = jnp.where(hit1, inv, 0.0) + jnp.where(hit2, ex * inv, 0.0)
        sel = jnp.where(hit1 | hit2, 1.0, 0.0)
        sel_scr[...] = sel
        sel16 = sel.astype(BF16)
        rank = _dot(tri_ref[...], sel16)
        rank_scr[...] = rank
        pick = jnp.where(lax.broadcasted_iota(jnp.int32, (8, LANES), 0)
                         == lax.broadcasted_iota(jnp.int32, (8, LANES), 1), 1.0, 0.0).astype(BF16)
        selr_scr[...] = _dot_nt(pick, sel16)
        hi = jnp.floor(rank * (1.0 / 32.0))
        lo = rank - 32.0 * hi
        rankr_scr[...] = 32.0 * _dot_nt(pick, hi.astype(BF16)) + _dot_nt(pick, lo.astype(BF16))

    @pl.when(f == 0)
    def _():
        sel_row = selr_scr[pl.ds(e, 1), :]
        rank_row = rankr_scr[pl.ds(e, 1), :]
        n_e = jnp.sum(sel_row).astype(jnp.int32)
        nb_scr[0] = (n_e + (MOE_MOVE_UNIT - 1)) // MOE_MOVE_UNIT
        nb_scr[1] = (n_e + (MOE_FFN_UNIT - 1)) // MOE_FFN_UNIT

        def gather(r0, bs):
            slot = lax.broadcasted_iota(jnp.int32, (bs, tm), 0).astype(F32) + r0.astype(F32)
            onehot = jnp.where((rank_row == slot) & (sel_row > 0.0), 1.0, 0.0)
            xg_scr[pl.ds(r0, bs), :] = _dot(onehot.astype(BF16), h_scr[...]).astype(BF16)
            yg_scr[pl.ds(r0, bs), :] = jnp.zeros((bs, d), F32)

        for_blocks(gather, 0, MOE_MOVE_UNIT)

    def expert(r0, bs):
        xb = xg_scr[pl.ds(r0, bs), :]
        t = _silu(_dot(xb, w1_ref[0])) * _dot(xb, w3_ref[0])
        yg_scr[pl.ds(r0, bs), :] += _dot(t.astype(BF16), w2_ref[0])

    for_blocks(expert, 1, MOE_FFN_UNIT)

    @pl.when(f == n_f - 1)
    def _():
        pick_e = lane == e
        rank_col = jnp.sum(jnp.where(pick_e, rank_scr[...], 0.0), axis=1, keepdims=True)
        sel_col = jnp.sum(jnp.where(pick_e, sel_scr[...], 0.0), axis=1, keepdims=True)
        gate_col = jnp.sum(jnp.where(pick_e, gate_scr[...], 0.0), axis=1, keepdims=True)

        def scatter(r0, bs):
            slot = lax.broadcasted_iota(jnp.int32, (tm, bs), 1).astype(F32) + r0.astype(F32)
            onehot = jnp.where((rank_col == slot) & (sel_col > 0.0), 1.0, 0.0)
            o_ref[...] += gate_col * _dot(onehot.astype(BF16), yg_scr[pl.ds(r0, bs), :].astype(BF16))

        for_blocks(scatter, 0, MOE_MOVE_UNIT)

    @pl.when((e == pl.num_programs(1) - 1) & (f == n_f - 1))
    def _():
        out = x_ref[...] + mod_ref[0, 5:6, :] * o_ref[...]
        if final:
            out = _rms_rows(out, fg_ref[...])
        o_ref[...] = out


def _moe(x, mod, g, router_w, final_g, w1, w3, w2, rows_per_mod, mod_off, tm, tf, final):
    t, d = x.shape
    n_e, _, dff = w1.shape
    per = rows_per_mod // tm
    return pl.pallas_call(
        functools.partial(_moe_kernel, final=final),
        grid=(t // tm, n_e, dff // tf),
        in_specs=[pl.BlockSpec((tm, d), lambda i, e, f: (i, 0)),
                  pl.BlockSpec((1, 6, d), lambda i, e, f: (i // per + mod_off, 0, 0)),
                  pl.BlockSpec((1, d), lambda i, e, f: (0, 0)),
                  pl.BlockSpec((d, LANES), lambda i, e, f: (0, 0)),
                  pl.BlockSpec((tm, tm), lambda i, e, f: (0, 0)),
                  pl.BlockSpec((1, d), lambda i, e, f: (0, 0)),
                  pl.BlockSpec((1, d, tf), lambda i, e, f: (e, 0, f)),
                  pl.BlockSpec((1, d, tf), lambda i, e, f: (e, 0, f)),
                  pl.BlockSpec((1, tf, d), lambda i, e, f: (e, f, 0))],
        out_specs=pl.BlockSpec((tm, d), lambda i, e, f: (i, 0)),
        out_shape=jax.ShapeDtypeStruct((t, d), F32),
        scratch_shapes=[pltpu.VMEM((tm, d), BF16),
                        pltpu.VMEM((tm, LANES), F32),
                        pltpu.VMEM((tm, LANES), F32),
                        pltpu.VMEM((tm, LANES), F32),
                        pltpu.VMEM((8, tm), F32),
                        pltpu.VMEM((8, tm), F32),
                        pltpu.VMEM((tm, d), BF16),
                        pltpu.VMEM((tm, d), F32),
                        pltpu.SMEM((2,), jnp.int32)],
        compiler_params=_cparams(3),
        name="routed_ffn_sparse",
    )(x, mod, g.reshape(1, d), router_w, jnp.tril(jnp.ones((tm, tm), BF16), -1),
      final_g.reshape(1, d), w1, w3, w2)


def _w_in_pieces(end):
    a_qkv = 2 * HA * DKA + HA * DVA
    a_z = a_qkv + HA * DVA
    b_q = a_z + 4 * HA
    b_v = b_q + 2 * HB * DKB
    b_glr = b_v + 2 * HB * DVB
    c_q = b_glr + 2 * GATE_RANK
    gates = c_q + 4 * HC * DKC + HC * DVC
    return [(0, a_qkv), (a_z, b_q), (b_glr, c_q), None,
            (gates, end), (a_qkv, a_z), (b_v, b_glr), (c_q, gates), (b_q, b_v)]


def _reorder_kernel(w_ref, o_ref):
    at = 0
    for piece in _w_in_pieces(w_ref.shape[1]):
        if piece is None:
            width = N_F32 - at
            o_ref[0, at:at + width, :] = jnp.zeros((width, o_ref.shape[2]), BF16)
        else:
            width = piece[1] - piece[0]
            o_ref[0, at:at + width, :] = w_ref[0, piece[0]:piece[1], :].astype(BF16)
        at += width
    assert at == N_F32 + N_B16


def _reorder_w_in(w_in):
    w_t = jnp.swapaxes(w_in, 1, 2)
    depth, cols, d = w_t.shape
    kb = LANES
    return pl.pallas_call(
        _reorder_kernel,
        grid=(depth, d // kb),
        in_specs=[pl.BlockSpec((1, cols, kb), lambda l, i: (l, 0, i))],
        out_specs=pl.BlockSpec((1, N_F32 + N_B16, kb), lambda l, i: (l, 0, i)),
        out_shape=jax.ShapeDtypeStruct((depth, N_F32 + N_B16, d), BF16),
        compiler_params=_cparams(2),
        name="reorder_w_in",
    )(w_t)


def kernel(x, c, ctx, c_ctx, w_mod, b_mod, norm1_g, norm2_g, w_in, conv_a, a_log, dt_bias, gn_a, w_gate2, b_gate, gn_b, lam_c, gn_c, w_o_a, w_o_b, w_o_c, w_out, ffn_w1, ffn_w3, ffn_w2, router_w, moe_w1, moe_w3, moe_w2, final_g):
    n_b, l, d = x.shape
    lc = ctx.shape[1]
    depth = w_mod.shape[0]
    assert n_b * lc == l and lc % CHUNK == 0 and l & (l - 1) == 0 and lc & (lc - 1) == 0
    t_ctx = n_b * lc
    tm = min(1024, l)
    tm_small = min(512, l)

    n_mod = 16
    cvec = jnp.concatenate([c_ctx[None, :], c, jnp.zeros((n_mod - 1 - n_b, d), F32)], axis=0)
    mod_all = _modulation(cvec, w_mod, b_mod).reshape(depth, n_mod, 6, d)

    rope_cos, rope_sin = _rope_tables(l)
    w_in16 = _reorder_w_in(w_in)
    t_all = t_ctx + n_b * l
    tokens = (ctx.reshape(t_ctx, d), x.reshape(n_b * l, d), 0)

    for layer in range(depth):
        ctx_out = layer < depth - 1
        lam_init = 0.8 - 0.6 * math.exp(-0.3 * layer)
        mod = mod_all[layer]
        proj32, proj16 = _in_projection(*tokens, t_all, mod, norm1_g[layer], w_in16, layer, l, tm)

        qkvc = _short_conv(proj32, conv_a[layer], lc, l)
        oa0, oa1, ob0, ob1 = _recurrent_branches(qkvc, proj32, proj16, a_log[layer], dt_bias[layer],
                                                 w_gate2[layer], b_gate[layer], n_b, lc, l)
        od = _attention(proj16, lam_c[layer], gn_c[layer], rope_cos, rope_sin, n_b, lc, l, lam_init, ctx_out)

        row_off = 0 if ctx_out else t_ctx
        xs_new = _merge(oa0, oa1, ob0, ob1, *od, proj16, *tokens, t_all, mod, gn_a[layer], gn_b[layer],
                        w_o_a[layer].astype(BF16), w_o_b[layer].astype(BF16), w_o_c[layer].astype(BF16),
                        w_out[layer].astype(BF16), l, row_off, tm_small)

        i = layer // 2
        final = layer == depth - 1
        mod_off = 0 if ctx_out else 1
        if layer % 2 == 0:
            xs = _ffn(xs_new, mod, norm2_g[layer], final_g, ffn_w1[i].astype(BF16),
                      ffn_w3[i].astype(BF16), ffn_w2[i].astype(BF16), l, mod_off, tm,
                      ffn_w1.shape[2] // 2, final)
        else:
            rw = jnp.pad(router_w[i], ((0, 0), (0, LANES - N_EXPERTS)))
            xs = _moe(xs_new, mod, norm2_g[layer], rw, final_g, moe_w1[i].astype(BF16),
                      moe_w3[i].astype(BF16), moe_w2[i].astype(BF16), l, mod_off, tm,
                      moe_w1.shape[3] // 2, final)
        tokens = (xs, xs, t_ctx)

    return xs.reshape(n_b, l, d)
```

```python
import functools
import math

import jax
import jax.numpy as jnp
from jax import lax
from jax.experimental import pallas as pl
from jax.experimental.pallas import tpu as pltpu

F32 = jnp.float32
BF16 = jnp.bfloat16
HIGHEST = lax.Precision.HIGHEST

EPS = 1e-6
HA, DKA, DVA = 4, 128, 128
HB, DKB, DVB = 4, 64, 128
GATE_RANK, GATE_NORM = 16, 16.0
HC, DKC, DVC = 4, 64, 128
GRID_W, ROPE_THETA = 64, 10000.0
N_EXPERTS, TOP_K = 8, 2
CHUNK = 64
LANES = 128
ATTN_SUB = 64
ATTN_TQ = 512
CONV_HEADS = 4

F_QKV = 0
F_SMALL = 1536
N_F32 = 1664
H_GATES = 0
H_AZ = 3072
H_BV = 3584
H_BR = 4096
H_CQ = 4608
H_CK = 5120
H_CV = 5632
H_BQ = 6144
H_BK = 6400
N_B16 = 6656
PROJ_TN = 1664

VMEM_LIMIT = 56 * 1024 * 1024


def _cparams(n_axes):
    return pltpu.CompilerParams(dimension_semantics=("arbitrary",) * n_axes,
                                vmem_limit_bytes=VMEM_LIMIT)


def _sigmoid(x):
    return 1.0 / (1.0 + jnp.exp2(x * -math.log2(math.e)))


def _silu(x):
    return x * _sigmoid(x)


def _softplus(x):
    return jnp.maximum(x, 0.0) + jnp.log(1.0 + jnp.exp(-jnp.abs(x)))


def _dot(a, b):
    return jnp.dot(a, b, preferred_element_type=F32)


def _dot_nt(a, b):
    return lax.dot_general(a, b, (((1,), (1,)), ((), ())), preferred_element_type=F32)


def _dot_tn(a, b):
    return lax.dot_general(a, b, (((0,), (0,)), ((), ())), preferred_element_type=F32)


def _split(a):
    hi = a.astype(BF16)
    lo = (a - hi.astype(F32)).astype(BF16)
    return hi, lo


def _dot3(a, b):
    ah, al = _split(a)
    bh, bl = _split(b)
    return _dot(ah, bh) + (_dot(ah, bl) + _dot(al, bh))


def _rms_rows(x, gain):
    ms = jnp.mean(x * x, axis=-1, keepdims=True)
    return x * lax.rsqrt(ms + EPS) * gain


def _mod_kernel(c_ref, w_ref, b_ref, o_ref):
    s = _silu(c_ref[...])
    o_ref[0] = jnp.dot(s, w_ref[0], precision=HIGHEST, preferred_element_type=F32) + b_ref[0]


def _modulation(cvec, w_mod, b_mod):
    depth, d, d6 = w_mod.shape
    n = cvec.shape[0]
    return pl.pallas_call(
        _mod_kernel,
        grid=(depth, d6 // d),
        in_specs=[pl.BlockSpec((n, d), lambda l, j: (0, 0)),
                  pl.BlockSpec((1, d, d), lambda l, j: (l, 0, j)),
                  pl.BlockSpec((1, 1, d), lambda l, j: (l, 0, j))],
        out_specs=pl.BlockSpec((1, n, d), lambda l, j: (l, 0, j)),
        out_shape=jax.ShapeDtypeStruct((depth, n, d6), F32),
        compiler_params=_cparams(2),
        name="adaln_mod",
    )(cvec, w_mod, b_mod.reshape(depth, 1, d6))


def _token_specs(tm, d, ctx_tiles, lat_first, off=0):
    return [pl.BlockSpec((tm, d), lambda i, *_: (jnp.minimum(i + off, ctx_tiles - 1), 0)),
            pl.BlockSpec((tm, d), lambda i, *_: (jnp.maximum(i + off - ctx_tiles, 0) + lat_first, 0))]


def _token_inputs(x_ctx, x_lat, lat_first, tm, ctx_tiles, off=0):
    d = x_ctx.shape[1]
    if x_lat is x_ctx:
        return [pl.BlockSpec((tm, d), lambda i, *_: (i + off, 0))], [x_ctx]
    return _token_specs(tm, d, ctx_tiles, lat_first // tm, off), [x_ctx, x_lat]


def _token_tile(x_refs, ctx_tiles):
    if len(x_refs) == 1:
        return x_refs[0][...]
    return jnp.where(pl.program_id(0) < ctx_tiles, x_refs[0][...], x_refs[1][...])


def _inproj_kernel(*refs, ctx_tiles, n_tok):
    x_refs, (mod_ref, g_ref, w_ref, o32_ref, o16_ref, h_scr) = refs[:n_tok], refs[n_tok:]
    j = pl.program_id(1)

    @pl.when(j == 0)
    def _():
        y = _rms_rows(_token_tile(x_refs, ctx_tiles), g_ref[...])
        h_scr[...] = (y * (1.0 + mod_ref[0, 1:2, :]) + mod_ref[0, 0:1, :]).astype(BF16)
        o32_ref[...] = _dot_nt(h_scr[...], w_ref[0])

    @pl.when(j > 0)
    def _():
        o16_ref[...] = _dot_nt(h_scr[...], w_ref[0]).astype(BF16)


def _in_projection(x_ctx, x_lat, lat_first, t, mod, g, w, layer, rows_per_mod, tm):
    d = x_ctx.shape[1]
    tn = PROJ_TN
    assert w.shape[1] == N_F32 + N_B16 and N_F32 == tn and N_B16 % tn == 0
    per = rows_per_mod // tm
    ctx_tiles = rows_per_mod // tm
    tok_specs, tok_args = _token_inputs(x_ctx, x_lat, lat_first, tm, ctx_tiles)
    return pl.pallas_call(
        functools.partial(_inproj_kernel, ctx_tiles=ctx_tiles, n_tok=len(tok_args)),
        grid=(t // tm, 1 + N_B16 // tn),
        in_specs=tok_specs
        + [pl.BlockSpec((1, 6, d), lambda i, j: (i // per, 0, 0)),
           pl.BlockSpec((1, d), lambda i, j: (0, 0)),
           pl.BlockSpec((1, tn, d), lambda i, j: (layer, j, 0))],
        out_specs=[pl.BlockSpec((tm, tn), lambda i, j: (i, 0)),
                   pl.BlockSpec((tm, tn), lambda i, j: (i, jnp.maximum(j - 1, 0)))],
        out_shape=[jax.ShapeDtypeStruct((t, N_F32), F32), jax.ShapeDtypeStruct((t, N_B16), BF16)],
        scratch_shapes=[pltpu.VMEM((tm, d), BF16)],
        compiler_params=_cparams(2),
        name="norm_mod_inproj",
    )(*tok_args, mod, g.reshape(1, d), w)


def _conv_kernel(x_ref, w_ref, o_ref, *, lc, l):
    p = pl.program_id(0)
    j = pl.program_id(1)
    x = x_ref[...]
    n = x.shape[0]
    row = lax.broadcasted_iota(jnp.int32, x.shape, 0)
    seq = jnp.where(p == 0, lc, l)
    pos = row & (seq - 1)
    prev = jnp.where(pos == 0, 0.0, pltpu.roll(x, 1, 0))
    nxt = jnp.where(pos == seq - 1, 0.0, pltpu.roll(x, n - 1, 0))
    w = w_ref[...]
    y = _silu(prev * w[0:1] + x * w[1:2] + nxt * w[2:3])
    heads_per_block = x.shape[1] // DKA
    for hh in range(heads_per_block):
        cols = slice(hh * DKA, (hh + 1) * DKA)
        yh = y[:, cols]
        inv = lax.rsqrt(jnp.sum(yh * yh, axis=-1, keepdims=True) + EPS)
        head = j * heads_per_block + hh
        fac = jnp.where(head < HA, inv * DKA ** -0.5, jnp.where(head < 2 * HA, inv, 1.0))
        o_ref[:, cols] = yh * fac


def _short_conv(proj, conv_w, lc, l):
    t = proj.shape[0]
    ncol = conv_w.shape[1]
    wide = CONV_HEADS * DKA
    return pl.pallas_call(
        functools.partial(_conv_kernel, lc=lc, l=l),
        grid=(t // l, ncol // wide),
        in_specs=[pl.BlockSpec((l, wide), lambda p, j: (p, j)),
                  pl.BlockSpec((conv_w.shape[0], wide), lambda p, j: (0, j))],
        out_specs=pl.BlockSpec((l, wide), lambda p, j: (p, j)),
        out_shape=jax.ShapeDtypeStruct((t, ncol), F32),
        compiler_params=_cparams(2),
        name="conv_silu_l2",
    )(proj, conv_w)


def _row_block(b, g, n_b, n_lat):
    return jnp.where(g == 0, b, n_b + b * n_lat + g - 1)


def _rev_group(g, n_g):
    return jnp.where(g == 0, 0, n_g - g)


def _tri_inverse_many(lms, ii, jj, uppers):
    c = lms[0].shape[0]
    eye = (ii == jj).astype(F32)
    n = len(lms)
    ts = None
    s, ls = 1, 0
    while s < c:
        same = (ii >> (ls + 1)) == (jj >> (ls + 1))
        io = (ii >> ls) & 1
        jo = (jj >> ls) & 1
        lower = same & (io == 1) & (jo == 0)
        upper = same & (io == 0) & (jo == 1)
        cms = [jnp.where(upper if uppers[i] else lower, lms[i], 0.0) for i in range(n)]
        if ts is None:
            ts = [eye - cm for cm in cms]
        else:
            t16 = [t.astype(BF16) for t in ts]
            ps = [_dot(t16[i], cms[i].astype(BF16)).astype(BF16) for i in range(n)]
            ts = [ts[i] - _dot(ps[i], t16[i]) for i in range(n)]
        s, ls = s * 2, ls + 1
    a_parts = [_split(eye + lm) for lm in lms]
    t_parts = [_split(t) for t in ts]
    res = [eye - (_dot(a_parts[i][0], t_parts[i][0])
                  + (_dot(a_parts[i][0], t_parts[i][1]) + _dot(a_parts[i][1], t_parts[i][0])))
           for i in range(n)]
    return [ts[i] + _dot(t_parts[i][0], res[i].astype(BF16)) for i in range(n)]


def _dn_kernel(alog_ref, dtb_ref, q0, k0, v0, sm0, q1, k1, v1, sm1, o0_ref, o1_ref, s_scr, *, n_chunks):
    @pl.when(pl.program_id(1) == 0)
    def _():
        s_scr[...] = jnp.zeros_like(s_scr)

    c_ = CHUNK
    refs = ((q0, k0, v0, o0_ref), (q1, k1, v1, o1_ref))
    sms = (sm0[...], sm1[...])
    sm_ts = tuple(s.T for s in sms)
    ii = lax.broadcasted_iota(jnp.int32, (c_, c_), 0)
    jj = lax.broadcasted_iota(jnp.int32, (c_, c_), 1)
    rows = [slice(c * c_, (c + 1) * c_) for c in range(n_chunks)]
    lanes = [slice(j * LANES, (j + 1) * LANES) for j in range(HA)]
    items = [(j, c, d) for j in range(HA) for c in range(n_chunks) for d in range(2)]
    qs = [refs[d][0][rows[c], lanes[j]] for j, c, d in items]
    ks = [refs[d][1][rows[c], lanes[j]] for j, c, d in items]
    vs = [refs[d][2][rows[c], lanes[j]] for j, c, d in items]
    k16 = [k.astype(BF16) for k in ks]
    kks = [_dot_nt(k, k) for k in k16]
    qks = [_dot_nt(qs[i].astype(BF16), k16[i]) for i in range(len(items))]

    betas, gcs, totals, gams, lms = [], [], [], [], []
    for i, (j, c, d) in enumerate(items):
        r = rows[c]
        incl = (jj <= ii) if d == 0 else (jj >= ii)
        strict = (jj < ii) if d == 0 else (jj > ii)
        incl_t = (ii <= jj) if d == 0 else (ii >= jj)
        a_scale = -jnp.exp(jnp.full((1, 1), alog_ref[d * HA + j], F32))
        dtb = dtb_ref[d * HA + j]
        col = d * HA + j
        beta = _sigmoid(sms[d][r, col:col + 1])
        g_col = a_scale * _softplus(sms[d][r, 2 * HA + col:2 * HA + col + 1] + dtb)
        g_row = a_scale * _softplus(sm_ts[d][2 * HA + col:2 * HA + col + 1, r] + dtb)
        gc_col = jnp.sum(jnp.where(incl, g_row, 0.0), axis=1, keepdims=True)
        gc_row = jnp.sum(jnp.where(incl_t, g_col, 0.0), axis=0, keepdims=True)
        gam = jnp.where(incl, jnp.exp(gc_col - gc_row), 0.0)
        betas.append(beta)
        gcs.append(gc_col)
        totals.append(jnp.sum(g_col, axis=0, keepdims=True))
        gams.append(gam)
        lms.append(jnp.where(strict, beta * kks[i] * gam, 0.0))

    tinvs = _tri_inverse_many(lms, ii, jj, [d == 1 for _, _, d in items])
    egs = [jnp.exp(gc) for gc in gcs]
    uws = []
    for i in range(len(items)):
        rhs = jnp.concatenate([vs[i] * betas[i], ks[i] * (betas[i] * egs[i])], axis=1).astype(BF16)
        uws.append(_dot(tinvs[i].astype(BF16), rhs))
    us = [uw[:, :DVA] for uw in uws]
    wq16 = [jnp.concatenate([uws[i][:, DVA:], qs[i] * egs[i]], axis=0).astype(BF16) for i in range(len(items))]
    kg16 = [(ks[i] * jnp.exp(totals[i] - gcs[i])).astype(BF16) for i in range(len(items))]
    qkg16 = [(qks[i] * gams[i]).astype(BF16) for i in range(len(items))]
    carry = [jnp.exp(t) for t in totals]

    index = {item: i for i, item in enumerate(items)}
    chains = [(j, d) for j in range(HA) for d in range(2)]
    states = [s_scr[d * HA + j] for j, d in chains]
    for step in range(n_chunks):
        at = [index[(j, step if d == 0 else n_chunks - 1 - step, d)] for j, d in chains]
        s16 = [s.astype(BF16) for s in states]
        prods = [_dot(wq16[i], s16[n]) for n, i in enumerate(at)]
        vn16 = [(us[i] - prods[n][:c_]).astype(BF16) for n, i in enumerate(at)]
        for n, i in enumerate(at):
            j, c, d = items[i]
            refs[d][3][rows[c], lanes[j]] = (prods[n][c_:] + _dot(qkg16[i], vn16[n])).astype(BF16)
            states[n] = carry[i] * states[n] + _dot_tn(kg16[i], vn16[n])
    for n, (j, d) in enumerate(chains):
        s_scr[d * HA + j] = states[n]


def _gla_kernel(q0, k0, v0, sm0, q1, k1, v1, sm1, wg_ref, bg_ref, o0_ref, o1_ref, s_scr, *, n_chunks):
    @pl.when(pl.program_id(1) == 0)
    def _():
        s_scr[...] = jnp.zeros_like(s_scr)

    c_ = CHUNK
    sh = c_.bit_length() - 1
    g_ = n_chunks * c_
    kw = HB * DKB
    vw = HB * DVB
    ii = lax.broadcasted_iota(jnp.int32, (g_, g_), 0)
    jj = lax.broadcasted_iota(jnp.int32, (g_, g_), 1)
    same_chunk = (ii >> sh) == (jj >> sh)
    klane = lax.broadcasted_iota(jnp.int32, (1, kw), 1)
    ind16 = jnp.where((lax.broadcasted_iota(jnp.int32, (g_, n_chunks * LANES), 0) >> sh)
                      == (lax.broadcasted_iota(jnp.int32, (g_, n_chunks * LANES), 1) >> (LANES.bit_length() - 1)),
                      1.0, 0.0).astype(BF16)
    rows = [slice(c * c_, (c + 1) * c_) for c in range(n_chunks)]
    refs = ((q0, k0, v0, sm0, o0_ref), (q1, k1, v1, sm1, o1_ref))

    def spread(x, picks):
        return jnp.concatenate([jnp.broadcast_to(x[p:p + 1, :], (c_, kw)) for p in picks], axis=0)

    dirs = (0, 1)
    incls = [same_chunk & ((jj <= ii) if d == 0 else (jj >= ii)) for d in dirs]
    tri16 = [jnp.where(m, 1.0, 0.0).astype(BF16) for m in incls]
    pres = [jnp.dot(refs[d][3][:, 4 * HA + d * GATE_RANK:4 * HA + (d + 1) * GATE_RANK], wg_ref[d],
                    precision=HIGHEST, preferred_element_type=F32) + bg_ref[d] for d in dirs]
    glogs = [(jnp.minimum(p, 0.0) - jnp.log(1.0 + jnp.exp(-jnp.abs(p)))) * (1.0 / GATE_NORM) for p in pres]
    parts = [_split(g) for g in glogs]
    gcs = [_dot(tri16[d], parts[d][0]) + _dot(tri16[d], parts[d][1]) for d in dirs]
    tot_cols = [_dot_tn(parts[d][0], ind16) + _dot_tn(parts[d][1], ind16) for d in dirs]
    mids = [spread(gcs[d], [c * c_ + c_ // 2 - 1 for c in range(n_chunks)]) for d in dirs]
    tots = [spread(gcs[d], [c * c_ + (c_ - 1 if d == 0 else 0) for c in range(n_chunks)]) for d in dirs]
    qs = [refs[d][0][...].astype(F32) * DKB ** -0.5 for d in dirs]
    ks = [refs[d][1][...].astype(F32) for d in dirs]
    v16s = [refs[d][2][...] for d in dirs]
    qts = [qs[d] * jnp.exp(gcs[d] - mids[d]) for d in dirs]
    kt16s = [(ks[d] * jnp.exp(mids[d] - gcs[d])).astype(BF16) for d in dirs]
    qg16s = [(qs[d] * jnp.exp(gcs[d])).astype(BF16) for d in dirs]
    kg16s = [(ks[d] * jnp.exp(tots[d] - gcs[d])).astype(BF16) for d in dirs]
    a16s = {}
    for h in range(HB):
        in_head = (klane >= h * DKB) & (klane < (h + 1) * DKB)
        for d in dirs:
            a = _dot_nt(jnp.where(in_head, qts[d], 0.0).astype(BF16), kt16s[d])
            a16s[(d, h)] = jnp.where(incls[d], a, 0.0).astype(BF16)
    o_intra = [[_dot(a16s[(d, h)], v16s[d][:, h * DVB:(h + 1) * DVB]) for h in range(HB)] for d in dirs]
    xs = [[_dot_tn(kg16s[d][r, :], v16s[d][r, :]) for r in rows] for d in dirs]
    decays = [[jnp.exp(tot_cols[d][:, c * LANES:(c + 1) * LANES]) for c in range(n_chunks)] for d in dirs]

    states = [[s_scr[d, h] for h in range(HB)] for d in dirs]
    zero16 = jnp.zeros((DKB, DVB), BF16)
    for step in range(n_chunks):
        for d in range(2):
            c = step if d == 0 else n_chunks - 1 - step
            r = rows[c]
            s16 = [s.astype(BF16) for s in states[d]]
            s_all = jnp.concatenate(
                [jnp.concatenate([s16[h] if hv == h else zero16 for hv in range(HB)], axis=1) for h in range(HB)],
                axis=0)
            o_inter = _dot(qg16s[d][r, :], s_all)
            for h in range(HB):
                vs = slice(h * DVB, (h + 1) * DVB)
                kr = slice(h * DKB, (h + 1) * DKB)
                refs[d][4][r, vs] = (o_inter[:, vs] + o_intra[d][h][r, :]).astype(BF16)
                states[d][h] = decays[d][c][kr, :] * states[d][h] + xs[d][c][kr, vs]
    for d in dirs:
        for h in range(HB):
            s_scr[d, h] = states[d][h]


def _recurrent_branches(qkvc, proj32, proj16, a_log, dt_bias, w_gate2, b_gate, n_b, lc, l):
    t = qkvc.shape[0]
    n_lat = l // lc
    n_g = n_lat + 1
    wide = HA * DVA
    kw, vw = HB * DKB, HB * DVB
    rb0 = lambda b, g: _row_block(b, g, n_b, n_lat)
    rb1 = lambda b, g: _row_block(b, _rev_group(g, n_g), n_b, n_lat)
    smem = pl.BlockSpec(memory_space=pltpu.SMEM)

    def dn_specs(rb):
        return [pl.BlockSpec((lc, wide), lambda b, g: (rb(b, g), 0)),
                pl.BlockSpec((lc, wide), lambda b, g: (rb(b, g), 1)),
                pl.BlockSpec((lc, wide), lambda b, g: (rb(b, g), 2)),
                pl.BlockSpec((lc, LANES), lambda b, g: (rb(b, g), F_SMALL // LANES))]

    def gla_specs(rb):
        return [pl.BlockSpec((lc, kw), lambda b, g: (rb(b, g), H_BQ // kw)),
                pl.BlockSpec((lc, kw), lambda b, g: (rb(b, g), H_BK // kw)),
                pl.BlockSpec((lc, vw), lambda b, g: (rb(b, g), H_BV // vw)),
                pl.BlockSpec((lc, LANES), lambda b, g: (rb(b, g), F_SMALL // LANES))]

    out = lambda rb, width: pl.BlockSpec((lc, width), lambda b, g: (rb(b, g), 0))
    oa0, oa1 = pl.pallas_call(
        functools.partial(_dn_kernel, n_chunks=lc // CHUNK),
        grid=(n_b, n_g),
        in_specs=[smem, smem] + dn_specs(rb0) + dn_specs(rb1),
        out_specs=[out(rb0, wide), out(rb1, wide)],
        out_shape=[jax.ShapeDtypeStruct((t, wide), BF16)] * 2,
        scratch_shapes=[pltpu.VMEM((2 * HA, DKA, DVA), F32)],
        compiler_params=_cparams(2),
        name="deltanet",
    )(a_log.reshape(-1), dt_bias.reshape(-1), qkvc, qkvc, qkvc, proj32, qkvc, qkvc, qkvc, proj32)
    ob0, ob1 = pl.pallas_call(
        functools.partial(_gla_kernel, n_chunks=lc // CHUNK),
        grid=(n_b, n_g),
        in_specs=gla_specs(rb0) + gla_specs(rb1)
        + [pl.BlockSpec((2, GATE_RANK, kw), lambda b, g: (0, 0, 0)),
           pl.BlockSpec((2, 1, kw), lambda b, g: (0, 0, 0))],
        out_specs=[out(rb0, vw), out(rb1, vw)],
        out_shape=[jax.ShapeDtypeStruct((t, vw), BF16)] * 2,
        scratch_shapes=[pltpu.VMEM((2, HB, DKB, DVB), F32)],
        compiler_params=_cparams(2),
        name="gla_bidir",
    )(proj16, proj16, proj16, proj32, proj16, proj16, proj16, proj32, w_gate2, b_gate.reshape(2, 1, kw))
    return oa0, oa1, ob0, ob1


def _lambda_full(lam, lam_init):
    return (jnp.exp(jnp.sum(lam[0:1] * lam[1:2], axis=1, keepdims=True))
            - jnp.exp(jnp.sum(lam[2:3] * lam[3:4], axis=1, keepdims=True)) + lam_init)


def _attend(q, keys, vals, lam_f, gain, lam_init, o_ref):
    lane = lax.broadcasted_iota(jnp.int32, (1, LANES), 1)
    n = ATTN_SUB
    q = q * (DKC ** -0.5 * math.log2(math.e))
    subs = [q[r * n:(r + 1) * n] for r in range(q.shape[0] // n)]
    q2s = [jnp.concatenate([jnp.where(lane < DKC, x, 0.0), jnp.where(lane >= DKC, x, 0.0)],
                           axis=0).astype(BF16) for x in subs]
    ss = [_dot_nt(q2, keys) for q2 in q2s]
    es = [jnp.exp2(s - jnp.max(s, axis=-1, keepdims=True)) for s in ss]
    invs = [1.0 / jnp.sum(e, axis=-1, keepdims=True) for e in es]
    o2s = [_dot(es[r].astype(BF16), vals) * invs[r] for r in range(len(es))]
    for r, o2 in enumerate(o2s):
        o = o2[:n] - lam_f * o2[n:]
        o_ref[r * n:(r + 1) * n, :] = (_rms_rows(o, gain) * (1.0 - lam_init)).astype(BF16)


def _attn_lat_kernel(lam_ref, gn_ref, q_ref, kc_ref, kl_ref, vc_ref, vl_ref, cq_ref, sq_ref, ck_ref, sk_ref,
                     o_ref, k_scr, v_scr, *, lc, lam_init):
    lane = lax.broadcasted_iota(jnp.int32, (1, LANES), 1)
    first_half = (lane & 31) < 16

    def rope(x, cos, sin):
        swapped = jnp.where(first_half, pltpu.roll(x, LANES - 16, 1), pltpu.roll(x, 16, 1))
        return x * cos + swapped * sin

    @pl.when(pl.program_id(2) == 0)
    def _():
        k_scr[0:lc, :] = kc_ref[...]
        k_scr[lc:, :] = rope(kl_ref[...].astype(F32), ck_ref[...], sk_ref[...]).astype(BF16)
        v_scr[0:lc, :] = vc_ref[...]
        v_scr[lc:, :] = vl_ref[...]

    _attend(rope(q_ref[...].astype(F32), cq_ref[...], sq_ref[...]), k_scr[...], v_scr[...],
            _lambda_full(lam_ref[...], lam_init), gn_ref[...], lam_init, o_ref)


def _attn_ctx_kernel(lam_ref, gn_ref, q_ref, k_ref, v_ref, o_ref, *, lam_init):
    _attend(q_ref[...].astype(F32), k_ref[...], v_ref[...], _lambda_full(lam_ref[...], lam_init),
            gn_ref[...], lam_init, o_ref)


def _attention(proj, lam, gn, rope_cos, rope_sin, n_b, lc, l, lam_init, ctx_out):
    tq = min(ATTN_TQ, l)
    per = l // tq
    cq, ck, cv = H_CQ // LANES, H_CK // LANES, H_CV // LANES
    small = [pl.BlockSpec((4, DKC), lambda b, h, *_: (0, 0)), pl.BlockSpec((1, DVC), lambda b, h, *_: (0, 0))]
    tab_q = pl.BlockSpec((tq, LANES), lambda b, h, i: (i, 0))
    tab_k = pl.BlockSpec((l, LANES), lambda b, h, i: (0, 0))
    od = pl.pallas_call(
        functools.partial(_attn_lat_kernel, lc=lc, lam_init=lam_init),
        grid=(n_b, HC, per),
        in_specs=small
        + [pl.BlockSpec((tq, LANES), lambda b, h, i: ((b + 1) * per + i, cq + h)),
           pl.BlockSpec((lc, LANES), lambda b, h, i: (b, ck + h)),
           pl.BlockSpec((l, LANES), lambda b, h, i: (b + 1, ck + h)),
           pl.BlockSpec((lc, LANES), lambda b, h, i: (b, cv + h)),
           pl.BlockSpec((l, LANES), lambda b, h, i: (b + 1, cv + h)),
           tab_q, tab_q, tab_k, tab_k],
        out_specs=pl.BlockSpec((tq, LANES), lambda b, h, i: (b * per + i, h)),
        out_shape=jax.ShapeDtypeStruct((n_b * l, HC * DVC), BF16),
        scratch_shapes=[pltpu.VMEM((lc + l, LANES), BF16), pltpu.VMEM((lc + l, LANES), BF16)],
        compiler_params=_cparams(3),
        name="diff_attention",
    )(lam, gn.reshape(1, DVC), proj, proj, proj, proj, proj, rope_cos, rope_sin, rope_cos, rope_sin)
    if not ctx_out:
        return None, od
    od_ctx = pl.pallas_call(
        functools.partial(_attn_ctx_kernel, lam_init=lam_init),
        grid=(n_b, HC),
        in_specs=small
        + [pl.BlockSpec((lc, LANES), lambda b, h: (b, cq + h)),
           pl.BlockSpec((lc, LANES), lambda b, h: (b, ck + h)),
           pl.BlockSpec((lc, LANES), lambda b, h: (b, cv + h))],
        out_specs=pl.BlockSpec((lc, LANES), lambda b, h: (b, h)),
        out_shape=jax.ShapeDtypeStruct((n_b * lc, HC * DVC), BF16),
        compiler_params=_cparams(2),
        name="diff_attention_ctx",
    )(lam, gn.reshape(1, DVC), proj, proj, proj)
    return od_ctx, od


def _rope_tables(n_tokens):
    rows = n_tokens // GRID_W
    row = jnp.repeat(jnp.arange(rows, dtype=F32), GRID_W)
    col = jnp.tile(jnp.arange(GRID_W, dtype=F32), rows)
    n_freq = DKC // 4
    inv_freq = ROPE_THETA ** (-jnp.arange(n_freq, dtype=F32) / n_freq)
    ang_r = row[:, None] * inv_freq
    ang_c = col[:, None] * inv_freq
    cos = jnp.concatenate([jnp.cos(ang_r)] * 2 + [jnp.cos(ang_c)] * 2, axis=1)
    sin = jnp.concatenate([-jnp.sin(ang_r), jnp.sin(ang_r), -jnp.sin(ang_c), jnp.sin(ang_c)], axis=1)
    return jnp.concatenate([cos, cos], axis=1), jnp.concatenate([sin, sin], axis=1)


def _head_rms(o, gain, n_heads, width):
    parts = []
    for h in range(n_heads):
        parts.append(_rms_rows(o[:, h * width:(h + 1) * width], gain))
    return jnp.concatenate(parts, axis=1)


def _merge_kernel(*refs, ctx_tiles, n_tok, n_od):
    x_refs, od_refs = refs[:n_tok], refs[n_tok:n_tok + n_od]
    (oa0, oa1, z_ref, ob0, ob1, r_ref, ga_ref, gb_ref, gd_ref, mod_ref,
     gna_ref, gnb_ref, woa_ref, wob_ref, woc_ref, wout_ref, o_ref) = refs[n_tok + n_od:]
    f32 = lambda ref: ref[...].astype(F32)
    ya = _head_rms(f32(oa0) + f32(oa1), gna_ref[...], HA, DVA) * _silu(f32(z_ref))
    yb = _head_rms(f32(ob0) + f32(ob1), gnb_ref[...], HB, DVB) * _silu(f32(r_ref))
    acc = _sigmoid(f32(ga_ref)) * _dot(ya.astype(BF16), woa_ref[...])
    acc = acc + _sigmoid(f32(gb_ref)) * _dot(yb.astype(BF16), wob_ref[...])
    acc = acc + _sigmoid(f32(gd_ref)) * _dot(_token_tile(od_refs, ctx_tiles), woc_ref[...])
    y = _dot(acc.astype(BF16), wout_ref[...])
    o_ref[...] = _token_tile(x_refs, ctx_tiles) + mod_ref[0, 2:3, :] * y


def _merge(oa0, oa1, ob0, ob1, od_ctx, od_lat, proj, x_ctx, x_lat, lat_first, t, mod, gn_a, gn_b,
           woa, wob, woc, wout, rows_per_mod, row_off, tm):
    d = x_ctx.shape[1]
    off = row_off // tm
    ctx_tiles = rows_per_mod // tm
    n_tiles = (t - row_off) // tm
    per = rows_per_mod // tm
    w5 = HA * DVA
    row = lambda i: (i + off, 0)
    col = lambda c: (lambda i: (i + off, c))
    tok_specs, tok_args = _token_inputs(x_ctx, x_lat, lat_first, tm, ctx_tiles, off)
    if od_ctx is None:
        assert off == ctx_tiles
        od_specs, od_args = [pl.BlockSpec((tm, w5), lambda i: (i, 0))], [od_lat]
    else:
        od_specs, od_args = _token_specs(tm, w5, ctx_tiles, 0, off), [od_ctx, od_lat]
    return pl.pallas_call(
        functools.partial(_merge_kernel, ctx_tiles=ctx_tiles - off, n_tok=len(tok_args), n_od=len(od_args)),
        grid=(n_tiles,),
        in_specs=tok_specs + od_specs
        + [pl.BlockSpec((tm, w5), row),
                  pl.BlockSpec((tm, w5), row),
                  pl.BlockSpec((tm, w5), col(H_AZ // w5)),
                  pl.BlockSpec((tm, w5), row),
                  pl.BlockSpec((tm, w5), row),
                  pl.BlockSpec((tm, w5), col(H_BR // w5)),
                  pl.BlockSpec((tm, d), col(H_GATES // d)),
                  pl.BlockSpec((tm, d), col(H_GATES // d + 1)),
                  pl.BlockSpec((tm, d), col(H_GATES // d + 2)),
                  pl.BlockSpec((1, 6, d), lambda i: ((i + off) // per, 0, 0)),
                  pl.BlockSpec((1, DVA), lambda i: (0, 0)),
                  pl.BlockSpec((1, DVB), lambda i: (0, 0)),
                  pl.BlockSpec((w5, d), lambda i: (0, 0)),
                  pl.BlockSpec((w5, d), lambda i: (0, 0)),
                  pl.BlockSpec((w5, d), lambda i: (0, 0)),
                  pl.BlockSpec((d, d), lambda i: (0, 0))],
        out_specs=pl.BlockSpec((tm, d), lambda i: (i, 0)),
        out_shape=jax.ShapeDtypeStruct((t - row_off, d), F32),
        compiler_params=_cparams(1),
        name="merge_outproj",
    )(*tok_args, *od_args, oa0, oa1, proj, ob0, ob1, proj, proj, proj, proj, mod,
      gn_a.reshape(1, DVA), gn_b.reshape(1, DVB), woa, wob, woc, wout)


def _ffn_kernel(x_ref, mod_ref, g_ref, fg_ref, w1_ref, w3_ref, w2_ref, o_ref, h_scr, *, final):
    f = pl.program_id(1)

    @pl.when(f == 0)
    def _():
        y = _rms_rows(x_ref[...], g_ref[...])
        h_scr[...] = (y * (1.0 + mod_ref[0, 4:5, :]) + mod_ref[0, 3:4, :]).astype(BF16)
        o_ref[...] = jnp.zeros_like(o_ref)

    h = h_scr[...]
    t = _silu(_dot(h, w1_ref[...])) * _dot(h, w3_ref[...])
    o_ref[...] += _dot(t.astype(BF16), w2_ref[...])

    @pl.when(f == pl.num_programs(1) - 1)
    def _():
        out = x_ref[...] + mod_ref[0, 5:6, :] * o_ref[...]
        if final:
            out = _rms_rows(out, fg_ref[...])
        o_ref[...] = out


def _ffn(x, mod, g, final_g, w1, w3, w2, rows_per_mod, mod_off, tm, tf, final):
    t, d = x.shape
    dff = w1.shape[1]
    per = rows_per_mod // tm
    return pl.pallas_call(
        functools.partial(_ffn_kernel, final=final),
        grid=(t // tm, dff // tf),
        in_specs=[pl.BlockSpec((tm, d), lambda i, f: (i, 0)),
                  pl.BlockSpec((1, 6, d), lambda i, f: (i // per + mod_off, 0, 0)),
                  pl.BlockSpec((1, d), lambda i, f: (0, 0)),
                  pl.BlockSpec((1, d), lambda i, f: (0, 0)),
                  pl.BlockSpec((d, tf), lambda i, f: (0, f)),
                  pl.BlockSpec((d, tf), lambda i, f: (0, f)),
                  pl.BlockSpec((tf, d), lambda i, f: (f, 0))],
        out_specs=pl.BlockSpec((tm, d), lambda i, f: (i, 0)),
        out_shape=jax.ShapeDtypeStruct((t, d), F32),
        scratch_shapes=[pltpu.VMEM((tm, d), BF16)],
        compiler_params=_cparams(2),
        name="dense_ffn",
    )(x, mod, g.reshape(1, d), final_g.reshape(1, d), w1, w3, w2)


MOE_ROWS = 256
MOE_MOVE_UNIT = 128
MOE_FFN_UNIT = 128


def _moe_kernel(x_ref, mod_ref, g_ref, rw_ref, tri_ref, fg_ref, w1_ref, w3_ref, w2_ref, o_ref,
                h_scr, gate_scr, sel_scr, rank_scr, selr_scr, rankr_scr, xg_scr, yg_scr, nb_scr, *, final):
    e = pl.program_id(1)
    f = pl.program_id(2)
    n_f = pl.num_programs(2)
    tm, d = x_ref.shape
    cb = MOE_ROWS
    lane = lax.broadcasted_iota(jnp.int32, (1, LANES), 1)

    def for_blocks(body, slot, unit):
        units = nb_scr[slot]
        per = cb // unit
        nb = units // per

        def full(j, carry):
            body(pl.multiple_of(j * cb, cb), cb)
            return carry

        lax.fori_loop(0, nb, full, 0)
        rem = units - nb * per
        r0 = nb * cb
        size = cb // 2
        while size >= unit:
            has = (rem // (size // unit)) % 2 == 1

            @pl.when(has)
            def _():
                body(pl.multiple_of(r0, unit), size)

            r0 = r0 + jnp.where(has, size, 0)
            size //= 2

    @pl.when((e == 0) & (f == 0))
    def _():
        y = _rms_rows(x_ref[...], g_ref[...])
        hmod = y * (1.0 + mod_ref[0, 4:5, :]) + mod_ref[0, 3:4, :]
        h_scr[...] = hmod.astype(BF16)
        o_ref[...] = jnp.zeros_like(o_ref)
        logits = _dot3(hmod, rw_ref[...])
        logits = jnp.where(lane < N_EXPERTS, logits, -jnp.inf)
        lanef = lane.astype(F32)
        m1 = jnp.max(logits, axis=1, keepdims=True)
        i1 = jnp.min(jnp.where(logits == m1, lanef, float(LANES)), axis=1, keepdims=True)
        hit1 = lanef == i1
        rest = jnp.where(hit1, -jnp.inf, logits)
        m2 = jnp.max(rest, axis=1, keepdims=True)
        i2 = jnp.min(jnp.where(rest == m2, lanef, float(LANES)), axis=1, keepdims=True)
        hit2 = lanef == i2
        ex = jnp.exp(m2 - m1)
        inv = 1.0 / (1.0 + ex)
        gate_scr[...] = jnp.where(hit1, inv, 0.0) + jnp.where(hit2, ex * inv, 0.0)
        sel = jnp.where(hit1 | hit2, 1.0, 0.0)
        sel_scr[...] = sel
        sel16 = sel.astype(BF16)
        rank = _dot(tri_ref[...], sel16)
        rank_scr[...] = rank
        pick = jnp.where(lax.broadcasted_iota(jnp.int32, (8, LANES), 0)
                         == lax.broadcasted_iota(jnp.int32, (8, LANES), 1), 1.0, 0.0).astype(BF16)
        selr_scr[...] = _dot_nt(pick, sel16)
        hi = jnp.floor(rank * (1.0 / 32.0))
        lo = rank - 32.0 * hi
        rankr_scr[...] = 32.0 * _dot_nt(pick, hi.astype(BF16)) + _dot_nt(pick, lo.astype(BF16))

    @pl.when(f == 0)
    def _():
        sel_row = selr_scr[pl.ds(e, 1), :]
        rank_row = rankr_scr[pl.ds(e, 1), :]
        n_e = jnp.sum(sel_row).astype(jnp.int32)
        nb_scr[0] = (n_e + (MOE_MOVE_UNIT - 1)) // MOE_MOVE_UNIT
        nb_scr[1] = (n_e + (MOE_FFN_UNIT - 1)) // MOE_FFN_UNIT

        def gather(r0, bs):
            slot = lax.broadcasted_iota(jnp.int32, (bs, tm), 0).astype(F32) + r0.astype(F32)
            onehot = jnp.where((rank_row == slot) & (sel_row > 0.0), 1.0, 0.0)
            xg_scr[pl.ds(r0, bs), :] = _dot(onehot.astype(BF16), h_scr[...]).astype(BF16)
            yg_scr[pl.ds(r0, bs), :] = jnp.zeros((bs, d), F32)

        for_blocks(gather, 0, MOE_MOVE_UNIT)

    def expert(r0, bs):
        xb = xg_scr[pl.ds(r0, bs), :]
        t = _silu(_dot(xb, w1_ref[0])) * _dot(xb, w3_ref[0])
        yg_scr[pl.ds(r0, bs), :] += _dot(t.astype(BF16), w2_ref[0])

    for_blocks(expert, 1, MOE_FFN_UNIT)

    @pl.when(f == n_f - 1)
    def _():
        pick_e = lane == e
        rank_col = jnp.sum(jnp.where(pick_e, rank_scr[...], 0.0), axis=1, keepdims=True)
        sel_col = jnp.sum(jnp.where(pick_e, sel_scr[...], 0.0), axis=1, keepdims=True)
        gate_col = jnp.sum(jnp.where(pick_e, gate_scr[...], 0.0), axis=1, keepdims=True)

        def scatter(r0, bs):
            slot = lax.broadcasted_iota(jnp.int32, (tm, bs), 1).astype(F32) + r0.astype(F32)
            onehot = jnp.where((rank_col == slot) & (sel_col > 0.0), 1.0, 0.0)
            o_ref[...] += gate_col * _dot(onehot.astype(BF16), yg_scr[pl.ds(r0, bs), :].astype(BF16))

        for_blocks(scatter, 0, MOE_MOVE_UNIT)

    @pl.when((e == pl.num_programs(1) - 1) & (f == n_f - 1))
    def _():
        out = x_ref[...] + mod_ref[0, 5:6, :] * o_ref[...]
        if final:
            out = _rms_rows(out, fg_ref[...])
        o_ref[...] = out


def _moe(x, mod, g, router_w, final_g, w1, w3, w2, rows_per_mod, mod_off, tm, tf, final):
    t, d = x.shape
    n_e, _, dff = w1.shape
    per = rows_per_mod // tm
    return pl.pallas_call(
        functools.partial(_moe_kernel, final=final),
        grid=(t // tm, n_e, dff // tf),
        in_specs=[pl.BlockSpec((tm, d), lambda i, e, f: (i, 0)),
                  pl.BlockSpec((1, 6, d), lambda i, e, f: (i // per + mod_off, 0, 0)),
                  pl.BlockSpec((1, d), lambda i, e, f: (0, 0)),
                  pl.BlockSpec((d, LANES), lambda i, e, f: (0, 0)),
                  pl.BlockSpec((tm, tm), lambda i, e, f: (0, 0)),
                  pl.BlockSpec((1, d), lambda i, e, f: (0, 0)),
                  pl.BlockSpec((1, d, tf), lambda i, e, f: (e, 0, f)),
                  pl.BlockSpec((1, d, tf), lambda i, e, f: (e, 0, f)),
                  pl.BlockSpec((1, tf, d), lambda i, e, f: (e, f, 0))],
        out_specs=pl.BlockSpec((tm, d), lambda i, e, f: (i, 0)),
        out_shape=jax.ShapeDtypeStruct((t, d), F32),
        scratch_shapes=[pltpu.VMEM((tm, d), BF16),
                        pltpu.VMEM((tm, LANES), F32),
                        pltpu.VMEM((tm, LANES), F32),
                        pltpu.VMEM((tm, LANES), F32),
                        pltpu.VMEM((8, tm), F32),
                        pltpu.VMEM((8, tm), F32),
                        pltpu.VMEM((tm, d), BF16),
                        pltpu.VMEM((tm, d), F32),
                        pltpu.SMEM((2,), jnp.int32)],
        compiler_params=_cparams(3),
        name="routed_ffn_sparse",
    )(x, mod, g.reshape(1, d), router_w, jnp.tril(jnp.ones((tm, tm), BF16), -1),
      final_g.reshape(1, d), w1, w3, w2)


def _w_in_pieces(end):
    a_qkv = 2 * HA * DKA + HA * DVA
    a_z = a_qkv + HA * DVA
    b_q = a_z + 4 * HA
    b_v = b_q + 2 * HB * DKB
    b_glr = b_v + 2 * HB * DVB
    c_q = b_glr + 2 * GATE_RANK
    gates = c_q + 4 * HC * DKC + HC * DVC
    return [(0, a_qkv), (a_z, b_q), (b_glr, c_q), None,
            (gates, end), (a_qkv, a_z), (b_v, b_glr), (c_q, gates), (b_q, b_v)]


def _reorder_kernel(w_ref, o_ref):
    at = 0
    for piece in _w_in_pieces(w_ref.shape[1]):
        if piece is None:
            width = N_F32 - at
            o_ref[0, at:at + width, :] = jnp.zeros((width, o_ref.shape[2]), BF16)
        else:
            width = piece[1] - piece[0]
            o_ref[0, at:at + width, :] = w_ref[0, piece[0]:piece[1], :].astype(BF16)
        at += width
    assert at == N_F32 + N_B16


def _reorder_w_in(w_in):
    w_t = jnp.swapaxes(w_in, 1, 2)
    depth, cols, d = w_t.shape
    kb = LANES
    return pl.pallas_call(
        _reorder_kernel,
        grid=(depth, d // kb),
        in_specs=[pl.BlockSpec((1, cols, kb), lambda l, i: (l, 0, i))],
        out_specs=pl.BlockSpec((1, N_F32 + N_B16, kb), lambda l, i: (l, 0, i)),
        out_shape=jax.ShapeDtypeStruct((depth, N_F32 + N_B16, d), BF16),
        compiler_params=_cparams(2),
        name="reorder_w_in",
    )(w_t)


def kernel(x, c, ctx, c_ctx, w_mod, b_mod, norm1_g, norm2_g, w_in, conv_a, a_log, dt_bias, gn_a, w_gate2, b_gate, gn_b, lam_c, gn_c, w_o_a, w_o_b, w_o_c, w_out, ffn_w1, ffn_w3, ffn_w2, router_w, moe_w1, moe_w3, moe_w2, final_g):
    n_b, l, d = x.shape
    lc = ctx.shape[1]
    depth = w_mod.shape[0]
    assert n_b * lc == l and lc % CHUNK == 0 and l & (l - 1) == 0 and lc & (lc - 1) == 0
    t_ctx = n_b * lc
    tm = min(1024, l)
    tm_small = min(512, l)

    n_mod = 16
    cvec = jnp.concatenate([c_ctx[None, :], c, jnp.zeros((n_mod - 1 - n_b, d), F32)], axis=0)
    mod_all = _modulation(cvec, w_mod, b_mod).reshape(depth, n_mod, 6, d)

    rope_cos, rope_sin = _rope_tables(l)
    w_in16 = _reorder_w_in(w_in)
    t_all = t_ctx + n_b * l
    tokens = (ctx.reshape(t_ctx, d), x.reshape(n_b * l, d), 0)

    for layer in range(depth):
        ctx_out = layer < depth - 1
        lam_init = 0.8 - 0.6 * math.exp(-0.3 * layer)
        mod = mod_all[layer]
        proj32, proj16 = _in_projection(*tokens, t_all, mod, norm1_g[layer], w_in16, layer, l, tm)

        qkvc = _short_conv(proj32, conv_a[layer], lc, l)
        oa0, oa1, ob0, ob1 = _recurrent_branches(qkvc, proj32, proj16, a_log[layer], dt_bias[layer],
                                                 w_gate2[layer], b_gate[layer], n_b, lc, l)
        od = _attention(proj16, lam_c[layer], gn_c[layer], rope_cos, rope_sin, n_b, lc, l, lam_init, ctx_out)

        row_off = 0 if ctx_out else t_ctx
        xs_new = _merge(oa0, oa1, ob0, ob1, *od, proj16, *tokens, t_all, mod, gn_a[layer], gn_b[layer],
                        w_o_a[layer].astype(BF16), w_o_b[layer].astype(BF16), w_o_c[layer].astype(BF16),
                        w_out[layer].astype(BF16), l, row_off, tm_small)

        i = layer // 2
        final = layer == depth - 1
        mod_off = 0 if ctx_out else 1
        if layer % 2 == 0:
            xs = _ffn(xs_new, mod, norm2_g[layer], final_g, ffn_w1[i].astype(BF16),
                      ffn_w3[i].astype(BF16), ffn_w2[i].astype(BF16), l, mod_off, tm,
                      ffn_w1.shape[2] // 2, final)
        else:
            rw = jnp.pad(router_w[i], ((0, 0), (0, LANES - N_EXPERTS)))
            xs = _moe(xs_new, mod, norm2_g[layer], rw, final_g, moe_w1[i].astype(BF16),
                      moe_w3[i].astype(BF16), moe_w2[i].astype(BF16), l, mod_off, tm,
                      moe_w1.shape[3] // 2, final)
        tokens = (xs, xs, t_ctx)

    return xs.reshape(n_b, l, d)
```

```python
import functools
import math

import jax
import jax.numpy as jnp
from jax import lax
from jax.experimental import pallas as pl
from jax.experimental.pallas import tpu as pltpu

F32 = jnp.float32
BF16 = jnp.bfloat16
HIGHEST = lax.Precision.HIGHEST

EPS = 1e-6
HA, DKA, DVA = 4, 128, 128
HB, DKB, DVB = 4, 64, 128
GATE_RANK, GATE_NORM = 16, 16.0
HC, DKC, DVC = 4, 64, 128
GRID_W, ROPE_THETA = 64, 10000.0
N_EXPERTS, TOP_K = 8, 2
CHUNK = 64
LANES = 128
ATTN_SUB = 64
ATTN_TQ = 1024
CONV_HEADS = 4

F_QKV = 0
F_SMALL = 1536
N_F32 = 1664
H_GATES = 0
H_AZ = 3072
H_BV = 3584
H_BR = 4096
H_CQ = 4608
H_CK = 5120
H_CV = 5632
H_BQ = 6144
H_BK = 6400
N_B16 = 6656
PROJ_TN = 1664

VMEM_LIMIT = 56 * 1024 * 1024


def _cparams(n_axes):
    return pltpu.CompilerParams(dimension_semantics=("arbitrary",) * n_axes,
                                vmem_limit_bytes=VMEM_LIMIT)


def _sigmoid(x):
    return 1.0 / (1.0 + jnp.exp2(x * -math.log2(math.e)))


def _silu(x):
    return x * _sigmoid(x)


def _softplus(x):
    return jnp.maximum(x, 0.0) + jnp.log(1.0 + jnp.exp(-jnp.abs(x)))


def _dot(a, b):
    return jnp.dot(a, b, preferred_element_type=F32)


def _dot_nt(a, b):
    return lax.dot_general(a, b, (((1,), (1,)), ((), ())), preferred_element_type=F32)


def _dot_tn(a, b):
    return lax.dot_general(a, b, (((0,), (0,)), ((), ())), preferred_element_type=F32)


def _split(a):
    hi = a.astype(BF16)
    lo = (a - hi.astype(F32)).astype(BF16)
    return hi, lo


def _dot3(a, b):
    ah, al = _split(a)
    bh, bl = _split(b)
    return _dot(ah, bh) + (_dot(ah, bl) + _dot(al, bh))


def _rms_rows(x, gain):
    ms = jnp.mean(x * x, axis=-1, keepdims=True)
    return x * lax.rsqrt(ms + EPS) * gain


def _mod_kernel(c_ref, w_ref, b_ref, o_ref):
    s = _silu(c_ref[...])
    o_ref[0] = jnp.dot(s, w_ref[0], precision=HIGHEST, preferred_element_type=F32) + b_ref[0]


def _modulation(cvec, w_mod, b_mod):
    depth, d, d6 = w_mod.shape
    n = cvec.shape[0]
    return pl.pallas_call(
        _mod_kernel,
        grid=(depth, d6 // d),
        in_specs=[pl.BlockSpec((n, d), lambda l, j: (0, 0)),
                  pl.BlockSpec((1, d, d), lambda l, j: (l, 0, j)),
                  pl.BlockSpec((1, 1, d), lambda l, j: (l, 0, j))],
        out_specs=pl.BlockSpec((1, n, d), lambda l, j: (l, 0, j)),
        out_shape=jax.ShapeDtypeStruct((depth, n, d6), F32),
        compiler_params=_cparams(2),
        name="adaln_mod",
    )(cvec, w_mod, b_mod.reshape(depth, 1, d6))


def _token_specs(tm, d, ctx_tiles, lat_first, off=0):
    return [pl.BlockSpec((tm, d), lambda i, *_: (jnp.minimum(i + off, ctx_tiles - 1), 0)),
            pl.BlockSpec((tm, d), lambda i, *_: (jnp.maximum(i + off - ctx_tiles, 0) + lat_first, 0))]


def _token_inputs(x_ctx, x_lat, lat_first, tm, ctx_tiles, off=0):
    d = x_ctx.shape[1]
    if x_lat is x_ctx:
        return [pl.BlockSpec((tm, d), lambda i, *_: (i + off, 0))], [x_ctx]
    return _token_specs(tm, d, ctx_tiles, lat_first // tm, off), [x_ctx, x_lat]


def _token_tile(x_refs, ctx_tiles):
    if len(x_refs) == 1:
        return x_refs[0][...]
    return jnp.where(pl.program_id(0) < ctx_tiles, x_refs[0][...], x_refs[1][...])


def _inproj_kernel(*refs, ctx_tiles, n_tok):
    x_refs, (mod_ref, g_ref, w_ref, o32_ref, o16_ref, h_scr) = refs[:n_tok], refs[n_tok:]
    j = pl.program_id(1)

    @pl.when(j == 0)
    def _():
        y = _rms_rows(_token_tile(x_refs, ctx_tiles), g_ref[...])
        h_scr[...] = (y * (1.0 + mod_ref[0, 1:2, :]) + mod_ref[0, 0:1, :]).astype(BF16)
        o32_ref[...] = _dot_nt(h_scr[...], w_ref[0])

    @pl.when(j > 0)
    def _():
        o16_ref[...] = _dot_nt(h_scr[...], w_ref[0]).astype(BF16)


def _in_projection(x_ctx, x_lat, lat_first, t, mod, g, w, layer, rows_per_mod, tm):
    d = x_ctx.shape[1]
    tn = PROJ_TN
    assert w.shape[1] == N_F32 + N_B16 and N_F32 == tn and N_B16 % tn == 0
    per = rows_per_mod // tm
    ctx_tiles = rows_per_mod // tm
    tok_specs, tok_args = _token_inputs(x_ctx, x_lat, lat_first, tm, ctx_tiles)
    return pl.pallas_call(
        functools.partial(_inproj_kernel, ctx_tiles=ctx_tiles, n_tok=len(tok_args)),
        grid=(t // tm, 1 + N_B16 // tn),
        in_specs=tok_specs
        + [pl.BlockSpec((1, 6, d), lambda i, j: (i // per, 0, 0)),
           pl.BlockSpec((1, d), lambda i, j: (0, 0)),
           pl.BlockSpec((1, tn, d), lambda i, j: (layer, j, 0))],
        out_specs=[pl.BlockSpec((tm, tn), lambda i, j: (i, 0)),
                   pl.BlockSpec((tm, tn), lambda i, j: (i, jnp.maximum(j - 1, 0)))],
        out_shape=[jax.ShapeDtypeStruct((t, N_F32), F32), jax.ShapeDtypeStruct((t, N_B16), BF16)],
        scratch_shapes=[pltpu.VMEM((tm, d), BF16)],
        compiler_params=_cparams(2),
        name="norm_mod_inproj",
    )(*tok_args, mod, g.reshape(1, d), w)


def _conv_kernel(x_ref, w_ref, o_ref, *, lc, l):
    p = pl.program_id(0)
    j = pl.program_id(1)
    x = x_ref[...]
    n = x.shape[0]
    row = lax.broadcasted_iota(jnp.int32, x.shape, 0)
    seq = jnp.where(p == 0, lc, l)
    pos = row & (seq - 1)
    prev = jnp.where(pos == 0, 0.0, pltpu.roll(x, 1, 0))
    nxt = jnp.where(pos == seq - 1, 0.0, pltpu.roll(x, n - 1, 0))
    w = w_ref[...]
    y = _silu(prev * w[0:1] + x * w[1:2] + nxt * w[2:3])
    heads_per_block = x.shape[1] // DKA
    for hh in range(heads_per_block):
        cols = slice(hh * DKA, (hh + 1) * DKA)
        yh = y[:, cols]
        inv = lax.rsqrt(jnp.sum(yh * yh, axis=-1, keepdims=True) + EPS)
        head = j * heads_per_block + hh
        fac = jnp.where(head < HA, inv * DKA ** -0.5, jnp.where(head < 2 * HA, inv, 1.0))
        o_ref[:, cols] = yh * fac


def _short_conv(proj, conv_w, lc, l):
    t = proj.shape[0]
    ncol = conv_w.shape[1]
    wide = CONV_HEADS * DKA
    return pl.pallas_call(
        functools.partial(_conv_kernel, lc=lc, l=l),
        grid=(t // l, ncol // wide),
        in_specs=[pl.BlockSpec((l, wide), lambda p, j: (p, j)),
                  pl.BlockSpec((conv_w.shape[0], wide), lambda p, j: (0, j))],
        out_specs=pl.BlockSpec((l, wide), lambda p, j: (p, j)),
        out_shape=jax.ShapeDtypeStruct((t, ncol), F32),
        compiler_params=_cparams(2),
        name="conv_silu_l2",
    )(proj, conv_w)


def _row_block(b, g, n_b, n_lat):
    return jnp.where(g == 0, b, n_b + b * n_lat + g - 1)


def _rev_group(g, n_g):
    return jnp.where(g == 0, 0, n_g - g)


def _tri_inverse_many(lms, ii, jj, uppers):
    c = lms[0].shape[0]
    eye = (ii == jj).astype(F32)
    n = len(lms)
    ts = None
    s, ls = 1, 0
    while s < c:
        same = (ii >> (ls + 1)) == (jj >> (ls + 1))
        io = (ii >> ls) & 1
        jo = (jj >> ls) & 1
        lower = same & (io == 1) & (jo == 0)
        upper = same & (io == 0) & (jo == 1)
        cms = [jnp.where(upper if uppers[i] else lower, lms[i], 0.0) for i in range(n)]
        if ts is None:
            ts = [eye - cm for cm in cms]
        else:
            t16 = [t.astype(BF16) for t in ts]
            ps = [_dot(t16[i], cms[i].astype(BF16)).astype(BF16) for i in range(n)]
            ts = [ts[i] - _dot(ps[i], t16[i]) for i in range(n)]
        s, ls = s * 2, ls + 1
    a_parts = [_split(eye + lm) for lm in lms]
    t_parts = [_split(t) for t in ts]
    res = [eye - (_dot(a_parts[i][0], t_parts[i][0])
                  + (_dot(a_parts[i][0], t_parts[i][1]) + _dot(a_parts[i][1], t_parts[i][0])))
           for i in range(n)]
    return [ts[i] + _dot(t_parts[i][0], res[i].astype(BF16)) for i in range(n)]


def _dn_kernel(alog_ref, dtb_ref, q0, k0, v0, sm0, q1, k1, v1, sm1, o0_ref, o1_ref, s_scr, *, n_chunks):
    @pl.when(pl.program_id(1) == 0)
    def _():
        s_scr[...] = jnp.zeros_like(s_scr)

    c_ = CHUNK
    refs = ((q0, k0, v0, o0_ref), (q1, k1, v1, o1_ref))
    sms = (sm0[...], sm1[...])
    sm_ts = tuple(s.T for s in sms)
    ii = lax.broadcasted_iota(jnp.int32, (c_, c_), 0)
    jj = lax.broadcasted_iota(jnp.int32, (c_, c_), 1)
    rows = [slice(c * c_, (c + 1) * c_) for c in range(n_chunks)]
    lanes = [slice(j * LANES, (j + 1) * LANES) for j in range(HA)]
    items = [(j, c, d) for j in range(HA) for c in range(n_chunks) for d in range(2)]
    qs = [refs[d][0][rows[c], lanes[j]] for j, c, d in items]
    ks = [refs[d][1][rows[c], lanes[j]] for j, c, d in items]
    vs = [refs[d][2][rows[c], lanes[j]] for j, c, d in items]
    k16 = [k.astype(BF16) for k in ks]
    kks = [_dot_nt(k, k) for k in k16]
    qks = [_dot_nt(qs[i].astype(BF16), k16[i]) for i in range(len(items))]

    betas, gcs, totals, gams, lms = [], [], [], [], []
    for i, (j, c, d) in enumerate(items):
        r = rows[c]
        incl = (jj <= ii) if d == 0 else (jj >= ii)
        strict = (jj < ii) if d == 0 else (jj > ii)
        incl_t = (ii <= jj) if d == 0 else (ii >= jj)
        a_scale = -jnp.exp(jnp.full((1, 1), alog_ref[d * HA + j], F32))
        dtb = dtb_ref[d * HA + j]
        col = d * HA + j
        beta = _sigmoid(sms[d][r, col:col + 1])
        g_col = a_scale * _softplus(sms[d][r, 2 * HA + col:2 * HA + col + 1] + dtb)
        g_row = a_scale * _softplus(sm_ts[d][2 * HA + col:2 * HA + col + 1, r] + dtb)
        gc_col = jnp.sum(jnp.where(incl, g_row, 0.0), axis=1, keepdims=True)
        gc_row = jnp.sum(jnp.where(incl_t, g_col, 0.0), axis=0, keepdims=True)
        gam = jnp.where(incl, jnp.exp(gc_col - gc_row), 0.0)
        betas.append(beta)
        gcs.append(gc_col)
        totals.append(jnp.sum(g_col, axis=0, keepdims=True))
        gams.append(gam)
        lms.append(jnp.where(strict, beta * kks[i] * gam, 0.0))

    tinvs = _tri_inverse_many(lms, ii, jj, [d == 1 for _, _, d in items])
    egs = [jnp.exp(gc) for gc in gcs]
    uws = []
    for i in range(len(items)):
        rhs = jnp.concatenate([vs[i] * betas[i], ks[i] * (betas[i] * egs[i])], axis=1).astype(BF16)
        uws.append(_dot(tinvs[i].astype(BF16), rhs))
    us = [uw[:, :DVA] for uw in uws]
    wq16 = [jnp.concatenate([uws[i][:, DVA:], qs[i] * egs[i]], axis=0).astype(BF16) for i in range(len(items))]
    kg16 = [(ks[i] * jnp.exp(totals[i] - gcs[i])).astype(BF16) for i in range(len(items))]
    qkg16 = [(qks[i] * gams[i]).astype(BF16) for i in range(len(items))]
    carry = [jnp.exp(t) for t in totals]

    index = {item: i for i, item in enumerate(items)}
    chains = [(j, d) for j in range(HA) for d in range(2)]
    states = [s_scr[d * HA + j] for j, d in chains]
    for step in range(n_chunks):
        at = [index[(j, step if d == 0 else n_chunks - 1 - step, d)] for j, d in chains]
        s16 = [s.astype(BF16) for s in states]
        prods = [_dot(wq16[i], s16[n]) for n, i in enumerate(at)]
        vn16 = [(us[i] - prods[n][:c_]).astype(BF16) for n, i in enumerate(at)]
        for n, i in enumerate(at):
            j, c, d = items[i]
            refs[d][3][rows[c], lanes[j]] = (prods[n][c_:] + _dot(qkg16[i], vn16[n])).astype(BF16)
            states[n] = carry[i] * states[n] + _dot_tn(kg16[i], vn16[n])
    for n, (j, d) in enumerate(chains):
        s_scr[d * HA + j] = states[n]


def _gla_kernel(q0, k0, v0, sm0, q1, k1, v1, sm1, wg_ref, bg_ref, o0_ref, o1_ref, s_scr, *, n_chunks):
    @pl.when(pl.program_id(1) == 0)
    def _():
        s_scr[...] = jnp.zeros_like(s_scr)

    c_ = CHUNK
    sh = c_.bit_length() - 1
    g_ = n_chunks * c_
    kw = HB * DKB
    vw = HB * DVB
    ii = lax.broadcasted_iota(jnp.int32, (g_, g_), 0)
    jj = lax.broadcasted_iota(jnp.int32, (g_, g_), 1)
    same_chunk = (ii >> sh) == (jj >> sh)
    klane = lax.broadcasted_iota(jnp.int32, (1, kw), 1)
    ind16 = jnp.where((lax.broadcasted_iota(jnp.int32, (g_, n_chunks * LANES), 0) >> sh)
                      == (lax.broadcasted_iota(jnp.int32, (g_, n_chunks * LANES), 1) >> (LANES.bit_length() - 1)),
                      1.0, 0.0).astype(BF16)
    rows = [slice(c * c_, (c + 1) * c_) for c in range(n_chunks)]
    refs = ((q0, k0, v0, sm0, o0_ref), (q1, k1, v1, sm1, o1_ref))

    def spread(x, picks):
        return jnp.concatenate([jnp.broadcast_to(x[p:p + 1, :], (c_, kw)) for p in picks], axis=0)

    dirs = (0, 1)
    incls = [same_chunk & ((jj <= ii) if d == 0 else (jj >= ii)) for d in dirs]
    tri16 = [jnp.where(m, 1.0, 0.0).astype(BF16) for m in incls]
    pres = [jnp.dot(refs[d][3][:, 4 * HA + d * GATE_RANK:4 * HA + (d + 1) * GATE_RANK], wg_ref[d],
                    precision=HIGHEST, preferred_element_type=F32) + bg_ref[d] for d in dirs]
    glogs = [(jnp.minimum(p, 0.0) - jnp.log(1.0 + jnp.exp(-jnp.abs(p)))) * (1.0 / GATE_NORM) for p in pres]
    parts = [_split(g) for g in glogs]
    gcs = [_dot(tri16[d], parts[d][0]) + _dot(tri16[d], parts[d][1]) for d in dirs]
    tot_cols = [_dot_tn(parts[d][0], ind16) + _dot_tn(parts[d][1], ind16) for d in dirs]
    mids = [spread(gcs[d], [c * c_ + c_ // 2 - 1 for c in range(n_chunks)]) for d in dirs]
    tots = [spread(gcs[d], [c * c_ + (c_ - 1 if d == 0 else 0) for c in range(n_chunks)]) for d in dirs]
    qs = [refs[d][0][...].astype(F32) * DKB ** -0.5 for d in dirs]
    ks = [refs[d][1][...].astype(F32) for d in dirs]
    v16s = [refs[d][2][...] for d in dirs]
    qts = [qs[d] * jnp.exp(gcs[d] - mids[d]) for d in dirs]
    kt16s = [(ks[d] * jnp.exp(mids[d] - gcs[d])).astype(BF16) for d in dirs]
    qg16s = [(qs[d] * jnp.exp(gcs[d])).astype(BF16) for d in dirs]
    kg16s = [(ks[d] * jnp.exp(tots[d] - gcs[d])).astype(BF16) for d in dirs]
    a16s = {}
    for h in range(HB):
        in_head = (klane >= h * DKB) & (klane < (h + 1) * DKB)
        for d in dirs:
            a = _dot_nt(jnp.where(in_head, qts[d], 0.0).astype(BF16), kt16s[d])
            a16s[(d, h)] = jnp.where(incls[d], a, 0.0).astype(BF16)
    o_intra = [[_dot(a16s[(d, h)], v16s[d][:, h * DVB:(h + 1) * DVB]) for h in range(HB)] for d in dirs]
    xs = [[_dot_tn(kg16s[d][r, :], v16s[d][r, :]) for r in rows] for d in dirs]
    decays = [[jnp.exp(tot_cols[d][:, c * LANES:(c + 1) * LANES]) for c in range(n_chunks)] for d in dirs]

    states = [[s_scr[d, h] for h in range(HB)] for d in dirs]
    zero16 = jnp.zeros((DKB, DVB), BF16)
    for step in range(n_chunks):
        for d in range(2):
            c = step if d == 0 else n_chunks - 1 - step
            r = rows[c]
            s16 = [s.astype(BF16) for s in states[d]]
            s_all = jnp.concatenate(
                [jnp.concatenate([s16[h] if hv == h else zero16 for hv in range(HB)], axis=1) for h in range(HB)],
                axis=0)
            o_inter = _dot(qg16s[d][r, :], s_all)
            for h in range(HB):
                vs = slice(h * DVB, (h + 1) * DVB)
                kr = slice(h * DKB, (h + 1) * DKB)
                refs[d][4][r, vs] = (o_inter[:, vs] + o_intra[d][h][r, :]).astype(BF16)
                states[d][h] = decays[d][c][kr, :] * states[d][h] + xs[d][c][kr, vs]
    for d in dirs:
        for h in range(HB):
            s_scr[d, h] = states[d][h]


def _recurrent_branches(qkvc, proj32, proj16, a_log, dt_bias, w_gate2, b_gate, n_b, lc, l):
    t = qkvc.shape[0]
    n_lat = l // lc
    n_g = n_lat + 1
    wide = HA * DVA
    kw, vw = HB * DKB, HB * DVB
    rb0 = lambda b, g: _row_block(b, g, n_b, n_lat)
    rb1 = lambda b, g: _row_block(b, _rev_group(g, n_g), n_b, n_lat)
    smem = pl.BlockSpec(memory_space=pltpu.SMEM)

    def dn_specs(rb):
        return [pl.BlockSpec((lc, wide), lambda b, g: (rb(b, g), 0)),
                pl.BlockSpec((lc, wide), lambda b, g: (rb(b, g), 1)),
                pl.BlockSpec((lc, wide), lambda b, g: (rb(b, g), 2)),
                pl.BlockSpec((lc, LANES), lambda b, g: (rb(b, g), F_SMALL // LANES))]

    def gla_specs(rb):
        return [pl.BlockSpec((lc, kw), lambda b, g: (rb(b, g), H_BQ // kw)),
                pl.BlockSpec((lc, kw), lambda b, g: (rb(b, g), H_BK // kw)),
                pl.BlockSpec((lc, vw), lambda b, g: (rb(b, g), H_BV // vw)),
                pl.BlockSpec((lc, LANES), lambda b, g: (rb(b, g), F_SMALL // LANES))]

    out = lambda rb, width: pl.BlockSpec((lc, width), lambda b, g: (rb(b, g), 0))
    oa0, oa1 = pl.pallas_call(
        functools.partial(_dn_kernel, n_chunks=lc // CHUNK),
        grid=(n_b, n_g),
        in_specs=[smem, smem] + dn_specs(rb0) + dn_specs(rb1),
        out_specs=[out(rb0, wide), out(rb1, wide)],
        out_shape=[jax.ShapeDtypeStruct((t, wide), BF16)] * 2,
        scratch_shapes=[pltpu.VMEM((2 * HA, DKA, DVA), F32)],
        compiler_params=_cparams(2),
        name="deltanet",
    )(a_log.reshape(-1), dt_bias.reshape(-1), qkvc, qkvc, qkvc, proj32, qkvc, qkvc, qkvc, proj32)
    ob0, ob1 = pl.pallas_call(
        functools.partial(_gla_kernel, n_chunks=lc // CHUNK),
        grid=(n_b, n_g),
        in_specs=gla_specs(rb0) + gla_specs(rb1)
        + [pl.BlockSpec((2, GATE_RANK, kw), lambda b, g: (0, 0, 0)),
           pl.BlockSpec((2, 1, kw), lambda b, g: (0, 0, 0))],
        out_specs=[out(rb0, vw), out(rb1, vw)],
        out_shape=[jax.ShapeDtypeStruct((t, vw), BF16)] * 2,
        scratch_shapes=[pltpu.VMEM((2, HB, DKB, DVB), F32)],
        compiler_params=_cparams(2),
        name="gla_bidir",
    )(proj16, proj16, proj16, proj32, proj16, proj16, proj16, proj32, w_gate2, b_gate.reshape(2, 1, kw))
    return oa0, oa1, ob0, ob1


def _lambda_full(lam, lam_init):
    return (jnp.exp(jnp.sum(lam[0:1] * lam[1:2], axis=1, keepdims=True))
            - jnp.exp(jnp.sum(lam[2:3] * lam[3:4], axis=1, keepdims=True)) + lam_init)


def _attend(q, keys, vals, lam_f, gain, lam_init, o_ref):
    lane = lax.broadcasted_iota(jnp.int32, (1, LANES), 1)
    n = ATTN_SUB
    q = q * (DKC ** -0.5 * math.log2(math.e))
    subs = [q[r * n:(r + 1) * n] for r in range(q.shape[0] // n)]
    q2s = [jnp.concatenate([jnp.where(lane < DKC, x, 0.0), jnp.where(lane >= DKC, x, 0.0)],
                           axis=0).astype(BF16) for x in subs]
    ss = [_dot_nt(q2, keys) for q2 in q2s]
    es = [jnp.exp2(s - jnp.max(s, axis=-1, keepdims=True)) for s in ss]
    invs = [1.0 / jnp.sum(e, axis=-1, keepdims=True) for e in es]
    o2s = [_dot(es[r].astype(BF16), vals) * invs[r] for r in range(len(es))]
    for r, o2 in enumerate(o2s):
        o = o2[:n] - lam_f * o2[n:]
        o_ref[r * n:(r + 1) * n, :] = (_rms_rows(o, gain) * (1.0 - lam_init)).astype(BF16)


def _attn_lat_kernel(lam_ref, gn_ref, q_ref, kc_ref, kl_ref, vc_ref, vl_ref, cq_ref, sq_ref, ck_ref, sk_ref,
                     o_ref, k_scr, v_scr, *, lc, lam_init):
    lane = lax.broadcasted_iota(jnp.int32, (1, LANES), 1)
    first_half = (lane & 31) < 16

    def rope(x, cos, sin):
        swapped = jnp.where(first_half, pltpu.roll(x, LANES - 16, 1), pltpu.roll(x, 16, 1))
        return x * cos + swapped * sin

    @pl.when(pl.program_id(2) == 0)
    def _():
        k_scr[0:lc, :] = kc_ref[...]
        k_scr[lc:, :] = rope(kl_ref[...].astype(F32), ck_ref[...], sk_ref[...]).astype(BF16)
        v_scr[0:lc, :] = vc_ref[...]
        v_scr[lc:, :] = vl_ref[...]

    _attend(rope(q_ref[...].astype(F32), cq_ref[...], sq_ref[...]), k_scr[...], v_scr[...],
            _lambda_full(lam_ref[...], lam_init), gn_ref[...], lam_init, o_ref)


def _attn_ctx_kernel(lam_ref, gn_ref, q_ref, k_ref, v_ref, o_ref, *, lam_init):
    _attend(q_ref[...].astype(F32), k_ref[...], v_ref[...], _lambda_full(lam_ref[...], lam_init),
            gn_ref[...], lam_init, o_ref)


def _attention(proj, lam, gn, rope_cos, rope_sin, n_b, lc, l, lam_init, ctx_out):
    tq = min(ATTN_TQ, l)
    per = l // tq
    cq, ck, cv = H_CQ // LANES, H_CK // LANES, H_CV // LANES
    small = [pl.BlockSpec((4, DKC), lambda b, h, *_: (0, 0)), pl.BlockSpec((1, DVC), lambda b, h, *_: (0, 0))]
    tab_q = pl.BlockSpec((tq, LANES), lambda b, h, i: (i, 0))
    tab_k = pl.BlockSpec((l, LANES), lambda b, h, i: (0, 0))
    od = pl.pallas_call(
        functools.partial(_attn_lat_kernel, lc=lc, lam_init=lam_init),
        grid=(n_b, HC, per),
        in_specs=small
        + [pl.BlockSpec((tq, LANES), lambda b, h, i: ((b + 1) * per + i, cq + h)),
           pl.BlockSpec((lc, LANES), lambda b, h, i: (b, ck + h)),
           pl.BlockSpec((l, LANES), lambda b, h, i: (b + 1, ck + h)),
           pl.BlockSpec((lc, LANES), lambda b, h, i: (b, cv + h)),
           pl.BlockSpec((l, LANES), lambda b, h, i: (b + 1, cv + h)),
           tab_q, tab_q, tab_k, tab_k],
        out_specs=pl.BlockSpec((tq, LANES), lambda b, h, i: (b * per + i, h)),
        out_shape=jax.ShapeDtypeStruct((n_b * l, HC * DVC), BF16),
        scratch_shapes=[pltpu.VMEM((lc + l, LANES), BF16), pltpu.VMEM((lc + l, LANES), BF16)],
        compiler_params=_cparams(3),
        name="diff_attention",
    )(lam, gn.reshape(1, DVC), proj, proj, proj, proj, proj, rope_cos, rope_sin, rope_cos, rope_sin)
    if not ctx_out:
        return None, od
    od_ctx = pl.pallas_call(
        functools.partial(_attn_ctx_kernel, lam_init=lam_init),
        grid=(n_b, HC),
        in_specs=small
        + [pl.BlockSpec((lc, LANES), lambda b, h: (b, cq + h)),
           pl.BlockSpec((lc, LANES), lambda b, h: (b, ck + h)),
           pl.BlockSpec((lc, LANES), lambda b, h: (b, cv + h))],
        out_specs=pl.BlockSpec((lc, LANES), lambda b, h: (b, h)),
        out_shape=jax.ShapeDtypeStruct((n_b * lc, HC * DVC), BF16),
        compiler_params=_cparams(2),
        name="diff_attention_ctx",
    )(lam, gn.reshape(1, DVC), proj, proj, proj)
    return od_ctx, od


def _rope_tables(n_tokens):
    rows = n_tokens // GRID_W
    row = jnp.repeat(jnp.arange(rows, dtype=F32), GRID_W)
    col = jnp.tile(jnp.arange(GRID_W, dtype=F32), rows)
    n_freq = DKC // 4
    inv_freq = ROPE_THETA ** (-jnp.arange(n_freq, dtype=F32) / n_freq)
    ang_r = row[:, None] * inv_freq
    ang_c = col[:, None] * inv_freq
    cos = jnp.concatenate([jnp.cos(ang_r)] * 2 + [jnp.cos(ang_c)] * 2, axis=1)
    sin = jnp.concatenate([-jnp.sin(ang_r), jnp.sin(ang_r), -jnp.sin(ang_c), jnp.sin(ang_c)], axis=1)
    return jnp.concatenate([cos, cos], axis=1), jnp.concatenate([sin, sin], axis=1)


def _head_rms(o, gain, n_heads, width):
    parts = []
    for h in range(n_heads):
        parts.append(_rms_rows(o[:, h * width:(h + 1) * width], gain))
    return jnp.concatenate(parts, axis=1)


def _merge_kernel(*refs, ctx_tiles, n_tok, n_od):
    x_refs, od_refs = refs[:n_tok], refs[n_tok:n_tok + n_od]
    (oa0, oa1, z_ref, ob0, ob1, r_ref, ga_ref, gb_ref, gd_ref, mod_ref,
     gna_ref, gnb_ref, woa_ref, wob_ref, woc_ref, wout_ref, o_ref) = refs[n_tok + n_od:]
    f32 = lambda ref: ref[...].astype(F32)
    ya = _head_rms(f32(oa0) + f32(oa1), gna_ref[...], HA, DVA) * _silu(f32(z_ref))
    yb = _head_rms(f32(ob0) + f32(ob1), gnb_ref[...], HB, DVB) * _silu(f32(r_ref))
    acc = _sigmoid(f32(ga_ref)) * _dot(ya.astype(BF16), woa_ref[...])
    acc = acc + _sigmoid(f32(gb_ref)) * _dot(yb.astype(BF16), wob_ref[...])
    acc = acc + _sigmoid(f32(gd_ref)) * _dot(_token_tile(od_refs, ctx_tiles), woc_ref[...])
    y = _dot(acc.astype(BF16), wout_ref[...])
    o_ref[...] = _token_tile(x_refs, ctx_tiles) + mod_ref[0, 2:3, :] * y


def _merge(oa0, oa1, ob0, ob1, od_ctx, od_lat, proj, x_ctx, x_lat, lat_first, t, mod, gn_a, gn_b,
           woa, wob, woc, wout, rows_per_mod, row_off, tm):
    d = x_ctx.shape[1]
    off = row_off // tm
    ctx_tiles = rows_per_mod // tm
    n_tiles = (t - row_off) // tm
    per = rows_per_mod // tm
    w5 = HA * DVA
    row = lambda i: (i + off, 0)
    col = lambda c: (lambda i: (i + off, c))
    tok_specs, tok_args = _token_inputs(x_ctx, x_lat, lat_first, tm, ctx_tiles, off)
    if od_ctx is None:
        assert off == ctx_tiles
        od_specs, od_args = [pl.BlockSpec((tm, w5), lambda i: (i, 0))], [od_lat]
    else:
        od_specs, od_args = _token_specs(tm, w5, ctx_tiles, 0, off), [od_ctx, od_lat]
    return pl.pallas_call(
        functools.partial(_merge_kernel, ctx_tiles=ctx_tiles - off, n_tok=len(tok_args), n_od=len(od_args)),
        grid=(n_tiles,),
        in_specs=tok_specs + od_specs
        + [pl.BlockSpec((tm, w5), row),
                  pl.BlockSpec((tm, w5), row),
                  pl.BlockSpec((tm, w5), col(H_AZ // w5)),
                  pl.BlockSpec((tm, w5), row),
                  pl.BlockSpec((tm, w5), row),
                  pl.BlockSpec((tm, w5), col(H_BR // w5)),
                  pl.BlockSpec((tm, d), col(H_GATES // d)),
                  pl.BlockSpec((tm, d), col(H_GATES // d + 1)),
                  pl.BlockSpec((tm, d), col(H_GATES // d + 2)),
                  pl.BlockSpec((1, 6, d), lambda i: ((i + off) // per, 0, 0)),
                  pl.BlockSpec((1, DVA), lambda i: (0, 0)),
                  pl.BlockSpec((1, DVB), lambda i: (0, 0)),
                  pl.BlockSpec((w5, d), lambda i: (0, 0)),
                  pl.BlockSpec((w5, d), lambda i: (0, 0)),
                  pl.BlockSpec((w5, d), lambda i: (0, 0)),
                  pl.BlockSpec((d, d), lambda i: (0, 0))],
        out_specs=pl.BlockSpec((tm, d), lambda i: (i, 0)),
        out_shape=jax.ShapeDtypeStruct((t - row_off, d), F32),
        compiler_params=_cparams(1),
        name="merge_outproj",
    )(*tok_args, *od_args, oa0, oa1, proj, ob0, ob1, proj, proj, proj, proj, mod,
      gn_a.reshape(1, DVA), gn_b.reshape(1, DVB), woa, wob, woc, wout)


def _ffn_kernel(x_ref, mod_ref, g_ref, fg_ref, w1_ref, w3_ref, w2_ref, o_ref, h_scr, *, final):
    f = pl.program_id(1)

    @pl.when(f == 0)
    def _():
        y = _rms_rows(x_ref[...], g_ref[...])
        h_scr[...] = (y * (1.0 + mod_ref[0, 4:5, :]) + mod_ref[0, 3:4, :]).astype(BF16)
        o_ref[...] = jnp.zeros_like(o_ref)

    h = h_scr[...]
    t = _silu(_dot(h, w1_ref[...])) * _dot(h, w3_ref[...])
    o_ref[...] += _dot(t.astype(BF16), w2_ref[...])

    @pl.when(f == pl.num_programs(1) - 1)
    def _():
        out = x_ref[...] + mod_ref[0, 5:6, :] * o_ref[...]
        if final:
            out = _rms_rows(out, fg_ref[...])
        o_ref[...] = out


def _ffn(x, mod, g, final_g, w1, w3, w2, rows_per_mod, mod_off, tm, tf, final):
    t, d = x.shape
    dff = w1.shape[1]
    per = rows_per_mod // tm
    return pl.pallas_call(
        functools.partial(_ffn_kernel, final=final),
        grid=(t // tm, dff // tf),
        in_specs=[pl.BlockSpec((tm, d), lambda i, f: (i, 0)),
                  pl.BlockSpec((1, 6, d), lambda i, f: (i // per + mod_off, 0, 0)),
                  pl.BlockSpec((1, d), lambda i, f: (0, 0)),
                  pl.BlockSpec((1, d), lambda i, f: (0, 0)),
                  pl.BlockSpec((d, tf), lambda i, f: (0, f)),
                  pl.BlockSpec((d, tf), lambda i, f: (0, f)),
                  pl.BlockSpec((tf, d), lambda i, f: (f, 0))],
        out_specs=pl.BlockSpec((tm, d), lambda i, f: (i, 0)),
        out_shape=jax.ShapeDtypeStruct((t, d), F32),
        scratch_shapes=[pltpu.VMEM((tm, d), BF16)],
        compiler_params=_cparams(2),
        name="dense_ffn",
    )(x, mod, g.reshape(1, d), final_g.reshape(1, d), w1, w3, w2)


MOE_ROWS = 256
MOE_MOVE_UNIT = 128
MOE_FFN_UNIT = 128


def _moe_kernel(x_ref, mod_ref, g_ref, rw_ref, tri_ref, fg_ref, w1_ref, w3_ref, w2_ref, o_ref,
                h_scr, gate_scr, sel_scr, rank_scr, selr_scr, rankr_scr, xg_scr, yg_scr, nb_scr, *, final):
    e = pl.program_id(1)
    f = pl.program_id(2)
    n_f = pl.num_programs(2)
    tm, d = x_ref.shape
    cb = MOE_ROWS
    lane = lax.broadcasted_iota(jnp.int32, (1, LANES), 1)

    def for_blocks(body, slot, unit):
        units = nb_scr[slot]
        per = cb // unit
        nb = units // per

        def full(j, carry):
            body(pl.multiple_of(j * cb, cb), cb)
            return carry

        lax.fori_loop(0, nb, full, 0)
        rem = units - nb * per
        r0 = nb * cb
        size = cb // 2
        while size >= unit:
            has = (rem // (size // unit)) % 2 == 1

            @pl.when(has)
            def _():
                body(pl.multiple_of(r0, unit), size)

            r0 = r0 + jnp.where(has, size, 0)
            size //= 2

    @pl.when((e == 0) & (f == 0))
    def _():
        y = _rms_rows(x_ref[...], g_ref[...])
        hmod = y * (1.0 + mod_ref[0, 4:5, :]) + mod_ref[0, 3:4, :]
        h_scr[...] = hmod.astype(BF16)
        o_ref[...] = jnp.zeros_like(o_ref)
        logits = _dot3(hmod, rw_ref[...])
        logits = jnp.where(lane < N_EXPERTS, logits, -jnp.inf)
        lanef = lane.astype(F32)
        m1 = jnp.max(logits, axis=1, keepdims=True)
        i1 = jnp.min(jnp.where(logits == m1, lanef, float(LANES)), axis=1, keepdims=True)
        hit1 = lanef == i1
        rest = jnp.where(hit1, -jnp.inf, logits)
        m2 = jnp.max(rest, axis=1, keepdims=True)
        i2 = jnp.min(jnp.where(rest == m2, lanef, float(LANES)), axis=1, keepdims=True)
        hit2 = lanef == i2
        ex = jnp.exp(m2 - m1)
        inv = 1.0 / (1.0 + ex)
        gate_scr[...] = jnp.where(hit1, inv, 0.0) + jnp.where(hit2, ex * inv, 0.0)
        sel = jnp.where(hit1 | hit2, 1.0, 0.0)
        sel_scr[...] = sel
        sel16 = sel.astype(BF16)
        rank = _dot(tri_ref[...], sel16)
        rank_scr[...] = rank
        pick = jnp.where(lax.broadcasted_iota(jnp.int32, (8, LANES), 0)
                         == lax.broadcasted_iota(jnp.int32, (8, LANES), 1), 1.0, 0.0).astype(BF16)
        selr_scr[...] = _dot_nt(pick, sel16)
        hi = jnp.floor(rank * (1.0 / 32.0))
        lo = rank - 32.0 * hi
        rankr_scr[...] = 32.0 * _dot_nt(pick, hi.astype(BF16)) + _dot_nt(pick, lo.astype(BF16))

    @pl.when(f == 0)
    def _():
        sel_row = selr_scr[pl.ds(e, 1), :]
        rank_row = rankr_scr[pl.ds(e, 1), :]
        n_e = jnp.sum(sel_row).astype(jnp.int32)
        nb_scr[0] = (n_e + (MOE_MOVE_UNIT - 1)) // MOE_MOVE_UNIT
        nb_scr[1] = (n_e + (MOE_FFN_UNIT - 1)) // MOE_FFN_UNIT

        def gather(r0, bs):
            slot = lax.broadcasted_iota(jnp.int32, (bs, tm), 0).astype(F32) + r0.astype(F32)
            onehot = jnp.where((rank_row == slot) & (sel_row > 0.0), 1.0, 0.0)
            xg_scr[pl.ds(r0, bs), :] = _dot(onehot.astype(BF16), h_scr[...]).astype(BF16)
            yg_scr[pl.ds(r0, bs), :] = jnp.zeros((bs, d), F32)

        for_blocks(gather, 0, MOE_MOVE_UNIT)

    def expert(r0, bs):
        xb = xg_scr[pl.ds(r0, bs), :]
        t = _silu(_dot(xb, w1_ref[0])) * _dot(xb, w3_ref[0])
        yg_scr[pl.ds(r0, bs), :] += _dot(t.astype(BF16), w2_ref[0])

    for_blocks(expert, 1, MOE_FFN_UNIT)

    @pl.when(f == n_f - 1)
    def _():
        pick_e = lane == e
        rank_col = jnp.sum(jnp.where(pick_e, rank_scr[...], 0.0), axis=1, keepdims=True)
        sel_col = jnp.sum(jnp.where(pick_e, sel_scr[...], 0.0), axis=1, keepdims=True)
        gate_col = jnp.sum(jnp.where(pick_e, gate_scr[...], 0.0), axis=1, keepdims=True)

        def scatter(r0, bs):
            slot = lax.broadcasted_iota(jnp.int32, (tm, bs), 1).astype(F32) + r0.astype(F32)
            onehot = jnp.where((rank_col == slot) & (sel_col > 0.0), 1.0, 0.0)
            o_ref[...] += gate_col * _dot(onehot.astype(BF16), yg_scr[pl.ds(r0, bs), :].astype(BF16))

        for_blocks(scatter, 0, MOE_MOVE_UNIT)

    @pl.when((e == pl.num_programs(1) - 1) & (f == n_f - 1))
    def _():
        out = x_ref[...] + mod_ref[0, 5:6, :] * o_ref[...]
        if final:
            out = _rms_rows(out, fg_ref[...])
        o_ref[...] = out


def _moe(x, mod, g, router_w, final_g, w1, w3, w2, rows_per_mod, mod_off, tm, tf, final):
    t, d = x.shape
    n_e, _, dff = w1.shape
    per = rows_per_mod // tm
    return pl.pallas_call(
        functools.partial(_moe_kernel, final=final),
        grid=(t // tm, n_e, dff // tf),
        in_specs=[pl.BlockSpec((tm, d), lambda i, e, f: (i, 0)),
                  pl.BlockSpec((1, 6, d), lambda i, e, f: (i // per + mod_off, 0, 0)),
                  pl.BlockSpec((1, d), lambda i, e, f: (0, 0)),
                  pl.BlockSpec((d, LANES), lambda i, e, f: (0, 0)),
                  pl.BlockSpec((tm, tm), lambda i, e, f: (0, 0)),
                  pl.BlockSpec((1, d), lambda i, e, f: (0, 0)),
                  pl.BlockSpec((1, d, tf), lambda i, e, f: (e, 0, f)),
                  pl.BlockSpec((1, d, tf), lambda i, e, f: (e, 0, f)),
                  pl.BlockSpec((1, tf, d), lambda i, e, f: (e, f, 0))],
        out_specs=pl.BlockSpec((tm, d), lambda i, e, f: (i, 0)),
        out_shape=jax.ShapeDtypeStruct((t, d), F32),
        scratch_shapes=[pltpu.VMEM((tm, d), BF16),
                        pltpu.VMEM((tm, LANES), F32),
                        pltpu.VMEM((tm, LANES), F32),
                        pltpu.VMEM((tm, LANES), F32),
                        pltpu.VMEM((8, tm), F32),
                        pltpu.VMEM((8, tm), F32),
                        pltpu.VMEM((tm, d), BF16),
                        pltpu.VMEM((tm, d), F32),
                        pltpu.SMEM((2,), jnp.int32)],
        compiler_params=_cparams(3),
        name="routed_ffn_sparse",
    )(x, mod, g.reshape(1, d), router_w, jnp.tril(jnp.ones((tm, tm), BF16), -1),
      final_g.reshape(1, d), w1, w3, w2)


def _w_in_pieces(end):
    a_qkv = 2 * HA * DKA + HA * DVA
    a_z = a_qkv + HA * DVA
    b_q = a_z + 4 * HA
    b_v = b_q + 2 * HB * DKB
    b_glr = b_v + 2 * HB * DVB
    c_q = b_glr + 2 * GATE_RANK
    gates = c_q + 4 * HC * DKC + HC * DVC
    return [(0, a_qkv), (a_z, b_q), (b_glr, c_q), None,
            (gates, end), (a_qkv, a_z), (b_v, b_glr), (c_q, gates), (b_q, b_v)]


def _reorder_kernel(w_ref, o_ref):
    at = 0
    for piece in _w_in_pieces(w_ref.shape[1]):
        if piece is None:
            width = N_F32 - at
            o_ref[0, at:at + width, :] = jnp.zeros((width, o_ref.shape[2]), BF16)
        else:
            width = piece[1] - piece[0]
            o_ref[0, at:at + width, :] = w_ref[0, piece[0]:piece[1], :].astype(BF16)
        at += width
    assert at == N_F32 + N_B16


def _reorder_w_in(w_in):
    w_t = jnp.swapaxes(w_in, 1, 2)
    depth, cols, d = w_t.shape
    kb = LANES
    return pl.pallas_call(
        _reorder_kernel,
        grid=(depth, d // kb),
        in_specs=[pl.BlockSpec((1, cols, kb), lambda l, i: (l, 0, i))],
        out_specs=pl.BlockSpec((1, N_F32 + N_B16, kb), lambda l, i: (l, 0, i)),
        out_shape=jax.ShapeDtypeStruct((depth, N_F32 + N_B16, d), BF16),
        compiler_params=_cparams(2),
        name="reorder_w_in",
    )(w_t)


def kernel(x, c, ctx, c_ctx, w_mod, b_mod, norm1_g, norm2_g, w_in, conv_a, a_log, dt_bias, gn_a, w_gate2, b_gate, gn_b, lam_c, gn_c, w_o_a, w_o_b, w_o_c, w_out, ffn_w1, ffn_w3, ffn_w2, router_w, moe_w1, moe_w3, moe_w2, final_g):
    n_b, l, d = x.shape
    lc = ctx.shape[1]
    depth = w_mod.shape[0]
    assert n_b * lc == l and lc % CHUNK == 0 and l & (l - 1) == 0 and lc & (lc - 1) == 0
    t_ctx = n_b * lc
    tm = min(1024, l)
    tm_small = min(512, l)

    n_mod = 16
    cvec = jnp.concatenate([c_ctx[None, :], c, jnp.zeros((n_mod - 1 - n_b, d), F32)], axis=0)
    mod_all = _modulation(cvec, w_mod, b_mod).reshape(depth, n_mod, 6, d)

    rope_cos, rope_sin = _rope_tables(l)
    w_in16 = _reorder_w_in(w_in)
    t_all = t_ctx + n_b * l
    tokens = (ctx.reshape(t_ctx, d), x.reshape(n_b * l, d), 0)

    for layer in range(depth):
        ctx_out = layer < depth - 1
        lam_init = 0.8 - 0.6 * math.exp(-0.3 * layer)
        mod = mod_all[layer]
        proj32, proj16 = _in_projection(*tokens, t_all, mod, norm1_g[layer], w_in16, layer, l, tm)

        qkvc = _short_conv(proj32, conv_a[layer], lc, l)
        oa0, oa1, ob0, ob1 = _recurrent_branches(qkvc, proj32, proj16, a_log[layer], dt_bias[layer],
                                                 w_gate2[layer], b_gate[layer], n_b, lc, l)
        od = _attention(proj16, lam_c[layer], gn_c[layer], rope_cos, rope_sin, n_b, lc, l, lam_init, ctx_out)

        row_off = 0 if ctx_out else t_ctx
        xs_new = _merge(oa0, oa1, ob0, ob1, *od, proj16, *tokens, t_all, mod, gn_a[layer], gn_b[layer],
                        w_o_a[layer].astype(BF16), w_o_b[layer].astype(BF16), w_o_c[layer].astype(BF16),
                        w_out[layer].astype(BF16), l, row_off, tm_small)

        i = layer // 2
        final = layer == depth - 1
        mod_off = 0 if ctx_out else 1
        if layer % 2 == 0:
            xs = _ffn(xs_new, mod, norm2_g[layer], final_g, ffn_w1[i].astype(BF16),
                      ffn_w3[i].astype(BF16), ffn_w2[i].astype(BF16), l, mod_off, tm,
                      ffn_w1.shape[2] // 2, final)
        else:
            rw = jnp.pad(router_w[i], ((0, 0), (0, LANES - N_EXPERTS)))
            xs = _moe(xs_new, mod, norm2_g[layer], rw, final_g, moe_w1[i].astype(BF16),
                      moe_w3[i].astype(BF16), moe_w2[i].astype(BF16), l, mod_off, tm,
                      moe_w1.shape[3] // 2, final)
        tokens = (xs, xs, t_ctx)

    return xs.reshape(n_b, l, d)
```

```python
import functools
import math

import jax
import jax.numpy as jnp
from jax import lax
from jax.experimental import pallas as pl
from jax.experimental.pallas import tpu as pltpu

F32 = jnp.float32
BF16 = jnp.bfloat16
HIGHEST = lax.Precision.HIGHEST

EPS = 1e-6
HA, DKA, DVA = 4, 128, 128
HB, DKB, DVB = 4, 64, 128
GATE_RANK, GATE_NORM = 16, 16.0
HC, DKC, DVC = 4, 64, 128
GRID_W, ROPE_THETA = 64, 10000.0
N_EXPERTS, TOP_K = 8, 2
CHUNK = 64
LANES = 128
ATTN_SUB = 64
ATTN_TQ = 1024
CONV_HEADS = 4

F_QKV = 0
F_SMALL = 1536
N_F32 = 1664
H_GATES = 0
H_AZ = 3072
H_BV = 3584
H_BR = 4096
H_CQ = 4608
H_CK = 5120
H_CV = 5632
H_BQ = 6144
H_BK = 6400
N_B16 = 6656
PROJ_TN = 1664

VMEM_LIMIT = 56 * 1024 * 1024


def _cparams(n_axes):
    return pltpu.CompilerParams(dimension_semantics=("arbitrary",) * n_axes,
                                vmem_limit_bytes=VMEM_LIMIT)


def _sigmoid(x):
    return 1.0 / (1.0 + jnp.exp2(x * -math.log2(math.e)))


def _silu(x):
    return x * _sigmoid(x)


def _softplus(x):
    return jnp.maximum(x, 0.0) + jnp.log(1.0 + jnp.exp(-jnp.abs(x)))


def _dot(a, b):
    return jnp.dot(a, b, preferred_element_type=F32)


def _dot_nt(a, b):
    return lax.dot_general(a, b, (((1,), (1,)), ((), ())), preferred_element_type=F32)


def _dot_tn(a, b):
    return lax.dot_general(a, b, (((0,), (0,)), ((), ())), preferred_element_type=F32)


def _split(a):
    hi = a.astype(BF16)
    lo = (a - hi.astype(F32)).astype(BF16)
    return hi, lo


def _dot3(a, b):
    ah, al = _split(a)
    bh, bl = _split(b)
    return _dot(ah, bh) + (_dot(ah, bl) + _dot(al, bh))


def _rms_rows(x, gain):
    ms = jnp.mean(x * x, axis=-1, keepdims=True)
    return x * lax.rsqrt(ms + EPS) * gain


def _mod_kernel(c_ref, w_ref, b_ref, o_ref):
    s = _silu(c_ref[...])
    o_ref[0] = jnp.dot(s, w_ref[0], precision=HIGHEST, preferred_element_type=F32) + b_ref[0]


def _modulation(cvec, w_mod, b_mod):
    depth, d, d6 = w_mod.shape
    n = cvec.shape[0]
    return pl.pallas_call(
        _mod_kernel,
        grid=(depth, d6 // d),
        in_specs=[pl.BlockSpec((n, d), lambda l, j: (0, 0)),
                  pl.BlockSpec((1, d, d), lambda l, j: (l, 0, j)),
                  pl.BlockSpec((1, 1, d), lambda l, j: (l, 0, j))],
        out_specs=pl.BlockSpec((1, n, d), lambda l, j: (l, 0, j)),
        out_shape=jax.ShapeDtypeStruct((depth, n, d6), F32),
        compiler_params=_cparams(2),
        name="adaln_mod",
    )(cvec, w_mod, b_mod.reshape(depth, 1, d6))


def _token_specs(tm, d, ctx_tiles, lat_first, off=0):
    return [pl.BlockSpec((tm, d), lambda i, *_: (jnp.minimum(i + off, ctx_tiles - 1), 0)),
            pl.BlockSpec((tm, d), lambda i, *_: (jnp.maximum(i + off - ctx_tiles, 0) + lat_first, 0))]


def _token_inputs(x_ctx, x_lat, lat_first, tm, ctx_tiles, off=0):
    d = x_ctx.shape[1]
    if x_lat is x_ctx:
        return [pl.BlockSpec((tm, d), lambda i, *_: (i + off, 0))], [x_ctx]
    return _token_specs(tm, d, ctx_tiles, lat_first // tm, off), [x_ctx, x_lat]


def _token_tile(x_refs, ctx_tiles):
    if len(x_refs) == 1:
        return x_refs[0][...]
    return jnp.where(pl.program_id(0) < ctx_tiles, x_refs[0][...], x_refs[1][...])


def _inproj_kernel(*refs, ctx_tiles, n_tok):
    x_refs, (mod_ref, g_ref, w_ref, o32_ref, o16_ref, h_scr) = refs[:n_tok], refs[n_tok:]
    j = pl.program_id(1)

    @pl.when(j == 0)
    def _():
        y = _rms_rows(_token_tile(x_refs, ctx_tiles), g_ref[...])
        h_scr[...] = (y * (1.0 + mod_ref[0, 1:2, :]) + mod_ref[0, 0:1, :]).astype(BF16)
        o32_ref[...] = _dot_nt(h_scr[...], w_ref[0])

    @pl.when(j > 0)
    def _():
        o16_ref[...] = _dot_nt(h_scr[...], w_ref[0]).astype(BF16)


def _in_projection(x_ctx, x_lat, lat_first, t, mod, g, w, layer, rows_per_mod, tm):
    d = x_ctx.shape[1]
    tn = PROJ_TN
    assert w.shape[1] == N_F32 + N_B16 and N_F32 == tn and N_B16 % tn == 0
    per = rows_per_mod // tm
    ctx_tiles = rows_per_mod // tm
    tok_specs, tok_args = _token_inputs(x_ctx, x_lat, lat_first, tm, ctx_tiles)
    return pl.pallas_call(
        functools.partial(_inproj_kernel, ctx_tiles=ctx_tiles, n_tok=len(tok_args)),
        grid=(t // tm, 1 + N_B16 // tn),
        in_specs=tok_specs
        + [pl.BlockSpec((1, 6, d), lambda i, j: (i // per, 0, 0)),
           pl.BlockSpec((1, d), lambda i, j: (0, 0)),
           pl.BlockSpec((1, tn, d), lambda i, j: (layer, j, 0))],
        out_specs=[pl.BlockSpec((tm, tn), lambda i, j: (i, 0)),
                   pl.BlockSpec((tm, tn), lambda i, j: (i, jnp.maximum(j - 1, 0)))],
        out_shape=[jax.ShapeDtypeStruct((t, N_F32), F32), jax.ShapeDtypeStruct((t, N_B16), BF16)],
        scratch_shapes=[pltpu.VMEM((tm, d), BF16)],
        compiler_params=_cparams(2),
        name="norm_mod_inproj",
    )(*tok_args, mod, g.reshape(1, d), w)


def _conv_kernel(x_ref, w_ref, o_ref, *, lc, l):
    p = pl.program_id(0)
    j = pl.program_id(1)
    x = x_ref[...]
    n = x.shape[0]
    row = lax.broadcasted_iota(jnp.int32, x.shape, 0)
    seq = jnp.where(p == 0, lc, l)
    pos = row & (seq - 1)
    prev = jnp.where(pos == 0, 0.0, pltpu.roll(x, 1, 0))
    nxt = jnp.where(pos == seq - 1, 0.0, pltpu.roll(x, n - 1, 0))
    w = w_ref[...]
    y = _silu(prev * w[0:1] + x * w[1:2] + nxt * w[2:3])
    heads_per_block = x.shape[1] // DKA
    for hh in range(heads_per_block):
        cols = slice(hh * DKA, (hh + 1) * DKA)
        yh = y[:, cols]
        inv = lax.rsqrt(jnp.sum(yh * yh, axis=-1, keepdims=True) + EPS)
        head = j * heads_per_block + hh
        fac = jnp.where(head < HA, inv * DKA ** -0.5, jnp.where(head < 2 * HA, inv, 1.0))
        o_ref[:, cols] = yh * fac


def _short_conv(proj, conv_w, lc, l):
    t = proj.shape[0]
    ncol = conv_w.shape[1]
    wide = CONV_HEADS * DKA
    return pl.pallas_call(
        functools.partial(_conv_kernel, lc=lc, l=l),
        grid=(t // l, ncol // wide),
        in_specs=[pl.BlockSpec((l, wide), lambda p, j: (p, j)),
                  pl.BlockSpec((conv_w.shape[0], wide), lambda p, j: (0, j))],
        out_specs=pl.BlockSpec((l, wide), lambda p, j: (p, j)),
        out_shape=jax.ShapeDtypeStruct((t, ncol), F32),
        compiler_params=_cparams(2),
        name="conv_silu_l2",
    )(proj, conv_w)


def _row_block(b, g, n_b, n_lat):
    return jnp.where(g == 0, b, n_b + b * n_lat + g - 1)


def _rev_group(g, n_g):
    return jnp.where(g == 0, 0, n_g - g)


def _tri_inverse_many(lms, ii, jj, uppers):
    c = lms[0].shape[0]
    eye = (ii == jj).astype(F32)
    n = len(lms)
    ts = None
    s, ls = 1, 0
    while s < c:
        same = (ii >> (ls + 1)) == (jj >> (ls + 1))
        io = (ii >> ls) & 1
        jo = (jj >> ls) & 1
        lower = same & (io == 1) & (jo == 0)
        upper = same & (io == 0) & (jo == 1)
        cms = [jnp.where(upper if uppers[i] else lower, lms[i], 0.0) for i in range(n)]
        if ts is None:
            ts = [eye - cm for cm in cms]
        else:
            t16 = [t.astype(BF16) for t in ts]
            ps = [_dot(t16[i], cms[i].astype(BF16)).astype(BF16) for i in range(n)]
            ts = [ts[i] - _dot(ps[i], t16[i]) for i in range(n)]
        s, ls = s * 2, ls + 1
    a_parts = [_split(eye + lm) for lm in lms]
    t_parts = [_split(t) for t in ts]
    res = [eye - (_dot(a_parts[i][0], t_parts[i][0])
                  + (_dot(a_parts[i][0], t_parts[i][1]) + _dot(a_parts[i][1], t_parts[i][0])))
           for i in range(n)]
    return [ts[i] + _dot(t_parts[i][0], res[i].astype(BF16)) for i in range(n)]


def _dn_kernel(alog_ref, dtb_ref, q0, k0, v0, sm0, q1, k1, v1, sm1, o0_ref, o1_ref, s_scr, *, n_chunks):
    @pl.when(pl.program_id(1) == 0)
    def _():
        s_scr[...] = jnp.zeros_like(s_scr)

    c_ = CHUNK
    refs = ((q0, k0, v0, o0_ref), (q1, k1, v1, o1_ref))
    sms = (sm0[...], sm1[...])
    sm_ts = tuple(s.T for s in sms)
    ii = lax.broadcasted_iota(jnp.int32, (c_, c_), 0)
    jj = lax.broadcasted_iota(jnp.int32, (c_, c_), 1)
    rows = [slice(c * c_, (c + 1) * c_) for c in range(n_chunks)]
    lanes = [slice(j * LANES, (j + 1) * LANES) for j in range(HA)]
    items = [(j, c, d) for j in range(HA) for c in range(n_chunks) for d in range(2)]
    class _Tiles:
        def __init__(self, which):
            self.which = which

        def __getitem__(self, i):
            j, c, d = items[i]
            return refs[d][self.which][rows[c], lanes[j]]

    qs, ks, vs = _Tiles(0), _Tiles(1), _Tiles(2)
    k16 = [ks[i].astype(BF16) for i in range(len(items))]
    kks = [_dot_nt(k, k) for k in k16]
    qks = [_dot_nt(qs[i].astype(BF16), k16[i]) for i in range(len(items))]

    betas, gcs, totals, gams, lms = [], [], [], [], []
    for i, (j, c, d) in enumerate(items):
        r = rows[c]
        incl = (jj <= ii) if d == 0 else (jj >= ii)
        strict = (jj < ii) if d == 0 else (jj > ii)
        incl_t = (ii <= jj) if d == 0 else (ii >= jj)
        a_scale = -jnp.exp(jnp.full((1, 1), alog_ref[d * HA + j], F32))
        dtb = dtb_ref[d * HA + j]
        col = d * HA + j
        beta = _sigmoid(sms[d][r, col:col + 1])
        g_col = a_scale * _softplus(sms[d][r, 2 * HA + col:2 * HA + col + 1] + dtb)
        g_row = a_scale * _softplus(sm_ts[d][2 * HA + col:2 * HA + col + 1, r] + dtb)
        gc_col = jnp.sum(jnp.where(incl, g_row, 0.0), axis=1, keepdims=True)
        gc_row = jnp.sum(jnp.where(incl_t, g_col, 0.0), axis=0, keepdims=True)
        gam = jnp.where(incl, jnp.exp(gc_col - gc_row), 0.0)
        betas.append(beta)
        gcs.append(gc_col)
        totals.append(jnp.sum(g_col, axis=0, keepdims=True))
        gams.append(gam)
        lms.append(jnp.where(strict, beta * kks[i] * gam, 0.0))

    tinvs = _tri_inverse_many(lms, ii, jj, [d == 1 for _, _, d in items])
    egs = [jnp.exp(gc) for gc in gcs]
    uws = []
    for i in range(len(items)):
        rhs = jnp.concatenate([vs[i] * betas[i], ks[i] * (betas[i] * egs[i])], axis=1).astype(BF16)
        uws.append(_dot(tinvs[i].astype(BF16), rhs))
    us = [uw[:, :DVA] for uw in uws]
    wq16 = [jnp.concatenate([uws[i][:, DVA:], qs[i] * egs[i]], axis=0).astype(BF16) for i in range(len(items))]
    kg16 = [(ks[i] * jnp.exp(totals[i] - gcs[i])).astype(BF16) for i in range(len(items))]
    qkg16 = [(qks[i] * gams[i]).astype(BF16) for i in range(len(items))]
    carry = [jnp.exp(t) for t in totals]

    index = {item: i for i, item in enumerate(items)}
    chains = [(j, d) for j in range(HA) for d in range(2)]
    states = [s_scr[d * HA + j] for j, d in chains]
    for step in range(n_chunks):
        at = [index[(j, step if d == 0 else n_chunks - 1 - step, d)] for j, d in chains]
        s16 = [s.astype(BF16) for s in states]
        prods = [_dot(wq16[i], s16[n]) for n, i in enumerate(at)]
        vn16 = [(us[i] - prods[n][:c_]).astype(BF16) for n, i in enumerate(at)]
        for n, i in enumerate(at):
            j, c, d = items[i]
            refs[d][3][rows[c], lanes[j]] = (prods[n][c_:] + _dot(qkg16[i], vn16[n])).astype(BF16)
            states[n] = carry[i] * states[n] + _dot_tn(kg16[i], vn16[n])
    for n, (j, d) in enumerate(chains):
        s_scr[d * HA + j] = states[n]


def _gla_kernel(q0, k0, v0, sm0, q1, k1, v1, sm1, wg_ref, bg_ref, o0_ref, o1_ref, s_scr, *, n_chunks):
    @pl.when(pl.program_id(1) == 0)
    def _():
        s_scr[...] = jnp.zeros_like(s_scr)

    c_ = CHUNK
    sh = c_.bit_length() - 1
    g_ = n_chunks * c_
    kw = HB * DKB
    vw = HB * DVB
    ii = lax.broadcasted_iota(jnp.int32, (g_, g_), 0)
    jj = lax.broadcasted_iota(jnp.int32, (g_, g_), 1)
    same_chunk = (ii >> sh) == (jj >> sh)
    klane = lax.broadcasted_iota(jnp.int32, (1, kw), 1)
    ind16 = jnp.where((lax.broadcasted_iota(jnp.int32, (g_, n_chunks * LANES), 0) >> sh)
                      == (lax.broadcasted_iota(jnp.int32, (g_, n_chunks * LANES), 1) >> (LANES.bit_length() - 1)),
                      1.0, 0.0).astype(BF16)
    rows = [slice(c * c_, (c + 1) * c_) for c in range(n_chunks)]
    refs = ((q0, k0, v0, sm0, o0_ref), (q1, k1, v1, sm1, o1_ref))

    def spread(x, picks):
        return jnp.concatenate([jnp.broadcast_to(x[p:p + 1, :], (c_, kw)) for p in picks], axis=0)

    dirs = (0, 1)
    incls = [same_chunk & ((jj <= ii) if d == 0 else (jj >= ii)) for d in dirs]
    tri16 = [jnp.where(m, 1.0, 0.0).astype(BF16) for m in incls]
    pres = [jnp.dot(refs[d][3][:, 4 * HA + d * GATE_RANK:4 * HA + (d + 1) * GATE_RANK], wg_ref[d],
                    precision=HIGHEST, preferred_element_type=F32) + bg_ref[d] for d in dirs]
    glogs = [(jnp.minimum(p, 0.0) - jnp.log(1.0 + jnp.exp(-jnp.abs(p)))) * (1.0 / GATE_NORM) for p in pres]
    parts = [_split(g) for g in glogs]
    gcs = [_dot(tri16[d], parts[d][0]) + _dot(tri16[d], parts[d][1]) for d in dirs]
    tot_cols = [_dot_tn(parts[d][0], ind16) + _dot_tn(parts[d][1], ind16) for d in dirs]
    mids = [spread(gcs[d], [c * c_ + c_ // 2 - 1 for c in range(n_chunks)]) for d in dirs]
    tots = [spread(gcs[d], [c * c_ + (c_ - 1 if d == 0 else 0) for c in range(n_chunks)]) for d in dirs]
    qs = [refs[d][0][...].astype(F32) * DKB ** -0.5 for d in dirs]
    ks = [refs[d][1][...].astype(F32) for d in dirs]
    v16s = [refs[d][2][...] for d in dirs]
    qts = [qs[d] * jnp.exp(gcs[d] - mids[d]) for d in dirs]
    kt16s = [(ks[d] * jnp.exp(mids[d] - gcs[d])).astype(BF16) for d in dirs]
    qg16s = [(qs[d] * jnp.exp(gcs[d])).astype(BF16) for d in dirs]
    kg16s = [(ks[d] * jnp.exp(tots[d] - gcs[d])).astype(BF16) for d in dirs]
    a16s = {}
    for h in range(HB):
        in_head = (klane >= h * DKB) & (klane < (h + 1) * DKB)
        for d in dirs:
            a = _dot_nt(jnp.where(in_head, qts[d], 0.0).astype(BF16), kt16s[d])
            a16s[(d, h)] = jnp.where(incls[d], a, 0.0).astype(BF16)
    o_intra = [[_dot(a16s[(d, h)], v16s[d][:, h * DVB:(h + 1) * DVB]) for h in range(HB)] for d in dirs]
    xs = [[_dot_tn(kg16s[d][r, :], v16s[d][r, :]) for r in rows] for d in dirs]
    decays = [[jnp.exp(tot_cols[d][:, c * LANES:(c + 1) * LANES]) for c in range(n_chunks)] for d in dirs]

    states = [[s_scr[d, h] for h in range(HB)] for d in dirs]
    zero16 = jnp.zeros((DKB, DVB), BF16)
    for step in range(n_chunks):
        for d in range(2):
            c = step if d == 0 else n_chunks - 1 - step
            r = rows[c]
            s16 = [s.astype(BF16) for s in states[d]]
            s_all = jnp.concatenate(
                [jnp.concatenate([s16[h] if hv == h else zero16 for hv in range(HB)], axis=1) for h in range(HB)],
                axis=0)
            o_inter = _dot(qg16s[d][r, :], s_all)
            for h in range(HB):
                vs = slice(h * DVB, (h + 1) * DVB)
                kr = slice(h * DKB, (h + 1) * DKB)
                refs[d][4][r, vs] = (o_inter[:, vs] + o_intra[d][h][r, :]).astype(BF16)
                states[d][h] = decays[d][c][kr, :] * states[d][h] + xs[d][c][kr, vs]
    for d in dirs:
        for h in range(HB):
            s_scr[d, h] = states[d][h]


def _recurrent_branches(qkvc, proj32, proj16, a_log, dt_bias, w_gate2, b_gate, n_b, lc, l):
    t = qkvc.shape[0]
    n_lat = l // lc
    n_g = n_lat + 1
    wide = HA * DVA
    kw, vw = HB * DKB, HB * DVB
    rb0 = lambda b, g: _row_block(b, g, n_b, n_lat)
    rb1 = lambda b, g: _row_block(b, _rev_group(g, n_g), n_b, n_lat)
    smem = pl.BlockSpec(memory_space=pltpu.SMEM)

    def dn_specs(rb):
        return [pl.BlockSpec((lc, wide), lambda b, g: (rb(b, g), 0)),
                pl.BlockSpec((lc, wide), lambda b, g: (rb(b, g), 1)),
                pl.BlockSpec((lc, wide), lambda b, g: (rb(b, g), 2)),
                pl.BlockSpec((lc, LANES), lambda b, g: (rb(b, g), F_SMALL // LANES))]

    def gla_specs(rb):
        return [pl.BlockSpec((lc, kw), lambda b, g: (rb(b, g), H_BQ // kw)),
                pl.BlockSpec((lc, kw), lambda b, g: (rb(b, g), H_BK // kw)),
                pl.BlockSpec((lc, vw), lambda b, g: (rb(b, g), H_BV // vw)),
                pl.BlockSpec((lc, LANES), lambda b, g: (rb(b, g), F_SMALL // LANES))]

    out = lambda rb, width: pl.BlockSpec((lc, width), lambda b, g: (rb(b, g), 0))
    oa0, oa1 = pl.pallas_call(
        functools.partial(_dn_kernel, n_chunks=lc // CHUNK),
        grid=(n_b, n_g),
        in_specs=[smem, smem] + dn_specs(rb0) + dn_specs(rb1),
        out_specs=[out(rb0, wide), out(rb1, wide)],
        out_shape=[jax.ShapeDtypeStruct((t, wide), BF16)] * 2,
        scratch_shapes=[pltpu.VMEM((2 * HA, DKA, DVA), F32)],
        compiler_params=_cparams(2),
        name="deltanet",
    )(a_log.reshape(-1), dt_bias.reshape(-1), qkvc, qkvc, qkvc, proj32, qkvc, qkvc, qkvc, proj32)
    ob0, ob1 = pl.pallas_call(
        functools.partial(_gla_kernel, n_chunks=lc // CHUNK),
        grid=(n_b, n_g),
        in_specs=gla_specs(rb0) + gla_specs(rb1)
        + [pl.BlockSpec((2, GATE_RANK, kw), lambda b, g: (0, 0, 0)),
           pl.BlockSpec((2, 1, kw), lambda b, g: (0, 0, 0))],
        out_specs=[out(rb0, vw), out(rb1, vw)],
        out_shape=[jax.ShapeDtypeStruct((t, vw), BF16)] * 2,
        scratch_shapes=[pltpu.VMEM((2, HB, DKB, DVB), F32)],
        compiler_params=_cparams(2),
        name="gla_bidir",
    )(proj16, proj16, proj16, proj32, proj16, proj16, proj16, proj32, w_gate2, b_gate.reshape(2, 1, kw))
    return oa0, oa1, ob0, ob1


def _lambda_full(lam, lam_init):
    return (jnp.exp(jnp.sum(lam[0:1] * lam[1:2], axis=1, keepdims=True))
            - jnp.exp(jnp.sum(lam[2:3] * lam[3:4], axis=1, keepdims=True)) + lam_init)


def _attend(q, keys, vals, lam_f, gain, lam_init, o_ref):
    lane = lax.broadcasted_iota(jnp.int32, (1, LANES), 1)
    n = ATTN_SUB
    q = q * (DKC ** -0.5 * math.log2(math.e))
    subs = [q[r * n:(r + 1) * n] for r in range(q.shape[0] // n)]
    q2s = [jnp.concatenate([jnp.where(lane < DKC, x, 0.0), jnp.where(lane >= DKC, x, 0.0)],
                           axis=0).astype(BF16) for x in subs]
    ss = [_dot_nt(q2, keys) for q2 in q2s]
    es = [jnp.exp2(s - jnp.max(s, axis=-1, keepdims=True)) for s in ss]
    invs = [1.0 / jnp.sum(e, axis=-1, keepdims=True) for e in es]
    o2s = [_dot(es[r].astype(BF16), vals) * invs[r] for r in range(len(es))]
    for r, o2 in enumerate(o2s):
        o = o2[:n] - lam_f * o2[n:]
        o_ref[r * n:(r + 1) * n, :] = (_rms_rows(o, gain) * (1.0 - lam_init)).astype(BF16)


def _attn_lat_kernel(lam_ref, gn_ref, q_ref, kc_ref, kl_ref, vc_ref, vl_ref, cq_ref, sq_ref, ck_ref, sk_ref,
                     o_ref, k_scr, v_scr, *, lc, lam_init):
    lane = lax.broadcasted_iota(jnp.int32, (1, LANES), 1)
    first_half = (lane & 31) < 16

    def rope(x, cos, sin):
        swapped = jnp.where(first_half, pltpu.roll(x, LANES - 16, 1), pltpu.roll(x, 16, 1))
        return x * cos + swapped * sin

    @pl.when(pl.program_id(2) == 0)
    def _():
        k_scr[0:lc, :] = kc_ref[...]
        k_scr[lc:, :] = rope(kl_ref[...].astype(F32), ck_ref[...], sk_ref[...]).astype(BF16)
        v_scr[0:lc, :] = vc_ref[...]
        v_scr[lc:, :] = vl_ref[...]

    _attend(rope(q_ref[...].astype(F32), cq_ref[...], sq_ref[...]), k_scr[...], v_scr[...],
            _lambda_full(lam_ref[...], lam_init), gn_ref[...], lam_init, o_ref)


def _attn_ctx_kernel(lam_ref, gn_ref, q_ref, k_ref, v_ref, o_ref, *, lam_init):
    _attend(q_ref[...].astype(F32), k_ref[...], v_ref[...], _lambda_full(lam_ref[...], lam_init),
            gn_ref[...], lam_init, o_ref)


def _attention(proj, lam, gn, rope_cos, rope_sin, n_b, lc, l, lam_init, ctx_out):
    tq = min(ATTN_TQ, l)
    per = l // tq
    cq, ck, cv = H_CQ // LANES, H_CK // LANES, H_CV // LANES
    small = [pl.BlockSpec((4, DKC), lambda b, h, *_: (0, 0)), pl.BlockSpec((1, DVC), lambda b, h, *_: (0, 0))]
    tab_q = pl.BlockSpec((tq, LANES), lambda b, h, i: (i, 0))
    tab_k = pl.BlockSpec((l, LANES), lambda b, h, i: (0, 0))
    od = pl.pallas_call(
        functools.partial(_attn_lat_kernel, lc=lc, lam_init=lam_init),
        grid=(n_b, HC, per),
        in_specs=small
        + [pl.BlockSpec((tq, LANES), lambda b, h, i: ((b + 1) * per + i, cq + h)),
           pl.BlockSpec((lc, LANES), lambda b, h, i: (b, ck + h)),
           pl.BlockSpec((l, LANES), lambda b, h, i: (b + 1, ck + h)),
           pl.BlockSpec((lc, LANES), lambda b, h, i: (b, cv + h)),
           pl.BlockSpec((l, LANES), lambda b, h, i: (b + 1, cv + h)),
           tab_q, tab_q, tab_k, tab_k],
        out_specs=pl.BlockSpec((tq, LANES), lambda b, h, i: (b * per + i, h)),
        out_shape=jax.ShapeDtypeStruct((n_b * l, HC * DVC), BF16),
        scratch_shapes=[pltpu.VMEM((lc + l, LANES), BF16), pltpu.VMEM((lc + l, LANES), BF16)],
        compiler_params=_cparams(3),
        name="diff_attention",
    )(lam, gn.reshape(1, DVC), proj, proj, proj, proj, proj, rope_cos, rope_sin, rope_cos, rope_sin)
    if not ctx_out:
        return None, od
    od_ctx = pl.pallas_call(
        functools.partial(_attn_ctx_kernel, lam_init=lam_init),
        grid=(n_b, HC),
        in_specs=small
        + [pl.BlockSpec((lc, LANES), lambda b, h: (b, cq + h)),
           pl.BlockSpec((lc, LANES), lambda b, h: (b, ck + h)),
           pl.BlockSpec((lc, LANES), lambda b, h: (b, cv + h))],
        out_specs=pl.BlockSpec((lc, LANES), lambda b, h: (b, h)),
        out_shape=jax.ShapeDtypeStruct((n_b * lc, HC * DVC), BF16),
        compiler_params=_cparams(2),
        name="diff_attention_ctx",
    )(lam, gn.reshape(1, DVC), proj, proj, proj)
    return od_ctx, od


def _rope_tables(n_tokens):
    rows = n_tokens // GRID_W
    row = jnp.repeat(jnp.arange(rows, dtype=F32), GRID_W)
    col = jnp.tile(jnp.arange(GRID_W, dtype=F32), rows)
    n_freq = DKC // 4
    inv_freq = ROPE_THETA ** (-jnp.arange(n_freq, dtype=F32) / n_freq)
    ang_r = row[:, None] * inv_freq
    ang_c = col[:, None] * inv_freq
    cos = jnp.concatenate([jnp.cos(ang_r)] * 2 + [jnp.cos(ang_c)] * 2, axis=1)
    sin = jnp.concatenate([-jnp.sin(ang_r), jnp.sin(ang_r), -jnp.sin(ang_c), jnp.sin(ang_c)], axis=1)
    return jnp.concatenate([cos, cos], axis=1), jnp.concatenate([sin, sin], axis=1)


def _head_rms(o, gain, n_heads, width):
    parts = []
    for h in range(n_heads):
        parts.append(_rms_rows(o[:, h * width:(h + 1) * width], gain))
    return jnp.concatenate(parts, axis=1)


def _merge_kernel(*refs, ctx_tiles, n_tok, n_od):
    x_refs, od_refs = refs[:n_tok], refs[n_tok:n_tok + n_od]
    (oa0, oa1, z_ref, ob0, ob1, r_ref, ga_ref, gb_ref, gd_ref, mod_ref,
     gna_ref, gnb_ref, woa_ref, wob_ref, woc_ref, wout_ref, o_ref) = refs[n_tok + n_od:]
    f32 = lambda ref: ref[...].astype(F32)
    ya = _head_rms(f32(oa0) + f32(oa1), gna_ref[...], HA, DVA) * _silu(f32(z_ref))
    yb = _head_rms(f32(ob0) + f32(ob1), gnb_ref[...], HB, DVB) * _silu(f32(r_ref))
    acc = _sigmoid(f32(ga_ref)) * _dot(ya.astype(BF16), woa_ref[...])
    acc = acc + _sigmoid(f32(gb_ref)) * _dot(yb.astype(BF16), wob_ref[...])
    acc = acc + _sigmoid(f32(gd_ref)) * _dot(_token_tile(od_refs, ctx_tiles), woc_ref[...])
    y = _dot(acc.astype(BF16), wout_ref[...])
    o_ref[...] = _token_tile(x_refs, ctx_tiles) + mod_ref[0, 2:3, :] * y


def _merge(oa0, oa1, ob0, ob1, od_ctx, od_lat, proj, x_ctx, x_lat, lat_first, t, mod, gn_a, gn_b,
           woa, wob, woc, wout, rows_per_mod, row_off, tm):
    d = x_ctx.shape[1]
    off = row_off // tm
    ctx_tiles = rows_per_mod // tm
    n_tiles = (t - row_off) // tm
    per = rows_per_mod // tm
    w5 = HA * DVA
    row = lambda i: (i + off, 0)
    col = lambda c: (lambda i: (i + off, c))
    tok_specs, tok_args = _token_inputs(x_ctx, x_lat, lat_first, tm, ctx_tiles, off)
    if od_ctx is None:
        assert off == ctx_tiles
        od_specs, od_args = [pl.BlockSpec((tm, w5), lambda i: (i, 0))], [od_lat]
    else:
        od_specs, od_args = _token_specs(tm, w5, ctx_tiles, 0, off), [od_ctx, od_lat]
    return pl.pallas_call(
        functools.partial(_merge_kernel, ctx_tiles=ctx_tiles - off, n_tok=len(tok_args), n_od=len(od_args)),
        grid=(n_tiles,),
        in_specs=tok_specs + od_specs
        + [pl.BlockSpec((tm, w5), row),
                  pl.BlockSpec((tm, w5), row),
                  pl.BlockSpec((tm, w5), col(H_AZ // w5)),
                  pl.BlockSpec((tm, w5), row),
                  pl.BlockSpec((tm, w5), row),
                  pl.BlockSpec((tm, w5), col(H_BR // w5)),
                  pl.BlockSpec((tm, d), col(H_GATES // d)),
                  pl.BlockSpec((tm, d), col(H_GATES // d + 1)),
                  pl.BlockSpec((tm, d), col(H_GATES // d + 2)),
                  pl.BlockSpec((1, 6, d), lambda i: ((i + off) // per, 0, 0)),
                  pl.BlockSpec((1, DVA), lambda i: (0, 0)),
                  pl.BlockSpec((1, DVB), lambda i: (0, 0)),
                  pl.BlockSpec((w5, d), lambda i: (0, 0)),
                  pl.BlockSpec((w5, d), lambda i: (0, 0)),
                  pl.BlockSpec((w5, d), lambda i: (0, 0)),
                  pl.BlockSpec((d, d), lambda i: (0, 0))],
        out_specs=pl.BlockSpec((tm, d), lambda i: (i, 0)),
        out_shape=jax.ShapeDtypeStruct((t - row_off, d), F32),
        compiler_params=_cparams(1),
        name="merge_outproj",
    )(*tok_args, *od_args, oa0, oa1, proj, ob0, ob1, proj, proj, proj, proj, mod,
      gn_a.reshape(1, DVA), gn_b.reshape(1, DVB), woa, wob, woc, wout)


def _ffn_kernel(x_ref, mod_ref, g_ref, fg_ref, w1_ref, w3_ref, w2_ref, o_ref, h_scr, *, final):
    f = pl.program_id(1)

    @pl.when(f == 0)
    def _():
        y = _rms_rows(x_ref[...], g_ref[...])
        h_scr[...] = (y * (1.0 + mod_ref[0, 4:5, :]) + mod_ref[0, 3:4, :]).astype(BF16)
        o_ref[...] = jnp.zeros_like(o_ref)

    h = h_scr[...]
    t = _silu(_dot(h, w1_ref[...])) * _dot(h, w3_ref[...])
    o_ref[...] += _dot(t.astype(BF16), w2_ref[...])

    @pl.when(f == pl.num_programs(1) - 1)
    def _():
        out = x_ref[...] + mod_ref[0, 5:6, :] * o_ref[...]
        if final:
            out = _rms_rows(out, fg_ref[...])
        o_ref[...] = out


def _ffn(x, mod, g, final_g, w1, w3, w2, rows_per_mod, mod_off, tm, tf, final):
    t, d = x.shape
    dff = w1.shape[1]
    per = rows_per_mod // tm
    resident = dict(pipeline_mode=pl.Buffered(1)) if tf == dff else {}
    return pl.pallas_call(
        functools.partial(_ffn_kernel, final=final),
        grid=(t // tm, dff // tf),
        in_specs=[pl.BlockSpec((tm, d), lambda i, f: (i, 0)),
                  pl.BlockSpec((1, 6, d), lambda i, f: (i // per + mod_off, 0, 0)),
                  pl.BlockSpec((1, d), lambda i, f: (0, 0)),
                  pl.BlockSpec((1, d), lambda i, f: (0, 0)),
                  pl.BlockSpec((d, tf), lambda i, f: (0, f), **resident),
                  pl.BlockSpec((d, tf), lambda i, f: (0, f), **resident),
                  pl.BlockSpec((tf, d), lambda i, f: (f, 0), **resident)],
        out_specs=pl.BlockSpec((tm, d), lambda i, f: (i, 0)),
        out_shape=jax.ShapeDtypeStruct((t, d), F32),
        scratch_shapes=[pltpu.VMEM((tm, d), BF16)],
        compiler_params=_cparams(2),
        name="dense_ffn",
    )(x, mod, g.reshape(1, d), final_g.reshape(1, d), w1, w3, w2)


MOE_ROWS = 256
MOE_MOVE_UNIT = 128
MOE_FFN_UNIT = 128


def _moe_kernel(x_ref, mod_ref, g_ref, rw_ref, tri_ref, fg_ref, w1_ref, w3_ref, w2_ref, o_ref,
                h_scr, gate_scr, sel_scr, rank_scr, selr_scr, rankr_scr, xg_scr, yg_scr, nb_scr, *, final):
    e = pl.program_id(1)
    f = pl.program_id(2)
    n_f = pl.num_programs(2)
    tm, d = x_ref.shape
    cb = MOE_ROWS
    lane = lax.broadcasted_iota(jnp.int32, (1, LANES), 1)

    def for_blocks(body, slot, unit):
        units = nb_scr[slot]
        per = cb // unit
        nb = units // per

        def full(j, carry):
            body(pl.multiple_of(j * cb, cb), cb)
            return carry

        lax.fori_loop(0, nb, full, 0)
        rem = units - nb * per
        r0 = nb * cb
        size = cb // 2
        while size >= unit:
            has = (rem // (size // unit)) % 2 == 1

            @pl.when(has)
            def _():
                body(pl.multiple_of(r0, unit), size)

            r0 = r0 + jnp.where(has, size, 0)
            size //= 2

    @pl.when((e == 0) & (f == 0))
    def _():
        y = _rms_rows(x_ref[...], g_ref[...])
        hmod = y * (1.0 + mod_ref[0, 4:5, :]) + mod_ref[0, 3:4, :]
        h_scr[...] = hmod.astype(BF16)
        o_ref[...] = jnp.zeros_like(o_ref)
        logits = _dot3(hmod, rw_ref[...])
        logits = jnp.where(lane < N_EXPERTS, logits, -jnp.inf)
        lanef = lane.astype(F32)
        m1 = jnp.max(logits, axis=1, keepdims=True)
        i1 = jnp.min(jnp.where(logits == m1, lanef, float(LANES)), axis=1, keepdims=True)
        hit1 = lanef == i1
        rest = jnp.where(hit1, -jnp.inf, logits)
        m2 = jnp.max(rest, axis=1, keepdims=True)
        i2 = jnp.min(jnp.where(rest == m2, lanef, float(LANES)), axis=1, keepdims=True)
        hit2 = lanef == i2
        ex = jnp.exp(m2 - m1)
        inv = 1.0 / (1.0 + ex)
        gate_scr[...] = jnp.where(hit1, inv, 0.0) + jnp.where(hit2, ex * inv, 0.0)
        sel = jnp.where(hit1 | hit2, 1.0, 0.0)
        sel_scr[...] = sel
        sel16 = sel.astype(BF16)
        rank = _dot(tri_ref[...], sel16)
        rank_scr[...] = rank
        pick = jnp.where(lax.broadcasted_iota(jnp.int32, (8, LANES), 0)
                         == lax.broadcasted_iota(jnp.int32, (8, LANES), 1), 1.0, 0.0).astype(BF16)
        selr_scr[...] = _dot_nt(pick, sel16)
        hi = jnp.floor(rank * (1.0 / 32.0))
        lo = rank - 32.0 * hi
        rankr_scr[...] = 32.0 * _dot_nt(pick, hi.astype(BF16)) + _dot_nt(pick, lo.astype(BF16))

    @pl.when(f == 0)
    def _():
        sel_row = selr_scr[pl.ds(e, 1), :]
        rank_row = rankr_scr[pl.ds(e, 1), :]
        n_e = jnp.sum(sel_row).astype(jnp.int32)
        nb_scr[0] = (n_e + (MOE_MOVE_UNIT - 1)) // MOE_MOVE_UNIT
        nb_scr[1] = (n_e + (MOE_FFN_UNIT - 1)) // MOE_FFN_UNIT

        def gather(r0, bs):
            slot = lax.broadcasted_iota(jnp.int32, (bs, tm), 0).astype(F32) + r0.astype(F32)
            onehot = jnp.where((rank_row == slot) & (sel_row > 0.0), 1.0, 0.0)
            xg_scr[pl.ds(r0, bs), :] = _dot(onehot.astype(BF16), h_scr[...]).astype(BF16)
            yg_scr[pl.ds(r0, bs), :] = jnp.zeros((bs, d), F32)

        for_blocks(gather, 0, MOE_MOVE_UNIT)

    def expert(r0, bs):
        xb = xg_scr[pl.ds(r0, bs), :]
        t = _silu(_dot(xb, w1_ref[0])) * _dot(xb, w3_ref[0])
        yg_scr[pl.ds(r0, bs), :] += _dot(t.astype(BF16), w2_ref[0])

    for_blocks(expert, 1, MOE_FFN_UNIT)

    @pl.when(f == n_f - 1)
    def _():
        pick_e = lane == e
        rank_col = jnp.sum(jnp.where(pick_e, rank_scr[...], 0.0), axis=1, keepdims=True)
        sel_col = jnp.sum(jnp.where(pick_e, sel_scr[...], 0.0), axis=1, keepdims=True)
        gate_col = jnp.sum(jnp.where(pick_e, gate_scr[...], 0.0), axis=1, keepdims=True)

        def scatter(r0, bs):
            slot = lax.broadcasted_iota(jnp.int32, (tm, bs), 1).astype(F32) + r0.astype(F32)
            onehot = jnp.where((rank_col == slot) & (sel_col > 0.0), 1.0, 0.0)
            o_ref[...] += gate_col * _dot(onehot.astype(BF16), yg_scr[pl.ds(r0, bs), :].astype(BF16))

        for_blocks(scatter, 0, MOE_MOVE_UNIT)

    @pl.when((e == pl.num_programs(1) - 1) & (f == n_f - 1))
    def _():
        out = x_ref[...] + mod_ref[0, 5:6, :] * o_ref[...]
        if final:
            out = _rms_rows(out, fg_ref[...])
        o_ref[...] = out


def _moe(x, mod, g, router_w, final_g, w1, w3, w2, rows_per_mod, mod_off, tm, tf, final):
    t, d = x.shape
    n_e, _, dff = w1.shape
    per = rows_per_mod // tm
    return pl.pallas_call(
        functools.partial(_moe_kernel, final=final),
        grid=(t // tm, n_e, dff // tf),
        in_specs=[pl.BlockSpec((tm, d), lambda i, e, f: (i, 0)),
                  pl.BlockSpec((1, 6, d), lambda i, e, f: (i // per + mod_off, 0, 0)),
                  pl.BlockSpec((1, d), lambda i, e, f: (0, 0)),
                  pl.BlockSpec((d, LANES), lambda i, e, f: (0, 0)),
                  pl.BlockSpec((tm, tm), lambda i, e, f: (0, 0)),
                  pl.BlockSpec((1, d), lambda i, e, f: (0, 0)),
                  pl.BlockSpec((1, d, tf), lambda i, e, f: (e, 0, f)),
                  pl.BlockSpec((1, d, tf), lambda i, e, f: (e, 0, f)),
                  pl.BlockSpec((1, tf, d), lambda i, e, f: (e, f, 0))],
        out_specs=pl.BlockSpec((tm, d), lambda i, e, f: (i, 0)),
        out_shape=jax.ShapeDtypeStruct((t, d), F32),
        scratch_shapes=[pltpu.VMEM((tm, d), BF16),
                        pltpu.VMEM((tm, LANES), F32),
                        pltpu.VMEM((tm, LANES), F32),
                        pltpu.VMEM((tm, LANES), F32),
                        pltpu.VMEM((8, tm), F32),
                        pltpu.VMEM((8, tm), F32),
                        pltpu.VMEM((tm, d), BF16),
                        pltpu.VMEM((tm, d), F32),
                        pltpu.SMEM((2,), jnp.int32)],
        compiler_params=_cparams(3),
        name="routed_ffn_sparse",
    )(x, mod, g.reshape(1, d), router_w, jnp.tril(jnp.ones((tm, tm), BF16), -1),
      final_g.reshape(1, d), w1, w3, w2)


def _w_in_pieces(end):
    a_qkv = 2 * HA * DKA + HA * DVA
    a_z = a_qkv + HA * DVA
    b_q = a_z + 4 * HA
    b_v = b_q + 2 * HB * DKB
    b_glr = b_v + 2 * HB * DVB
    c_q = b_glr + 2 * GATE_RANK
    gates = c_q + 4 * HC * DKC + HC * DVC
    return [(0, a_qkv), (a_z, b_q), (b_glr, c_q), None,
            (gates, end), (a_qkv, a_z), (b_v, b_glr), (c_q, gates), (b_q, b_v)]


def _reorder_kernel(w_ref, o_ref):
    at = 0
    for piece in _w_in_pieces(w_ref.shape[1]):
        if piece is None:
            width = N_F32 - at
            o_ref[0, at:at + width, :] = jnp.zeros((width, o_ref.shape[2]), BF16)
        else:
            width = piece[1] - piece[0]
            o_ref[0, at:at + width, :] = w_ref[0, piece[0]:piece[1], :].astype(BF16)
        at += width
    assert at == N_F32 + N_B16


def _reorder_w_in(w_in):
    w_t = jnp.swapaxes(w_in, 1, 2)
    depth, cols, d = w_t.shape
    kb = LANES
    return pl.pallas_call(
        _reorder_kernel,
        grid=(depth, d // kb),
        in_specs=[pl.BlockSpec((1, cols, kb), lambda l, i: (l, 0, i))],
        out_specs=pl.BlockSpec((1, N_F32 + N_B16, kb), lambda l, i: (l, 0, i)),
        out_shape=jax.ShapeDtypeStruct((depth, N_F32 + N_B16, d), BF16),
        compiler_params=_cparams(2),
        name="reorder_w_in",
    )(w_t)


def kernel(x, c, ctx, c_ctx, w_mod, b_mod, norm1_g, norm2_g, w_in, conv_a, a_log, dt_bias, gn_a, w_gate2, b_gate, gn_b, lam_c, gn_c, w_o_a, w_o_b, w_o_c, w_out, ffn_w1, ffn_w3, ffn_w2, router_w, moe_w1, moe_w3, moe_w2, final_g):
    n_b, l, d = x.shape
    lc = ctx.shape[1]
    depth = w_mod.shape[0]
    assert n_b * lc == l and lc % CHUNK == 0 and l & (l - 1) == 0 and lc & (lc - 1) == 0
    t_ctx = n_b * lc
    tm = min(1024, l)
    tm_small = min(512, l)

    n_mod = 16
    cvec = jnp.concatenate([c_ctx[None, :], c, jnp.zeros((n_mod - 1 - n_b, d), F32)], axis=0)
    mod_all = _modulation(cvec, w_mod, b_mod).reshape(depth, n_mod, 6, d)

    rope_cos, rope_sin = _rope_tables(l)
    w_in16 = _reorder_w_in(w_in)
    t_all = t_ctx + n_b * l
    tokens = (ctx.reshape(t_ctx, d), x.reshape(n_b * l, d), 0)

    for layer in range(depth):
        ctx_out = layer < depth - 1
        lam_init = 0.8 - 0.6 * math.exp(-0.3 * layer)
        mod = mod_all[layer]
        proj32, proj16 = _in_projection(*tokens, t_all, mod, norm1_g[layer], w_in16, layer, l, tm)

        qkvc = _short_conv(proj32, conv_a[layer], lc, l)
        oa0, oa1, ob0, ob1 = _recurrent_branches(qkvc, proj32, proj16, a_log[layer], dt_bias[layer],
                                                 w_gate2[layer], b_gate[layer], n_b, lc, l)
        od = _attention(proj16, lam_c[layer], gn_c[layer], rope_cos, rope_sin, n_b, lc, l, lam_init, ctx_out)

        row_off = 0 if ctx_out else t_ctx
        xs_new = _merge(oa0, oa1, ob0, ob1, *od, proj16, *tokens, t_all, mod, gn_a[layer], gn_b[layer],
                        w_o_a[layer].astype(BF16), w_o_b[layer].astype(BF16), w_o_c[layer].astype(BF16),
                        w_out[layer].astype(BF16), l, row_off, tm_small)

        i = layer // 2
        final = layer == depth - 1
        mod_off = 0 if ctx_out else 1
        if layer % 2 == 0:
            xs = _ffn(xs_new, mod, norm2_g[layer], final_g, ffn_w1[i].astype(BF16),
                      ffn_w3[i].astype(BF16), ffn_w2[i].astype(BF16), l, mod_off, tm_small,
                      ffn_w1.shape[2], final)
        else:
            rw = jnp.pad(router_w[i], ((0, 0), (0, LANES - N_EXPERTS)))
            xs = _moe(xs_new, mod, norm2_g[layer], rw, final_g, moe_w1[i].astype(BF16),
                      moe_w3[i].astype(BF16), moe_w2[i].astype(BF16), l, mod_off, tm,
                      moe_w1.shape[3] // 2, final)
        tokens = (xs, xs, t_ctx)

    return xs.reshape(n_b, l, d)
```
